```python
import math
import jax, jax.numpy as jnp
from jax import lax
import numpy as np

D_MODEL = 1024
BATCH = 8
SEQ = 2048
DEPTH = 2

CHUNK = 64
Q_BLOCK = 128
RMS_EPS = 1e-6
LN_EPS = 1e-5
ROPE_THETA = 500000.0
ROPE_FRACTION = 4
DIFF_HEADS = D_MODEL // 256
DIFF_HEAD_DIM = 64
DIFF_WIDTH = DIFF_HEADS * 2 * DIFF_HEAD_DIM
CONV_CHANNELS = D_MODEL - DIFF_WIDTH
CONV_WIDTH = 31
SSM_CHANNELS = D_MODEL // 4
SSM_GROUP = 16
SSM_GROUPS = SSM_CHANNELS // SSM_GROUP
SSM_STATE = 64
DT_MIN = 1e-3
DT_MAX = 1e-1
MLA_V_DIM = 128
MLA_HEADS = (D_MODEL - SSM_CHANNELS) // MLA_V_DIM
MLA_NOPE_DIM = 128
MLA_ROPE_DIM = 64
MLA_Q_RANK = D_MODEL // 4
MLA_KV_RANK = D_MODEL // 8
MLA_ROPE_THETA = 10000.0
D_FF = -(-8 * D_MODEL // (3 * 256)) * 256
EVEN_IN = 3 * DIFF_WIDTH + 2 * CONV_CHANNELS
ODD_IN = SSM_CHANNELS + MLA_Q_RANK + MLA_KV_RANK + MLA_ROPE_DIM

kernel_name = 'chunk_causal_hybrid_diffattn_conformer_s5_mla'


def rmsnorm(x, g):
    x32 = x.astype(jnp.float32)
    y = x32 * lax.rsqrt(jnp.mean(x32 * x32, axis=-1, keepdims=True) + RMS_EPS)
    return (y * g.astype(jnp.float32)).astype(x.dtype)


def layernorm(x, g, b):
    x32 = x.astype(jnp.float32)
    mu = jnp.mean(x32, axis=-1, keepdims=True)
    xc = x32 - mu
    y = xc * lax.rsqrt(jnp.mean(xc * xc, axis=-1, keepdims=True) + LN_EPS)
    return (y * g.astype(jnp.float32) + b.astype(jnp.float32)).astype(x.dtype)


def rope_tables(s, rot_dim, theta):
    inv = theta ** (-jnp.arange(0, rot_dim, 2, dtype=jnp.float32) / rot_dim)
    ang = jnp.arange(s, dtype=jnp.float32)[:, None] * inv[None, :]
    return jnp.cos(ang), jnp.sin(ang)


def apply_rope(x, cos, sin):
    half = cos.shape[-1]
    shape = (1, cos.shape[0]) + (1,) * (x.ndim - 3) + (half,)
    c = cos.reshape(shape).astype(x.dtype)
    s = sin.reshape(shape).astype(x.dtype)
    x1 = x[..., :half]
    x2 = x[..., half:2 * half]
    return jnp.concatenate([x1 * c - x2 * s, x2 * c + x1 * s, x[..., 2 * half:]], axis=-1)


def chunk_causal_mask(i, s):
    qpos = i * Q_BLOCK + jnp.arange(Q_BLOCK)
    kpos = jnp.arange(s)
    return (kpos // CHUNK)[None, :] <= (qpos // CHUNK)[:, None]


def sweep_query_blocks(block_fn, q):
    b, s = q.shape[:2]
    nb = s // Q_BLOCK
    qb = jnp.moveaxis(q.reshape((b, nb, Q_BLOCK) + q.shape[2:]), 1, 0)
    out = lax.map(lambda a: block_fn(a[0], a[1]), (jnp.arange(nb), qb))
    out = jnp.moveaxis(out, 0, 1)
    return out.reshape((b, s) + out.shape[3:])


def sandwich(x, g_pre, g_post, fn):
    return x + rmsnorm(fn(rmsnorm(x, g_pre)), g_post)


def swiglu(h, w_gate, w_up, w_down):
    return (jax.nn.silu(h @ w_gate) * (h @ w_up)) @ w_down


def diff_attention(qkv, layer_idx, lq1, lk1, lq2, lk2, subln, cos, sin):
    b, s, _ = qkv.shape
    f32 = jnp.float32
    q = qkv[..., :DIFF_WIDTH].reshape(b, s, DIFF_HEADS, 2, DIFF_HEAD_DIM)
    k = qkv[..., DIFF_WIDTH:2 * DIFF_WIDTH].reshape(b, s, DIFF_HEADS, 2, DIFF_HEAD_DIM)
    v = qkv[..., 2 * DIFF_WIDTH:].reshape(b, s, DIFF_HEADS, 2 * DIFF_HEAD_DIM)
    q = apply_rope(q, cos, sin)
    k = apply_rope(k, cos, sin)
    lam_init = 0.8 - 0.6 * math.exp(-0.3 * layer_idx)
    lam = (jnp.exp(jnp.sum(lq1.astype(f32) * lk1.astype(f32)))
           - jnp.exp(jnp.sum(lq2.astype(f32) * lk2.astype(f32))) + lam_init)
    scale = DIFF_HEAD_DIM ** -0.5

    def block(i, qb):
        sc = jnp.einsum('bqhcd,bkhcd->bhcqk', qb, k, preferred_element_type=f32) * scale
        p = jax.nn.softmax(jnp.where(chunk_causal_mask(i, s), sc, -jnp.inf), axis=-1)
        w = p[:, :, 0] - lam * p[:, :, 1]
        return jnp.einsum('bhqk,bkhe->bqhe', w.astype(v.dtype), v)

    o = sweep_query_blocks(block, q)
    o = rmsnorm(o, subln) * (1.0 - lam_init)
    return o.reshape(b, s, DIFF_WIDTH)


def conformer_conv(g, dw_w, dw_b, ln_g, ln_b):
    u = g[..., :CONV_CHANNELS] * jax.nn.sigmoid(g[..., CONV_CHANNELS:])
    y = lax.conv_general_dilated(
        u, dw_w[:, None, :].astype(u.dtype), window_strides=(1,),
        padding=[(CONV_WIDTH - 1, 0)], dimension_numbers=('NWC', 'WIO', 'NWC'),
        feature_group_count=CONV_CHANNELS) + dw_b
    return jax.nn.silu(layernorm(y, ln_g, ln_b))


def s5_ssm(u, a_re, a_im, log_dt, b_re, b_im, c_re, c_im, d_skip, w_glu, b_glu):
    b, s, _ = u.shape
    f32 = jnp.float32
    ug = u.reshape(b, s, SSM_GROUPS, SSM_GROUP).astype(f32)
    lr = a_re.astype(f32)
    li = a_im.astype(f32)
    dt = jnp.exp(log_dt.astype(f32))[:, None]
    mag = jnp.exp(lr * dt)
    ab_re = mag * jnp.cos(li * dt)
    ab_im = mag * jnp.sin(li * dt)
    den = lr * lr + li * li
    n_re = ab_re - 1.0
    f_re = (n_re * lr + ab_im * li) / den
    f_im = (ab_im * lr - n_re * li) / den
    br = b_re.astype(f32)
    bi = b_im.astype(f32)
    bb_re = f_re[:, :, None] * br - f_im[:, :, None] * bi
    bb_im = f_re[:, :, None] * bi + f_im[:, :, None] * br
    bu_re = jnp.einsum('bsgh,gph->bsgp', ug, bb_re)
    bu_im = jnp.einsum('bsgh,gph->bsgp', ug, bb_im)
    aa_re = jnp.broadcast_to(ab_re, bu_re.shape)
    aa_im = jnp.broadcast_to(ab_im, bu_im.shape)

    def combine(e1, e2):
        ar1, ai1, xr1, xi1 = e1
        ar2, ai2, xr2, xi2 = e2
        return (ar2 * ar1 - ai2 * ai1, ar2 * ai1 + ai2 * ar1,
                ar2 * xr1 - ai2 * xi1 + xr2, ar2 * xi1 + ai2 * xr1 + xi2)

    _, _, x_re, x_im = lax.associative_scan(combine, (aa_re, aa_im, bu_re, bu_im), axis=1)
    y = (jnp.einsum('bsgp,ghp->bsgh', x_re, c_re.astype(f32))
         - jnp.einsum('bsgp,ghp->bsgh', x_im, c_im.astype(f32))
         + d_skip.astype(f32)[None, None] * ug)
    y = y.reshape(b, s, SSM_CHANNELS).astype(u.dtype)
    z = jax.nn.gelu(y)
    return z * jax.nn.sigmoid(z @ w_glu + b_glu)


def mla_attention(c_q, c_kv, k_r, q_norm, w_uq, kv_norm, w_ukv, cos, sin):
    b, s, _ = c_q.shape
    f32 = jnp.float32
    q = (rmsnorm(c_q, q_norm) @ w_uq).reshape(b, s, MLA_HEADS, MLA_NOPE_DIM + MLA_ROPE_DIM)
    q = jnp.concatenate([q[..., :MLA_NOPE_DIM], apply_rope(q[..., MLA_NOPE_DIM:], cos, sin)], axis=-1)
    kv = (rmsnorm(c_kv, kv_norm) @ w_ukv).reshape(b, s, MLA_HEADS, MLA_NOPE_DIM + MLA_V_DIM)
    k_rope = jnp.broadcast_to(apply_rope(k_r, cos, sin)[:, :, None, :], (b, s, MLA_HEADS, MLA_ROPE_DIM))
    k = jnp.concatenate([kv[..., :MLA_NOPE_DIM], k_rope], axis=-1)
    v = kv[..., MLA_NOPE_DIM:]
    scale = (MLA_NOPE_DIM + MLA_ROPE_DIM) ** -0.5

    def block(i, qb):
        sc = jnp.einsum('bqhd,bkhd->bhqk', qb, k, preferred_element_type=f32) * scale
        p = jax.nn.softmax(jnp.where(chunk_causal_mask(i, s), sc, -jnp.inf), axis=-1)
        return jnp.einsum('bhqk,bkhd->bqhd', p.astype(v.dtype), v)

    o = sweep_query_blocks(block, q)
    return o.reshape(b, s, MLA_HEADS * MLA_V_DIM)


def even_mixer(t, layer_idx, w_in, lq1, lk1, lq2, lk2, subln, dw_w, dw_b, ln_g, ln_b, w_out, cos, sin):
    proj = t @ w_in
    y_a = diff_attention(proj[..., :3 * DIFF_WIDTH], layer_idx, lq1, lk1, lq2, lk2, subln, cos, sin)
    y_b = conformer_conv(proj[..., 3 * DIFF_WIDTH:], dw_w, dw_b, ln_g, ln_b)
    return jnp.concatenate([y_a, y_b], axis=-1) @ w_out


def odd_mixer(t, w_in, a_re, a_im, log_dt, b_re, b_im, c_re, c_im, d_skip, w_glu, b_glu,
              q_norm, w_uq, kv_norm, w_ukv, w_out, cos, sin):
    proj = t @ w_in
    o1 = SSM_CHANNELS
    o2 = o1 + MLA_Q_RANK
    o3 = o2 + MLA_KV_RANK
    y_c = s5_ssm(proj[..., :o1], a_re, a_im, log_dt, b_re, b_im, c_re, c_im, d_skip, w_glu, b_glu)
    y_d = mla_attention(proj[..., o1:o2], proj[..., o2:o3], proj[..., o3:], q_norm, w_uq, kv_norm, w_ukv, cos, sin)
    return jnp.concatenate([y_c, y_d], axis=-1) @ w_out


def setup_inputs(seed: int = 0) -> dict:
    key = jax.random.key(seed)
    ks = iter(jax.random.split(key, 64))
    f32 = jnp.float32

    def nrm(shape, scale):
        return jax.random.normal(next(ks), shape, f32) * scale

    def gain(n):
        return 1.0 + 0.02 * jax.random.normal(next(ks), (n,), f32)

    d = D_MODEL
    inp = {}
    inp['x'] = nrm((BATCH, SEQ, d), 1.0)
    inp['l0_mix_pre'] = gain(d)
    inp['l0_mix_post'] = gain(d)
    inp['l0_w_in'] = nrm((d, EVEN_IN), d ** -0.5)
    inp['l0_lambda_q1'] = nrm((DIFF_HEAD_DIM,), 0.1)
    inp['l0_lambda_k1'] = nrm((DIFF_HEAD_DIM,), 0.1)
    inp['l0_lambda_q2'] = nrm((DIFF_HEAD_DIM,), 0.1)
    inp['l0_lambda_k2'] = nrm((DIFF_HEAD_DIM,), 0.1)
    inp['l0_subln'] = gain(2 * DIFF_HEAD_DIM)
    inp['l0_dw_w'] = nrm((CONV_WIDTH, CONV_CHANNELS), CONV_WIDTH ** -0.5)
    inp['l0_dw_b'] = nrm((CONV_CHANNELS,), 0.02)
    inp['l0_conv_ln_g'] = gain(CONV_CHANNELS)
    inp['l0_conv_ln_b'] = nrm((CONV_CHANNELS,), 0.02)
    inp['l0_w_out'] = nrm((DIFF_WIDTH + CONV_CHANNELS, d), (DIFF_WIDTH + CONV_CHANNELS) ** -0.5)
    inp['l0_ffn_pre'] = gain(d)
    inp['l0_ffn_post'] = gain(d)
    inp['l0_w_gate'] = nrm((d, D_FF), d ** -0.5)
    inp['l0_w_up'] = nrm((d, D_FF), d ** -0.5)
    inp['l0_w_down'] = nrm((D_FF, d), D_FF ** -0.5)
    inp['l1_mix_pre'] = gain(d)
    inp['l1_mix_post'] = gain(d)
    inp['l1_w_in'] = nrm((d, ODD_IN), d ** -0.5)
    inp['l1_a_re'] = -0.5 + nrm((SSM_GROUPS, SSM_STATE), 0.01)
    inp['l1_a_im'] = (jnp.pi * jnp.arange(SSM_STATE, dtype=f32))[None, :] + nrm((SSM_GROUPS, SSM_STATE), 0.01)
    inp['l1_log_dt'] = jax.random.uniform(next(ks), (SSM_GROUPS,), f32, math.log(DT_MIN), math.log(DT_MAX))
    inp['l1_b_re'] = nrm((SSM_GROUPS, SSM_STATE, SSM_GROUP), (2 * SSM_GROUP) ** -0.5)
    inp['l1_b_im'] = nrm((SSM_GROUPS, SSM_STATE, SSM_GROUP), (2 * SSM_GROUP) ** -0.5)
    inp['l1_c_re'] = nrm((SSM_GROUPS, SSM_GROUP, SSM_STATE), (2 * SSM_STATE) ** -0.5)
    inp['l1_c_im'] = nrm((SSM_GROUPS, SSM_GROUP, SSM_STATE), (2 * SSM_STATE) ** -0.5)
    inp['l1_d_skip'] = nrm((SSM_GROUPS, SSM_GROUP), 1.0)
    inp['l1_w_glu'] = nrm((SSM_CHANNELS, SSM_CHANNELS), SSM_CHANNELS ** -0.5)
    inp['l1_b_glu'] = nrm((SSM_CHANNELS,), 0.02)
    inp['l1_q_norm'] = gain(MLA_Q_RANK)
    inp['l1_w_uq'] = nrm((MLA_Q_RANK, MLA_HEADS * (MLA_NOPE_DIM + MLA_ROPE_DIM)), MLA_Q_RANK ** -0.5)
    inp['l1_kv_norm'] = gain(MLA_KV_RANK)
    inp['l1_w_ukv'] = nrm((MLA_KV_RANK, MLA_HEADS * (MLA_NOPE_DIM + MLA_V_DIM)), MLA_KV_RANK ** -0.5)
    inp['l1_w_out'] = nrm((SSM_CHANNELS + MLA_HEADS * MLA_V_DIM, d), (SSM_CHANNELS + MLA_HEADS * MLA_V_DIM) ** -0.5)
    inp['l1_ffn_pre'] = gain(d)
    inp['l1_ffn_post'] = gain(d)
    inp['l1_w_gate'] = nrm((d, D_FF), d ** -0.5)
    inp['l1_w_up'] = nrm((d, D_FF), d ** -0.5)
    inp['l1_w_down'] = nrm((D_FF, d), D_FF ** -0.5)
    return inp


def reference(x,
              l0_mix_pre, l0_mix_post, l0_w_in, l0_lambda_q1, l0_lambda_k1, l0_lambda_q2, l0_lambda_k2,
              l0_subln, l0_dw_w, l0_dw_b, l0_conv_ln_g, l0_conv_ln_b, l0_w_out,
              l0_ffn_pre, l0_ffn_post, l0_w_gate, l0_w_up, l0_w_down,
              l1_mix_pre, l1_mix_post, l1_w_in, l1_a_re, l1_a_im, l1_log_dt, l1_b_re, l1_b_im,
              l1_c_re, l1_c_im, l1_d_skip, l1_w_glu, l1_b_glu, l1_q_norm, l1_w_uq, l1_kv_norm, l1_w_ukv,
              l1_w_out, l1_ffn_pre, l1_ffn_post, l1_w_gate, l1_w_up, l1_w_down):
    s = x.shape[1]
    cos_a, sin_a = rope_tables(s, DIFF_HEAD_DIM // ROPE_FRACTION, ROPE_THETA)
    cos_d, sin_d = rope_tables(s, MLA_ROPE_DIM, MLA_ROPE_THETA)

    def layer0(h):
        h = sandwich(h, l0_mix_pre, l0_mix_post, lambda t: even_mixer(
            t, 0, l0_w_in, l0_lambda_q1, l0_lambda_k1, l0_lambda_q2, l0_lambda_k2, l0_subln,
            l0_dw_w, l0_dw_b, l0_conv_ln_g, l0_conv_ln_b, l0_w_out, cos_a, sin_a))
        return sandwich(h, l0_ffn_pre, l0_ffn_post, lambda t: swiglu(t, l0_w_gate, l0_w_up, l0_w_down))

    def layer1(h):
        h = sandwich(h, l1_mix_pre, l1_mix_post, lambda t: odd_mixer(
            t, l1_w_in, l1_a_re, l1_a_im, l1_log_dt, l1_b_re, l1_b_im, l1_c_re, l1_c_im, l1_d_skip,
            l1_w_glu, l1_b_glu, l1_q_norm, l1_w_uq, l1_kv_norm, l1_w_ukv, l1_w_out, cos_d, sin_d))
        return sandwich(h, l1_ffn_pre, l1_ffn_post, lambda t: swiglu(t, l1_w_gate, l1_w_up, l1_w_down))

    layers = (layer0, layer1)
    h = x
    for i in range(DEPTH):
        h = layers[i](h)
    return h
```

```python
import functools
import math

import jax
import jax.numpy as jnp
from jax import lax
from jax.experimental import pallas as pl
from jax.experimental.pallas import tpu as pltpu

F32 = jnp.float32
BF16 = jnp.bfloat16

LANES = 128
SUBLANES = 8
VMEM_LIMIT_BYTES = 56 * 1024 * 1024

CHUNK = 64
RMS_EPS = 1e-6
LN_EPS = 1e-5
ROPE_THETA = 500000.0
MLA_ROPE_THETA = 10000.0
DIFF_HEAD_DIM = 64
DIFF_ROT = 16
CONV_WIDTH = 31
CONV_HALO = 32
SSM_GROUP = 16
SSM_STATE = 64
MLA_NOPE = 128
MLA_ROPE = 64
MLA_V = 128
MLA_QK_PAD = 256

ROW_TILE = 512
ATTN_TILE = 256
CONV_TILE = 256
SSM_TILE = 128
FFN_CHUNK = 512


def _params(*sem):
    return pltpu.CompilerParams(dimension_semantics=sem, vmem_limit_bytes=VMEM_LIMIT_BYTES)


def _rms(x, g):
    return x * lax.rsqrt(jnp.mean(x * x, axis=-1, keepdims=True) + RMS_EPS) * g


def _sigmoid(x):
    return 1.0 / (1.0 + jnp.exp(-x))


def _dot(a, b):
    return jnp.dot(a, b, preferred_element_type=F32)


def _dot_nt(a, b):
    return lax.dot_general(a, b, (((1,), (1,)), ((), ())), preferred_element_type=F32)


def _rope_block(x, c, sa, sb, shift):
    return x * c + pltpu.roll(x, LANES - shift, 1) * sa + pltpu.roll(x, shift, 1) * sb


def _const_spec(shape):
    nd = len(shape)
    return pl.BlockSpec(shape, lambda *_: (0,) * nd, pipeline_mode=pl.Buffered(1))


def _l0_in_kernel(x_ref, g_ref, w_ref, c_ref, sa_ref, sb_ref, o_ref, *, n_rope_blocks, n_q_blocks, scale):
    t = _rms(x_ref[...], g_ref[...]).astype(BF16)
    p = _dot(t, w_ref[...])
    c, sa, sb = c_ref[...], sa_ref[...], sb_ref[...]
    for j in range(n_rope_blocks):
        blk = _rope_block(p[:, j * LANES:(j + 1) * LANES], c, sa, sb, DIFF_ROT // 2)
        if j < n_q_blocks:
            blk = blk * scale
        o_ref[:, j * LANES:(j + 1) * LANES] = blk.astype(o_ref.dtype)
    rest = n_rope_blocks * LANES
    o_ref[:, rest:] = p[:, rest:].astype(o_ref.dtype)


def _l0_in(x2, g_pre, w_in, tabs, seq):
    n, d = x2.shape
    n_out = w_in.shape[1]
    tiles_per_seq = seq // ROW_TILE
    kern = functools.partial(_l0_in_kernel, n_rope_blocks=8, n_q_blocks=4, scale=DIFF_HEAD_DIM ** -0.5)
    tab_spec = pl.BlockSpec((ROW_TILE, LANES), lambda i: (i % tiles_per_seq, 0))
    return pl.pallas_call(
        kern,
        grid=(n // ROW_TILE,),
        in_specs=[pl.BlockSpec((ROW_TILE, d), lambda i: (i, 0)),
                  _const_spec((1, d)), _const_spec((d, n_out)),
                  tab_spec, tab_spec, tab_spec],
        out_specs=pl.BlockSpec((ROW_TILE, n_out), lambda i: (i, 0)),
        out_shape=jax.ShapeDtypeStruct((n, n_out), BF16),
        compiler_params=_params("parallel"),
        name="l0_in",
    )(x2, g_pre, w_in, *tabs)


def _attn_body(q_ref, k_ref, v_ref, m_ref, l_ref, acc_ref, *, n_maps, map_width):
    i = pl.program_id(2)
    tq = q_ref.shape[1]
    q = q_ref[0]
    if n_maps == 1:
        qs = [q]
    else:
        lane = lax.broadcasted_iota(jnp.int32, q.shape, 1)
        qs = [jnp.where((lane >= c * map_width) & (lane < (c + 1) * map_width), q, jnp.zeros_like(q))
              for c in range(n_maps)]
    m_ref[...] = jnp.full(m_ref.shape, -jnp.inf, F32)
    l_ref[...] = jnp.zeros(l_ref.shape, F32)
    acc_ref[...] = jnp.zeros(acc_ref.shape, F32)

    def step(kb, masked):
        start = pl.multiple_of(kb * tq, tq)
        k = k_ref[0, pl.ds(start, tq), :]
        v = v_ref[0, pl.ds(start, tq), :]
        if masked:
            qc = lax.broadcasted_iota(jnp.int32, (tq, tq), 0) // CHUNK
            kc = lax.broadcasted_iota(jnp.int32, (tq, tq), 1) // CHUNK
            keep = kc <= qc
        for c in range(n_maps):
            s = _dot_nt(qs[c], k)
            if masked:
                s = jnp.where(keep, s, -jnp.inf)
            m_old = m_ref[c]
            m_new = jnp.maximum(m_old, jnp.max(s, axis=-1, keepdims=True))
            alpha = jnp.exp(m_old - m_new)
            p = jnp.exp(s - m_new)
            l_ref[c] = alpha * l_ref[c] + jnp.sum(p, axis=-1, keepdims=True)
            acc_ref[c] = alpha * acc_ref[c] + _dot(p.astype(v.dtype), v)
            m_ref[c] = m_new

    def loop_body(kb, carry):
        step(kb, False)
        return carry

    lax.fori_loop(0, i, loop_body, 0)
    step(i, True)


def _diff_attn_kernel(q_ref, k_ref, v_ref, lam_ref, sub_ref, o_ref, m_ref, l_ref, acc_ref, *, lam_init):
    _attn_body(q_ref, k_ref, v_ref, m_ref, l_ref, acc_ref, n_maps=2, map_width=DIFF_HEAD_DIM)
    lv = lam_ref[...]
    lam = (jnp.exp(jnp.sum(lv[0:1] * lv[1:2], axis=-1, keepdims=True))
           - jnp.exp(jnp.sum(lv[2:3] * lv[3:4], axis=-1, keepdims=True)) + lam_init)
    o = acc_ref[0] / l_ref[0] - lam * (acc_ref[1] / l_ref[1])
    o_ref[0] = (_rms(o, sub_ref[...]) * (1.0 - lam_init)).astype(o_ref.dtype)


def _diff_attn(p0, lam_vecs, subln, n_heads):
    b, s, _ = p0.shape
    kern = functools.partial(_diff_attn_kernel, lam_init=0.8 - 0.6 * math.exp(-0.3 * 0))
    return pl.pallas_call(
        kern,
        grid=(b, n_heads, s // ATTN_TILE),
        in_specs=[pl.BlockSpec((1, ATTN_TILE, LANES), lambda bb, h, i: (bb, i, h)),
                  pl.BlockSpec((1, s, LANES), lambda bb, h, i: (bb, 0, n_heads + h)),
                  pl.BlockSpec((1, s, LANES), lambda bb, h, i: (bb, 0, 2 * n_heads + h)),
                  pl.BlockSpec(lam_vecs.shape, lambda bb, h, i: (0, 0)),
                  pl.BlockSpec(subln.shape, lambda bb, h, i: (0, 0))],
        out_specs=pl.BlockSpec((1, ATTN_TILE, LANES), lambda bb, h, i: (bb, i, h)),
        out_shape=jax.ShapeDtypeStruct((b, s, n_heads * LANES), BF16),
        scratch_shapes=[pltpu.VMEM((2, ATTN_TILE, 1), F32), pltpu.VMEM((2, ATTN_TILE, 1), F32),
                        pltpu.VMEM((2, ATTN_TILE, LANES), F32)],
        compiler_params=_params("parallel", "parallel", "arbitrary"),
        name="diff_attn",
    )(p0, p0, p0, lam_vecs, subln)


def _mla_attn_kernel(q_ref, k_ref, v_ref, o_ref, m_ref, l_ref, acc_ref):
    _attn_body(q_ref, k_ref, v_ref, m_ref, l_ref, acc_ref, n_maps=1, map_width=MLA_QK_PAD)
    o_ref[0] = (acc_ref[0] / l_ref[0]).astype(o_ref.dtype)


def _mla_attn(q_cat, k_cat, v, n_heads):
    b, s, _ = q_cat.shape
    return pl.pallas_call(
        _mla_attn_kernel,
        grid=(b, n_heads, s // ATTN_TILE),
        in_specs=[pl.BlockSpec((1, ATTN_TILE, MLA_QK_PAD), lambda bb, h, i: (bb, i, h)),
                  pl.BlockSpec((1, s, MLA_QK_PAD), lambda bb, h, i: (bb, 0, h)),
                  pl.BlockSpec((1, s, MLA_V), lambda bb, h, i: (bb, 0, h))],
        out_specs=pl.BlockSpec((1, ATTN_TILE, MLA_V), lambda bb, h, i: (bb, i, h)),
        out_shape=jax.ShapeDtypeStruct((b, s, n_heads * MLA_V), BF16),
        scratch_shapes=[pltpu.VMEM((1, ATTN_TILE, 1), F32), pltpu.VMEM((1, ATTN_TILE, 1), F32),
                        pltpu.VMEM((1, ATTN_TILE, MLA_V), F32)],
        compiler_params=_params("parallel", "parallel", "arbitrary"),
        name="mla_attn",
    )(q_cat, k_cat, v)


def _conv_kernel(a_ref, gate_ref, ah_ref, gh_ref, w_ref, b_ref, lg_ref, lb_ref, o_ref, u_ref):
    tt = a_ref.shape[1]
    u_ref[CONV_HALO:, :] = a_ref[0].astype(F32) * _sigmoid(gate_ref[0].astype(F32))
    halo = ah_ref[0].astype(F32) * _sigmoid(gh_ref[0].astype(F32))
    u_ref[:CONV_HALO, :] = jnp.where(pl.program_id(1) > 0, halo, jnp.zeros_like(halo))
    acc = jnp.zeros((tt, a_ref.shape[2]), F32)
    first = CONV_HALO - (CONV_WIDTH - 1)
    for k in range(CONV_WIDTH):
        acc = acc + u_ref[first + k:first + k + tt, :] * w_ref[k:k + 1, :]
    y = acc + b_ref[...]
    mu = jnp.mean(y, axis=-1, keepdims=True)
    yc = y - mu
    yn = yc * lax.rsqrt(jnp.mean(yc * yc, axis=-1, keepdims=True) + LN_EPS) * lg_ref[...] + lb_ref[...]
    o_ref[0] = (yn * _sigmoid(yn)).astype(o_ref.dtype)


def _conv_module(p0, dw_w, dw_b, ln_g, ln_b, col0):
    b, s, _ = p0.shape
    c = dw_w.shape[1]
    a_blk, g_blk = col0 // c, col0 // c + 1
    ratio = CONV_TILE // CONV_HALO
    main = lambda blk: pl.BlockSpec((1, CONV_TILE, c), lambda bb, t: (bb, t, blk))
    halo = lambda blk: pl.BlockSpec((1, CONV_HALO, c), lambda bb, t: (bb, jnp.maximum(t * ratio - 1, 0), blk))
    vec = lambda a: pl.BlockSpec(a.shape, lambda bb, t: (0, 0))
    return pl.pallas_call(
        _conv_kernel,
        grid=(b, s // CONV_TILE),
        in_specs=[main(a_blk), main(g_blk), halo(a_blk), halo(g_blk),
                  vec(dw_w), vec(dw_b), vec(ln_g), vec(ln_b)],
        out_specs=pl.BlockSpec((1, CONV_TILE, c), lambda bb, t: (bb, t, 0)),
        out_shape=jax.ShapeDtypeStruct((b, s, c), BF16),
        scratch_shapes=[pltpu.VMEM((CONV_HALO + CONV_TILE, c), F32)],
        compiler_params=_params("parallel", "parallel"),
        name="conv_module",
    )(p0, p0, p0, p0, dw_w, dw_b, ln_g, ln_b)


def _post_kernel(*refs, n_mix, with_next):
    x_ref = refs[0]
    mix_refs = refs[1:1 + n_mix]
    (wo_ref, gpost_ref, gfpre_ref, wg_ref, wu_ref, wd_ref, gfpost_ref) = refs[1 + n_mix:8 + n_mix]
    pos = 8 + n_mix
    if with_next:
        gnext_ref, wnext_ref = refs[pos:pos + 2]
        pos += 2
    h_out_ref = refs[pos]
    pos += 1
    if with_next:
        p_out_ref = refs[pos]
        pos += 1
    hid_ref = refs[pos]

    y = None
    row = 0
    for r in mix_refs:
        w = r.shape[-1]
        part = _dot(r[...], wo_ref[row:row + w, :])
        y = part if y is None else y + part
        row += w
    h1 = x_ref[...] + _rms(y, gpost_ref[...])
    t = _rms(h1, gfpre_ref[...]).astype(BF16)
    d_ff = wg_ref.shape[1]
    for j in range(0, d_ff, FFN_CHUNK):
        wdt = min(FFN_CHUNK, d_ff - j)
        gate = _dot(t, wg_ref[:, j:j + wdt])
        up = _dot(t, wu_ref[:, j:j + wdt])
        hid_ref[:, j:j + wdt] = (gate * _sigmoid(gate) * up).astype(BF16)
    f = _dot(hid_ref[...], wd_ref[...])
    h2 = h1 + _rms(f, gfpost_ref[...])
    h_out_ref[...] = h2
    if with_next:
        t2 = _rms(h2, gnext_ref[...]).astype(BF16)
        p_out_ref[...] = _dot(t2, wnext_ref[...])


def _post(x2, mix_parts, w_out, g_post, g_fpre, w_gate, w_up, w_down, g_fpost, nxt=None):
    n, d = x2.shape
    d_ff = w_gate.shape[1]
    row = lambda a: pl.BlockSpec((ROW_TILE, a.shape[1]), lambda i: (i, 0))
    consts = [w_out, g_post, g_fpre, w_gate, w_up, w_down, g_fpost] + (list(nxt) if nxt else [])
    out_shape = [jax.ShapeDtypeStruct((n, d), F32)]
    out_specs = [pl.BlockSpec((ROW_TILE, d), lambda i: (i, 0))]
    if nxt:
        n_next = nxt[1].shape[1]
        out_shape.append(jax.ShapeDtypeStruct((n, n_next), F32))
        out_specs.append(pl.BlockSpec((ROW_TILE, n_next), lambda i: (i, 0)))
    kern = functools.partial(_post_kernel, n_mix=len(mix_parts), with_next=bool(nxt))
    return pl.pallas_call(
        kern,
        grid=(n // ROW_TILE,),
        in_specs=[row(x2)] + [row(m) for m in mix_parts] + [_const_spec(c.shape) for c in consts],
        out_specs=out_specs,
        out_shape=out_shape,
        scratch_shapes=[pltpu.VMEM((ROW_TILE, d_ff), BF16)],
        compiler_params=_params("parallel"),
        name="post_next" if nxt else "post",
    )(x2, *mix_parts, *consts)


def _mla_prep_kernel(cq_ref, ckv_ref, kr_ref, qn_ref, kvn_ref, wq_ref, wk_ref, wv_ref,
                     c_ref, sa_ref, sb_ref, q_out, k_out, v_out, *, n_heads, scale):
    c, sa, sb = c_ref[...], sa_ref[...], sb_ref[...]
    half = MLA_ROPE // 2
    q = _dot(_rms(cq_ref[...], qn_ref[...]).astype(BF16), wq_ref[...])
    for h in range(n_heads):
        lo = h * MLA_QK_PAD
        q_out[:, lo:lo + LANES] = (q[:, lo:lo + LANES] * scale).astype(q_out.dtype)
        roped = _rope_block(q[:, lo + LANES:lo + 2 * LANES], c, sa, sb, half)
        q_out[:, lo + LANES:lo + 2 * LANES] = (roped * scale).astype(q_out.dtype)
    ckv = _rms(ckv_ref[...], kvn_ref[...]).astype(BF16)
    kr = _rope_block(kr_ref[...], c, sa, sb, half).astype(BF16)
    k_out[...] = _dot(jnp.concatenate([ckv, kr], axis=-1), wk_ref[...]).astype(k_out.dtype)
    v_out[...] = _dot(ckv, wv_ref[...]).astype(v_out.dtype)


def _mla_prep(p1, q_norm, kv_norm, wq, wk, wv, tabs, seq, n_heads):
    n = p1.shape[0]
    tiles_per_seq = seq // ROW_TILE
    tab_spec = pl.BlockSpec((ROW_TILE, LANES), lambda i: (i % tiles_per_seq, 0))
    kern = functools.partial(_mla_prep_kernel, n_heads=n_heads, scale=(MLA_NOPE + MLA_ROPE) ** -0.5)
    out_w = (n_heads * MLA_QK_PAD, n_heads * MLA_QK_PAD, n_heads * MLA_V)
    return pl.pallas_call(
        kern,
        grid=(n // ROW_TILE,),
        in_specs=[pl.BlockSpec((ROW_TILE, 2 * LANES), lambda i: (i, 1)),
                  pl.BlockSpec((ROW_TILE, LANES), lambda i: (i, 4)),
                  pl.BlockSpec((ROW_TILE, LANES), lambda i: (i, 5)),
                  _const_spec(q_norm.shape), _const_spec(kv_norm.shape),
                  _const_spec(wq.shape), _const_spec(wk.shape), _const_spec(wv.shape),
                  tab_spec, tab_spec, tab_spec],
        out_specs=[pl.BlockSpec((ROW_TILE, w), lambda i: (i, 0)) for w in out_w],
        out_shape=[jax.ShapeDtypeStruct((n, w), BF16) for w in out_w],
        compiler_params=_params("parallel"),
        name="mla_prep",
    )(p1, p1, p1, q_norm, kv_norm, wq, wk, wv, *tabs)


def _gelu_tanh(x):
    return 0.5 * x * (1.0 + jnp.tanh(math.sqrt(2.0 / math.pi) * (x + 0.044715 * (x * x * x))))


def _ssm_kernel(u_ref, lr_ref, li_ref, ldt_ref, bre_ref, bim_ref, cre_ref, cim_ref, d_ref, wg_ref, bg_ref,
                o_ref, bmat_ref, cmat_ref, a_ref, st_ref, bu_ref, x_ref, y_ref):
    nb, tt, _ = u_ref.shape
    n_state = lr_ref.shape[1]

    @pl.when(pl.program_id(0) == 0)
    def _init():
        lr, li = lr_ref[...], li_ref[...]
        dt = jnp.exp(ldt_ref[...])
        mag = jnp.exp(lr * dt)
        ab_re = mag * jnp.cos(li * dt)
        ab_im = mag * jnp.sin(li * dt)
        den = lr * lr + li * li
        n_re = ab_re - 1.0
        f_re = (n_re * lr + ab_im * li) / den
        f_im = (ab_im * lr - n_re * li) / den
        br, bi = bre_ref[...], bim_ref[...]
        bmat_ref[:, :n_state] = (f_re * br - f_im * bi).astype(BF16)
        bmat_ref[:, n_state:] = (f_re * bi + f_im * br).astype(BF16)
        cmat_ref[:n_state, :] = cre_ref[...].astype(BF16)
        cmat_ref[n_state:, :] = (-cim_ref[...]).astype(BF16)
        a_ref[0:1, :] = ab_re
        a_ref[1:2, :] = ab_im
        st_ref[...] = jnp.zeros(st_ref.shape, F32)

    n_blk = 2 * n_state // LANES
    for b in range(nb):
        bu = _dot(u_ref[b].astype(BF16), bmat_ref[...])
        for j in range(n_blk):
            bu_ref[j, b * tt:(b + 1) * tt, :] = bu[:, j * LANES:(j + 1) * LANES]

    a_re = jnp.broadcast_to(a_ref[0:1, :], (nb, n_state))
    a_im = jnp.broadcast_to(a_ref[1:2, :], (nb, n_state))

    def scan_step(t, carry):
        x_re, x_im = carry
        rows = jnp.concatenate([bu_ref[j, pl.ds(t, nb, stride=tt), :] for j in range(n_blk)], axis=-1)
        n_re = a_re * x_re - a_im * x_im + rows[:, :n_state]
        n_im = a_re * x_im + a_im * x_re + rows[:, n_state:]
        dst = pl.multiple_of(t * nb, nb)
        x_ref[pl.ds(dst, nb), :n_state] = n_re
        x_ref[pl.ds(dst, nb), n_state:] = n_im
        return n_re, n_im

    x_re, x_im = lax.fori_loop(0, tt, scan_step, (st_ref[:, :n_state], st_ref[:, n_state:]))
    st_ref[:, :n_state] = x_re
    st_ref[:, n_state:] = x_im

    y_tm = _dot(x_ref[...].astype(BF16), cmat_ref[...])
    n_yblk = y_tm.shape[1] // LANES
    for j in range(n_yblk):
        y_ref[j] = y_tm[:, j * LANES:(j + 1) * LANES]
    for b in range(nb):
        y = jnp.concatenate([y_ref[j, pl.ds(b, tt, stride=nb), :] for j in range(n_yblk)], axis=-1)
        y = y + d_ref[...] * u_ref[b]
        z = _gelu_tanh(y)
        gate = _dot(z.astype(BF16), wg_ref[...]) + bg_ref[...]
        o_ref[b] = (z * _sigmoid(gate)).astype(o_ref.dtype)


def _ssm(p1, rows, b_bd, c_bd, d_row, w_glu, b_glu):
    b, s, _ = p1.shape
    ch = w_glu.shape[0]
    n_state = rows[0].shape[1]
    consts = list(rows) + list(b_bd) + list(c_bd) + [d_row, w_glu, b_glu]
    return pl.pallas_call(
        _ssm_kernel,
        grid=(s // SSM_TILE,),
        in_specs=[pl.BlockSpec((b, SSM_TILE, ch), lambda t: (0, t, 0))] + [_const_spec(c.shape) for c in consts],
        out_specs=pl.BlockSpec((b, SSM_TILE, ch), lambda t: (0, t, 0)),
        out_shape=jax.ShapeDtypeStruct((b, s, ch), BF16),
        scratch_shapes=[pltpu.VMEM((ch, 2 * n_state), BF16),
                        pltpu.VMEM((2 * n_state, ch), BF16),
                        pltpu.VMEM((SUBLANES, n_state), F32),
                        pltpu.VMEM((b, 2 * n_state), F32),
                        pltpu.VMEM((2 * n_state // LANES, b * SSM_TILE, LANES), F32),
                        pltpu.VMEM((SSM_TILE * b, 2 * n_state), F32),
                        pltpu.VMEM((ch // LANES, SSM_TILE * b, LANES), F32)],
        compiler_params=_params("arbitrary"),
        name="s5_ssm",
    )(p1, *consts)


def _rope_tables(s, rot_dim, theta):
    inv = theta ** (-jnp.arange(0, rot_dim, 2, dtype=F32) / rot_dim)
    ang = jnp.arange(s, dtype=F32)[:, None] * inv[None, :]
    return jnp.cos(ang), jnp.sin(ang)


def _lane_tables(cos, sin, period):
    s, half = cos.shape
    reps = LANES // period
    one = jnp.ones((s, period - 2 * half), F32)
    zero = jnp.zeros((s, period - 2 * half), F32)
    zh = jnp.zeros((s, half), F32)
    c = jnp.tile(jnp.concatenate([cos, cos, one], axis=1), (1, reps))
    sa = jnp.tile(jnp.concatenate([-sin, zh, zero], axis=1), (1, reps))
    sb = jnp.tile(jnp.concatenate([zh, sin, zero], axis=1), (1, reps))
    return c, sa, sb


def _block_diag(blocks):
    g, r, c = blocks.shape
    eye = jnp.eye(g, dtype=blocks.dtype)
    return (eye[:, None, :, None] * blocks[:, :, None, :]).reshape(g * r, g * c)


def kernel(x, l0_mix_pre, l0_mix_post, l0_w_in, l0_lambda_q1, l0_lambda_k1, l0_lambda_q2, l0_lambda_k2, l0_subln, l0_dw_w, l0_dw_b, l0_conv_ln_g, l0_conv_ln_b, l0_w_out, l0_ffn_pre, l0_ffn_post, l0_w_gate, l0_w_up, l0_w_down, l1_mix_pre, l1_mix_post, l1_w_in, l1_a_re, l1_a_im, l1_log_dt, l1_b_re, l1_b_im, l1_c_re, l1_c_im, l1_d_skip, l1_w_glu, l1_b_glu, l1_q_norm, l1_w_uq, l1_kv_norm, l1_w_ukv, l1_w_out, l1_ffn_pre, l1_ffn_post, l1_w_gate, l1_w_up, l1_w_down):
    b, s, d = x.shape
    n = b * s
    row = lambda v: v.reshape(1, -1).astype(F32)
    bf = lambda w: w.astype(BF16)

    diff_width = 4 * LANES
    n_diff_heads = diff_width // LANES
    conv_ch = l0_dw_w.shape[1]
    ssm_ch = l1_w_glu.shape[0]
    n_groups, n_state_g = l1_a_re.shape
    q_rank = l1_q_norm.shape[0]
    kv_rank = l1_kv_norm.shape[0]
    n_mla_heads = l1_w_uq.shape[1] // (MLA_NOPE + MLA_ROPE)

    tabs_a = _lane_tables(*_rope_tables(s, DIFF_ROT, ROPE_THETA), period=DIFF_HEAD_DIM)
    x2 = x.reshape(n, d)
    p0 = _l0_in(x2, row(l0_mix_pre), bf(l0_w_in), tabs_a, s).reshape(b, s, -1)
    lam_vecs = jnp.stack([l0_lambda_q1, l0_lambda_k1, l0_lambda_q2, l0_lambda_k2]).astype(F32)
    y_a = _diff_attn(p0, lam_vecs, row(l0_subln), n_diff_heads)
    y_b = _conv_module(p0, l0_dw_w.astype(F32), row(l0_dw_b), row(l0_conv_ln_g), row(l0_conv_ln_b),
                       col0=3 * diff_width)

    pad = (-l1_w_in.shape[1]) % LANES
    w_in1 = bf(jnp.pad(l1_w_in, ((0, 0), (0, pad))))
    h2, p1 = _post(x2, [y_a.reshape(n, -1), y_b.reshape(n, -1)], bf(l0_w_out), row(l0_mix_post),
                   row(l0_ffn_pre), bf(l0_w_gate), bf(l0_w_up), bf(l0_w_down), row(l0_ffn_post),
                   nxt=(row(l1_mix_pre), w_in1))

    state_row = lambda a: a.reshape(1, -1).astype(F32)
    ssm_rows = (state_row(l1_a_re), state_row(l1_a_im),
                state_row(jnp.broadcast_to(l1_log_dt[:, None], (n_groups, n_state_g))))
    b_bd = tuple(_block_diag(jnp.swapaxes(m, 1, 2).astype(F32)) for m in (l1_b_re, l1_b_im))
    c_bd = tuple(_block_diag(jnp.swapaxes(m, 1, 2).astype(F32)) for m in (l1_c_re, l1_c_im))
    y_c = _ssm(p1.reshape(b, s, -1), ssm_rows, b_bd, c_bd, row(l1_d_skip), bf(l1_w_glu), row(l1_b_glu))

    wq = l1_w_uq.reshape(q_rank, n_mla_heads, MLA_NOPE + MLA_ROPE)
    wq = jnp.pad(wq, ((0, 0), (0, 0), (0, MLA_QK_PAD - MLA_NOPE - MLA_ROPE))).reshape(q_rank, -1)
    wkv = l1_w_ukv.reshape(kv_rank, n_mla_heads, MLA_NOPE + MLA_V)
    wk_nope = jnp.pad(wkv[:, :, :MLA_NOPE], ((0, 0), (0, 0), (0, MLA_QK_PAD - MLA_NOPE)))
    route = jnp.pad(jnp.eye(MLA_ROPE, dtype=F32), ((0, LANES - MLA_ROPE), (MLA_NOPE, MLA_QK_PAD - MLA_NOPE - MLA_ROPE)))
    wk_rope = jnp.broadcast_to(route[:, None, :], (LANES, n_mla_heads, MLA_QK_PAD))
    wk = jnp.concatenate([wk_nope, wk_rope], axis=0).reshape(kv_rank + LANES, -1)
    wv = wkv[:, :, MLA_NOPE:].reshape(kv_rank, -1)
    tabs_d = _lane_tables(*_rope_tables(s, MLA_ROPE, MLA_ROPE_THETA), period=LANES)
    q_cat, k_cat, v_d = _mla_prep(p1, row(l1_q_norm), row(l1_kv_norm), bf(wq), bf(wk), bf(wv), tabs_d, s,
                                  n_mla_heads)
    y_d = _mla_attn(q_cat.reshape(b, s, -1), k_cat.reshape(b, s, -1), v_d.reshape(b, s, -1), n_mla_heads)

    (out,) = _post(h2, [y_c.reshape(n, -1), y_d.reshape(n, -1)], bf(l1_w_out), row(l1_mix_post),
                   row(l1_ffn_pre), bf(l1_w_gate), bf(l1_w_up), bf(l1_w_down), row(l1_ffn_post))
    return out.reshape(b, s, d)
```

```python
import functools
import math

import jax
import jax.numpy as jnp
from jax import lax
from jax.experimental import pallas as pl
from jax.experimental.pallas import tpu as pltpu

F32 = jnp.float32
BF16 = jnp.bfloat16

LANES = 128
SUBLANES = 8
VMEM_LIMIT_BYTES = 56 * 1024 * 1024

CHUNK = 64
RMS_EPS = 1e-6
LN_EPS = 1e-5
ROPE_THETA = 500000.0
MLA_ROPE_THETA = 10000.0
DIFF_HEAD_DIM = 64
DIFF_ROT = 16
CONV_WIDTH = 31
CONV_HALO = 32
SSM_GROUP = 16
SSM_STATE = 64
MLA_NOPE = 128
MLA_ROPE = 64
MLA_V = 128
MLA_QK_PAD = 256

ROW_TILE = 512
ATTN_TILE = 256
ATTN_LOOKAHEAD = 3
CONV_TILE = 256
SSM_TILE = 128
FFN_CHUNK = 512


def _params(*sem):
    return pltpu.CompilerParams(dimension_semantics=sem, vmem_limit_bytes=VMEM_LIMIT_BYTES)


def _rms(x, g):
    return x * lax.rsqrt(jnp.mean(x * x, axis=-1, keepdims=True) + RMS_EPS) * g


def _sigmoid(x):
    return 1.0 / (1.0 + jnp.exp(-x))


def _dot(a, b):
    return jnp.dot(a, b, preferred_element_type=F32)


def _dot_nt(a, b):
    return lax.dot_general(a, b, (((1,), (1,)), ((), ())), preferred_element_type=F32)


def _rope_block(x, c, sa, sb, shift):
    return x * c + pltpu.roll(x, LANES - shift, 1) * sa + pltpu.roll(x, shift, 1) * sb


def _const_spec(shape):
    nd = len(shape)
    return pl.BlockSpec(shape, lambda *_: (0,) * nd, pipeline_mode=pl.Buffered(1))


def _store_vt(vt_ref, vt):
    for j in range(ROW_TILE // ATTN_TILE):
        vt_ref[0, j] = vt[:, j * ATTN_TILE:(j + 1) * ATTN_TILE].astype(vt_ref.dtype)


def _vt_out(b, seq, width):
    tiles_per_seq = seq // ROW_TILE
    per_tile = ROW_TILE // ATTN_TILE
    spec = pl.BlockSpec((1, per_tile, width, ATTN_TILE),
                        lambda i: (i // tiles_per_seq, i % tiles_per_seq, 0, 0))
    return spec, jax.ShapeDtypeStruct((b, seq // ATTN_TILE, width, ATTN_TILE), BF16)


def _l0_in_kernel(x_ref, g_ref, w_ref, wvt_ref, c_ref, sa_ref, sb_ref, o_ref, vt_ref, *,
                  n_rope_blocks, n_q_blocks, scale):
    t = _rms(x_ref[...], g_ref[...]).astype(BF16)
    p = _dot(t, w_ref[...])
    c, sa, sb = c_ref[...], sa_ref[...], sb_ref[...]
    for j in range(n_rope_blocks):
        blk = _rope_block(p[:, j * LANES:(j + 1) * LANES], c, sa, sb, DIFF_ROT // 2)
        if j < n_q_blocks:
            blk = blk * scale
        o_ref[:, j * LANES:(j + 1) * LANES] = blk.astype(o_ref.dtype)
    rest = n_rope_blocks * LANES
    o_ref[:, rest:] = p[:, rest:].astype(o_ref.dtype)
    _store_vt(vt_ref, _dot_nt(wvt_ref[...], t))


def _l0_in(x2, g_pre, w_main, w_vt, tabs, b, seq):
    n, d = x2.shape
    n_out = w_main.shape[1]
    tiles_per_seq = seq // ROW_TILE
    kern = functools.partial(_l0_in_kernel, n_rope_blocks=8, n_q_blocks=4, scale=DIFF_HEAD_DIM ** -0.5)
    tab_spec = pl.BlockSpec((ROW_TILE, LANES), lambda i: (i % tiles_per_seq, 0))
    vt_spec, vt_shape = _vt_out(b, seq, w_vt.shape[0])
    return pl.pallas_call(
        kern,
        grid=(n // ROW_TILE,),
        in_specs=[pl.BlockSpec((ROW_TILE, d), lambda i: (i, 0)),
                  _const_spec((1, d)), _const_spec(w_main.shape), _const_spec(w_vt.shape),
                  tab_spec, tab_spec, tab_spec],
        out_specs=[pl.BlockSpec((ROW_TILE, n_out), lambda i: (i, 0)), vt_spec],
        out_shape=[jax.ShapeDtypeStruct((n, n_out), BF16), vt_shape],
        compiler_params=_params("parallel"),
        name="l0_in",
    )(x2, g_pre, w_main, w_vt, *tabs)


def _attn_body(q_ref, k_ref, vt_ref, m_ref, l_ref, acc_ref, *, n_heads, n_maps, dk, dv):
    i = pl.program_id(1)
    tq = q_ref.shape[1]
    map_width = dk // n_maps
    qs = []
    for h in range(n_heads):
        q = q_ref[0, :, h * dk:(h + 1) * dk]
        if n_maps == 1:
            qs.append(q)
        else:
            lane = lax.broadcasted_iota(jnp.int32, q.shape, 1)
            for c in range(n_maps):
                sel = (lane >= c * map_width) & (lane < (c + 1) * map_width)
                qs.append(jnp.where(sel, q, jnp.zeros_like(q)))
    m_ref[...] = jnp.full(m_ref.shape, -jnp.inf, F32)
    l_ref[...] = jnp.zeros(l_ref.shape, F32)
    acc_ref[...] = jnp.zeros(acc_ref.shape, F32)

    def step(kb, masked):
        start = pl.multiple_of(kb * tq, tq)
        if masked:
            kc = lax.broadcasted_iota(jnp.int32, (tq, tq), 0) // CHUNK
            qc = lax.broadcasted_iota(jnp.int32, (tq, tq), 1) // CHUNK
            keep = kc <= qc
        n_chains = n_heads * n_maps

        def scores(ch):
            h = ch // n_maps
            s = _dot_nt(k_ref[0, pl.ds(start, tq), h * dk:(h + 1) * dk], qs[ch])
            return jnp.where(keep, s, -jnp.inf) if masked else s

        ss = [scores(ch) if ch < ATTN_LOOKAHEAD else None for ch in range(n_chains)]
        new = []
        for ch in range(n_chains):
            if ch + ATTN_LOOKAHEAD < n_chains:
                ss[ch + ATTN_LOOKAHEAD] = scores(ch + ATTN_LOOKAHEAD)
            h = ch // n_maps
            vt = vt_ref[0, kb, h * dv:(h + 1) * dv, :]
            s = ss[ch]
            m_old = m_ref[ch]
            m_new = jnp.maximum(m_old, jnp.max(s, axis=0, keepdims=True))
            alpha = jnp.exp(m_old - m_new)
            p = jnp.exp(s - m_new)
            l_new = alpha * l_ref[ch] + jnp.sum(p, axis=0, keepdims=True)
            acc_new = alpha * acc_ref[ch] + _dot(vt, p.astype(vt.dtype))
            new.append((m_new, l_new, acc_new))
        for ch, (m_new, l_new, acc_new) in enumerate(new):
            m_ref[ch] = m_new
            l_ref[ch] = l_new
            acc_ref[ch] = acc_new

    def loop_body(kb, carry):
        step(kb, False)
        return carry

    lax.fori_loop(0, i, loop_body, 0)
    step(i, True)


def _attn_scratch(n_chains, dv):
    return [pltpu.VMEM((n_chains, 1, ATTN_TILE), F32), pltpu.VMEM((n_chains, 1, ATTN_TILE), F32),
            pltpu.VMEM((n_chains, dv, ATTN_TILE), F32)]


def _diff_attn_kernel(q_ref, k_ref, vt_ref, lam_ref, sub_ref, o_ref, m_ref, l_ref, acc_ref, *,
                      n_heads, lam_init):
    _attn_body(q_ref, k_ref, vt_ref, m_ref, l_ref, acc_ref, n_heads=n_heads, n_maps=2, dk=LANES, dv=LANES)
    lv = lam_ref[...]
    lam = (jnp.exp(jnp.sum(lv[0:1] * lv[1:2], axis=-1, keepdims=True))
           - jnp.exp(jnp.sum(lv[2:3] * lv[3:4], axis=-1, keepdims=True)) + lam_init)
    for h in range(n_heads):
        o_t = acc_ref[2 * h] / l_ref[2 * h] - lam * (acc_ref[2 * h + 1] / l_ref[2 * h + 1])
        inv = lax.rsqrt(jnp.mean(o_t * o_t, axis=0, keepdims=True) + RMS_EPS)
        y_t = o_t * inv * sub_ref[...] * (1.0 - lam_init)
        o_ref[0, :, h * LANES:(h + 1) * LANES] = y_t.T.astype(o_ref.dtype)


def _diff_attn(p0, vt, lam_vecs, subln_col, n_heads):
    b, s, _ = p0.shape
    n_kb = s // ATTN_TILE
    width = n_heads * LANES
    kern = functools.partial(_diff_attn_kernel, n_heads=n_heads, lam_init=0.8 - 0.6 * math.exp(-0.3 * 0))
    return pl.pallas_call(
        kern,
        grid=(b, n_kb),
        in_specs=[pl.BlockSpec((1, ATTN_TILE, width), lambda bb, i: (bb, i, 0)),
                  pl.BlockSpec((1, s, width), lambda bb, i: (bb, 0, 1)),
                  pl.BlockSpec((1, n_kb, width, ATTN_TILE), lambda bb, i: (bb, 0, 0, 0)),
                  pl.BlockSpec(lam_vecs.shape, lambda bb, i: (0, 0)),
                  pl.BlockSpec(subln_col.shape, lambda bb, i: (0, 0))],
        out_specs=pl.BlockSpec((1, ATTN_TILE, width), lambda bb, i: (bb, i, 0)),
        out_shape=jax.ShapeDtypeStruct((b, s, width), BF16),
        scratch_shapes=_attn_scratch(2 * n_heads, LANES),
        compiler_params=_params("parallel", "arbitrary"),
        name="diff_attn",
    )(p0, p0, vt, lam_vecs, subln_col)


def _mla_attn_kernel(q_ref, k_ref, vt_ref, o_ref, m_ref, l_ref, acc_ref, *, n_heads):
    _attn_body(q_ref, k_ref, vt_ref, m_ref, l_ref, acc_ref, n_heads=n_heads, n_maps=1, dk=MLA_QK_PAD, dv=MLA_V)
    for h in range(n_heads):
        o_ref[0, :, h * MLA_V:(h + 1) * MLA_V] = (acc_ref[h] / l_ref[h]).T.astype(o_ref.dtype)


def _mla_attn(q_cat, k_cat, vt, n_heads):
    b, s, _ = q_cat.shape
    n_kb = s // ATTN_TILE
    return pl.pallas_call(
        functools.partial(_mla_attn_kernel, n_heads=n_heads),
        grid=(b, n_kb),
        in_specs=[pl.BlockSpec((1, ATTN_TILE, n_heads * MLA_QK_PAD), lambda bb, i: (bb, i, 0)),
                  pl.BlockSpec((1, s, n_heads * MLA_QK_PAD), lambda bb, i: (bb, 0, 0)),
                  pl.BlockSpec((1, n_kb, n_heads * MLA_V, ATTN_TILE), lambda bb, i: (bb, 0, 0, 0))],
        out_specs=pl.BlockSpec((1, ATTN_TILE, n_heads * MLA_V), lambda bb, i: (bb, i, 0)),
        out_shape=jax.ShapeDtypeStruct((b, s, n_heads * MLA_V), BF16),
        scratch_shapes=_attn_scratch(n_heads, MLA_V),
        compiler_params=_params("parallel", "arbitrary"),
        name="mla_attn",
    )(q_cat, k_cat, vt)


def _conv_kernel(a_ref, gate_ref, ah_ref, gh_ref, w_ref, b_ref, lg_ref, lb_ref, o_ref, u_ref):
    tt = a_ref.shape[1]
    u_ref[CONV_HALO:, :] = a_ref[0].astype(F32) * _sigmoid(gate_ref[0].astype(F32))
    halo = ah_ref[0].astype(F32) * _sigmoid(gh_ref[0].astype(F32))
    u_ref[:CONV_HALO, :] = jnp.where(pl.program_id(1) > 0, halo, jnp.zeros_like(halo))
    acc = jnp.zeros((tt, a_ref.shape[2]), F32)
    first = CONV_HALO - (CONV_WIDTH - 1)
    for k in range(CONV_WIDTH):
        acc = acc + u_ref[first + k:first + k + tt, :] * w_ref[k:k + 1, :]
    y = acc + b_ref[...]
    mu = jnp.mean(y, axis=-1, keepdims=True)
    yc = y - mu
    yn = yc * lax.rsqrt(jnp.mean(yc * yc, axis=-1, keepdims=True) + LN_EPS) * lg_ref[...] + lb_ref[...]
    o_ref[0] = (yn * _sigmoid(yn)).astype(o_ref.dtype)


def _conv_module(p0, dw_w, dw_b, ln_g, ln_b, col0):
    b, s, _ = p0.shape
    c = dw_w.shape[1]
    a_blk, g_blk = col0 // c, col0 // c + 1
    ratio = CONV_TILE // CONV_HALO
    main = lambda blk: pl.BlockSpec((1, CONV_TILE, c), lambda bb, t: (bb, t, blk))
    halo = lambda blk: pl.BlockSpec((1, CONV_HALO, c), lambda bb, t: (bb, jnp.maximum(t * ratio - 1, 0), blk))
    vec = lambda a: pl.BlockSpec(a.shape, lambda bb, t: (0, 0))
    return pl.pallas_call(
        _conv_kernel,
        grid=(b, s // CONV_TILE),
        in_specs=[main(a_blk), main(g_blk), halo(a_blk), halo(g_blk),
                  vec(dw_w), vec(dw_b), vec(ln_g), vec(ln_b)],
        out_specs=pl.BlockSpec((1, CONV_TILE, c), lambda bb, t: (bb, t, 0)),
        out_shape=jax.ShapeDtypeStruct((b, s, c), BF16),
        scratch_shapes=[pltpu.VMEM((CONV_HALO + CONV_TILE, c), F32)],
        compiler_params=_params("parallel", "parallel"),
        name="conv_module",
    )(p0, p0, p0, p0, dw_w, dw_b, ln_g, ln_b)


def _post_kernel(*refs, n_mix, with_next):
    x_ref = refs[0]
    mix_refs = refs[1:1 + n_mix]
    (wo_ref, gpost_ref, gfpre_ref, wg_ref, wu_ref, wd_ref, gfpost_ref) = refs[1 + n_mix:8 + n_mix]
    pos = 8 + n_mix
    if with_next:
        gnext_ref, wnext_ref = refs[pos:pos + 2]
        pos += 2
    h_out_ref = refs[pos]
    pos += 1
    if with_next:
        p_out_ref = refs[pos]
        pos += 1
    hid_ref = refs[pos]

    y = None
    row = 0
    for r in mix_refs:
        w = r.shape[-1]
        part = _dot(r[...], wo_ref[row:row + w, :])
        y = part if y is None else y + part
        row += w
    h1 = x_ref[...] + _rms(y, gpost_ref[...])
    t = _rms(h1, gfpre_ref[...]).astype(BF16)
    d_ff = wg_ref.shape[1]
    for j in range(0, d_ff, FFN_CHUNK):
        wdt = min(FFN_CHUNK, d_ff - j)
        gate = _dot(t, wg_ref[:, j:j + wdt])
        up = _dot(t, wu_ref[:, j:j + wdt])
        hid_ref[:, j:j + wdt] = (gate * _sigmoid(gate) * up).astype(BF16)
    f = _dot(hid_ref[...], wd_ref[...])
    h2 = h1 + _rms(f, gfpost_ref[...])
    h_out_ref[...] = h2
    if with_next:
        t2 = _rms(h2, gnext_ref[...]).astype(BF16)
        p_out_ref[...] = _dot(t2, wnext_ref[...])


def _post(x2, mix_parts, w_out, g_post, g_fpre, w_gate, w_up, w_down, g_fpost, nxt=None):
    n, d = x2.shape
    d_ff = w_gate.shape[1]
    row = lambda a: pl.BlockSpec((ROW_TILE, a.shape[1]), lambda i: (i, 0))
    consts = [w_out, g_post, g_fpre, w_gate, w_up, w_down, g_fpost] + (list(nxt) if nxt else [])
    out_shape = [jax.ShapeDtypeStruct((n, d), F32)]
    out_specs = [pl.BlockSpec((ROW_TILE, d), lambda i: (i, 0))]
    if nxt:
        n_next = nxt[1].shape[1]
        out_shape.append(jax.ShapeDtypeStruct((n, n_next), F32))
        out_specs.append(pl.BlockSpec((ROW_TILE, n_next), lambda i: (i, 0)))
    kern = functools.partial(_post_kernel, n_mix=len(mix_parts), with_next=bool(nxt))
    return pl.pallas_call(
        kern,
        grid=(n // ROW_TILE,),
        in_specs=[row(x2)] + [row(m) for m in mix_parts] + [_const_spec(c.shape) for c in consts],
        out_specs=out_specs,
        out_shape=out_shape,
        scratch_shapes=[pltpu.VMEM((ROW_TILE, d_ff), BF16)],
        compiler_params=_params("parallel"),
        name="post_next" if nxt else "post",
    )(x2, *mix_parts, *consts)


def _mla_prep_kernel(cq_ref, ckv_ref, kr_ref, qn_ref, kvn_ref, wq_ref, wk_ref, wv_ref,
                     c_ref, sa_ref, sb_ref, q_out, k_out, vt_out, *, n_heads, scale):
    c, sa, sb = c_ref[...], sa_ref[...], sb_ref[...]
    half = MLA_ROPE // 2
    q = _dot(_rms(cq_ref[...], qn_ref[...]).astype(BF16), wq_ref[...])
    for h in range(n_heads):
        lo = h * MLA_QK_PAD
        q_out[:, lo:lo + LANES] = (q[:, lo:lo + LANES] * scale).astype(q_out.dtype)
        roped = _rope_block(q[:, lo + LANES:lo + 2 * LANES], c, sa, sb, half)
        q_out[:, lo + LANES:lo + 2 * LANES] = (roped * scale).astype(q_out.dtype)
    ckv = _rms(ckv_ref[...], kvn_ref[...]).astype(BF16)
    kr = _rope_block(kr_ref[...], c, sa, sb, half).astype(BF16)
    k_out[...] = _dot(jnp.concatenate([ckv, kr], axis=-1), wk_ref[...]).astype(k_out.dtype)
    _store_vt(vt_out, _dot_nt(wv_ref[...], ckv))


def _mla_prep(p1, q_norm, kv_norm, wq, wk, wv_t, tabs, b, seq, n_heads):
    n = p1.shape[0]
    tiles_per_seq = seq // ROW_TILE
    tab_spec = pl.BlockSpec((ROW_TILE, LANES), lambda i: (i % tiles_per_seq, 0))
    kern = functools.partial(_mla_prep_kernel, n_heads=n_heads, scale=(MLA_NOPE + MLA_ROPE) ** -0.5)
    out_w = (n_heads * MLA_QK_PAD, n_heads * MLA_QK_PAD)
    vt_spec, vt_shape = _vt_out(b, seq, wv_t.shape[0])
    return pl.pallas_call(
        kern,
        grid=(n // ROW_TILE,),
        in_specs=[pl.BlockSpec((ROW_TILE, 2 * LANES), lambda i: (i, 1)),
                  pl.BlockSpec((ROW_TILE, LANES), lambda i: (i, 4)),
                  pl.BlockSpec((ROW_TILE, LANES), lambda i: (i, 5)),
                  _const_spec(q_norm.shape), _const_spec(kv_norm.shape),
                  _const_spec(wq.shape), _const_spec(wk.shape), _const_spec(wv_t.shape),
                  tab_spec, tab_spec, tab_spec],
        out_specs=[pl.BlockSpec((ROW_TILE, w), lambda i: (i, 0)) for w in out_w] + [vt_spec],
        out_shape=[jax.ShapeDtypeStruct((n, w), BF16) for w in out_w] + [vt_shape],
        compiler_params=_params("parallel"),
        name="mla_prep",
    )(p1, p1, p1, q_norm, kv_norm, wq, wk, wv_t, *tabs)


def _gelu_tanh(x):
    return 0.5 * x * (1.0 + jnp.tanh(math.sqrt(2.0 / math.pi) * (x + 0.044715 * (x * x * x))))


def _ssm_kernel(u_ref, lr_ref, li_ref, ldt_ref, bre_ref, bim_ref, cre_ref, cim_ref, d_ref, wg_ref, bg_ref,
                o_ref, bmat_ref, cmat_ref, a_ref, st_ref, bu_ref, x_ref, y_ref):
    nb, tt, _ = u_ref.shape
    n_state = lr_ref.shape[1]

    @pl.when(pl.program_id(0) == 0)
    def _init():
        lr, li = lr_ref[...], li_ref[...]
        dt = jnp.exp(ldt_ref[...])
        mag = jnp.exp(lr * dt)
        ab_re = mag * jnp.cos(li * dt)
        ab_im = mag * jnp.sin(li * dt)
        den = lr * lr + li * li
        n_re = ab_re - 1.0
        f_re = (n_re * lr + ab_im * li) / den
        f_im = (ab_im * lr - n_re * li) / den
        br, bi = bre_ref[...], bim_ref[...]
        bmat_ref[:, :n_state] = (f_re * br - f_im * bi).astype(BF16)
        bmat_ref[:, n_state:] = (f_re * bi + f_im * br).astype(BF16)
        cmat_ref[:n_state, :] = cre_ref[...].astype(BF16)
        cmat_ref[n_state:, :] = (-cim_ref[...]).astype(BF16)
        a_ref[0:1, :] = ab_re
        a_ref[1:2, :] = ab_im
        st_ref[...] = jnp.zeros(st_ref.shape, F32)

    n_blk = 2 * n_state // LANES
    for b in range(nb):
        bu = _dot(u_ref[b].astype(BF16), bmat_ref[...])
        for j in range(n_blk):
            bu_ref[j, b * tt:(b + 1) * tt, :] = bu[:, j * LANES:(j + 1) * LANES]

    a_re = jnp.broadcast_to(a_ref[0:1, :], (nb, n_state))
    a_im = jnp.broadcast_to(a_ref[1:2, :], (nb, n_state))

    def scan_step(t, carry):
        x_re, x_im = carry
        rows = jnp.concatenate([bu_ref[j, pl.ds(t, nb, stride=tt), :] for j in range(n_blk)], axis=-1)
        n_re = a_re * x_re - a_im * x_im + rows[:, :n_state]
        n_im = a_re * x_im + a_im * x_re + rows[:, n_state:]
        dst = pl.multiple_of(t * nb, nb)
        x_ref[pl.ds(dst, nb), :n_state] = n_re
        x_ref[pl.ds(dst, nb), n_state:] = n_im
        return n_re, n_im

    x_re, x_im = lax.fori_loop(0, tt, scan_step, (st_ref[:, :n_state], st_ref[:, n_state:]))
    st_ref[:, :n_state] = x_re
    st_ref[:, n_state:] = x_im

    y_tm = _dot(x_ref[...].astype(BF16), cmat_ref[...])
    n_yblk = y_tm.shape[1] // LANES
    for j in range(n_yblk):
        y_ref[j] = y_tm[:, j * LANES:(j + 1) * LANES]
    for b in range(nb):
        y = jnp.concatenate([y_ref[j, pl.ds(b, tt, stride=nb), :] for j in range(n_yblk)], axis=-1)
        y = y + d_ref[...] * u_ref[b]
        z = _gelu_tanh(y)
        gate = _dot(z.astype(BF16), wg_ref[...]) + bg_ref[...]
        o_ref[b] = (z * _sigmoid(gate)).astype(o_ref.dtype)


def _ssm(p1, rows, b_bd, c_bd, d_row, w_glu, b_glu):
    b, s, _ = p1.shape
    ch = w_glu.shape[0]
    n_state = rows[0].shape[1]
    consts = list(rows) + list(b_bd) + list(c_bd) + [d_row, w_glu, b_glu]
    return pl.pallas_call(
        _ssm_kernel,
        grid=(s // SSM_TILE,),
        in_specs=[pl.BlockSpec((b, SSM_TILE, ch), lambda t: (0, t, 0))] + [_const_spec(c.shape) for c in consts],
        out_specs=pl.BlockSpec((b, SSM_TILE, ch), lambda t: (0, t, 0)),
        out_shape=jax.ShapeDtypeStruct((b, s, ch), BF16),
        scratch_shapes=[pltpu.VMEM((ch, 2 * n_state), BF16),
                        pltpu.VMEM((2 * n_state, ch), BF16),
                        pltpu.VMEM((SUBLANES, n_state), F32),
                        pltpu.VMEM((b, 2 * n_state), F32),
                        pltpu.VMEM((2 * n_state // LANES, b * SSM_TILE, LANES), F32),
                        pltpu.VMEM((SSM_TILE * b, 2 * n_state), F32),
                        pltpu.VMEM((ch // LANES, SSM_TILE * b, LANES), F32)],
        compiler_params=_params("arbitrary"),
        name="s5_ssm",
    )(p1, *consts)


def _rope_tables(s, rot_dim, theta):
    inv = theta ** (-jnp.arange(0, rot_dim, 2, dtype=F32) / rot_dim)
    ang = jnp.arange(s, dtype=F32)[:, None] * inv[None, :]
    return jnp.cos(ang), jnp.sin(ang)


def _lane_tables(cos, sin, period):
    s, half = cos.shape
    reps = LANES // period
    one = jnp.ones((s, period - 2 * half), F32)
    zero = jnp.zeros((s, period - 2 * half), F32)
    zh = jnp.zeros((s, half), F32)
    c = jnp.tile(jnp.concatenate([cos, cos, one], axis=1), (1, reps))
    sa = jnp.tile(jnp.concatenate([-sin, zh, zero], axis=1), (1, reps))
    sb = jnp.tile(jnp.concatenate([zh, sin, zero], axis=1), (1, reps))
    return c, sa, sb


def _block_diag(blocks):
    g, r, c = blocks.shape
    eye = jnp.eye(g, dtype=blocks.dtype)
    return (eye[:, None, :, None] * blocks[:, :, None, :]).reshape(g * r, g * c)


def kernel(x, l0_mix_pre, l0_mix_post, l0_w_in, l0_lambda_q1, l0_lambda_k1, l0_lambda_q2, l0_lambda_k2, l0_subln, l0_dw_w, l0_dw_b, l0_conv_ln_g, l0_conv_ln_b, l0_w_out, l0_ffn_pre, l0_ffn_post, l0_w_gate, l0_w_up, l0_w_down, l1_mix_pre, l1_mix_post, l1_w_in, l1_a_re, l1_a_im, l1_log_dt, l1_b_re, l1_b_im, l1_c_re, l1_c_im, l1_d_skip, l1_w_glu, l1_b_glu, l1_q_norm, l1_w_uq, l1_kv_norm, l1_w_ukv, l1_w_out, l1_ffn_pre, l1_ffn_post, l1_w_gate, l1_w_up, l1_w_down):
    b, s, d = x.shape
    n = b * s
    row = lambda v: v.reshape(1, -1).astype(F32)
    bf = lambda w: w.astype(BF16)

    diff_width = 4 * LANES
    n_diff_heads = diff_width // LANES
    conv_ch = l0_dw_w.shape[1]
    ssm_ch = l1_w_glu.shape[0]
    n_groups, n_state_g = l1_a_re.shape
    q_rank = l1_q_norm.shape[0]
    kv_rank = l1_kv_norm.shape[0]
    n_mla_heads = l1_w_uq.shape[1] // (MLA_NOPE + MLA_ROPE)

    tabs_a = _lane_tables(*_rope_tables(s, DIFF_ROT, ROPE_THETA), period=DIFF_HEAD_DIM)
    x2 = x.reshape(n, d)
    w_main = bf(jnp.concatenate([l0_w_in[:, :2 * diff_width], l0_w_in[:, 3 * diff_width:]], axis=1))
    w_vt = bf(l0_w_in[:, 2 * diff_width:3 * diff_width].T)
    p0, vt_a = _l0_in(x2, row(l0_mix_pre), w_main, w_vt, tabs_a, b, s)
    p0 = p0.reshape(b, s, -1)
    lam_vecs = jnp.stack([l0_lambda_q1, l0_lambda_k1, l0_lambda_q2, l0_lambda_k2]).astype(F32)
    y_a = _diff_attn(p0, vt_a, lam_vecs, l0_subln.reshape(-1, 1).astype(F32), n_diff_heads)
    y_b = _conv_module(p0, l0_dw_w.astype(F32), row(l0_dw_b), row(l0_conv_ln_g), row(l0_conv_ln_b),
                       col0=2 * diff_width)

    pad = (-l1_w_in.shape[1]) % LANES
    w_in1 = bf(jnp.pad(l1_w_in, ((0, 0), (0, pad))))
    h2, p1 = _post(x2, [y_a.reshape(n, -1), y_b.reshape(n, -1)], bf(l0_w_out), row(l0_mix_post),
                   row(l0_ffn_pre), bf(l0_w_gate), bf(l0_w_up), bf(l0_w_down), row(l0_ffn_post),
                   nxt=(row(l1_mix_pre), w_in1))

    state_row = lambda a: a.reshape(1, -1).astype(F32)
    ssm_rows = (state_row(l1_a_re), state_row(l1_a_im),
                state_row(jnp.broadcast_to(l1_log_dt[:, None], (n_groups, n_state_g))))
    b_bd = tuple(_block_diag(jnp.swapaxes(m, 1, 2).astype(F32)) for m in (l1_b_re, l1_b_im))
    c_bd = tuple(_block_diag(jnp.swapaxes(m, 1, 2).astype(F32)) for m in (l1_c_re, l1_c_im))
    y_c = _ssm(p1.reshape(b, s, -1), ssm_rows, b_bd, c_bd, row(l1_d_skip), bf(l1_w_glu), row(l1_b_glu))

    wq = l1_w_uq.reshape(q_rank, n_mla_heads, MLA_NOPE + MLA_ROPE)
    wq = jnp.pad(wq, ((0, 0), (0, 0), (0, MLA_QK_PAD - MLA_NOPE - MLA_ROPE))).reshape(q_rank, -1)
    wkv = l1_w_ukv.reshape(kv_rank, n_mla_heads, MLA_NOPE + MLA_V)
    wk_nope = jnp.pad(wkv[:, :, :MLA_NOPE], ((0, 0), (0, 0), (0, MLA_QK_PAD - MLA_NOPE)))
    route = jnp.pad(jnp.eye(MLA_ROPE, dtype=F32), ((0, LANES - MLA_ROPE), (MLA_NOPE, MLA_QK_PAD - MLA_NOPE - MLA_ROPE)))
    wk_rope = jnp.broadcast_to(route[:, None, :], (LANES, n_mla_heads, MLA_QK_PAD))
    wk = jnp.concatenate([wk_nope, wk_rope], axis=0).reshape(kv_rank + LANES, -1)
    wv_t = wkv[:, :, MLA_NOPE:].reshape(kv_rank, -1).T
    tabs_d = _lane_tables(*_rope_tables(s, MLA_ROPE, MLA_ROPE_THETA), period=LANES)
    q_cat, k_cat, vt_d = _mla_prep(p1, row(l1_q_norm), row(l1_kv_norm), bf(wq), bf(wk), bf(wv_t), tabs_d,
                                   b, s, n_mla_heads)
    y_d = _mla_attn(q_cat.reshape(b, s, -1), k_cat.reshape(b, s, -1), vt_d, n_mla_heads)

    (out,) = _post(h2, [y_c.reshape(n, -1), y_d.reshape(n, -1)], bf(l1_w_out), row(l1_mix_post),
                   row(l1_ffn_pre), bf(l1_w_gate), bf(l1_w_up), bf(l1_w_down), row(l1_ffn_post))
    return out.reshape(b, s, d)
```

```python
import functools
import math

import jax
import jax.numpy as jnp
from jax import lax
from jax.experimental import pallas as pl
from jax.experimental.pallas import tpu as pltpu

F32 = jnp.float32
BF16 = jnp.bfloat16

LANES = 128
SUBLANES = 8
VMEM_LIMIT_BYTES = 56 * 1024 * 1024

CHUNK = 64
RMS_EPS = 1e-6
LN_EPS = 1e-5
ROPE_THETA = 500000.0
MLA_ROPE_THETA = 10000.0
DIFF_HEAD_DIM = 64
DIFF_ROT = 16
CONV_WIDTH = 31
CONV_HALO = 32
SSM_GROUP = 16
SSM_STATE = 64
MLA_NOPE = 128
MLA_ROPE = 64
MLA_V = 128
MLA_QK_PAD = 256

ROW_TILE = 512
ATTN_TILE = 256
ATTN_LOOKAHEAD = 3
ATTN_SUM_ROWS = 16
LOG2_E = math.log2(math.e)
CONV_TILE = 256
SSM_TILE = 128
SSM_PITCH = SSM_TILE + SUBLANES
FFN_CHUNK = 512


def _params(*sem):
    return pltpu.CompilerParams(dimension_semantics=sem, vmem_limit_bytes=VMEM_LIMIT_BYTES)


def _rms(x, g):
    return x * lax.rsqrt(jnp.mean(x * x, axis=-1, keepdims=True) + RMS_EPS) * g


def _sigmoid(x):
    return 1.0 / (1.0 + jnp.exp(-x))


def _dot(a, b):
    return jnp.dot(a, b, preferred_element_type=F32)


def _dot_nt(a, b):
    return lax.dot_general(a, b, (((1,), (1,)), ((), ())), preferred_element_type=F32)


def _rope_block(x, c, sa, sb, shift):
    return x * c + pltpu.roll(x, LANES - shift, 1) * sa + pltpu.roll(x, shift, 1) * sb


def _const_spec(shape):
    nd = len(shape)
    return pl.BlockSpec(shape, lambda *_: (0,) * nd, pipeline_mode=pl.Buffered(1))


def _store_vt(vt_ref, vt):
    for j in range(ROW_TILE // ATTN_TILE):
        vt_ref[0, j] = vt[:, j * ATTN_TILE:(j + 1) * ATTN_TILE].astype(vt_ref.dtype)


def _vt_out(b, seq, width):
    tiles_per_seq = seq // ROW_TILE
    per_tile = ROW_TILE // ATTN_TILE
    spec = pl.BlockSpec((1, per_tile, width, ATTN_TILE),
                        lambda i: (i // tiles_per_seq, i % tiles_per_seq, 0, 0))
    return spec, jax.ShapeDtypeStruct((b, seq // ATTN_TILE, width, ATTN_TILE), BF16)


def _l0_in_kernel(x_ref, g_ref, w_ref, wvt_ref, c_ref, sa_ref, sb_ref, o_ref, vt_ref, *,
                  n_rope_blocks, n_q_blocks, scale):
    t = _rms(x_ref[...], g_ref[...]).astype(BF16)
    p = _dot(t, w_ref[...])
    c, sa, sb = c_ref[...], sa_ref[...], sb_ref[...]
    for j in range(n_rope_blocks):
        blk = _rope_block(p[:, j * LANES:(j + 1) * LANES], c, sa, sb, DIFF_ROT // 2)
        if j < n_q_blocks:
            blk = blk * scale
        o_ref[:, j * LANES:(j + 1) * LANES] = blk.astype(o_ref.dtype)
    rest = n_rope_blocks * LANES
    o_ref[:, rest:] = p[:, rest:].astype(o_ref.dtype)
    _store_vt(vt_ref, _dot_nt(wvt_ref[...], t))


def _l0_in(x2, g_pre, w_main, w_vt, tabs, b, seq):
    n, d = x2.shape
    n_out = w_main.shape[1]
    tiles_per_seq = seq // ROW_TILE
    kern = functools.partial(_l0_in_kernel, n_rope_blocks=8, n_q_blocks=4,
                             scale=DIFF_HEAD_DIM ** -0.5 * LOG2_E)
    tab_spec = pl.BlockSpec((ROW_TILE, LANES), lambda i: (i % tiles_per_seq, 0))
    vt_spec, vt_shape = _vt_out(b, seq, w_vt.shape[0])
    return pl.pallas_call(
        kern,
        grid=(n // ROW_TILE,),
        in_specs=[pl.BlockSpec((ROW_TILE, d), lambda i: (i, 0)),
                  _const_spec((1, d)), _const_spec(w_main.shape), _const_spec(w_vt.shape),
                  tab_spec, tab_spec, tab_spec],
        out_specs=[pl.BlockSpec((ROW_TILE, n_out), lambda i: (i, 0)), vt_spec],
        out_shape=[jax.ShapeDtypeStruct((n, n_out), BF16), vt_shape],
        compiler_params=_params("parallel"),
        name="l0_in",
    )(x2, g_pre, w_main, w_vt, *tabs)


def _attn_body(q_ref, k_ref, vt_ref, m_ref, acc_ref, spre_ref, *, n_heads, n_maps, dk, dv):
    i = pl.program_id(1)
    tq = q_ref.shape[1]
    map_width = dk // n_maps
    n_chains = n_heads * n_maps
    qs = []
    for h in range(n_heads):
        q = q_ref[0, :, h * dk:(h + 1) * dk]
        if n_maps == 1:
            qs.append(q)
        else:
            lane = lax.broadcasted_iota(jnp.int32, q.shape, 1)
            for c in range(n_maps):
                sel = (lane >= c * map_width) & (lane < (c + 1) * map_width)
                qs.append(jnp.where(sel, q, jnp.zeros_like(q)))
    m_ref[...] = jnp.full(m_ref.shape, -jnp.inf, F32)
    acc_ref[...] = jnp.zeros(acc_ref.shape, F32)
    ones = jnp.ones((ATTN_SUM_ROWS, tq), vt_ref.dtype)

    def scores(kb, ch):
        h = ch // n_maps
        start = pl.multiple_of(kb * tq, tq)
        return _dot_nt(k_ref[0, pl.ds(start, tq), h * dk:(h + 1) * dk], qs[ch])

    def step(kb, masked, prefetch):
        if masked:
            kc = lax.broadcasted_iota(jnp.int32, (tq, tq), 0) // CHUNK
            qc = lax.broadcasted_iota(jnp.int32, (tq, tq), 1) // CHUNK
            keep = kc <= qc
        ss = [spre_ref[ch] if ch < ATTN_LOOKAHEAD else None for ch in range(n_chains)]
        new = []
        for ch in range(n_chains):
            ahead = ch + ATTN_LOOKAHEAD
            if ahead < n_chains:
                ss[ahead] = scores(kb, ahead)
            elif prefetch:
                spre_ref[ahead - n_chains] = scores(kb + 1, ahead - n_chains)
            h = ch // n_maps
            vt = jnp.concatenate([vt_ref[0, kb, h * dv:(h + 1) * dv, :], ones], axis=0)
            s = jnp.where(keep, ss[ch], -jnp.inf) if masked else ss[ch]
            m_old = m_ref[ch]
            m_new = jnp.maximum(m_old, jnp.max(s, axis=0, keepdims=True))
            alpha = jnp.exp2(m_old - m_new)
            p = jnp.exp2(s - m_new)
            new.append((m_new, alpha * acc_ref[ch] + _dot(vt, p.astype(vt.dtype))))
        for ch, (m_new, acc_new) in enumerate(new):
            m_ref[ch] = m_new
            acc_ref[ch] = acc_new

    for ch in range(ATTN_LOOKAHEAD):
        spre_ref[ch] = scores(0, ch)

    def loop_body(kb, carry):
        step(kb, False, True)
        return carry

    lax.fori_loop(0, i, loop_body, 0)
    step(i, True, False)


def _attn_scratch(n_chains, dv):
    return [pltpu.VMEM((n_chains, 1, ATTN_TILE), F32),
            pltpu.VMEM((n_chains, dv + ATTN_SUM_ROWS, ATTN_TILE), F32),
            pltpu.VMEM((ATTN_LOOKAHEAD, ATTN_TILE, ATTN_TILE), F32)]


def _attn_out(acc_ref, ch, dv):
    return acc_ref[ch, :dv, :] / acc_ref[ch, dv:dv + 1, :]


def _diff_attn_kernel(q_ref, k_ref, vt_ref, lam_ref, sub_ref, o_ref, m_ref, acc_ref, spre_ref, *,
                      n_heads, lam_init):
    _attn_body(q_ref, k_ref, vt_ref, m_ref, acc_ref, spre_ref, n_heads=n_heads, n_maps=2, dk=LANES, dv=LANES)
    lv = lam_ref[...]
    lam = (jnp.exp(jnp.sum(lv[0:1] * lv[1:2], axis=-1, keepdims=True))
           - jnp.exp(jnp.sum(lv[2:3] * lv[3:4], axis=-1, keepdims=True)) + lam_init)
    for h in range(n_heads):
        o_t = _attn_out(acc_ref, 2 * h, LANES) - lam * _attn_out(acc_ref, 2 * h + 1, LANES)
        inv = lax.rsqrt(jnp.mean(o_t * o_t, axis=0, keepdims=True) + RMS_EPS)
        y_t = o_t * inv * sub_ref[...] * (1.0 - lam_init)
        o_ref[0, :, h * LANES:(h + 1) * LANES] = y_t.T.astype(o_ref.dtype)


def _diff_attn(p0, vt, lam_vecs, subln_col, n_heads):
    b, s, _ = p0.shape
    n_kb = s // ATTN_TILE
    width = n_heads * LANES
    kern = functools.partial(_diff_attn_kernel, n_heads=n_heads, lam_init=0.8 - 0.6 * math.exp(-0.3 * 0))
    return pl.pallas_call(
        kern,
        grid=(b, n_kb),
        in_specs=[pl.BlockSpec((1, ATTN_TILE, width), lambda bb, i: (bb, i, 0)),
                  pl.BlockSpec((1, s, width), lambda bb, i: (bb, 0, 1)),
                  pl.BlockSpec((1, n_kb, width, ATTN_TILE), lambda bb, i: (bb, 0, 0, 0)),
                  pl.BlockSpec(lam_vecs.shape, lambda bb, i: (0, 0)),
                  pl.BlockSpec(subln_col.shape, lambda bb, i: (0, 0))],
        out_specs=pl.BlockSpec((1, ATTN_TILE, width), lambda bb, i: (bb, i, 0)),
        out_shape=jax.ShapeDtypeStruct((b, s, width), BF16),
        scratch_shapes=_attn_scratch(2 * n_heads, LANES),
        compiler_params=_params("parallel", "arbitrary"),
        name="diff_attn",
    )(p0, p0, vt, lam_vecs, subln_col)


def _mla_attn_kernel(q_ref, k_ref, vt_ref, o_ref, m_ref, acc_ref, spre_ref, *, n_heads):
    _attn_body(q_ref, k_ref, vt_ref, m_ref, acc_ref, spre_ref, n_heads=n_heads, n_maps=1, dk=MLA_QK_PAD,
               dv=MLA_V)
    for h in range(n_heads):
        o_ref[0, :, h * MLA_V:(h + 1) * MLA_V] = _attn_out(acc_ref, h, MLA_V).T.astype(o_ref.dtype)


def _mla_attn(q_cat, k_cat, vt, n_heads):
    b, s, _ = q_cat.shape
    n_kb = s // ATTN_TILE
    return pl.pallas_call(
        functools.partial(_mla_attn_kernel, n_heads=n_heads),
        grid=(b, n_kb),
        in_specs=[pl.BlockSpec((1, ATTN_TILE, n_heads * MLA_QK_PAD), lambda bb, i: (bb, i, 0)),
                  pl.BlockSpec((1, s, n_heads * MLA_QK_PAD), lambda bb, i: (bb, 0, 0)),
                  pl.BlockSpec((1, n_kb, n_heads * MLA_V, ATTN_TILE), lambda bb, i: (bb, 0, 0, 0))],
        out_specs=pl.BlockSpec((1, ATTN_TILE, n_heads * MLA_V), lambda bb, i: (bb, i, 0)),
        out_shape=jax.ShapeDtypeStruct((b, s, n_heads * MLA_V), BF16),
        scratch_shapes=_attn_scratch(n_heads, MLA_V),
        compiler_params=_params("parallel", "arbitrary"),
        name="mla_attn",
    )(q_cat, k_cat, vt)


def _conv_kernel(a_ref, gate_ref, ah_ref, gh_ref, w_ref, b_ref, lg_ref, lb_ref, o_ref, u_ref, ur_ref):
    tt = a_ref.shape[1]
    u_ref[CONV_HALO:, :] = a_ref[0].astype(F32) * _sigmoid(gate_ref[0].astype(F32))
    halo = ah_ref[0].astype(F32) * _sigmoid(gh_ref[0].astype(F32))
    u_ref[:CONV_HALO, :] = jnp.where(pl.program_id(1) > 0, halo, jnp.zeros_like(halo))
    rows = ur_ref.shape[1]
    for r in range(1, SUBLANES):
        ur_ref[r - 1] = u_ref[r:r + rows, :]
    acc = jnp.zeros((tt, a_ref.shape[2]), F32)
    first = CONV_HALO - (CONV_WIDTH - 1)
    for k in range(CONV_WIDTH):
        base, r = divmod(first + k, SUBLANES)
        src = u_ref if r == 0 else ur_ref.at[r - 1]
        acc = acc + src[base * SUBLANES:base * SUBLANES + tt, :] * w_ref[k:k + 1, :]
    y = acc + b_ref[...]
    mu = jnp.mean(y, axis=-1, keepdims=True)
    yc = y - mu
    yn = yc * lax.rsqrt(jnp.mean(yc * yc, axis=-1, keepdims=True) + LN_EPS) * lg_ref[...] + lb_ref[...]
    o_ref[0] = (yn * _sigmoid(yn)).astype(o_ref.dtype)


def _conv_module(p0, dw_w, dw_b, ln_g, ln_b, col0):
    b, s, _ = p0.shape
    c = dw_w.shape[1]
    a_blk, g_blk = col0 // c, col0 // c + 1
    ratio = CONV_TILE // CONV_HALO
    main = lambda blk: pl.BlockSpec((1, CONV_TILE, c), lambda bb, t: (bb, t, blk))
    halo = lambda blk: pl.BlockSpec((1, CONV_HALO, c), lambda bb, t: (bb, jnp.maximum(t * ratio - 1, 0), blk))
    vec = lambda a: pl.BlockSpec(a.shape, lambda bb, t: (0, 0))
    return pl.pallas_call(
        _conv_kernel,
        grid=(b, s // CONV_TILE),
        in_specs=[main(a_blk), main(g_blk), halo(a_blk), halo(g_blk),
                  vec(dw_w), vec(dw_b), vec(ln_g), vec(ln_b)],
        out_specs=pl.BlockSpec((1, CONV_TILE, c), lambda bb, t: (bb, t, 0)),
        out_shape=jax.ShapeDtypeStruct((b, s, c), BF16),
        scratch_shapes=[pltpu.VMEM((CONV_HALO + CONV_TILE, c), F32),
                        pltpu.VMEM((SUBLANES - 1, CONV_HALO + CONV_TILE - SUBLANES, c), F32)],
        compiler_params=_params("parallel", "parallel"),
        name="conv_module",
    )(p0, p0, p0, p0, dw_w, dw_b, ln_g, ln_b)


def _post_kernel(*refs, n_mix, with_next):
    x_ref = refs[0]
    mix_refs = refs[1:1 + n_mix]
    (wo_ref, gpost_ref, gfpre_ref, wg_ref, wu_ref, wd_ref, gfpost_ref) = refs[1 + n_mix:8 + n_mix]
    pos = 8 + n_mix
    if with_next:
        gnext_ref, wnext_ref = refs[pos:pos + 2]
        pos += 2
    h_out_ref = refs[pos]
    pos += 1
    if with_next:
        p_out_ref = refs[pos]
        pos += 1
    hid_ref = refs[pos]

    y = None
    row = 0
    for r in mix_refs:
        w = r.shape[-1]
        part = _dot(r[...], wo_ref[row:row + w, :])
        y = part if y is None else y + part
        row += w
    h1 = x_ref[...] + _rms(y, gpost_ref[...])
    t = _rms(h1, gfpre_ref[...]).astype(BF16)
    d_ff = wg_ref.shape[1]
    for j in range(0, d_ff, FFN_CHUNK):
        wdt = min(FFN_CHUNK, d_ff - j)
        gate = _dot(t, wg_ref[:, j:j + wdt])
        up = _dot(t, wu_ref[:, j:j + wdt])
        hid_ref[:, j:j + wdt] = (gate * _sigmoid(gate) * up).astype(BF16)
    f = _dot(hid_ref[...], wd_ref[...])
    h2 = h1 + _rms(f, gfpost_ref[...])
    h_out_ref[...] = h2
    if with_next:
        t2 = _rms(h2, gnext_ref[...]).astype(BF16)
        p_out_ref[...] = _dot(t2, wnext_ref[...])


def _post(x2, mix_parts, w_out, g_post, g_fpre, w_gate, w_up, w_down, g_fpost, nxt=None):
    n, d = x2.shape
    d_ff = w_gate.shape[1]
    row = lambda a: pl.BlockSpec((ROW_TILE, a.shape[1]), lambda i: (i, 0))
    consts = [w_out, g_post, g_fpre, w_gate, w_up, w_down, g_fpost] + (list(nxt) if nxt else [])
    out_shape = [jax.ShapeDtypeStruct((n, d), F32)]
    out_specs = [pl.BlockSpec((ROW_TILE, d), lambda i: (i, 0))]
    if nxt:
        n_next = nxt[1].shape[1]
        out_shape.append(jax.ShapeDtypeStruct((n, n_next), F32))
        out_specs.append(pl.BlockSpec((ROW_TILE, n_next), lambda i: (i, 0)))
    kern = functools.partial(_post_kernel, n_mix=len(mix_parts), with_next=bool(nxt))
    return pl.pallas_call(
        kern,
        grid=(n // ROW_TILE,),
        in_specs=[row(x2)] + [row(m) for m in mix_parts] + [_const_spec(c.shape) for c in consts],
        out_specs=out_specs,
        out_shape=out_shape,
        scratch_shapes=[pltpu.VMEM((ROW_TILE, d_ff), BF16)],
        compiler_params=_params("parallel"),
        name="post_next" if nxt else "post",
    )(x2, *mix_parts, *consts)


def _mla_prep_kernel(cq_ref, ckv_ref, kr_ref, qn_ref, kvn_ref, wq_ref, wk_ref, wv_ref,
                     c_ref, sa_ref, sb_ref, q_out, k_out, vt_out, *, n_heads, scale):
    c, sa, sb = c_ref[...], sa_ref[...], sb_ref[...]
    half = MLA_ROPE // 2
    q = _dot(_rms(cq_ref[...], qn_ref[...]).astype(BF16), wq_ref[...])
    for h in range(n_heads):
        lo = h * MLA_QK_PAD
        q_out[:, lo:lo + LANES] = (q[:, lo:lo + LANES] * scale).astype(q_out.dtype)
        roped = _rope_block(q[:, lo + LANES:lo + 2 * LANES], c, sa, sb, half)
        q_out[:, lo + LANES:lo + 2 * LANES] = (roped * scale).astype(q_out.dtype)
    ckv = _rms(ckv_ref[...], kvn_ref[...]).astype(BF16)
    kr = _rope_block(kr_ref[...], c, sa, sb, half).astype(BF16)
    k_out[...] = _dot(jnp.concatenate([ckv, kr], axis=-1), wk_ref[...]).astype(k_out.dtype)
    _store_vt(vt_out, _dot_nt(wv_ref[...], ckv))


def _mla_prep(p1, q_norm, kv_norm, wq, wk, wv_t, tabs, b, seq, n_heads):
    n = p1.shape[0]
    tiles_per_seq = seq // ROW_TILE
    tab_spec = pl.BlockSpec((ROW_TILE, LANES), lambda i: (i % tiles_per_seq, 0))
    kern = functools.partial(_mla_prep_kernel, n_heads=n_heads,
                             scale=(MLA_NOPE + MLA_ROPE) ** -0.5 * LOG2_E)
    out_w = (n_heads * MLA_QK_PAD, n_heads * MLA_QK_PAD)
    vt_spec, vt_shape = _vt_out(b, seq, wv_t.shape[0])
    return pl.pallas_call(
        kern,
        grid=(n // ROW_TILE,),
        in_specs=[pl.BlockSpec((ROW_TILE, 2 * LANES), lambda i: (i, 1)),
                  pl.BlockSpec((ROW_TILE, LANES), lambda i: (i, 4)),
                  pl.BlockSpec((ROW_TILE, LANES), lambda i: (i, 5)),
                  _const_spec(q_norm.shape), _const_spec(kv_norm.shape),
                  _const_spec(wq.shape), _const_spec(wk.shape), _const_spec(wv_t.shape),
                  tab_spec, tab_spec, tab_spec],
        out_specs=[pl.BlockSpec((ROW_TILE, w), lambda i: (i, 0)) for w in out_w] + [vt_spec],
        out_shape=[jax.ShapeDtypeStruct((n, w), BF16) for w in out_w] + [vt_shape],
        compiler_params=_params("parallel"),
        name="mla_prep",
    )(p1, p1, p1, q_norm, kv_norm, wq, wk, wv_t, *tabs)


def _gelu_tanh(x):
    return 0.5 * x * (1.0 + jnp.tanh(math.sqrt(2.0 / math.pi) * (x + 0.044715 * (x * x * x))))


def _ssm_kernel(u_ref, lr_ref, li_ref, ldt_ref, bre_ref, bim_ref, cre_ref, cim_ref, d_ref, wg_ref, bg_ref,
                o_ref, bmat_ref, cmat_ref, a_ref, st_ref, us_ref, utm_ref, x_ref, y_ref):
    nb, tt, ch = u_ref.shape
    n_state = lr_ref.shape[1]

    @pl.when(pl.program_id(0) == 0)
    def _init():
        lr, li = lr_ref[...], li_ref[...]
        dt = jnp.exp(ldt_ref[...])
        mag = jnp.exp(lr * dt)
        ab_re = mag * jnp.cos(li * dt)
        ab_im = mag * jnp.sin(li * dt)
        den = lr * lr + li * li
        n_re = ab_re - 1.0
        f_re = (n_re * lr + ab_im * li) / den
        f_im = (ab_im * lr - n_re * li) / den
        br, bi = bre_ref[...], bim_ref[...]
        bmat_ref[:, :n_state] = (f_re * br - f_im * bi).astype(BF16)
        bmat_ref[:, n_state:] = (f_re * bi + f_im * br).astype(BF16)
        cmat_ref[:n_state, :] = cre_ref[...].astype(BF16)
        cmat_ref[n_state:, :] = (-cim_ref[...]).astype(BF16)
        a_ref[0:1, :] = ab_re
        a_ref[1:2, :] = ab_im
        st_ref[...] = jnp.zeros(st_ref.shape, F32)

    n_ublk = ch // LANES
    for b in range(nb):
        for j in range(n_ublk):
            us_ref[j, b * SSM_PITCH:b * SSM_PITCH + tt, :] = u_ref[b, :, j * LANES:(j + 1) * LANES]

    def gather_step(t, carry):
        dst = pl.multiple_of(t * nb, nb)
        for j in range(n_ublk):
            utm_ref[pl.ds(dst, nb), j * LANES:(j + 1) * LANES] = us_ref[j, pl.ds(t, nb, stride=SSM_PITCH), :]
        return carry

    lax.fori_loop(0, tt, gather_step, 0, unroll=8)

    u_tm = utm_ref[...].astype(BF16)
    x_ref[:, :n_state] = _dot(u_tm, bmat_ref[:, :n_state])
    x_ref[:, n_state:] = _dot(u_tm, bmat_ref[:, n_state:])

    a_re = jnp.broadcast_to(a_ref[0:1, :], (nb, n_state))
    a_im = jnp.broadcast_to(a_ref[1:2, :], (nb, n_state))

    def scan_step(t, carry):
        x_re, x_im = carry
        row = pl.multiple_of(t * nb, nb)
        n_re = a_re * x_re - a_im * x_im + x_ref[pl.ds(row, nb), :n_state]
        n_im = a_re * x_im + a_im * x_re + x_ref[pl.ds(row, nb), n_state:]
        x_ref[pl.ds(row, nb), :n_state] = n_re
        x_ref[pl.ds(row, nb), n_state:] = n_im
        return n_re, n_im

    x_re, x_im = lax.fori_loop(0, tt, scan_step, (st_ref[:, :n_state], st_ref[:, n_state:]), unroll=4)
    st_ref[:, :n_state] = x_re
    st_ref[:, n_state:] = x_im

    y_tm = (_dot(x_ref[:, :n_state].astype(BF16), cmat_ref[:n_state, :])
            + _dot(x_ref[:, n_state:].astype(BF16), cmat_ref[n_state:, :]))
    n_yblk = y_tm.shape[1] // LANES
    for j in range(n_yblk):
        y_ref[j] = y_tm[:, j * LANES:(j + 1) * LANES]
    for b in range(nb):
        y = jnp.concatenate([y_ref[j, pl.ds(b, tt, stride=nb), :] for j in range(n_yblk)], axis=-1)
        y = y + d_ref[...] * u_ref[b]
        z = _gelu_tanh(y)
        gate = _dot(z.astype(BF16), wg_ref[...]) + bg_ref[...]
        o_ref[b] = (z * _sigmoid(gate)).astype(o_ref.dtype)


def _ssm(p1, rows, b_bd, c_bd, d_row, w_glu, b_glu):
    b, s, _ = p1.shape
    ch = w_glu.shape[0]
    n_state = rows[0].shape[1]
    consts = list(rows) + list(b_bd) + list(c_bd) + [d_row, w_glu, b_glu]
    return pl.pallas_call(
        _ssm_kernel,
        grid=(s // SSM_TILE,),
        in_specs=[pl.BlockSpec((b, SSM_TILE, ch), lambda t: (0, t, 0))] + [_const_spec(c.shape) for c in consts],
        out_specs=pl.BlockSpec((b, SSM_TILE, ch), lambda t: (0, t, 0)),
        out_shape=jax.ShapeDtypeStruct((b, s, ch), BF16),
        scratch_shapes=[pltpu.VMEM((ch, 2 * n_state), BF16),
                        pltpu.VMEM((2 * n_state, ch), BF16),
                        pltpu.VMEM((SUBLANES, n_state), F32),
                        pltpu.VMEM((b, 2 * n_state), F32),
                        pltpu.VMEM((ch // LANES, b * SSM_PITCH, LANES), F32),
                        pltpu.VMEM((SSM_TILE * b, ch), F32),
                        pltpu.VMEM((SSM_TILE * b, 2 * n_state), F32),
                        pltpu.VMEM((ch // LANES, SSM_TILE * b, LANES), F32)],
        compiler_params=_params("arbitrary"),
        name="s5_ssm",
    )(p1, *consts)


def _rope_tables(s, rot_dim, theta):
    inv = theta ** (-jnp.arange(0, rot_dim, 2, dtype=F32) / rot_dim)
    ang = jnp.arange(s, dtype=F32)[:, None] * inv[None, :]
    return jnp.cos(ang), jnp.sin(ang)


def _lane_tables(cos, sin, period):
    s, half = cos.shape
    reps = LANES // period
    one = jnp.ones((s, period - 2 * half), F32)
    zero = jnp.zeros((s, period - 2 * half), F32)
    zh = jnp.zeros((s, half), F32)
    c = jnp.tile(jnp.concatenate([cos, cos, one], axis=1), (1, reps))
    sa = jnp.tile(jnp.concatenate([-sin, zh, zero], axis=1), (1, reps))
    sb = jnp.tile(jnp.concatenate([zh, sin, zero], axis=1), (1, reps))
    return c, sa, sb


def _block_diag(blocks):
    g, r, c = blocks.shape
    eye = jnp.eye(g, dtype=blocks.dtype)
    return (eye[:, None, :, None] * blocks[:, :, None, :]).reshape(g * r, g * c)


def kernel(x, l0_mix_pre, l0_mix_post, l0_w_in, l0_lambda_q1, l0_lambda_k1, l0_lambda_q2, l0_lambda_k2, l0_subln, l0_dw_w, l0_dw_b, l0_conv_ln_g, l0_conv_ln_b, l0_w_out, l0_ffn_pre, l0_ffn_post, l0_w_gate, l0_w_up, l0_w_down, l1_mix_pre, l1_mix_post, l1_w_in, l1_a_re, l1_a_im, l1_log_dt, l1_b_re, l1_b_im, l1_c_re, l1_c_im, l1_d_skip, l1_w_glu, l1_b_glu, l1_q_norm, l1_w_uq, l1_kv_norm, l1_w_ukv, l1_w_out, l1_ffn_pre, l1_ffn_post, l1_w_gate, l1_w_up, l1_w_down):
    b, s, d = x.shape
    n = b * s
    row = lambda v: v.reshape(1, -1).astype(F32)
    bf = lambda w: w.astype(BF16)

    diff_width = 4 * LANES
    n_diff_heads = diff_width // LANES
    conv_ch = l0_dw_w.shape[1]
    ssm_ch = l1_w_glu.shape[0]
    n_groups, n_state_g = l1_a_re.shape
    q_rank = l1_q_norm.shape[0]
    kv_rank = l1_kv_norm.shape[0]
    n_mla_heads = l1_w_uq.shape[1] // (MLA_NOPE + MLA_ROPE)

    tabs_a = _lane_tables(*_rope_tables(s, DIFF_ROT, ROPE_THETA), period=DIFF_HEAD_DIM)
    x2 = x.reshape(n, d)
    w_main = bf(jnp.concatenate([l0_w_in[:, :2 * diff_width], l0_w_in[:, 3 * diff_width:]], axis=1))
    w_vt = bf(l0_w_in[:, 2 * diff_width:3 * diff_width].T)
    p0, vt_a = _l0_in(x2, row(l0_mix_pre), w_main, w_vt, tabs_a, b, s)
    p0 = p0.reshape(b, s, -1)
    lam_vecs = jnp.stack([l0_lambda_q1, l0_lambda_k1, l0_lambda_q2, l0_lambda_k2]).astype(F32)
    y_a = _diff_attn(p0, vt_a, lam_vecs, l0_subln.reshape(-1, 1).astype(F32), n_diff_heads)
    y_b = _conv_module(p0, l0_dw_w.astype(F32), row(l0_dw_b), row(l0_conv_ln_g), row(l0_conv_ln_b),
                       col0=2 * diff_width)

    pad = (-l1_w_in.shape[1]) % LANES
    w_in1 = bf(jnp.pad(l1_w_in, ((0, 0), (0, pad))))
    h2, p1 = _post(x2, [y_a.reshape(n, -1), y_b.reshape(n, -1)], bf(l0_w_out), row(l0_mix_post),
                   row(l0_ffn_pre), bf(l0_w_gate), bf(l0_w_up), bf(l0_w_down), row(l0_ffn_post),
                   nxt=(row(l1_mix_pre), w_in1))

    state_row = lambda a: a.reshape(1, -1).astype(F32)
    ssm_rows = (state_row(l1_a_re), state_row(l1_a_im),
                state_row(jnp.broadcast_to(l1_log_dt[:, None], (n_groups, n_state_g))))
    b_bd = tuple(_block_diag(jnp.swapaxes(m, 1, 2).astype(F32)) for m in (l1_b_re, l1_b_im))
    c_bd = tuple(_block_diag(jnp.swapaxes(m, 1, 2).astype(F32)) for m in (l1_c_re, l1_c_im))
    y_c = _ssm(p1.reshape(b, s, -1), ssm_rows, b_bd, c_bd, row(l1_d_skip), bf(l1_w_glu), row(l1_b_glu))

    wq = l1_w_uq.reshape(q_rank, n_mla_heads, MLA_NOPE + MLA_ROPE)
    wq = jnp.pad(wq, ((0, 0), (0, 0), (0, MLA_QK_PAD - MLA_NOPE - MLA_ROPE))).reshape(q_rank, -1)
    wkv = l1_w_ukv.reshape(kv_rank, n_mla_heads, MLA_NOPE + MLA_V)
    wk_nope = jnp.pad(wkv[:, :, :MLA_NOPE], ((0, 0), (0, 0), (0, MLA_QK_PAD - MLA_NOPE)))
    route = jnp.pad(jnp.eye(MLA_ROPE, dtype=F32), ((0, LANES - MLA_ROPE), (MLA_NOPE, MLA_QK_PAD - MLA_NOPE - MLA_ROPE)))
    wk_rope = jnp.broadcast_to(route[:, None, :], (LANES, n_mla_heads, MLA_QK_PAD))
    wk = jnp.concatenate([wk_nope, wk_rope], axis=0).reshape(kv_rank + LANES, -1)
    wv_t = wkv[:, :, MLA_NOPE:].reshape(kv_rank, -1).T
    tabs_d = _lane_tables(*_rope_tables(s, MLA_ROPE, MLA_ROPE_THETA), period=LANES)
    q_cat, k_cat, vt_d = _mla_prep(p1, row(l1_q_norm), row(l1_kv_norm), bf(wq), bf(wk), bf(wv_t), tabs_d,
                                   b, s, n_mla_heads)
    y_d = _mla_attn(q_cat.reshape(b, s, -1), k_cat.reshape(b, s, -1), vt_d, n_mla_heads)

    (out,) = _post(h2, [y_c.reshape(n, -1), y_d.reshape(n, -1)], bf(l1_w_out), row(l1_mix_post),
                   row(l1_ffn_pre), bf(l1_w_gate), bf(l1_w_up), bf(l1_w_down), row(l1_ffn_post))
    return out.reshape(b, s, d)
```

```python
import functools
import math

import jax
import jax.numpy as jnp
from jax import lax
from jax.experimental import pallas as pl
from jax.experimental.pallas import tpu as pltpu

F32 = jnp.float32
BF16 = jnp.bfloat16

LANES = 128
SUBLANES = 8
VMEM_LIMIT_BYTES = 56 * 1024 * 1024

CHUNK = 64
RMS_EPS = 1e-6
LN_EPS = 1e-5
ROPE_THETA = 500000.0
MLA_ROPE_THETA = 10000.0
DIFF_HEAD_DIM = 64
DIFF_ROT = 16
CONV_WIDTH = 31
CONV_HALO = 32
SSM_GROUP = 16
SSM_STATE = 64
MLA_NOPE = 128
MLA_ROPE = 64
MLA_V = 128
MLA_QK_PAD = 256

ROW_TILE = 512
ATTN_TILE = 256
ATTN_LOOKAHEAD = 3
ATTN_SUM_ROWS = 16
LOG2_E = math.log2(math.e)
CONV_TILE = 256
SSM_TILE = 128
SSM_PITCH = SSM_TILE + SUBLANES
FFN_CHUNK = 512
POST_ROW_GROUPS = 2


def _params(*sem):
    return pltpu.CompilerParams(dimension_semantics=sem, vmem_limit_bytes=VMEM_LIMIT_BYTES)


def _rms(x, g):
    return x * lax.rsqrt(jnp.mean(x * x, axis=-1, keepdims=True) + RMS_EPS) * g


def _sigmoid(x):
    return 1.0 / (1.0 + jnp.exp(-x))


def _dot(a, b):
    return jnp.dot(a, b, preferred_element_type=F32)


def _dot_nt(a, b):
    return lax.dot_general(a, b, (((1,), (1,)), ((), ())), preferred_element_type=F32)


def _rope_block(x, c, sa, sb, shift):
    return x * c + pltpu.roll(x, LANES - shift, 1) * sa + pltpu.roll(x, shift, 1) * sb


def _const_spec(shape):
    nd = len(shape)
    return pl.BlockSpec(shape, lambda *_: (0,) * nd, pipeline_mode=pl.Buffered(1))


def _store_vt(vt_ref, vt):
    for j in range(ROW_TILE // ATTN_TILE):
        vt_ref[0, j] = vt[:, j * ATTN_TILE:(j + 1) * ATTN_TILE].astype(vt_ref.dtype)


def _vt_out(b, seq, width):
    tiles_per_seq = seq // ROW_TILE
    per_tile = ROW_TILE // ATTN_TILE
    spec = pl.BlockSpec((1, per_tile, width, ATTN_TILE),
                        lambda i: (i // tiles_per_seq, i % tiles_per_seq, 0, 0))
    return spec, jax.ShapeDtypeStruct((b, seq // ATTN_TILE, width, ATTN_TILE), BF16)


def _l0_in_kernel(x_ref, g_ref, w_ref, wvt_ref, c_ref, sa_ref, sb_ref, o_ref, vt_ref, *,
                  n_rope_blocks, n_q_blocks, scale):
    t = _rms(x_ref[...], g_ref[...]).astype(BF16)
    p = _dot(t, w_ref[...])
    c, sa, sb = c_ref[...], sa_ref[...], sb_ref[...]
    for j in range(n_rope_blocks):
        blk = _rope_block(p[:, j * LANES:(j + 1) * LANES], c, sa, sb, DIFF_ROT // 2)
        if j < n_q_blocks:
            blk = blk * scale
        o_ref[:, j * LANES:(j + 1) * LANES] = blk.astype(o_ref.dtype)
    rest = n_rope_blocks * LANES
    o_ref[:, rest:] = p[:, rest:].astype(o_ref.dtype)
    _store_vt(vt_ref, _dot_nt(wvt_ref[...], t))


def _l0_in(x2, g_pre, w_main, w_vt, tabs, b, seq):
    n, d = x2.shape
    n_out = w_main.shape[1]
    tiles_per_seq = seq // ROW_TILE
    kern = functools.partial(_l0_in_kernel, n_rope_blocks=8, n_q_blocks=4,
                             scale=DIFF_HEAD_DIM ** -0.5 * LOG2_E)
    tab_spec = pl.BlockSpec((ROW_TILE, LANES), lambda i: (i % tiles_per_seq, 0))
    vt_spec, vt_shape = _vt_out(b, seq, w_vt.shape[0])
    return pl.pallas_call(
        kern,
        grid=(n // ROW_TILE,),
        in_specs=[pl.BlockSpec((ROW_TILE, d), lambda i: (i, 0)),
                  _const_spec((1, d)), _const_spec(w_main.shape), _const_spec(w_vt.shape),
                  tab_spec, tab_spec, tab_spec],
        out_specs=[pl.BlockSpec((ROW_TILE, n_out), lambda i: (i, 0)), vt_spec],
        out_shape=[jax.ShapeDtypeStruct((n, n_out), BF16), vt_shape],
        compiler_params=_params("parallel"),
        name="l0_in",
    )(x2, g_pre, w_main, w_vt, *tabs)


def _attn_body(q_heads, k_at, vt_at, m_ref, acc_ref, spre_ref, *, n_maps):
    i = pl.program_id(1)
    tq, dk = q_heads[0].shape
    map_width = dk // n_maps
    n_chains = len(q_heads) * n_maps
    qs = []
    for q in q_heads:
        if n_maps == 1:
            qs.append(q)
        else:
            lane = lax.broadcasted_iota(jnp.int32, q.shape, 1)
            for c in range(n_maps):
                sel = (lane >= c * map_width) & (lane < (c + 1) * map_width)
                qs.append(jnp.where(sel, q, jnp.zeros_like(q)))
    m_ref[...] = jnp.full(m_ref.shape, -jnp.inf, F32)
    acc_ref[...] = jnp.zeros(acc_ref.shape, F32)
    ones = jnp.ones((ATTN_SUM_ROWS, tq), BF16)

    def scores(kb, ch):
        start = pl.multiple_of(kb * tq, tq)
        return _dot_nt(k_at(start, ch // n_maps), qs[ch])

    def step(kb, masked, prefetch):
        if masked:
            kc = lax.broadcasted_iota(jnp.int32, (tq, tq), 0) // CHUNK
            qc = lax.broadcasted_iota(jnp.int32, (tq, tq), 1) // CHUNK
            keep = kc <= qc
        ss = [spre_ref[ch] if ch < ATTN_LOOKAHEAD else None for ch in range(n_chains)]
        new = []
        for ch in range(n_chains):
            ahead = ch + ATTN_LOOKAHEAD
            if ahead < n_chains:
                ss[ahead] = scores(kb, ahead)
            elif prefetch:
                spre_ref[ahead - n_chains] = scores(kb + 1, ahead - n_chains)
            vt = jnp.concatenate([vt_at(kb, ch // n_maps), ones], axis=0)
            s = jnp.where(keep, ss[ch], -jnp.inf) if masked else ss[ch]
            m_old = m_ref[ch]
            m_new = jnp.maximum(m_old, jnp.max(s, axis=0, keepdims=True))
            alpha = jnp.exp2(m_old - m_new)
            p = jnp.exp2(s - m_new)
            new.append((m_new, alpha * acc_ref[ch] + _dot(vt, p.astype(vt.dtype))))
        for ch, (m_new, acc_new) in enumerate(new):
            m_ref[ch] = m_new
            acc_ref[ch] = acc_new

    for ch in range(ATTN_LOOKAHEAD):
        spre_ref[ch] = scores(0, ch)

    def loop_body(kb, carry):
        step(kb, False, True)
        return carry

    lax.fori_loop(0, i, loop_body, 0)
    step(i, True, False)


def _attn_scratch(n_chains, dv):
    return [pltpu.VMEM((n_chains, 1, ATTN_TILE), F32),
            pltpu.VMEM((n_chains, dv + ATTN_SUM_ROWS, ATTN_TILE), F32),
            pltpu.VMEM((ATTN_LOOKAHEAD, ATTN_TILE, ATTN_TILE), F32)]


def _attn_out(acc_ref, ch, dv):
    return acc_ref[ch, :dv, :] / acc_ref[ch, dv:dv + 1, :]


def _diff_attn_kernel(q_ref, k_ref, vt_ref, lam_ref, sub_ref, o_ref, m_ref, acc_ref, spre_ref, *,
                      n_heads, lam_init):
    tq = q_ref.shape[1]
    _attn_body([q_ref[0, :, h * LANES:(h + 1) * LANES] for h in range(n_heads)],
               lambda start, h: k_ref[0, pl.ds(start, tq), h * LANES:(h + 1) * LANES],
               lambda kb, h: vt_ref[0, kb, h * LANES:(h + 1) * LANES, :],
               m_ref, acc_ref, spre_ref, n_maps=2)
    lv = lam_ref[...]
    lam = (jnp.exp(jnp.sum(lv[0:1] * lv[1:2], axis=-1, keepdims=True))
           - jnp.exp(jnp.sum(lv[2:3] * lv[3:4], axis=-1, keepdims=True)) + lam_init)
    for h in range(n_heads):
        o_t = _attn_out(acc_ref, 2 * h, LANES) - lam * _attn_out(acc_ref, 2 * h + 1, LANES)
        inv = lax.rsqrt(jnp.mean(o_t * o_t, axis=0, keepdims=True) + RMS_EPS)
        y_t = o_t * inv * sub_ref[...] * (1.0 - lam_init)
        o_ref[0, :, h * LANES:(h + 1) * LANES] = y_t.T.astype(o_ref.dtype)


def _diff_attn(p0, vt, lam_vecs, subln_col, n_heads):
    b, s, _ = p0.shape
    n_kb = s // ATTN_TILE
    width = n_heads * LANES
    kern = functools.partial(_diff_attn_kernel, n_heads=n_heads, lam_init=0.8 - 0.6 * math.exp(-0.3 * 0))
    return pl.pallas_call(
        kern,
        grid=(b, n_kb),
        in_specs=[pl.BlockSpec((1, ATTN_TILE, width), lambda bb, i: (bb, i, 0)),
                  pl.BlockSpec((1, s, width), lambda bb, i: (bb, 0, 1)),
                  pl.BlockSpec((1, n_kb, width, ATTN_TILE), lambda bb, i: (bb, 0, 0, 0)),
                  pl.BlockSpec(lam_vecs.shape, lambda bb, i: (0, 0)),
                  pl.BlockSpec(subln_col.shape, lambda bb, i: (0, 0))],
        out_specs=pl.BlockSpec((1, ATTN_TILE, width), lambda bb, i: (bb, i, 0)),
        out_shape=jax.ShapeDtypeStruct((b, s, width), BF16),
        scratch_shapes=_attn_scratch(2 * n_heads, LANES),
        compiler_params=_params("parallel", "arbitrary"),
        name="diff_attn",
    )(p0, p0, vt, lam_vecs, subln_col)


def _mla_attn_kernel(cq_ref, kv_ref, qn_ref, kvn_ref, wq_ref, wk_ref, wvt_ref,
                     cq_tab, saq_tab, sbq_tab, ck_tab, sak_tab, sbk_tab,
                     o_ref, m_ref, acc_ref, spre_ref, k_scr, vt_scr, *, n_heads, scale):
    seq = kv_ref.shape[1]
    tq = cq_ref.shape[1]
    half = MLA_ROPE // 2

    @pl.when(pl.program_id(1) == 0)
    def _project_keys():
        for r0 in range(0, seq, ROW_TILE):
            rows = slice(r0, r0 + ROW_TILE)
            ckv = _rms(kv_ref[0, rows, :LANES], kvn_ref[...]).astype(BF16)
            kr = _rope_block(kv_ref[0, rows, LANES:], ck_tab[rows, :], sak_tab[rows, :], sbk_tab[rows, :], half)
            k_scr[rows, :] = _dot(jnp.concatenate([ckv, kr.astype(BF16)], axis=-1), wk_ref[...]).astype(BF16)
            vt = _dot_nt(wvt_ref[...], ckv)
            for j in range(ROW_TILE // tq):
                vt_scr[r0 // tq + j] = vt[:, j * tq:(j + 1) * tq].astype(BF16)

    q = _dot(_rms(cq_ref[0], qn_ref[...]).astype(BF16), wq_ref[...])
    c, sa, sb = cq_tab[...], saq_tab[...], sbq_tab[...]
    q_heads = []
    for h in range(n_heads):
        lo = h * MLA_QK_PAD
        roped = _rope_block(q[:, lo + LANES:lo + 2 * LANES], c, sa, sb, half)
        q_heads.append((jnp.concatenate([q[:, lo:lo + LANES], roped], axis=-1) * scale).astype(BF16))

    _attn_body(q_heads,
               lambda start, h: k_scr[pl.ds(start, tq), h * MLA_QK_PAD:(h + 1) * MLA_QK_PAD],
               lambda kb, h: vt_scr[kb, h * MLA_V:(h + 1) * MLA_V, :],
               m_ref, acc_ref, spre_ref, n_maps=1)
    for h in range(n_heads):
        o_ref[0, :, h * MLA_V:(h + 1) * MLA_V] = _attn_out(acc_ref, h, MLA_V).T.astype(o_ref.dtype)


def _mla_attn(p1, q_norm, kv_norm, wq, wk, wv_t, tabs, n_heads):
    b, s, _ = p1.shape
    n_kb = s // ATTN_TILE
    kern = functools.partial(_mla_attn_kernel, n_heads=n_heads, scale=(MLA_NOPE + MLA_ROPE) ** -0.5 * LOG2_E)
    const = lambda a: pl.BlockSpec(a.shape, lambda bb, i: (0,) * a.ndim)
    q_tab = pl.BlockSpec((ATTN_TILE, LANES), lambda bb, i: (i, 0))
    return pl.pallas_call(
        kern,
        grid=(b, n_kb),
        in_specs=[pl.BlockSpec((1, ATTN_TILE, 2 * LANES), lambda bb, i: (bb, i, 1)),
                  pl.BlockSpec((1, s, 2 * LANES), lambda bb, i: (bb, 0, 2)),
                  const(q_norm), const(kv_norm), const(wq), const(wk), const(wv_t),
                  q_tab, q_tab, q_tab, const(tabs[0]), const(tabs[1]), const(tabs[2])],
        out_specs=pl.BlockSpec((1, ATTN_TILE, n_heads * MLA_V), lambda bb, i: (bb, i, 0)),
        out_shape=jax.ShapeDtypeStruct((b, s, n_heads * MLA_V), BF16),
        scratch_shapes=_attn_scratch(n_heads, MLA_V) + [
            pltpu.VMEM((s, n_heads * MLA_QK_PAD), BF16),
            pltpu.VMEM((n_kb, n_heads * MLA_V, ATTN_TILE), BF16)],
        compiler_params=_params("parallel", "arbitrary"),
        name="mla_attn",
    )(p1, p1, q_norm, kv_norm, wq, wk, wv_t, *tabs, *tabs)


def _conv_kernel(a_ref, gate_ref, ah_ref, gh_ref, w_ref, b_ref, lg_ref, lb_ref, o_ref, u_ref, ur_ref):
    tt = a_ref.shape[1]
    u_ref[CONV_HALO:, :] = a_ref[0].astype(F32) * _sigmoid(gate_ref[0].astype(F32))
    halo = ah_ref[0].astype(F32) * _sigmoid(gh_ref[0].astype(F32))
    u_ref[:CONV_HALO, :] = jnp.where(pl.program_id(1) > 0, halo, jnp.zeros_like(halo))
    rows = ur_ref.shape[1]
    for r in range(1, SUBLANES):
        ur_ref[r - 1] = u_ref[r:r + rows, :]
    acc = jnp.zeros((tt, a_ref.shape[2]), F32)
    first = CONV_HALO - (CONV_WIDTH - 1)
    for k in range(CONV_WIDTH):
        base, r = divmod(first + k, SUBLANES)
        src = u_ref if r == 0 else ur_ref.at[r - 1]
        acc = acc + src[base * SUBLANES:base * SUBLANES + tt, :] * w_ref[k:k + 1, :]
    y = acc + b_ref[...]
    mu = jnp.mean(y, axis=-1, keepdims=True)
    yc = y - mu
    yn = yc * lax.rsqrt(jnp.mean(yc * yc, axis=-1, keepdims=True) + LN_EPS) * lg_ref[...] + lb_ref[...]
    o_ref[0] = (yn * _sigmoid(yn)).astype(o_ref.dtype)


def _conv_module(p0, dw_w, dw_b, ln_g, ln_b, col0):
    b, s, _ = p0.shape
    c = dw_w.shape[1]
    a_blk, g_blk = col0 // c, col0 // c + 1
    ratio = CONV_TILE // CONV_HALO
    main = lambda blk: pl.BlockSpec((1, CONV_TILE, c), lambda bb, t: (bb, t, blk))
    halo = lambda blk: pl.BlockSpec((1, CONV_HALO, c), lambda bb, t: (bb, jnp.maximum(t * ratio - 1, 0), blk))
    vec = lambda a: pl.BlockSpec(a.shape, lambda bb, t: (0, 0))
    return pl.pallas_call(
        _conv_kernel,
        grid=(b, s // CONV_TILE),
        in_specs=[main(a_blk), main(g_blk), halo(a_blk), halo(g_blk),
                  vec(dw_w), vec(dw_b), vec(ln_g), vec(ln_b)],
        out_specs=pl.BlockSpec((1, CONV_TILE, c), lambda bb, t: (bb, t, 0)),
        out_shape=jax.ShapeDtypeStruct((b, s, c), BF16),
        scratch_shapes=[pltpu.VMEM((CONV_HALO + CONV_TILE, c), F32),
                        pltpu.VMEM((SUBLANES - 1, CONV_HALO + CONV_TILE - SUBLANES, c), F32)],
        compiler_params=_params("parallel", "parallel"),
        name="conv_module",
    )(p0, p0, p0, p0, dw_w, dw_b, ln_g, ln_b)


def _post_kernel(*refs, n_mix, with_next):
    x_ref = refs[0]
    mix_refs = refs[1:1 + n_mix]
    (wo_ref, gpost_ref, gfpre_ref, wg_ref, wu_ref, wd_ref, gfpost_ref) = refs[1 + n_mix:8 + n_mix]
    pos = 8 + n_mix
    if with_next:
        gnext_ref, wnext_ref = refs[pos:pos + 2]
        pos += 2
    h_out_ref = refs[pos]
    pos += 1
    if with_next:
        p_out_ref = refs[pos]
        pos += 1
    hid_ref = refs[pos]

    n_rows = x_ref.shape[0]
    groups = [slice(r0, r0 + n_rows // POST_ROW_GROUPS) for r0 in range(0, n_rows, n_rows // POST_ROW_GROUPS)]

    def out_proj(rows):
        y = None
        row = 0
        for r in mix_refs:
            w = r.shape[-1]
            part = _dot(r[rows, :], wo_ref[row:row + w, :])
            y = part if y is None else y + part
            row += w
        return y

    ys = [out_proj(rows) for rows in groups]
    h1s = [x_ref[rows, :] + _rms(y, gpost_ref[...]) for rows, y in zip(groups, ys)]
    ts = [_rms(h1, gfpre_ref[...]).astype(BF16) for h1 in h1s]
    d_ff = wg_ref.shape[1]
    for j in range(0, d_ff, FFN_CHUNK):
        wdt = min(FFN_CHUNK, d_ff - j)
        for rows, t in zip(groups, ts):
            gate = _dot(t, wg_ref[:, j:j + wdt])
            up = _dot(t, wu_ref[:, j:j + wdt])
            hid_ref[rows, j:j + wdt] = (gate * _sigmoid(gate) * up).astype(BF16)
    fs = [_dot(hid_ref[rows, :], wd_ref[...]) for rows in groups]
    h2s = [h1 + _rms(f, gfpost_ref[...]) for h1, f in zip(h1s, fs)]
    for rows, h2 in zip(groups, h2s):
        h_out_ref[rows, :] = h2
    if with_next:
        t2s = [_rms(h2, gnext_ref[...]).astype(BF16) for h2 in h2s]
        for rows, t2 in zip(groups, t2s):
            p_out_ref[rows, :] = _dot(t2, wnext_ref[...])


def _post(x2, mix_parts, w_out, g_post, g_fpre, w_gate, w_up, w_down, g_fpost, nxt=None):
    n, d = x2.shape
    d_ff = w_gate.shape[1]
    row = lambda a: pl.BlockSpec((ROW_TILE, a.shape[1]), lambda i: (i, 0))
    consts = [w_out, g_post, g_fpre, w_gate, w_up, w_down, g_fpost] + (list(nxt) if nxt else [])
    out_shape = [jax.ShapeDtypeStruct((n, d), F32)]
    out_specs = [pl.BlockSpec((ROW_TILE, d), lambda i: (i, 0))]
    if nxt:
        n_next = nxt[1].shape[1]
        out_shape.append(jax.ShapeDtypeStruct((n, n_next), F32))
        out_specs.append(pl.BlockSpec((ROW_TILE, n_next), lambda i: (i, 0)))
    kern = functools.partial(_post_kernel, n_mix=len(mix_parts), with_next=bool(nxt))
    return pl.pallas_call(
        kern,
        grid=(n // ROW_TILE,),
        in_specs=[row(x2)] + [row(m) for m in mix_parts] + [_const_spec(c.shape) for c in consts],
        out_specs=out_specs,
        out_shape=out_shape,
        scratch_shapes=[pltpu.VMEM((ROW_TILE, d_ff), BF16)],
        compiler_params=_params("parallel"),
        name="post_next" if nxt else "post",
    )(x2, *mix_parts, *consts)


def _gelu_tanh(x):
    return 0.5 * x * (1.0 + jnp.tanh(math.sqrt(2.0 / math.pi) * (x + 0.044715 * (x * x * x))))


def _ssm_kernel(u_ref, lr_ref, li_ref, ldt_ref, bre_ref, bim_ref, cre_ref, cim_ref, d_ref, wg_ref, bg_ref,
                o_ref, bmat_ref, cmat_ref, a_ref, st_ref, us_ref, utm_ref, x_ref, y_ref):
    nb, tt, ch = u_ref.shape
    n_state = lr_ref.shape[1]

    @pl.when(pl.program_id(0) == 0)
    def _init():
        lr, li = lr_ref[...], li_ref[...]
        dt = jnp.exp(ldt_ref[...])
        mag = jnp.exp(lr * dt)
        ab_re = mag * jnp.cos(li * dt)
        ab_im = mag * jnp.sin(li * dt)
        den = lr * lr + li * li
        n_re = ab_re - 1.0
        f_re = (n_re * lr + ab_im * li) / den
        f_im = (ab_im * lr - n_re * li) / den
        br, bi = bre_ref[...], bim_ref[...]
        bmat_ref[:, :n_state] = (f_re * br - f_im * bi).astype(BF16)
        bmat_ref[:, n_state:] = (f_re * bi + f_im * br).astype(BF16)
        cmat_ref[:n_state, :] = cre_ref[...].astype(BF16)
        cmat_ref[n_state:, :] = (-cim_ref[...]).astype(BF16)
        a_ref[0:1, :] = ab_re
        a_ref[1:2, :] = ab_im
        st_ref[...] = jnp.zeros(st_ref.shape, F32)

    n_ublk = ch // LANES
    for b in range(nb):
        for j in range(n_ublk):
            us_ref[j, b * SSM_PITCH:b * SSM_PITCH + tt, :] = u_ref[b, :, j * LANES:(j + 1) * LANES]

    def gather_step(t, carry):
        dst = pl.multiple_of(t * nb, nb)
        for j in range(n_ublk):
            utm_ref[pl.ds(dst, nb), j * LANES:(j + 1) * LANES] = us_ref[j, pl.ds(t, nb, stride=SSM_PITCH), :]
        return carry

    lax.fori_loop(0, tt, gather_step, 0, unroll=8)

    half_rows = tt * nb // 2
    for r0 in (0, half_rows):
        x_ref[r0:r0 + half_rows, :] = _dot(utm_ref[r0:r0 + half_rows, :].astype(BF16), bmat_ref[...])

    a_re = jnp.broadcast_to(a_ref[0:1, :], (nb, n_state))
    a_im = jnp.broadcast_to(a_ref[1:2, :], (nb, n_state))

    def scan_step(t, carry):
        x_re, x_im = carry
        row = pl.multiple_of(t * nb, nb)
        n_re = a_re * x_re - a_im * x_im + x_ref[pl.ds(row, nb), :n_state]
        n_im = a_re * x_im + a_im * x_re + x_ref[pl.ds(row, nb), n_state:]
        x_ref[pl.ds(row, nb), :n_state] = n_re
        x_ref[pl.ds(row, nb), n_state:] = n_im
        return n_re, n_im

    x_re, x_im = lax.fori_loop(0, tt, scan_step, (st_ref[:, :n_state], st_ref[:, n_state:]), unroll=4)
    st_ref[:, :n_state] = x_re
    st_ref[:, n_state:] = x_im

    n_yblk = ch // LANES
    for r0 in (0, half_rows):
        y_tm = _dot(x_ref[r0:r0 + half_rows, :].astype(BF16), cmat_ref[...])
        for j in range(n_yblk):
            y_ref[j, r0:r0 + half_rows, :] = y_tm[:, j * LANES:(j + 1) * LANES]
    for b in range(nb):
        y = jnp.concatenate([y_ref[j, pl.ds(b, tt, stride=nb), :] for j in range(n_yblk)], axis=-1)
        y = y + d_ref[...] * u_ref[b]
        z = _gelu_tanh(y)
        gate = _dot(z.astype(BF16), wg_ref[...]) + bg_ref[...]
        o_ref[b] = (z * _sigmoid(gate)).astype(o_ref.dtype)


def _ssm(p1, rows, b_bd, c_bd, d_row, w_glu, b_glu):
    b, s, _ = p1.shape
    ch = w_glu.shape[0]
    n_state = rows[0].shape[1]
    consts = list(rows) + list(b_bd) + list(c_bd) + [d_row, w_glu, b_glu]
    return pl.pallas_call(
        _ssm_kernel,
        grid=(s // SSM_TILE,),
        in_specs=[pl.BlockSpec((b, SSM_TILE, ch), lambda t: (0, t, 0))] + [_const_spec(c.shape) for c in consts],
        out_specs=pl.BlockSpec((b, SSM_TILE, ch), lambda t: (0, t, 0)),
        out_shape=jax.ShapeDtypeStruct((b, s, ch), BF16),
        scratch_shapes=[pltpu.VMEM((ch, 2 * n_state), BF16),
                        pltpu.VMEM((2 * n_state, ch), BF16),
                        pltpu.VMEM((SUBLANES, n_state), F32),
                        pltpu.VMEM((b, 2 * n_state), F32),
                        pltpu.VMEM((ch // LANES, b * SSM_PITCH, LANES), F32),
                        pltpu.VMEM((SSM_TILE * b, ch), F32),
                        pltpu.VMEM((SSM_TILE * b, 2 * n_state), F32),
                        pltpu.VMEM((ch // LANES, SSM_TILE * b, LANES), F32)],
        compiler_params=_params("arbitrary"),
        name="s5_ssm",
    )(p1, *consts)


def _rope_tables(s, rot_dim, theta):
    inv = theta ** (-jnp.arange(0, rot_dim, 2, dtype=F32) / rot_dim)
    ang = jnp.arange(s, dtype=F32)[:, None] * inv[None, :]
    return jnp.cos(ang), jnp.sin(ang)


def _lane_tables(cos, sin, period):
    s, half = cos.shape
    reps = LANES // period
    one = jnp.ones((s, period - 2 * half), F32)
    zero = jnp.zeros((s, period - 2 * half), F32)
    zh = jnp.zeros((s, half), F32)
    c = jnp.tile(jnp.concatenate([cos, cos, one], axis=1), (1, reps))
    sa = jnp.tile(jnp.concatenate([-sin, zh, zero], axis=1), (1, reps))
    sb = jnp.tile(jnp.concatenate([zh, sin, zero], axis=1), (1, reps))
    return c, sa, sb


def _block_diag(blocks):
    g, r, c = blocks.shape
    eye = jnp.eye(g, dtype=blocks.dtype)
    return (eye[:, None, :, None] * blocks[:, :, None, :]).reshape(g * r, g * c)


def kernel(x, l0_mix_pre, l0_mix_post, l0_w_in, l0_lambda_q1, l0_lambda_k1, l0_lambda_q2, l0_lambda_k2, l0_subln, l0_dw_w, l0_dw_b, l0_conv_ln_g, l0_conv_ln_b, l0_w_out, l0_ffn_pre, l0_ffn_post, l0_w_gate, l0_w_up, l0_w_down, l1_mix_pre, l1_mix_post, l1_w_in, l1_a_re, l1_a_im, l1_log_dt, l1_b_re, l1_b_im, l1_c_re, l1_c_im, l1_d_skip, l1_w_glu, l1_b_glu, l1_q_norm, l1_w_uq, l1_kv_norm, l1_w_ukv, l1_w_out, l1_ffn_pre, l1_ffn_post, l1_w_gate, l1_w_up, l1_w_down):
    b, s, d = x.shape
    n = b * s
    row = lambda v: v.reshape(1, -1).astype(F32)
    bf = lambda w: w.astype(BF16)

    diff_width = 4 * LANES
    n_diff_heads = diff_width // LANES
    conv_ch = l0_dw_w.shape[1]
    ssm_ch = l1_w_glu.shape[0]
    n_groups, n_state_g = l1_a_re.shape
    q_rank = l1_q_norm.shape[0]
    kv_rank = l1_kv_norm.shape[0]
    n_mla_heads = l1_w_uq.shape[1] // (MLA_NOPE + MLA_ROPE)

    tabs_a = _lane_tables(*_rope_tables(s, DIFF_ROT, ROPE_THETA), period=DIFF_HEAD_DIM)
    x2 = x.reshape(n, d)
    w_main = bf(jnp.concatenate([l0_w_in[:, :2 * diff_width], l0_w_in[:, 3 * diff_width:]], axis=1))
    w_vt = bf(l0_w_in[:, 2 * diff_width:3 * diff_width].T)
    p0, vt_a = _l0_in(x2, row(l0_mix_pre), w_main, w_vt, tabs_a, b, s)
    p0 = p0.reshape(b, s, -1)
    lam_vecs = jnp.stack([l0_lambda_q1, l0_lambda_k1, l0_lambda_q2, l0_lambda_k2]).astype(F32)
    y_a = _diff_attn(p0, vt_a, lam_vecs, l0_subln.reshape(-1, 1).astype(F32), n_diff_heads)
    y_b = _conv_module(p0, l0_dw_w.astype(F32), row(l0_dw_b), row(l0_conv_ln_g), row(l0_conv_ln_b),
                       col0=2 * diff_width)

    pad = (-l1_w_in.shape[1]) % LANES
    w_in1 = bf(jnp.pad(l1_w_in, ((0, 0), (0, pad))))
    h2, p1 = _post(x2, [y_a.reshape(n, -1), y_b.reshape(n, -1)], bf(l0_w_out), row(l0_mix_post),
                   row(l0_ffn_pre), bf(l0_w_gate), bf(l0_w_up), bf(l0_w_down), row(l0_ffn_post),
                   nxt=(row(l1_mix_pre), w_in1))

    state_row = lambda a: a.reshape(1, -1).astype(F32)
    ssm_rows = (state_row(l1_a_re), state_row(l1_a_im),
                state_row(jnp.broadcast_to(l1_log_dt[:, None], (n_groups, n_state_g))))
    b_bd = tuple(_block_diag(jnp.swapaxes(m, 1, 2).astype(F32)) for m in (l1_b_re, l1_b_im))
    c_bd = tuple(_block_diag(jnp.swapaxes(m, 1, 2).astype(F32)) for m in (l1_c_re, l1_c_im))
    y_c = _ssm(p1.reshape(b, s, -1), ssm_rows, b_bd, c_bd, row(l1_d_skip), bf(l1_w_glu), row(l1_b_glu))

    wq = l1_w_uq.reshape(q_rank, n_mla_heads, MLA_NOPE + MLA_ROPE)
    wq = jnp.pad(wq, ((0, 0), (0, 0), (0, MLA_QK_PAD - MLA_NOPE - MLA_ROPE))).reshape(q_rank, -1)
    wkv = l1_w_ukv.reshape(kv_rank, n_mla_heads, MLA_NOPE + MLA_V)
    wk_nope = jnp.pad(wkv[:, :, :MLA_NOPE], ((0, 0), (0, 0), (0, MLA_QK_PAD - MLA_NOPE)))
    route = jnp.pad(jnp.eye(MLA_ROPE, dtype=F32), ((0, LANES - MLA_ROPE), (MLA_NOPE, MLA_QK_PAD - MLA_NOPE - MLA_ROPE)))
    wk_rope = jnp.broadcast_to(route[:, None, :], (LANES, n_mla_heads, MLA_QK_PAD))
    wk = jnp.concatenate([wk_nope, wk_rope], axis=0).reshape(kv_rank + LANES, -1)
    wv_t = wkv[:, :, MLA_NOPE:].reshape(kv_rank, -1).T
    tabs_d = _lane_tables(*_rope_tables(s, MLA_ROPE, MLA_ROPE_THETA), period=LANES)
    y_d = _mla_attn(p1.reshape(b, s, -1), row(l1_q_norm), row(l1_kv_norm), bf(wq), bf(wk), bf(wv_t), tabs_d,
                    n_mla_heads)

    (out,) = _post(h2, [y_c.reshape(n, -1), y_d.reshape(n, -1)], bf(l1_w_out), row(l1_mix_post),
                   row(l1_ffn_pre), bf(l1_w_gate), bf(l1_w_up), bf(l1_w_down), row(l1_ffn_post))
    return out.reshape(b, s, d)
```

```python
import functools
import math

import jax
import jax.numpy as jnp
from jax import lax
from jax.experimental import pallas as pl
from jax.experimental.pallas import tpu as pltpu

F32 = jnp.float32
BF16 = jnp.bfloat16

LANES = 128
SUBLANES = 8
VMEM_LIMIT_BYTES = 56 * 1024 * 1024

CHUNK = 64
RMS_EPS = 1e-6
LN_EPS = 1e-5
ROPE_THETA = 500000.0
MLA_ROPE_THETA = 10000.0
DIFF_HEAD_DIM = 64
DIFF_ROT = 16
CONV_WIDTH = 31
CONV_HALO = 32
SSM_GROUP = 16
SSM_STATE = 64
MLA_NOPE = 128
MLA_ROPE = 64
MLA_V = 128
MLA_QK_PAD = 256

ROW_TILE = 512
ATTN_TILE = 256
ATTN_LOOKAHEAD = 3
ATTN_SUM_ROWS = 16
LOG2_E = math.log2(math.e)
CONV_TILE = 256
SSM_TILE = 128
SSM_PITCH = SSM_TILE + SUBLANES
FFN_CHUNK = 512
POST_ROW_GROUPS = 2


def _params(*sem):
    return pltpu.CompilerParams(dimension_semantics=sem, vmem_limit_bytes=VMEM_LIMIT_BYTES)


def _rms(x, g):
    return x * lax.rsqrt(jnp.mean(x * x, axis=-1, keepdims=True) + RMS_EPS) * g


def _sigmoid(x):
    return 1.0 / (1.0 + jnp.exp(-x))


def _dot(a, b):
    return jnp.dot(a, b, preferred_element_type=F32)


def _dot_nt(a, b):
    return lax.dot_general(a, b, (((1,), (1,)), ((), ())), preferred_element_type=F32)


def _rope_block(x, c, sa, sb, shift):
    return x * c + pltpu.roll(x, LANES - shift, 1) * sa + pltpu.roll(x, shift, 1) * sb


def _const_spec(shape):
    nd = len(shape)
    return pl.BlockSpec(shape, lambda *_: (0,) * nd, pipeline_mode=pl.Buffered(1))


def _store_vt(vt_ref, vt):
    for j in range(ROW_TILE // ATTN_TILE):
        vt_ref[0, j] = vt[:, j * ATTN_TILE:(j + 1) * ATTN_TILE].astype(vt_ref.dtype)


def _vt_out(b, seq, width):
    tiles_per_seq = seq // ROW_TILE
    per_tile = ROW_TILE // ATTN_TILE
    spec = pl.BlockSpec((1, per_tile, width, ATTN_TILE),
                        lambda i: (i // tiles_per_seq, i % tiles_per_seq, 0, 0))
    return spec, jax.ShapeDtypeStruct((b, seq // ATTN_TILE, width, ATTN_TILE), BF16)


def _l0_in_kernel(x_ref, g_ref, w_ref, wt_ref, c_ref, sa_ref, sb_ref, cosq_ref, sinq_ref, o_ref, qvt_ref, *,
                  n_k_blocks, q_rows, scale):
    t = _rms(x_ref[...], g_ref[...]).astype(BF16)
    p = _dot(t, w_ref[...])
    c, sa, sb = c_ref[...], sa_ref[...], sb_ref[...]
    half = DIFF_ROT // 2
    for j in range(n_k_blocks):
        blk = _rope_block(p[:, j * LANES:(j + 1) * LANES], c, sa, sb, half)
        o_ref[:, j * LANES:(j + 1) * LANES] = blk.astype(o_ref.dtype)
    rest = n_k_blocks * LANES
    o_ref[:, rest:] = p[:, rest:].astype(o_ref.dtype)
    qv_t = _dot_nt(wt_ref[...], t)
    cos_t, sin_t = cosq_ref[...], sinq_ref[...]
    pieces = []
    for lo in range(0, q_rows, DIFF_HEAD_DIM):
        x1, x2 = qv_t[lo:lo + half, :], qv_t[lo + half:lo + 2 * half, :]
        pieces += [(x1 * cos_t - x2 * sin_t) * scale, (x2 * cos_t + x1 * sin_t) * scale,
                   qv_t[lo + 2 * half:lo + DIFF_HEAD_DIM, :] * scale]
    _store_vt(qvt_ref, jnp.concatenate(pieces + [qv_t[q_rows:, :]], axis=0))


def _l0_in(x2, g_pre, w_main, w_t, tabs, q_tabs_t, b, seq, q_rows):
    n, d = x2.shape
    n_out = w_main.shape[1]
    tiles_per_seq = seq // ROW_TILE
    kern = functools.partial(_l0_in_kernel, n_k_blocks=q_rows // LANES, q_rows=q_rows,
                             scale=DIFF_HEAD_DIM ** -0.5 * LOG2_E)
    tab_spec = pl.BlockSpec((ROW_TILE, LANES), lambda i: (i % tiles_per_seq, 0))
    qtab_spec = pl.BlockSpec((q_tabs_t[0].shape[0], ROW_TILE), lambda i: (0, i % tiles_per_seq))
    vt_spec, vt_shape = _vt_out(b, seq, w_t.shape[0])
    return pl.pallas_call(
        kern,
        grid=(n // ROW_TILE,),
        in_specs=[pl.BlockSpec((ROW_TILE, d), lambda i: (i, 0)),
                  _const_spec((1, d)), _const_spec(w_main.shape), _const_spec(w_t.shape),
                  tab_spec, tab_spec, tab_spec, qtab_spec, qtab_spec],
        out_specs=[pl.BlockSpec((ROW_TILE, n_out), lambda i: (i, 0)), vt_spec],
        out_shape=[jax.ShapeDtypeStruct((n, n_out), BF16), vt_shape],
        compiler_params=_params("parallel"),
        name="l0_in",
    )(x2, g_pre, w_main, w_t, *tabs, *q_tabs_t)


def _attn_body(q_heads, k_at, vt_at, m_ref, acc_ref, spre_ref, *, n_maps, q_transposed=False):
    i = pl.program_id(1)
    dk_axis = 0 if q_transposed else 1
    tq, dk = q_heads[0].shape[1 - dk_axis], q_heads[0].shape[dk_axis]
    map_width = dk // n_maps
    n_chains = len(q_heads) * n_maps
    qs = []
    for q in q_heads:
        if n_maps == 1:
            qs.append(q)
        else:
            pos = lax.broadcasted_iota(jnp.int32, q.shape, dk_axis)
            for c in range(n_maps):
                sel = (pos >= c * map_width) & (pos < (c + 1) * map_width)
                qs.append(jnp.where(sel, q, jnp.zeros_like(q)))
    m_ref[...] = jnp.full(m_ref.shape, -jnp.inf, F32)
    acc_ref[...] = jnp.zeros(acc_ref.shape, F32)
    ones = jnp.ones((ATTN_SUM_ROWS, tq), BF16)

    def scores(kb, ch):
        start = pl.multiple_of(kb * tq, tq)
        k = k_at(start, ch // n_maps)
        return _dot(k, qs[ch]) if q_transposed else _dot_nt(k, qs[ch])

    def step(kb, masked, prefetch):
        if masked:
            kc = lax.broadcasted_iota(jnp.int32, (tq, tq), 0) // CHUNK
            qc = lax.broadcasted_iota(jnp.int32, (tq, tq), 1) // CHUNK
            keep = kc <= qc
        ss = [spre_ref[ch] if ch < ATTN_LOOKAHEAD else None for ch in range(n_chains)]
        new = []
        for ch in range(n_chains):
            ahead = ch + ATTN_LOOKAHEAD
            if ahead < n_chains:
                ss[ahead] = scores(kb, ahead)
            elif prefetch:
                spre_ref[ahead - n_chains] = scores(kb + 1, ahead - n_chains)
            vt = jnp.concatenate([vt_at(kb, ch // n_maps), ones], axis=0)
            s = jnp.where(keep, ss[ch], -jnp.inf) if masked else ss[ch]
            m_old = m_ref[ch]
            m_new = jnp.maximum(m_old, jnp.max(s, axis=0, keepdims=True))
            alpha = jnp.exp2(m_old - m_new)
            p = jnp.exp2(s - m_new)
            new.append((m_new, alpha * acc_ref[ch] + _dot(vt, p.astype(vt.dtype))))
        for ch, (m_new, acc_new) in enumerate(new):
            m_ref[ch] = m_new
            acc_ref[ch] = acc_new

    for ch in range(ATTN_LOOKAHEAD):
        spre_ref[ch] = scores(0, ch)

    def loop_body(kb, carry):
        step(kb, False, True)
        return carry

    lax.fori_loop(0, i, loop_body, 0)
    step(i, True, False)


def _attn_scratch(n_chains, dv):
    return [pltpu.VMEM((n_chains, 1, ATTN_TILE), F32),
            pltpu.VMEM((n_chains, dv + ATTN_SUM_ROWS, ATTN_TILE), F32),
            pltpu.VMEM((ATTN_LOOKAHEAD, ATTN_TILE, ATTN_TILE), F32)]


def _attn_out(acc_ref, ch, dv):
    return acc_ref[ch, :dv, :] / acc_ref[ch, dv:dv + 1, :]


def _diff_attn_kernel(qt_ref, k_ref, vt_ref, lam_ref, sub_ref, o_ref, m_ref, acc_ref, spre_ref, *,
                      n_heads, lam_init):
    tq = qt_ref.shape[3]
    _attn_body([qt_ref[0, 0, h * LANES:(h + 1) * LANES, :] for h in range(n_heads)],
               lambda start, h: k_ref[0, pl.ds(start, tq), h * LANES:(h + 1) * LANES],
               lambda kb, h: vt_ref[0, kb, h * LANES:(h + 1) * LANES, :],
               m_ref, acc_ref, spre_ref, n_maps=2, q_transposed=True)
    lv = lam_ref[...]
    lam = (jnp.exp(jnp.sum(lv[0:1] * lv[1:2], axis=-1, keepdims=True))
           - jnp.exp(jnp.sum(lv[2:3] * lv[3:4], axis=-1, keepdims=True)) + lam_init)
    for h in range(n_heads):
        o_t = _attn_out(acc_ref, 2 * h, LANES) - lam * _attn_out(acc_ref, 2 * h + 1, LANES)
        inv = lax.rsqrt(jnp.mean(o_t * o_t, axis=0, keepdims=True) + RMS_EPS)
        y_t = o_t * inv * sub_ref[...] * (1.0 - lam_init)
        o_ref[0, :, h * LANES:(h + 1) * LANES] = y_t.T.astype(o_ref.dtype)


def _diff_attn(p0, qvt, lam_vecs, subln_col, n_heads):
    b, s, _ = p0.shape
    n_kb = s // ATTN_TILE
    width = n_heads * LANES
    kern = functools.partial(_diff_attn_kernel, n_heads=n_heads, lam_init=0.8 - 0.6 * math.exp(-0.3 * 0))
    return pl.pallas_call(
        kern,
        grid=(b, n_kb),
        in_specs=[pl.BlockSpec((1, 1, width, ATTN_TILE), lambda bb, i: (bb, i, 0, 0)),
                  pl.BlockSpec((1, s, width), lambda bb, i: (bb, 0, 0)),
                  pl.BlockSpec((1, n_kb, width, ATTN_TILE), lambda bb, i: (bb, 0, 1, 0)),
                  pl.BlockSpec(lam_vecs.shape, lambda bb, i: (0, 0)),
                  pl.BlockSpec(subln_col.shape, lambda bb, i: (0, 0))],
        out_specs=pl.BlockSpec((1, ATTN_TILE, width), lambda bb, i: (bb, i, 0)),
        out_shape=jax.ShapeDtypeStruct((b, s, width), BF16),
        scratch_shapes=_attn_scratch(2 * n_heads, LANES),
        compiler_params=_params("parallel", "arbitrary"),
        name="diff_attn",
    )(qvt, p0, qvt, lam_vecs, subln_col)


def _mla_attn_kernel(cq_ref, kv_ref, qn_ref, kvn_ref, wqt_ref, wk_ref, wvt_ref,
                     cosq_tab, sinq_tab, ck_tab, sak_tab, sbk_tab,
                     o_ref, m_ref, acc_ref, spre_ref, k_scr, vt_scr, *, n_heads, scale):
    seq = kv_ref.shape[1]
    tq = cq_ref.shape[1]
    half = MLA_ROPE // 2

    @pl.when(pl.program_id(1) == 0)
    def _project_keys():
        for r0 in range(0, seq, ROW_TILE):
            rows = slice(r0, r0 + ROW_TILE)
            ckv = _rms(kv_ref[0, rows, :LANES], kvn_ref[...]).astype(BF16)
            kr = _rope_block(kv_ref[0, rows, LANES:], ck_tab[rows, :], sak_tab[rows, :], sbk_tab[rows, :], half)
            k_scr[rows, :] = _dot(jnp.concatenate([ckv, kr.astype(BF16)], axis=-1), wk_ref[...]).astype(BF16)
            vt = _dot_nt(wvt_ref[...], ckv)
            for j in range(ROW_TILE // tq):
                vt_scr[r0 // tq + j] = vt[:, j * tq:(j + 1) * tq].astype(BF16)

    q_t = _dot_nt(wqt_ref[...], _rms(cq_ref[0], qn_ref[...]).astype(BF16))
    cos_t, sin_t = cosq_tab[...], sinq_tab[...]
    q_heads = []
    for h in range(n_heads):
        lo = h * MLA_QK_PAD
        x1 = q_t[lo + MLA_NOPE:lo + MLA_NOPE + half, :]
        x2 = q_t[lo + MLA_NOPE + half:lo + MLA_NOPE + 2 * half, :]
        q_h = jnp.concatenate([q_t[lo:lo + MLA_NOPE, :], x1 * cos_t - x2 * sin_t, x2 * cos_t + x1 * sin_t,
                               q_t[lo + MLA_NOPE + 2 * half:lo + MLA_QK_PAD, :]], axis=0)
        q_heads.append((q_h * scale).astype(BF16))

    _attn_body(q_heads,
               lambda start, h: k_scr[pl.ds(start, tq), h * MLA_QK_PAD:(h + 1) * MLA_QK_PAD],
               lambda kb, h: vt_scr[kb, h * MLA_V:(h + 1) * MLA_V, :],
               m_ref, acc_ref, spre_ref, n_maps=1, q_transposed=True)
    for h in range(n_heads):
        o_ref[0, :, h * MLA_V:(h + 1) * MLA_V] = _attn_out(acc_ref, h, MLA_V).T.astype(o_ref.dtype)


def _mla_attn(p1, q_norm, kv_norm, wq_t, wk, wv_t, q_tabs_t, tabs, n_heads):
    b, s, _ = p1.shape
    n_kb = s // ATTN_TILE
    kern = functools.partial(_mla_attn_kernel, n_heads=n_heads, scale=(MLA_NOPE + MLA_ROPE) ** -0.5 * LOG2_E)
    const = lambda a: pl.BlockSpec(a.shape, lambda bb, i: (0,) * a.ndim)
    q_tab = pl.BlockSpec((q_tabs_t[0].shape[0], ATTN_TILE), lambda bb, i: (0, i))
    return pl.pallas_call(
        kern,
        grid=(b, n_kb),
        in_specs=[pl.BlockSpec((1, ATTN_TILE, 2 * LANES), lambda bb, i: (bb, i, 1)),
                  pl.BlockSpec((1, s, 2 * LANES), lambda bb, i: (bb, 0, 2)),
                  const(q_norm), const(kv_norm), const(wq_t), const(wk), const(wv_t),
                  q_tab, q_tab, const(tabs[0]), const(tabs[1]), const(tabs[2])],
        out_specs=pl.BlockSpec((1, ATTN_TILE, n_heads * MLA_V), lambda bb, i: (bb, i, 0)),
        out_shape=jax.ShapeDtypeStruct((b, s, n_heads * MLA_V), BF16),
        scratch_shapes=_attn_scratch(n_heads, MLA_V) + [
            pltpu.VMEM((s, n_heads * MLA_QK_PAD), BF16),
            pltpu.VMEM((n_kb, n_heads * MLA_V, ATTN_TILE), BF16)],
        compiler_params=_params("parallel", "arbitrary"),
        name="mla_attn",
    )(p1, p1, q_norm, kv_norm, wq_t, wk, wv_t, *q_tabs_t, *tabs)


def _conv_kernel(a_ref, gate_ref, ah_ref, gh_ref, w_ref, b_ref, lg_ref, lb_ref, o_ref, u_ref, ur_ref):
    tt = a_ref.shape[1]
    u_ref[CONV_HALO:, :] = a_ref[0].astype(F32) * _sigmoid(gate_ref[0].astype(F32))
    halo = ah_ref[0].astype(F32) * _sigmoid(gh_ref[0].astype(F32))
    u_ref[:CONV_HALO, :] = jnp.where(pl.program_id(1) > 0, halo, jnp.zeros_like(halo))
    rows = ur_ref.shape[1]
    for r in range(1, SUBLANES):
        ur_ref[r - 1] = u_ref[r:r + rows, :]
    acc = jnp.zeros((tt, a_ref.shape[2]), F32)
    first = CONV_HALO - (CONV_WIDTH - 1)
    for k in range(CONV_WIDTH):
        base, r = divmod(first + k, SUBLANES)
        src = u_ref if r == 0 else ur_ref.at[r - 1]
        acc = acc + src[base * SUBLANES:base * SUBLANES + tt, :] * w_ref[k:k + 1, :]
    y = acc + b_ref[...]
    mu = jnp.mean(y, axis=-1, keepdims=True)
    yc = y - mu
    yn = yc * lax.rsqrt(jnp.mean(yc * yc, axis=-1, keepdims=True) + LN_EPS) * lg_ref[...] + lb_ref[...]
    o_ref[0] = (yn * _sigmoid(yn)).astype(o_ref.dtype)


def _conv_module(p0, dw_w, dw_b, ln_g, ln_b, col0):
    b, s, _ = p0.shape
    c = dw_w.shape[1]
    a_blk, g_blk = col0 // c, col0 // c + 1
    ratio = CONV_TILE // CONV_HALO
    main = lambda blk: pl.BlockSpec((1, CONV_TILE, c), lambda bb, t: (bb, t, blk))
    halo = lambda blk: pl.BlockSpec((1, CONV_HALO, c), lambda bb, t: (bb, jnp.maximum(t * ratio - 1, 0), blk))
    vec = lambda a: pl.BlockSpec(a.shape, lambda bb, t: (0, 0))
    return pl.pallas_call(
        _conv_kernel,
        grid=(b, s // CONV_TILE),
        in_specs=[main(a_blk), main(g_blk), halo(a_blk), halo(g_blk),
                  vec(dw_w), vec(dw_b), vec(ln_g), vec(ln_b)],
        out_specs=pl.BlockSpec((1, CONV_TILE, c), lambda bb, t: (bb, t, 0)),
        out_shape=jax.ShapeDtypeStruct((b, s, c), BF16),
        scratch_shapes=[pltpu.VMEM((CONV_HALO + CONV_TILE, c), F32),
                        pltpu.VMEM((SUBLANES - 1, CONV_HALO + CONV_TILE - SUBLANES, c), F32)],
        compiler_params=_params("parallel", "parallel"),
        name="conv_module",
    )(p0, p0, p0, p0, dw_w, dw_b, ln_g, ln_b)


def _post_kernel(*refs, n_mix, with_next):
    x_ref = refs[0]
    mix_refs = refs[1:1 + n_mix]
    (wo_ref, gpost_ref, gfpre_ref, wg_ref, wu_ref, wd_ref, gfpost_ref) = refs[1 + n_mix:8 + n_mix]
    pos = 8 + n_mix
    if with_next:
        gnext_ref, wnext_ref = refs[pos:pos + 2]
        pos += 2
    h_out_ref = refs[pos]
    pos += 1
    if with_next:
        p_out_ref = refs[pos]
        pos += 1
    hid_ref = refs[pos]

    n_rows = x_ref.shape[0]
    groups = [slice(r0, r0 + n_rows // POST_ROW_GROUPS) for r0 in range(0, n_rows, n_rows // POST_ROW_GROUPS)]

    def out_proj(rows):
        y = None
        row = 0
        for r in mix_refs:
            w = r.shape[-1]
            part = _dot(r[rows, :], wo_ref[row:row + w, :])
            y = part if y is None else y + part
            row += w
        return y

    ys = [out_proj(rows) for rows in groups]
    h1s = [x_ref[rows, :] + _rms(y, gpost_ref[...]) for rows, y in zip(groups, ys)]
    ts = [_rms(h1, gfpre_ref[...]).astype(BF16) for h1 in h1s]
    d_ff = wg_ref.shape[1]
    for j in range(0, d_ff, FFN_CHUNK):
        wdt = min(FFN_CHUNK, d_ff - j)
        for rows, t in zip(groups, ts):
            gate = _dot(t, wg_ref[:, j:j + wdt])
            up = _dot(t, wu_ref[:, j:j + wdt])
            hid_ref[rows, j:j + wdt] = (gate * _sigmoid(gate) * up).astype(BF16)
    fs = [_dot(hid_ref[rows, :], wd_ref[...]) for rows in groups]
    h2s = [h1 + _rms(f, gfpost_ref[...]) for h1, f in zip(h1s, fs)]
    for rows, h2 in zip(groups, h2s):
        h_out_ref[rows, :] = h2
    if with_next:
        t2s = [_rms(h2, gnext_ref[...]).astype(BF16) for h2 in h2s]
        for rows, t2 in zip(groups, t2s):
            p_out_ref[rows, :] = _dot(t2, wnext_ref[...])


def _post(x2, mix_parts, w_out, g_post, g_fpre, w_gate, w_up, w_down, g_fpost, nxt=None):
    n, d = x2.shape
    d_ff = w_gate.shape[1]
    row = lambda a: pl.BlockSpec((ROW_TILE, a.shape[1]), lambda i: (i, 0))
    consts = [w_out, g_post, g_fpre, w_gate, w_up, w_down, g_fpost] + (list(nxt) if nxt else [])
    out_shape = [jax.ShapeDtypeStruct((n, d), F32)]
    out_specs = [pl.BlockSpec((ROW_TILE, d), lambda i: (i, 0))]
    if nxt:
        n_next = nxt[1].shape[1]
        out_shape.append(jax.ShapeDtypeStruct((n, n_next), F32))
        out_specs.append(pl.BlockSpec((ROW_TILE, n_next), lambda i: (i, 0)))
    kern = functools.partial(_post_kernel, n_mix=len(mix_parts), with_next=bool(nxt))
    return pl.pallas_call(
        kern,
        grid=(n // ROW_TILE,),
        in_specs=[row(x2)] + [row(m) for m in mix_parts] + [_const_spec(c.shape) for c in consts],
        out_specs=out_specs,
        out_shape=out_shape,
        scratch_shapes=[pltpu.VMEM((ROW_TILE, d_ff), BF16)],
        compiler_params=_params("parallel"),
        name="post_next" if nxt else "post",
    )(x2, *mix_parts, *consts)


def _gelu_tanh(x):
    return 0.5 * x * (1.0 + jnp.tanh(math.sqrt(2.0 / math.pi) * (x + 0.044715 * (x * x * x))))


def _ssm_kernel(u_ref, lr_ref, li_ref, ldt_ref, bre_ref, bim_ref, cre_ref, cim_ref, d_ref, wg_ref, bg_ref,
                o_ref, bmat_ref, cmat_ref, a_ref, st_ref, us_ref, utm_ref, x_ref, y_ref):
    nb, tt, ch = u_ref.shape
    n_state = lr_ref.shape[1]

    @pl.when(pl.program_id(0) == 0)
    def _init():
        lr, li = lr_ref[...], li_ref[...]
        dt = jnp.exp(ldt_ref[...])
        mag = jnp.exp(lr * dt)
        ab_re = mag * jnp.cos(li * dt)
        ab_im = mag * jnp.sin(li * dt)
        den = lr * lr + li * li
        n_re = ab_re - 1.0
        f_re = (n_re * lr + ab_im * li) / den
        f_im = (ab_im * lr - n_re * li) / den
        br, bi = bre_ref[...], bim_ref[...]
        bmat_ref[:, :n_state] = (f_re * br - f_im * bi).astype(BF16)
        bmat_ref[:, n_state:] = (f_re * bi + f_im * br).astype(BF16)
        cmat_ref[:n_state, :] = cre_ref[...].astype(BF16)
        cmat_ref[n_state:, :] = (-cim_ref[...]).astype(BF16)
        a_ref[0:1, :] = ab_re
        a_ref[1:2, :] = ab_im
        st_ref[...] = jnp.zeros(st_ref.shape, F32)

    n_ublk = ch // LANES
    for b in range(nb):
        for j in range(n_ublk):
            us_ref[j, b * SSM_PITCH:b * SSM_PITCH + tt, :] = u_ref[b, :, j * LANES:(j + 1) * LANES]

    def gather_step(t, carry):
        dst = pl.multiple_of(t * nb, nb)
        for j in range(n_ublk):
            utm_ref[pl.ds(dst, nb), j * LANES:(j + 1) * LANES] = us_ref[j, pl.ds(t, nb, stride=SSM_PITCH), :]
        return carry

    lax.fori_loop(0, tt, gather_step, 0, unroll=8)

    half_rows = tt * nb // 2
    for r0 in (0, half_rows):
        x_ref[r0:r0 + half_rows, :] = _dot(utm_ref[r0:r0 + half_rows, :].astype(BF16), bmat_ref[...])

    a_re = jnp.broadcast_to(a_ref[0:1, :], (nb, n_state))
    a_im = jnp.broadcast_to(a_ref[1:2, :], (nb, n_state))

    def scan_step(t, carry):
        x_re, x_im = carry
        row = pl.multiple_of(t * nb, nb)
        n_re = a_re * x_re - a_im * x_im + x_ref[pl.ds(row, nb), :n_state]
        n_im = a_re * x_im + a_im * x_re + x_ref[pl.ds(row, nb), n_state:]
        x_ref[pl.ds(row, nb), :n_state] = n_re
        x_ref[pl.ds(row, nb), n_state:] = n_im
        return n_re, n_im

    x_re, x_im = lax.fori_loop(0, tt, scan_step, (st_ref[:, :n_state], st_ref[:, n_state:]), unroll=4)
    st_ref[:, :n_state] = x_re
    st_ref[:, n_state:] = x_im

    n_yblk = ch // LANES
    for r0 in (0, half_rows):
        y_tm = _dot(x_ref[r0:r0 + half_rows, :].astype(BF16), cmat_ref[...])
        for j in range(n_yblk):
            y_ref[j, r0:r0 + half_rows, :] = y_tm[:, j * LANES:(j + 1) * LANES]
    for b in range(nb):
        y = jnp.concatenate([y_ref[j, pl.ds(b, tt, stride=nb), :] for j in range(n_yblk)], axis=-1)
        y = y + d_ref[...] * u_ref[b]
        z = _gelu_tanh(y)
        gate = _dot(z.astype(BF16), wg_ref[...]) + bg_ref[...]
        o_ref[b] = (z * _sigmoid(gate)).astype(o_ref.dtype)


def _ssm(p1, rows, b_bd, c_bd, d_row, w_glu, b_glu):
    b, s, _ = p1.shape
    ch = w_glu.shape[0]
    n_state = rows[0].shape[1]
    consts = list(rows) + list(b_bd) + list(c_bd) + [d_row, w_glu, b_glu]
    return pl.pallas_call(
        _ssm_kernel,
        grid=(s // SSM_TILE,),
        in_specs=[pl.BlockSpec((b, SSM_TILE, ch), lambda t: (0, t, 0))] + [_const_spec(c.shape) for c in consts],
        out_specs=pl.BlockSpec((b, SSM_TILE, ch), lambda t: (0, t, 0)),
        out_shape=jax.ShapeDtypeStruct((b, s, ch), BF16),
        scratch_shapes=[pltpu.VMEM((ch, 2 * n_state), BF16),
                        pltpu.VMEM((2 * n_state, ch), BF16),
                        pltpu.VMEM((SUBLANES, n_state), F32),
                        pltpu.VMEM((b, 2 * n_state), F32),
                        pltpu.VMEM((ch // LANES, b * SSM_PITCH, LANES), F32),
                        pltpu.VMEM((SSM_TILE * b, ch), F32),
                        pltpu.VMEM((SSM_TILE * b, 2 * n_state), F32),
                        pltpu.VMEM((ch // LANES, SSM_TILE * b, LANES), F32)],
        compiler_params=_params("arbitrary"),
        name="s5_ssm",
    )(p1, *consts)


def _rope_tables(s, rot_dim, theta):
    inv = theta ** (-jnp.arange(0, rot_dim, 2, dtype=F32) / rot_dim)
    ang = jnp.arange(s, dtype=F32)[:, None] * inv[None, :]
    return jnp.cos(ang), jnp.sin(ang)


def _lane_tables(cos, sin, period):
    s, half = cos.shape
    reps = LANES // period
    one = jnp.ones((s, period - 2 * half), F32)
    zero = jnp.zeros((s, period - 2 * half), F32)
    zh = jnp.zeros((s, half), F32)
    c = jnp.tile(jnp.concatenate([cos, cos, one], axis=1), (1, reps))
    sa = jnp.tile(jnp.concatenate([-sin, zh, zero], axis=1), (1, reps))
    sb = jnp.tile(jnp.concatenate([zh, sin, zero], axis=1), (1, reps))
    return c, sa, sb


def _block_diag(blocks):
    g, r, c = blocks.shape
    eye = jnp.eye(g, dtype=blocks.dtype)
    return (eye[:, None, :, None] * blocks[:, :, None, :]).reshape(g * r, g * c)


def kernel(x, l0_mix_pre, l0_mix_post, l0_w_in, l0_lambda_q1, l0_lambda_k1, l0_lambda_q2, l0_lambda_k2, l0_subln, l0_dw_w, l0_dw_b, l0_conv_ln_g, l0_conv_ln_b, l0_w_out, l0_ffn_pre, l0_ffn_post, l0_w_gate, l0_w_up, l0_w_down, l1_mix_pre, l1_mix_post, l1_w_in, l1_a_re, l1_a_im, l1_log_dt, l1_b_re, l1_b_im, l1_c_re, l1_c_im, l1_d_skip, l1_w_glu, l1_b_glu, l1_q_norm, l1_w_uq, l1_kv_norm, l1_w_ukv, l1_w_out, l1_ffn_pre, l1_ffn_post, l1_w_gate, l1_w_up, l1_w_down):
    b, s, d = x.shape
    n = b * s
    row = lambda v: v.reshape(1, -1).astype(F32)
    bf = lambda w: w.astype(BF16)

    diff_width = 4 * LANES
    n_diff_heads = diff_width // LANES
    conv_ch = l0_dw_w.shape[1]
    ssm_ch = l1_w_glu.shape[0]
    n_groups, n_state_g = l1_a_re.shape
    q_rank = l1_q_norm.shape[0]
    kv_rank = l1_kv_norm.shape[0]
    n_mla_heads = l1_w_uq.shape[1] // (MLA_NOPE + MLA_ROPE)

    cos_a, sin_a = _rope_tables(s, DIFF_ROT, ROPE_THETA)
    tabs_a = _lane_tables(cos_a, sin_a, period=DIFF_HEAD_DIM)
    x2 = x.reshape(n, d)
    w_main = bf(jnp.concatenate([l0_w_in[:, diff_width:2 * diff_width], l0_w_in[:, 3 * diff_width:]], axis=1))
    w_t = bf(jnp.concatenate([l0_w_in[:, :diff_width], l0_w_in[:, 2 * diff_width:3 * diff_width]], axis=1).T)
    p0, qvt_a = _l0_in(x2, row(l0_mix_pre), w_main, w_t, tabs_a, (cos_a.T, sin_a.T), b, s, diff_width)
    p0 = p0.reshape(b, s, -1)
    lam_vecs = jnp.stack([l0_lambda_q1, l0_lambda_k1, l0_lambda_q2, l0_lambda_k2]).astype(F32)
    y_a = _diff_attn(p0, qvt_a, lam_vecs, l0_subln.reshape(-1, 1).astype(F32), n_diff_heads)
    y_b = _conv_module(p0, l0_dw_w.astype(F32), row(l0_dw_b), row(l0_conv_ln_g), row(l0_conv_ln_b),
                       col0=diff_width)

    pad = (-l1_w_in.shape[1]) % LANES
    w_in1 = bf(jnp.pad(l1_w_in, ((0, 0), (0, pad))))
    h2, p1 = _post(x2, [y_a.reshape(n, -1), y_b.reshape(n, -1)], bf(l0_w_out), row(l0_mix_post),
                   row(l0_ffn_pre), bf(l0_w_gate), bf(l0_w_up), bf(l0_w_down), row(l0_ffn_post),
                   nxt=(row(l1_mix_pre), w_in1))

    state_row = lambda a: a.reshape(1, -1).astype(F32)
    ssm_rows = (state_row(l1_a_re), state_row(l1_a_im),
                state_row(jnp.broadcast_to(l1_log_dt[:, None], (n_groups, n_state_g))))
    b_bd = tuple(_block_diag(jnp.swapaxes(m, 1, 2).astype(F32)) for m in (l1_b_re, l1_b_im))
    c_bd = tuple(_block_diag(jnp.swapaxes(m, 1, 2).astype(F32)) for m in (l1_c_re, l1_c_im))
    y_c = _ssm(p1.reshape(b, s, -1), ssm_rows, b_bd, c_bd, row(l1_d_skip), bf(l1_w_glu), row(l1_b_glu))

    wq = l1_w_uq.reshape(q_rank, n_mla_heads, MLA_NOPE + MLA_ROPE)
    wq = jnp.pad(wq, ((0, 0), (0, 0), (0, MLA_QK_PAD - MLA_NOPE - MLA_ROPE))).reshape(q_rank, -1)
    wkv = l1_w_ukv.reshape(kv_rank, n_mla_heads, MLA_NOPE + MLA_V)
    wk_nope = jnp.pad(wkv[:, :, :MLA_NOPE], ((0, 0), (0, 0), (0, MLA_QK_PAD - MLA_NOPE)))
    route = jnp.pad(jnp.eye(MLA_ROPE, dtype=F32), ((0, LANES - MLA_ROPE), (MLA_NOPE, MLA_QK_PAD - MLA_NOPE - MLA_ROPE)))
    wk_rope = jnp.broadcast_to(route[:, None, :], (LANES, n_mla_heads, MLA_QK_PAD))
    wk = jnp.concatenate([wk_nope, wk_rope], axis=0).reshape(kv_rank + LANES, -1)
    wv_t = wkv[:, :, MLA_NOPE:].reshape(kv_rank, -1).T
    cos_d, sin_d = _rope_tables(s, MLA_ROPE, MLA_ROPE_THETA)
    tabs_d = _lane_tables(cos_d, sin_d, period=LANES)
    y_d = _mla_attn(p1.reshape(b, s, -1), row(l1_q_norm), row(l1_kv_norm), bf(wq.T), bf(wk), bf(wv_t),
                    (cos_d.T, sin_d.T), tabs_d, n_mla_heads)

    (out,) = _post(h2, [y_c.reshape(n, -1), y_d.reshape(n, -1)], bf(l1_w_out), row(l1_mix_post),
                   row(l1_ffn_pre), bf(l1_w_gate), bf(l1_w_up), bf(l1_w_down), row(l1_ffn_post))
    return out.reshape(b, s, d)
```

```python
import functools
import math

import jax
import jax.numpy as jnp
from jax import lax
from jax.experimental import pallas as pl
from jax.experimental.pallas import tpu as pltpu

F32 = jnp.float32
BF16 = jnp.bfloat16

LANES = 128
SUBLANES = 8
BF16_SUBLANES = 16
CAST_ROWS = 256
VMEM_LIMIT_BYTES = 56 * 1024 * 1024

CHUNK = 64
RMS_EPS = 1e-6
LN_EPS = 1e-5
ROPE_THETA = 500000.0
MLA_ROPE_THETA = 10000.0
DIFF_HEAD_DIM = 64
DIFF_ROT = 16
CONV_WIDTH = 31
CONV_HALO = 32
SSM_GROUP = 16
SSM_STATE = 64
MLA_NOPE = 128
MLA_ROPE = 64
MLA_V = 128
MLA_QK_PAD = 256

ROW_TILE = 512
ATTN_TILE = 256
ATTN_LOOKAHEAD = 3
ATTN_SUM_ROWS = 16
LOG2_E = math.log2(math.e)
CONV_TILE = 256
SSM_TILE = 128
SSM_PITCH = SSM_TILE + SUBLANES
FFN_CHUNK = 512
POST_ROW_GROUPS = 2


def _params(*sem):
    return pltpu.CompilerParams(dimension_semantics=sem, vmem_limit_bytes=VMEM_LIMIT_BYTES)


def _rms(x, g):
    return x * lax.rsqrt(jnp.mean(x * x, axis=-1, keepdims=True) + RMS_EPS) * g


def _sigmoid(x):
    return 1.0 / (1.0 + jnp.exp(-x))


def _dot(a, b):
    return jnp.dot(a, b, preferred_element_type=F32)


def _dot_nt(a, b):
    return lax.dot_general(a, b, (((1,), (1,)), ((), ())), preferred_element_type=F32)


def _rope_block(x, c, sa, sb, shift):
    return x * c + pltpu.roll(x, LANES - shift, 1) * sa + pltpu.roll(x, shift, 1) * sb


def _const_spec(shape):
    nd = len(shape)
    return pl.BlockSpec(shape, lambda *_: (0,) * nd, pipeline_mode=pl.Buffered(1))


def _store_vt(vt_ref, vt):
    for j in range(ROW_TILE // ATTN_TILE):
        vt_ref[0, j] = vt[:, j * ATTN_TILE:(j + 1) * ATTN_TILE].astype(vt_ref.dtype)


def _vt_out(b, seq, width):
    tiles_per_seq = seq // ROW_TILE
    per_tile = ROW_TILE // ATTN_TILE
    spec = pl.BlockSpec((1, per_tile, width, ATTN_TILE),
                        lambda i: (i // tiles_per_seq, i % tiles_per_seq, 0, 0))
    return spec, jax.ShapeDtypeStruct((b, seq // ATTN_TILE, width, ATTN_TILE), BF16)


def _l0_in_kernel(x_ref, g_ref, w_ref, wt_ref, c_ref, sa_ref, sb_ref, cosq_ref, sinq_ref, o_ref, qvt_ref, *,
                  n_k_blocks, q_rows, scale):
    t = _rms(x_ref[...], g_ref[...]).astype(BF16)
    p = _dot(t, w_ref[...])
    c, sa, sb = c_ref[...], sa_ref[...], sb_ref[...]
    half = DIFF_ROT // 2
    for j in range(n_k_blocks):
        blk = _rope_block(p[:, j * LANES:(j + 1) * LANES], c, sa, sb, half)
        o_ref[:, j * LANES:(j + 1) * LANES] = blk.astype(o_ref.dtype)
    rest = n_k_blocks * LANES
    o_ref[:, rest:] = p[:, rest:].astype(o_ref.dtype)
    qv_t = _dot_nt(wt_ref[...], t)
    cos_t, sin_t = cosq_ref[...], sinq_ref[...]
    pieces = []
    for lo in range(0, q_rows, DIFF_HEAD_DIM):
        x1, x2 = qv_t[lo:lo + half, :], qv_t[lo + half:lo + 2 * half, :]
        pieces += [(x1 * cos_t - x2 * sin_t) * scale, (x2 * cos_t + x1 * sin_t) * scale,
                   qv_t[lo + 2 * half:lo + DIFF_HEAD_DIM, :] * scale]
    _store_vt(qvt_ref, jnp.concatenate(pieces + [qv_t[q_rows:, :]], axis=0))


def _l0_in(x2, g_pre, w_main, w_t, tabs, q_tabs_t, b, seq, q_rows):
    n, d = x2.shape
    n_out = w_main.shape[1]
    tiles_per_seq = seq // ROW_TILE
    kern = functools.partial(_l0_in_kernel, n_k_blocks=q_rows // LANES, q_rows=q_rows,
                             scale=DIFF_HEAD_DIM ** -0.5 * LOG2_E)
    tab_spec = pl.BlockSpec((ROW_TILE, LANES), lambda i: (i % tiles_per_seq, 0))
    qtab_spec = pl.BlockSpec((q_tabs_t[0].shape[0], ROW_TILE), lambda i: (0, i % tiles_per_seq))
    vt_spec, vt_shape = _vt_out(b, seq, w_t.shape[0])
    return pl.pallas_call(
        kern,
        grid=(n // ROW_TILE,),
        in_specs=[pl.BlockSpec((ROW_TILE, d), lambda i: (i, 0)),
                  _const_spec((1, d)), _const_spec(w_main.shape), _const_spec(w_t.shape),
                  tab_spec, tab_spec, tab_spec, qtab_spec, qtab_spec],
        out_specs=[pl.BlockSpec((ROW_TILE, n_out), lambda i: (i, 0)), vt_spec],
        out_shape=[jax.ShapeDtypeStruct((n, n_out), BF16), vt_shape],
        compiler_params=_params("parallel"),
        name="l0_in",
    )(x2, g_pre, w_main, w_t, *tabs, *q_tabs_t)


def _attn_body(q_heads, k_at, vt_at, m_ref, acc_ref, spre_ref, *, n_maps, q_transposed=False):
    i = pl.program_id(1)
    dk_axis = 0 if q_transposed else 1
    tq, dk = q_heads[0].shape[1 - dk_axis], q_heads[0].shape[dk_axis]
    map_width = dk // n_maps
    n_chains = len(q_heads) * n_maps
    qs = []
    for q in q_heads:
        if n_maps == 1:
            qs.append(q)
        else:
            pos = lax.broadcasted_iota(jnp.int32, q.shape, dk_axis)
            for c in range(n_maps):
                sel = (pos >= c * map_width) & (pos < (c + 1) * map_width)
                qs.append(jnp.where(sel, q, jnp.zeros_like(q)))
    m_ref[...] = jnp.full(m_ref.shape, -jnp.inf, F32)
    acc_ref[...] = jnp.zeros(acc_ref.shape, F32)
    ones = jnp.ones((ATTN_SUM_ROWS, tq), BF16)

    def scores(kb, ch):
        start = pl.multiple_of(kb * tq, tq)
        k = k_at(start, ch // n_maps)
        return _dot(k, qs[ch]) if q_transposed else _dot_nt(k, qs[ch])

    def step(kb, masked, prefetch):
        if masked:
            kc = lax.broadcasted_iota(jnp.int32, (tq, tq), 0) // CHUNK
            qc = lax.broadcasted_iota(jnp.int32, (tq, tq), 1) // CHUNK
            keep = kc <= qc
        ss = [spre_ref[ch] if ch < ATTN_LOOKAHEAD else None for ch in range(n_chains)]
        new = []
        for ch in range(n_chains):
            ahead = ch + ATTN_LOOKAHEAD
            if ahead < n_chains:
                ss[ahead] = scores(kb, ahead)
            elif prefetch:
                spre_ref[ahead - n_chains] = scores(kb + 1, ahead - n_chains)
            vt = jnp.concatenate([vt_at(kb, ch // n_maps), ones], axis=0)
            s = jnp.where(keep, ss[ch], -jnp.inf) if masked else ss[ch]
            m_old = m_ref[ch]
            m_new = jnp.maximum(m_old, jnp.max(s, axis=0, keepdims=True))
            alpha = jnp.exp2(m_old - m_new)
            p = jnp.exp2(s - m_new)
            new.append((m_new, alpha * acc_ref[ch] + _dot(vt, p.astype(vt.dtype))))
        for ch, (m_new, acc_new) in enumerate(new):
            m_ref[ch] = m_new
            acc_ref[ch] = acc_new

    for ch in range(ATTN_LOOKAHEAD):
        spre_ref[ch] = scores(0, ch)

    def loop_body(kb, carry):
        step(kb, False, True)
        return carry

    lax.fori_loop(0, i, loop_body, 0)
    step(i, True, False)


def _attn_scratch(n_chains, dv):
    return [pltpu.VMEM((n_chains, 1, ATTN_TILE), F32),
            pltpu.VMEM((n_chains, dv + ATTN_SUM_ROWS, ATTN_TILE), F32),
            pltpu.VMEM((ATTN_LOOKAHEAD, ATTN_TILE, ATTN_TILE), F32)]


def _attn_out(acc_ref, ch, dv):
    return acc_ref[ch, :dv, :] / acc_ref[ch, dv:dv + 1, :]


def _cast_plan(weights, n_steps):
    plan = []
    for w in weights:
        rows = w.shape[0]
        if rows % n_steps == 0 and (rows // n_steps) % BF16_SUBLANES == 0:
            plan.append((rows // n_steps, n_steps))
        else:
            plan.append((CAST_ROWS, rows // CAST_ROWS))
            assert rows % CAST_ROWS == 0 and rows // CAST_ROWS <= n_steps
    return plan


def _cast_specs(weights, plan, step_of):
    def spec(w, rows, n_blocks):
        return pl.BlockSpec((rows, w.shape[1]), lambda *g: (jnp.minimum(step_of(*g), n_blocks - 1), 0))
    specs = [spec(w, rows, nb) for w, (rows, nb) in zip(weights, plan)]
    shapes = [jax.ShapeDtypeStruct(w.shape, BF16) for w in weights]
    return specs, shapes


def _cast_blocks(step, plan, n_steps, src_refs, dst_refs):
    for (rows, n_blocks), src, dst in zip(plan, src_refs, dst_refs):
        if n_blocks == n_steps:
            dst[...] = src[...].astype(dst.dtype)
        else:
            @pl.when(step < n_blocks)
            def _(src=src, dst=dst):
                dst[...] = src[...].astype(dst.dtype)


def _split_cast_refs(refs, n_in, n_out, n_cast):
    main_in, cast_src = refs[:n_in], refs[n_in:n_in + n_cast]
    rest = refs[n_in + n_cast:]
    return main_in, cast_src, rest[:n_out], rest[n_out:n_out + n_cast], rest[n_out + n_cast:]


def _grid_step(n_inner):
    return pl.program_id(0) * n_inner + pl.program_id(1)


def _diff_attn_kernel(*refs, n_heads, lam_init, cast_plan, n_steps, n_inner):
    (qt_ref, k_ref, vt_ref, lam_ref, sub_ref), cast_src, (o_ref,), cast_dst, (m_ref, acc_ref, spre_ref) = (
        _split_cast_refs(refs, 5, 1, len(cast_plan)))
    _cast_blocks(_grid_step(n_inner), cast_plan, n_steps, cast_src, cast_dst)
    tq = qt_ref.shape[3]
    _attn_body([qt_ref[0, 0, h * LANES:(h + 1) * LANES, :] for h in range(n_heads)],
               lambda start, h: k_ref[0, pl.ds(start, tq), h * LANES:(h + 1) * LANES],
               lambda kb, h: vt_ref[0, kb, h * LANES:(h + 1) * LANES, :],
               m_ref, acc_ref, spre_ref, n_maps=2, q_transposed=True)
    lv = lam_ref[...]
    lam = (jnp.exp(jnp.sum(lv[0:1] * lv[1:2], axis=-1, keepdims=True))
           - jnp.exp(jnp.sum(lv[2:3] * lv[3:4], axis=-1, keepdims=True)) + lam_init)
    for h in range(n_heads):
        o_t = _attn_out(acc_ref, 2 * h, LANES) - lam * _attn_out(acc_ref, 2 * h + 1, LANES)
        inv = lax.rsqrt(jnp.mean(o_t * o_t, axis=0, keepdims=True) + RMS_EPS)
        y_t = o_t * inv * sub_ref[...] * (1.0 - lam_init)
        o_ref[0, :, h * LANES:(h + 1) * LANES] = y_t.T.astype(o_ref.dtype)


def _diff_attn(p0, qvt, lam_vecs, subln_col, n_heads, cast_weights):
    b, s, _ = p0.shape
    n_kb = s // ATTN_TILE
    width = n_heads * LANES
    plan = _cast_plan(cast_weights, b * n_kb)
    cast_specs, cast_shapes = _cast_specs(cast_weights, plan, lambda bb, i: bb * n_kb + i)
    kern = functools.partial(_diff_attn_kernel, n_heads=n_heads, lam_init=0.8 - 0.6 * math.exp(-0.3 * 0),
                             cast_plan=tuple(plan), n_steps=b * n_kb, n_inner=n_kb)
    out = pl.pallas_call(
        kern,
        grid=(b, n_kb),
        in_specs=[pl.BlockSpec((1, 1, width, ATTN_TILE), lambda bb, i: (bb, i, 0, 0)),
                  pl.BlockSpec((1, s, width), lambda bb, i: (bb, 0, 0)),
                  pl.BlockSpec((1, n_kb, width, ATTN_TILE), lambda bb, i: (bb, 0, 1, 0)),
                  pl.BlockSpec(lam_vecs.shape, lambda bb, i: (0, 0)),
                  pl.BlockSpec(subln_col.shape, lambda bb, i: (0, 0))] + cast_specs,
        out_specs=[pl.BlockSpec((1, ATTN_TILE, width), lambda bb, i: (bb, i, 0))] + cast_specs,
        out_shape=[jax.ShapeDtypeStruct((b, s, width), BF16)] + cast_shapes,
        scratch_shapes=_attn_scratch(2 * n_heads, LANES),
        compiler_params=_params("arbitrary", "arbitrary"),
        name="diff_attn",
    )(qvt, p0, qvt, lam_vecs, subln_col, *cast_weights)
    return out[0], out[1:]


def _mla_attn_kernel(*refs, n_heads, scale, cast_plan, n_steps, n_inner):
    ((cq_ref, kv_ref, qn_ref, kvn_ref, wqt_ref, wk_ref, wvt_ref, cosq_tab, sinq_tab, ck_tab, sak_tab, sbk_tab),
     cast_src, (o_ref,), cast_dst, (m_ref, acc_ref, spre_ref, k_scr, vt_scr)) = (
        _split_cast_refs(refs, 12, 1, len(cast_plan)))
    _cast_blocks(_grid_step(n_inner), cast_plan, n_steps, cast_src, cast_dst)
    seq = kv_ref.shape[1]
    tq = cq_ref.shape[1]
    half = MLA_ROPE // 2

    @pl.when(pl.program_id(1) == 0)
    def _project_keys():
        for r0 in range(0, seq, ROW_TILE):
            rows = slice(r0, r0 + ROW_TILE)
            ckv = _rms(kv_ref[0, rows, :LANES], kvn_ref[...]).astype(BF16)
            kr = _rope_block(kv_ref[0, rows, LANES:], ck_tab[rows, :], sak_tab[rows, :], sbk_tab[rows, :], half)
            k_scr[rows, :] = _dot(jnp.concatenate([ckv, kr.astype(BF16)], axis=-1), wk_ref[...]).astype(BF16)
            vt = _dot_nt(wvt_ref[...], ckv)
            for j in range(ROW_TILE // tq):
                vt_scr[r0 // tq + j] = vt[:, j * tq:(j + 1) * tq].astype(BF16)

    q_t = _dot_nt(wqt_ref[...], _rms(cq_ref[0], qn_ref[...]).astype(BF16))
    cos_t, sin_t = cosq_tab[...], sinq_tab[...]
    q_heads = []
    for h in range(n_heads):
        lo = h * MLA_QK_PAD
        x1 = q_t[lo + MLA_NOPE:lo + MLA_NOPE + half, :]
        x2 = q_t[lo + MLA_NOPE + half:lo + MLA_NOPE + 2 * half, :]
        q_h = jnp.concatenate([q_t[lo:lo + MLA_NOPE, :], x1 * cos_t - x2 * sin_t, x2 * cos_t + x1 * sin_t,
                               q_t[lo + MLA_NOPE + 2 * half:lo + MLA_QK_PAD, :]], axis=0)
        q_heads.append((q_h * scale).astype(BF16))

    _attn_body(q_heads,
               lambda start, h: k_scr[pl.ds(start, tq), h * MLA_QK_PAD:(h + 1) * MLA_QK_PAD],
               lambda kb, h: vt_scr[kb, h * MLA_V:(h + 1) * MLA_V, :],
               m_ref, acc_ref, spre_ref, n_maps=1, q_transposed=True)
    for h in range(n_heads):
        o_ref[0, :, h * MLA_V:(h + 1) * MLA_V] = _attn_out(acc_ref, h, MLA_V).T.astype(o_ref.dtype)


def _mla_attn(p1, q_norm, kv_norm, wq_t, wk, wv_t, q_tabs_t, tabs, n_heads, cast_weights):
    b, s, _ = p1.shape
    n_kb = s // ATTN_TILE
    plan = _cast_plan(cast_weights, b * n_kb)
    cast_specs, cast_shapes = _cast_specs(cast_weights, plan, lambda bb, i: bb * n_kb + i)
    kern = functools.partial(_mla_attn_kernel, n_heads=n_heads, scale=(MLA_NOPE + MLA_ROPE) ** -0.5 * LOG2_E,
                             cast_plan=tuple(plan), n_steps=b * n_kb, n_inner=n_kb)
    const = lambda a: pl.BlockSpec(a.shape, lambda bb, i: (0,) * a.ndim)
    q_tab = pl.BlockSpec((q_tabs_t[0].shape[0], ATTN_TILE), lambda bb, i: (0, i))
    out = pl.pallas_call(
        kern,
        grid=(b, n_kb),
        in_specs=[pl.BlockSpec((1, ATTN_TILE, 2 * LANES), lambda bb, i: (bb, i, 1)),
                  pl.BlockSpec((1, s, 2 * LANES), lambda bb, i: (bb, 0, 2)),
                  const(q_norm), const(kv_norm), const(wq_t), const(wk), const(wv_t),
                  q_tab, q_tab, const(tabs[0]), const(tabs[1]), const(tabs[2])] + cast_specs,
        out_specs=[pl.BlockSpec((1, ATTN_TILE, n_heads * MLA_V), lambda bb, i: (bb, i, 0))] + cast_specs,
        out_shape=[jax.ShapeDtypeStruct((b, s, n_heads * MLA_V), BF16)] + cast_shapes,
        scratch_shapes=_attn_scratch(n_heads, MLA_V) + [
            pltpu.VMEM((s, n_heads * MLA_QK_PAD), BF16),
            pltpu.VMEM((n_kb, n_heads * MLA_V, ATTN_TILE), BF16)],
        compiler_params=_params("arbitrary", "arbitrary"),
        name="mla_attn",
    )(p1, p1, q_norm, kv_norm, wq_t, wk, wv_t, *q_tabs_t, *tabs, *cast_weights)
    return out[0], out[1:]


def _conv_kernel(a_ref, gate_ref, ah_ref, gh_ref, w_ref, b_ref, lg_ref, lb_ref, o_ref, u_ref, ur_ref):
    tt = a_ref.shape[1]
    u_ref[CONV_HALO:, :] = a_ref[0].astype(F32) * _sigmoid(gate_ref[0].astype(F32))
    halo = ah_ref[0].astype(F32) * _sigmoid(gh_ref[0].astype(F32))
    u_ref[:CONV_HALO, :] = jnp.where(pl.program_id(1) > 0, halo, jnp.zeros_like(halo))
    rows = ur_ref.shape[1]
    for r in range(1, SUBLANES):
        ur_ref[r - 1] = u_ref[r:r + rows, :]
    acc = jnp.zeros((tt, a_ref.shape[2]), F32)
    first = CONV_HALO - (CONV_WIDTH - 1)
    for k in range(CONV_WIDTH):
        base, r = divmod(first + k, SUBLANES)
        src = u_ref if r == 0 else ur_ref.at[r - 1]
        acc = acc + src[base * SUBLANES:base * SUBLANES + tt, :] * w_ref[k:k + 1, :]
    y = acc + b_ref[...]
    mu = jnp.mean(y, axis=-1, keepdims=True)
    yc = y - mu
    yn = yc * lax.rsqrt(jnp.mean(yc * yc, axis=-1, keepdims=True) + LN_EPS) * lg_ref[...] + lb_ref[...]
    o_ref[0] = (yn * _sigmoid(yn)).astype(o_ref.dtype)


def _conv_module(p0, dw_w, dw_b, ln_g, ln_b, col0):
    b, s, _ = p0.shape
    c = dw_w.shape[1]
    a_blk, g_blk = col0 // c, col0 // c + 1
    ratio = CONV_TILE // CONV_HALO
    main = lambda blk: pl.BlockSpec((1, CONV_TILE, c), lambda bb, t: (bb, t, blk))
    halo = lambda blk: pl.BlockSpec((1, CONV_HALO, c), lambda bb, t: (bb, jnp.maximum(t * ratio - 1, 0), blk))
    vec = lambda a: pl.BlockSpec(a.shape, lambda bb, t: (0, 0))
    return pl.pallas_call(
        _conv_kernel,
        grid=(b, s // CONV_TILE),
        in_specs=[main(a_blk), main(g_blk), halo(a_blk), halo(g_blk),
                  vec(dw_w), vec(dw_b), vec(ln_g), vec(ln_b)],
        out_specs=pl.BlockSpec((1, CONV_TILE, c), lambda bb, t: (bb, t, 0)),
        out_shape=jax.ShapeDtypeStruct((b, s, c), BF16),
        scratch_shapes=[pltpu.VMEM((CONV_HALO + CONV_TILE, c), F32),
                        pltpu.VMEM((SUBLANES - 1, CONV_HALO + CONV_TILE - SUBLANES, c), F32)],
        compiler_params=_params("parallel", "parallel"),
        name="conv_module",
    )(p0, p0, p0, p0, dw_w, dw_b, ln_g, ln_b)


def _post_kernel(*refs, n_mix, with_next):
    x_ref = refs[0]
    mix_refs = refs[1:1 + n_mix]
    (wo_ref, gpost_ref, gfpre_ref, wg_ref, wu_ref, wd_ref, gfpost_ref) = refs[1 + n_mix:8 + n_mix]
    pos = 8 + n_mix
    if with_next:
        gnext_ref, wnext_ref = refs[pos:pos + 2]
        pos += 2
    h_out_ref = refs[pos]
    pos += 1
    if with_next:
        p_out_ref = refs[pos]
        pos += 1
    hid_ref = refs[pos]

    n_rows = x_ref.shape[0]
    groups = [slice(r0, r0 + n_rows // POST_ROW_GROUPS) for r0 in range(0, n_rows, n_rows // POST_ROW_GROUPS)]

    def out_proj(rows):
        y = None
        row = 0
        for r in mix_refs:
            w = r.shape[-1]
            part = _dot(r[rows, :], wo_ref[row:row + w, :])
            y = part if y is None else y + part
            row += w
        return y

    ys = [out_proj(rows) for rows in groups]
    h1s = [x_ref[rows, :] + _rms(y, gpost_ref[...]) for rows, y in zip(groups, ys)]
    ts = [_rms(h1, gfpre_ref[...]).astype(BF16) for h1 in h1s]
    d_ff = wg_ref.shape[1]
    for j in range(0, d_ff, FFN_CHUNK):
        wdt = min(FFN_CHUNK, d_ff - j)
        for rows, t in zip(groups, ts):
            gate = _dot(t, wg_ref[:, j:j + wdt])
            up = _dot(t, wu_ref[:, j:j + wdt])
            hid_ref[rows, j:j + wdt] = (gate * _sigmoid(gate) * up).astype(BF16)
    fs = [_dot(hid_ref[rows, :], wd_ref[...]) for rows in groups]
    h2s = [h1 + _rms(f, gfpost_ref[...]) for h1, f in zip(h1s, fs)]
    for rows, h2 in zip(groups, h2s):
        h_out_ref[rows, :] = h2
    if with_next:
        t2s = [_rms(h2, gnext_ref[...]).astype(BF16) for h2 in h2s]
        for rows, t2 in zip(groups, t2s):
            p_out_ref[rows, :] = _dot(t2, wnext_ref[...])


def _post(x2, mix_parts, w_out, g_post, g_fpre, w_gate, w_up, w_down, g_fpost, nxt=None):
    n, d = x2.shape
    d_ff = w_gate.shape[1]
    row = lambda a: pl.BlockSpec((ROW_TILE, a.shape[1]), lambda i: (i, 0))
    consts = [w_out, g_post, g_fpre, w_gate, w_up, w_down, g_fpost] + (list(nxt) if nxt else [])
    out_shape = [jax.ShapeDtypeStruct((n, d), F32)]
    out_specs = [pl.BlockSpec((ROW_TILE, d), lambda i: (i, 0))]
    if nxt:
        n_next = nxt[1].shape[1]
        out_shape.append(jax.ShapeDtypeStruct((n, n_next), F32))
        out_specs.append(pl.BlockSpec((ROW_TILE, n_next), lambda i: (i, 0)))
    kern = functools.partial(_post_kernel, n_mix=len(mix_parts), with_next=bool(nxt))
    return pl.pallas_call(
        kern,
        grid=(n // ROW_TILE,),
        in_specs=[row(x2)] + [row(m) for m in mix_parts] + [_const_spec(c.shape) for c in consts],
        out_specs=out_specs,
        out_shape=out_shape,
        scratch_shapes=[pltpu.VMEM((ROW_TILE, d_ff), BF16)],
        compiler_params=_params("parallel"),
        name="post_next" if nxt else "post",
    )(x2, *mix_parts, *consts)


def _gelu_tanh(x):
    return 0.5 * x * (1.0 + jnp.tanh(math.sqrt(2.0 / math.pi) * (x + 0.044715 * (x * x * x))))


def _ssm_kernel(u_ref, lr_ref, li_ref, ldt_ref, bre_ref, bim_ref, cre_ref, cim_ref, d_ref, wg_ref, bg_ref,
                o_ref, bmat_ref, cmat_ref, a_ref, st_ref, us_ref, utm_ref, x_ref, y_ref):
    nb, tt, ch = u_ref.shape
    n_state = lr_ref.shape[1]

    @pl.when(pl.program_id(0) == 0)
    def _init():
        lr, li = lr_ref[...], li_ref[...]
        dt = jnp.exp(ldt_ref[...])
        mag = jnp.exp(lr * dt)
        ab_re = mag * jnp.cos(li * dt)
        ab_im = mag * jnp.sin(li * dt)
        den = lr * lr + li * li
        n_re = ab_re - 1.0
        f_re = (n_re * lr + ab_im * li) / den
        f_im = (ab_im * lr - n_re * li) / den
        br, bi = bre_ref[...], bim_ref[...]
        bmat_ref[:, :n_state] = (f_re * br - f_im * bi).astype(BF16)
        bmat_ref[:, n_state:] = (f_re * bi + f_im * br).astype(BF16)
        cmat_ref[:n_state, :] = cre_ref[...].astype(BF16)
        cmat_ref[n_state:, :] = (-cim_ref[...]).astype(BF16)
        a_ref[0:1, :] = ab_re
        a_ref[1:2, :] = ab_im
        st_ref[...] = jnp.zeros(st_ref.shape, F32)

    n_ublk = ch // LANES
    for b in range(nb):
        for j in range(n_ublk):
            us_ref[j, b * SSM_PITCH:b * SSM_PITCH + tt, :] = u_ref[b, :, j * LANES:(j + 1) * LANES]

    def gather_step(t, carry):
        dst = pl.multiple_of(t * nb, nb)
        for j in range(n_ublk):
            utm_ref[pl.ds(dst, nb), j * LANES:(j + 1) * LANES] = us_ref[j, pl.ds(t, nb, stride=SSM_PITCH), :]
        return carry

    lax.fori_loop(0, tt, gather_step, 0, unroll=8)

    half_rows = tt * nb // 2
    for r0 in (0, half_rows):
        x_ref[r0:r0 + half_rows, :] = _dot(utm_ref[r0:r0 + half_rows, :].astype(BF16), bmat_ref[...])

    a_re = jnp.broadcast_to(a_ref[0:1, :], (nb, n_state))
    a_im = jnp.broadcast_to(a_ref[1:2, :], (nb, n_state))

    def scan_step(t, carry):
        x_re, x_im = carry
        row = pl.multiple_of(t * nb, nb)
        n_re = a_re * x_re - a_im * x_im + x_ref[pl.ds(row, nb), :n_state]
        n_im = a_re * x_im + a_im * x_re + x_ref[pl.ds(row, nb), n_state:]
        x_ref[pl.ds(row, nb), :n_state] = n_re
        x_ref[pl.ds(row, nb), n_state:] = n_im
        return n_re, n_im

    x_re, x_im = lax.fori_loop(0, tt, scan_step, (st_ref[:, :n_state], st_ref[:, n_state:]), unroll=4)
    st_ref[:, :n_state] = x_re
    st_ref[:, n_state:] = x_im

    n_yblk = ch // LANES
    for r0 in (0, half_rows):
        y_tm = _dot(x_ref[r0:r0 + half_rows, :].astype(BF16), cmat_ref[...])
        for j in range(n_yblk):
            y_ref[j, r0:r0 + half_rows, :] = y_tm[:, j * LANES:(j + 1) * LANES]
    for b in range(nb):
        y = jnp.concatenate([y_ref[j, pl.ds(b, tt, stride=nb), :] for j in range(n_yblk)], axis=-1)
        y = y + d_ref[...] * u_ref[b]
        z = _gelu_tanh(y)
        gate = _dot(z.astype(BF16), wg_ref[...]) + bg_ref[...]
        o_ref[b] = (z * _sigmoid(gate)).astype(o_ref.dtype)


def _ssm(p1, rows, b_bd, c_bd, d_row, w_glu, b_glu):
    b, s, _ = p1.shape
    ch = w_glu.shape[0]
    n_state = rows[0].shape[1]
    consts = list(rows) + list(b_bd) + list(c_bd) + [d_row, w_glu, b_glu]
    return pl.pallas_call(
        _ssm_kernel,
        grid=(s // SSM_TILE,),
        in_specs=[pl.BlockSpec((b, SSM_TILE, ch), lambda t: (0, t, 0))] + [_const_spec(c.shape) for c in consts],
        out_specs=pl.BlockSpec((b, SSM_TILE, ch), lambda t: (0, t, 0)),
        out_shape=jax.ShapeDtypeStruct((b, s, ch), BF16),
        scratch_shapes=[pltpu.VMEM((ch, 2 * n_state), BF16),
                        pltpu.VMEM((2 * n_state, ch), BF16),
                        pltpu.VMEM((SUBLANES, n_state), F32),
                        pltpu.VMEM((b, 2 * n_state), F32),
                        pltpu.VMEM((ch // LANES, b * SSM_PITCH, LANES), F32),
                        pltpu.VMEM((SSM_TILE * b, ch), F32),
                        pltpu.VMEM((SSM_TILE * b, 2 * n_state), F32),
                        pltpu.VMEM((ch // LANES, SSM_TILE * b, LANES), F32)],
        compiler_params=_params("arbitrary"),
        name="s5_ssm",
    )(p1, *consts)


def _rope_tables(s, rot_dim, theta):
    inv = theta ** (-jnp.arange(0, rot_dim, 2, dtype=F32) / rot_dim)
    ang = jnp.arange(s, dtype=F32)[:, None] * inv[None, :]
    return jnp.cos(ang), jnp.sin(ang)


def _lane_tables(cos, sin, period):
    s, half = cos.shape
    reps = LANES // period
    one = jnp.ones((s, period - 2 * half), F32)
    zero = jnp.zeros((s, period - 2 * half), F32)
    zh = jnp.zeros((s, half), F32)
    c = jnp.tile(jnp.concatenate([cos, cos, one], axis=1), (1, reps))
    sa = jnp.tile(jnp.concatenate([-sin, zh, zero], axis=1), (1, reps))
    sb = jnp.tile(jnp.concatenate([zh, sin, zero], axis=1), (1, reps))
    return c, sa, sb


def _block_diag(blocks):
    g, r, c = blocks.shape
    eye = jnp.eye(g, dtype=blocks.dtype)
    return (eye[:, None, :, None] * blocks[:, :, None, :]).reshape(g * r, g * c)


def kernel(x, l0_mix_pre, l0_mix_post, l0_w_in, l0_lambda_q1, l0_lambda_k1, l0_lambda_q2, l0_lambda_k2, l0_subln, l0_dw_w, l0_dw_b, l0_conv_ln_g, l0_conv_ln_b, l0_w_out, l0_ffn_pre, l0_ffn_post, l0_w_gate, l0_w_up, l0_w_down, l1_mix_pre, l1_mix_post, l1_w_in, l1_a_re, l1_a_im, l1_log_dt, l1_b_re, l1_b_im, l1_c_re, l1_c_im, l1_d_skip, l1_w_glu, l1_b_glu, l1_q_norm, l1_w_uq, l1_kv_norm, l1_w_ukv, l1_w_out, l1_ffn_pre, l1_ffn_post, l1_w_gate, l1_w_up, l1_w_down):
    b, s, d = x.shape
    n = b * s
    row = lambda v: v.reshape(1, -1).astype(F32)
    bf = lambda w: w.astype(BF16)

    diff_width = 4 * LANES
    n_diff_heads = diff_width // LANES
    conv_ch = l0_dw_w.shape[1]
    ssm_ch = l1_w_glu.shape[0]
    n_groups, n_state_g = l1_a_re.shape
    q_rank = l1_q_norm.shape[0]
    kv_rank = l1_kv_norm.shape[0]
    n_mla_heads = l1_w_uq.shape[1] // (MLA_NOPE + MLA_ROPE)

    cos_a, sin_a = _rope_tables(s, DIFF_ROT, ROPE_THETA)
    tabs_a = _lane_tables(cos_a, sin_a, period=DIFF_HEAD_DIM)
    x2 = x.reshape(n, d)
    w_main = bf(jnp.concatenate([l0_w_in[:, diff_width:2 * diff_width], l0_w_in[:, 3 * diff_width:]], axis=1))
    w_t = bf(jnp.concatenate([l0_w_in[:, :diff_width], l0_w_in[:, 2 * diff_width:3 * diff_width]], axis=1).T)
    p0, qvt_a = _l0_in(x2, row(l0_mix_pre), w_main, w_t, tabs_a, (cos_a.T, sin_a.T), b, s, diff_width)
    p0 = p0.reshape(b, s, -1)
    lam_vecs = jnp.stack([l0_lambda_q1, l0_lambda_k1, l0_lambda_q2, l0_lambda_k2]).astype(F32)
    y_a, (w_out0, w_gate0, w_up0, w_down0) = _diff_attn(
        p0, qvt_a, lam_vecs, l0_subln.reshape(-1, 1).astype(F32), n_diff_heads,
        [l0_w_out, l0_w_gate, l0_w_up, l0_w_down])
    y_b = _conv_module(p0, l0_dw_w.astype(F32), row(l0_dw_b), row(l0_conv_ln_g), row(l0_conv_ln_b),
                       col0=diff_width)

    pad = (-l1_w_in.shape[1]) % LANES
    w_in1 = bf(jnp.pad(l1_w_in, ((0, 0), (0, pad))))
    h2, p1 = _post(x2, [y_a.reshape(n, -1), y_b.reshape(n, -1)], w_out0, row(l0_mix_post),
                   row(l0_ffn_pre), w_gate0, w_up0, w_down0, row(l0_ffn_post),
                   nxt=(row(l1_mix_pre), w_in1))

    state_row = lambda a: a.reshape(1, -1).astype(F32)
    ssm_rows = (state_row(l1_a_re), state_row(l1_a_im),
                state_row(jnp.broadcast_to(l1_log_dt[:, None], (n_groups, n_state_g))))
    b_bd = tuple(_block_diag(jnp.swapaxes(m, 1, 2).astype(F32)) for m in (l1_b_re, l1_b_im))
    c_bd = tuple(_block_diag(jnp.swapaxes(m, 1, 2).astype(F32)) for m in (l1_c_re, l1_c_im))
    y_c = _ssm(p1.reshape(b, s, -1), ssm_rows, b_bd, c_bd, row(l1_d_skip), bf(l1_w_glu), row(l1_b_glu))

    wq = l1_w_uq.reshape(q_rank, n_mla_heads, MLA_NOPE + MLA_ROPE)
    wq = jnp.pad(wq, ((0, 0), (0, 0), (0, MLA_QK_PAD - MLA_NOPE - MLA_ROPE))).reshape(q_rank, -1)
    wkv = l1_w_ukv.reshape(kv_rank, n_mla_heads, MLA_NOPE + MLA_V)
    wk_nope = jnp.pad(wkv[:, :, :MLA_NOPE], ((0, 0), (0, 0), (0, MLA_QK_PAD - MLA_NOPE)))
    route = jnp.pad(jnp.eye(MLA_ROPE, dtype=F32), ((0, LANES - MLA_ROPE), (MLA_NOPE, MLA_QK_PAD - MLA_NOPE - MLA_ROPE)))
    wk_rope = jnp.broadcast_to(route[:, None, :], (LANES, n_mla_heads, MLA_QK_PAD))
    wk = jnp.concatenate([wk_nope, wk_rope], axis=0).reshape(kv_rank + LANES, -1)
    wv_t = wkv[:, :, MLA_NOPE:].reshape(kv_rank, -1).T
    cos_d, sin_d = _rope_tables(s, MLA_ROPE, MLA_ROPE_THETA)
    tabs_d = _lane_tables(cos_d, sin_d, period=LANES)
    y_d, (w_out1, w_gate1, w_up1, w_down1) = _mla_attn(
        p1.reshape(b, s, -1), row(l1_q_norm), row(l1_kv_norm), bf(wq.T), bf(wk), bf(wv_t),
        (cos_d.T, sin_d.T), tabs_d, n_mla_heads, [l1_w_out, l1_w_gate, l1_w_up, l1_w_down])

    (out,) = _post(h2, [y_c.reshape(n, -1), y_d.reshape(n, -1)], w_out1, row(l1_mix_post),
                   row(l1_ffn_pre), w_gate1, w_up1, w_down1, row(l1_ffn_post))
    return out.reshape(b, s, d)
```

```python
import functools
import math

import jax
import jax.numpy as jnp
from jax import lax
from jax.experimental import pallas as pl
from jax.experimental.pallas import tpu as pltpu

F32 = jnp.float32
BF16 = jnp.bfloat16

LANES = 128
SUBLANES = 8
BF16_SUBLANES = 16
CAST_ROWS = 256
VMEM_LIMIT_BYTES = 56 * 1024 * 1024

CHUNK = 64
RMS_EPS = 1e-6
LN_EPS = 1e-5
ROPE_THETA = 500000.0
MLA_ROPE_THETA = 10000.0
DIFF_HEAD_DIM = 64
DIFF_ROT = 16
CONV_WIDTH = 31
CONV_HALO = 32
SSM_GROUP = 16
SSM_STATE = 64
MLA_NOPE = 128
MLA_ROPE = 64
MLA_V = 128
MLA_QK_PAD = 256

ROW_TILE = 512
ATTN_TILE = 256
ATTN_LOOKAHEAD = 3
ATTN_SUM_ROWS = 16
LOG2_E = math.log2(math.e)
CONV_TILE = 256
SSM_TILE = 128
SSM_PITCH = SSM_TILE + SUBLANES
FFN_CHUNK = 512
POST_ROW_GROUPS = 2


def _params(*sem):
    return pltpu.CompilerParams(dimension_semantics=sem, vmem_limit_bytes=VMEM_LIMIT_BYTES)


def _rms(x, g):
    return x * lax.rsqrt(jnp.mean(x * x, axis=-1, keepdims=True) + RMS_EPS) * g


def _sigmoid(x):
    return 1.0 / (1.0 + jnp.exp(-x))


def _dot(a, b):
    return jnp.dot(a, b, preferred_element_type=F32)


def _dot_nt(a, b):
    return lax.dot_general(a, b, (((1,), (1,)), ((), ())), preferred_element_type=F32)


def _rope_block(x, c, sa, sb, shift):
    return x * c + pltpu.roll(x, LANES - shift, 1) * sa + pltpu.roll(x, shift, 1) * sb


def _const_spec(shape):
    nd = len(shape)
    return pl.BlockSpec(shape, lambda *_: (0,) * nd, pipeline_mode=pl.Buffered(1))


def _store_vt(vt_ref, vt):
    for j in range(ROW_TILE // ATTN_TILE):
        vt_ref[0, j] = vt[:, j * ATTN_TILE:(j + 1) * ATTN_TILE].astype(vt_ref.dtype)


def _vt_out(b, seq, width):
    tiles_per_seq = seq // ROW_TILE
    per_tile = ROW_TILE // ATTN_TILE
    spec = pl.BlockSpec((1, per_tile, width, ATTN_TILE),
                        lambda i: (i // tiles_per_seq, i % tiles_per_seq, 0, 0))
    return spec, jax.ShapeDtypeStruct((b, seq // ATTN_TILE, width, ATTN_TILE), BF16)


def _l0_in_kernel(x_ref, g_ref, w_ref, wt_ref, c_ref, sa_ref, sb_ref, cosq_ref, sinq_ref, o_ref, qvt_ref, *,
                  n_k_blocks, q_rows, scale):
    t = _rms(x_ref[...], g_ref[...]).astype(BF16)
    p = _dot(t, w_ref[...])
    c, sa, sb = c_ref[...], sa_ref[...], sb_ref[...]
    half = DIFF_ROT // 2
    for j in range(n_k_blocks):
        blk = _rope_block(p[:, j * LANES:(j + 1) * LANES], c, sa, sb, half)
        o_ref[:, j * LANES:(j + 1) * LANES] = blk.astype(o_ref.dtype)
    rest = n_k_blocks * LANES
    o_ref[:, rest:] = p[:, rest:].astype(o_ref.dtype)
    qv_t = _dot_nt(wt_ref[...], t)
    cos_t, sin_t = cosq_ref[...], sinq_ref[...]
    pieces = []
    for lo in range(0, q_rows, DIFF_HEAD_DIM):
        x1, x2 = qv_t[lo:lo + half, :], qv_t[lo + half:lo + 2 * half, :]
        pieces += [(x1 * cos_t - x2 * sin_t) * scale, (x2 * cos_t + x1 * sin_t) * scale,
                   qv_t[lo + 2 * half:lo + DIFF_HEAD_DIM, :] * scale]
    _store_vt(qvt_ref, jnp.concatenate(pieces + [qv_t[q_rows:, :]], axis=0))


def _l0_in(x2, g_pre, w_main, w_t, tabs, q_tabs_t, b, seq, q_rows):
    n, d = x2.shape
    n_out = w_main.shape[1]
    tiles_per_seq = seq // ROW_TILE
    kern = functools.partial(_l0_in_kernel, n_k_blocks=q_rows // LANES, q_rows=q_rows,
                             scale=DIFF_HEAD_DIM ** -0.5 * LOG2_E)
    tab_spec = pl.BlockSpec((ROW_TILE, LANES), lambda i: (i % tiles_per_seq, 0))
    qtab_spec = pl.BlockSpec((q_tabs_t[0].shape[0], ROW_TILE), lambda i: (0, i % tiles_per_seq))
    vt_spec, vt_shape = _vt_out(b, seq, w_t.shape[0])
    return pl.pallas_call(
        kern,
        grid=(n // ROW_TILE,),
        in_specs=[pl.BlockSpec((ROW_TILE, d), lambda i: (i, 0)),
                  _const_spec((1, d)), _const_spec(w_main.shape), _const_spec(w_t.shape),
                  tab_spec, tab_spec, tab_spec, qtab_spec, qtab_spec],
        out_specs=[pl.BlockSpec((ROW_TILE, n_out), lambda i: (i, 0)), vt_spec],
        out_shape=[jax.ShapeDtypeStruct((n, n_out), BF16), vt_shape],
        compiler_params=_params("parallel"),
        name="l0_in",
    )(x2, g_pre, w_main, w_t, *tabs, *q_tabs_t)


def _attn_body(q_heads, k_at, vt_at, m_ref, acc_ref, spre_ref, *, n_maps, q_transposed=False):
    i = pl.program_id(1)
    dk_axis = 0 if q_transposed else 1
    tq, dk = q_heads[0].shape[1 - dk_axis], q_heads[0].shape[dk_axis]
    map_width = dk // n_maps
    n_chains = len(q_heads) * n_maps
    qs = []
    for q in q_heads:
        if n_maps == 1:
            qs.append(q)
        else:
            pos = lax.broadcasted_iota(jnp.int32, q.shape, dk_axis)
            for c in range(n_maps):
                sel = (pos >= c * map_width) & (pos < (c + 1) * map_width)
                qs.append(jnp.where(sel, q, jnp.zeros_like(q)))
    m_ref[...] = jnp.full(m_ref.shape, -jnp.inf, F32)
    acc_ref[...] = jnp.zeros(acc_ref.shape, F32)
    ones = jnp.ones((ATTN_SUM_ROWS, tq), BF16)

    def scores(kb, ch):
        start = pl.multiple_of(kb * tq, tq)
        k = k_at(start, ch // n_maps)
        return _dot(k, qs[ch]) if q_transposed else _dot_nt(k, qs[ch])

    def step(kb, masked, prefetch):
        if masked:
            kc = lax.broadcasted_iota(jnp.int32, (tq, tq), 0) // CHUNK
            qc = lax.broadcasted_iota(jnp.int32, (tq, tq), 1) // CHUNK
            keep = kc <= qc
        ss = [spre_ref[ch] if ch < ATTN_LOOKAHEAD else None for ch in range(n_chains)]
        new = []
        for ch in range(n_chains):
            ahead = ch + ATTN_LOOKAHEAD
            if ahead < n_chains:
                ss[ahead] = scores(kb, ahead)
            elif prefetch:
                spre_ref[ahead - n_chains] = scores(kb + 1, ahead - n_chains)
            vt = jnp.concatenate([vt_at(kb, ch // n_maps), ones], axis=0)
            s = jnp.where(keep, ss[ch], -jnp.inf) if masked else ss[ch]
            m_old = m_ref[ch]
            m_new = jnp.maximum(m_old, jnp.max(s, axis=0, keepdims=True))
            alpha = jnp.exp2(m_old - m_new)
            p = jnp.exp2(s - m_new)
            new.append((m_new, alpha * acc_ref[ch] + _dot(vt, p.astype(vt.dtype))))
        for ch, (m_new, acc_new) in enumerate(new):
            m_ref[ch] = m_new
            acc_ref[ch] = acc_new

    for ch in range(ATTN_LOOKAHEAD):
        spre_ref[ch] = scores(0, ch)

    def loop_body(kb, carry):
        step(kb, False, True)
        return carry

    lax.fori_loop(0, i, loop_body, 0)
    step(i, True, False)


def _attn_flat(q_heads_at, k_at, vt_at, finish_tile, m_ref, acc_ref, *, n_tiles, n_maps, tq, q_transposed):
    dk_axis = 0 if q_transposed else 1
    qs_cache = {}

    def qs_of(i):
        if i not in qs_cache:
            qs = []
            for q in q_heads_at(i):
                if n_maps == 1:
                    qs.append(q)
                else:
                    width = q.shape[dk_axis] // n_maps
                    pos = lax.broadcasted_iota(jnp.int32, q.shape, dk_axis)
                    for c in range(n_maps):
                        qs.append(jnp.where((pos >= c * width) & (pos < (c + 1) * width), q, jnp.zeros_like(q)))
            qs_cache[i] = qs
        return qs_cache[i]

    n_chains = len(qs_of(0))
    ones = jnp.ones((ATTN_SUM_ROWS, tq), BF16)
    kc = lax.broadcasted_iota(jnp.int32, (tq, tq), 0) // CHUNK
    qc = lax.broadcasted_iota(jnp.int32, (tq, tq), 1) // CHUNK
    keep = kc <= qc
    items = [(i, kb, ch) for i in range(n_tiles) for kb in range(i + 1) for ch in range(n_chains)]

    def scores(i, kb, ch):
        k = k_at(kb, ch // n_maps)
        return _dot(k, qs_of(i)[ch]) if q_transposed else _dot_nt(k, qs_of(i)[ch])

    ss = {n: scores(*items[n]) for n in range(min(ATTN_LOOKAHEAD, len(items)))}
    for n, (i, kb, ch) in enumerate(items):
        if n + ATTN_LOOKAHEAD < len(items):
            ss[n + ATTN_LOOKAHEAD] = scores(*items[n + ATTN_LOOKAHEAD])
        s = ss.pop(n)
        if kb == i:
            s = jnp.where(keep, s, -jnp.inf)
        vt = jnp.concatenate([vt_at(kb, ch // n_maps), ones], axis=0)
        slot = i % 2
        if kb == 0:
            m_new = jnp.max(s, axis=0, keepdims=True)
            acc_new = _dot(vt, jnp.exp2(s - m_new).astype(vt.dtype))
        else:
            m_old = m_ref[slot, ch]
            m_new = jnp.maximum(m_old, jnp.max(s, axis=0, keepdims=True))
            acc_new = (jnp.exp2(m_old - m_new) * acc_ref[slot, ch]
                       + _dot(vt, jnp.exp2(s - m_new).astype(vt.dtype)))
        m_ref[slot, ch] = m_new
        acc_ref[slot, ch] = acc_new
        if kb == i and ch == n_chains - 1:
            finish_tile(i, slot)
            del qs_cache[i]


def _attn_scratch(n_chains, dv):
    return [pltpu.VMEM((n_chains, 1, ATTN_TILE), F32),
            pltpu.VMEM((n_chains, dv + ATTN_SUM_ROWS, ATTN_TILE), F32),
            pltpu.VMEM((ATTN_LOOKAHEAD, ATTN_TILE, ATTN_TILE), F32)]


def _attn_flat_scratch(n_chains, dv):
    return [pltpu.VMEM((2, n_chains, 1, ATTN_TILE), F32),
            pltpu.VMEM((2, n_chains, dv + ATTN_SUM_ROWS, ATTN_TILE), F32)]


def _attn_out(acc_ref, ch, dv):
    return acc_ref[ch, :dv, :] / acc_ref[ch, dv:dv + 1, :]


def _cast_plan(weights, n_steps):
    plan = []
    for w in weights:
        rows = w.shape[0]
        if rows % n_steps == 0 and (rows // n_steps) % BF16_SUBLANES == 0:
            plan.append((rows // n_steps, n_steps))
        else:
            plan.append((CAST_ROWS, rows // CAST_ROWS))
            assert rows % CAST_ROWS == 0 and rows // CAST_ROWS <= n_steps
    return plan


def _cast_specs(weights, plan, step_of):
    def spec(w, rows, n_blocks):
        return pl.BlockSpec((rows, w.shape[1]), lambda *g: (jnp.minimum(step_of(*g), n_blocks - 1), 0))
    specs = [spec(w, rows, nb) for w, (rows, nb) in zip(weights, plan)]
    shapes = [jax.ShapeDtypeStruct(w.shape, BF16) for w in weights]
    return specs, shapes


def _cast_blocks(step, plan, n_steps, src_refs, dst_refs):
    for (rows, n_blocks), src, dst in zip(plan, src_refs, dst_refs):
        if n_blocks == n_steps:
            dst[...] = src[...].astype(dst.dtype)
        else:
            @pl.when(step < n_blocks)
            def _(src=src, dst=dst):
                dst[...] = src[...].astype(dst.dtype)


def _split_cast_refs(refs, n_in, n_out, n_cast):
    main_in, cast_src = refs[:n_in], refs[n_in:n_in + n_cast]
    rest = refs[n_in + n_cast:]
    return main_in, cast_src, rest[:n_out], rest[n_out:n_out + n_cast], rest[n_out + n_cast:]


def _grid_step(n_inner):
    return pl.program_id(0) * n_inner + pl.program_id(1)


def _diff_attn_kernel(*refs, n_heads, lam_init, cast_plan, n_steps):
    (qt_ref, k_ref, vt_ref, lam_ref, sub_ref), cast_src, (o_ref,), cast_dst, (m_ref, acc_ref) = (
        _split_cast_refs(refs, 5, 1, len(cast_plan)))
    _cast_blocks(pl.program_id(0), cast_plan, n_steps, cast_src, cast_dst)
    n_tiles, tq = qt_ref.shape[1], qt_ref.shape[3]
    lv = lam_ref[...]
    lam = (jnp.exp(jnp.sum(lv[0:1] * lv[1:2], axis=-1, keepdims=True))
           - jnp.exp(jnp.sum(lv[2:3] * lv[3:4], axis=-1, keepdims=True)) + lam_init)

    def finish_tile(i, slot):
        for h in range(n_heads):
            o_t = _attn_out(acc_ref.at[slot], 2 * h, LANES) - lam * _attn_out(acc_ref.at[slot], 2 * h + 1, LANES)
            inv = lax.rsqrt(jnp.mean(o_t * o_t, axis=0, keepdims=True) + RMS_EPS)
            y_t = o_t * inv * sub_ref[...] * (1.0 - lam_init)
            o_ref[0, i * tq:(i + 1) * tq, h * LANES:(h + 1) * LANES] = y_t.T.astype(o_ref.dtype)

    _attn_flat(lambda i: [qt_ref[0, i, h * LANES:(h + 1) * LANES, :] for h in range(n_heads)],
               lambda kb, h: k_ref[0, kb * tq:(kb + 1) * tq, h * LANES:(h + 1) * LANES],
               lambda kb, h: vt_ref[0, kb, h * LANES:(h + 1) * LANES, :],
               finish_tile, m_ref, acc_ref, n_tiles=n_tiles, n_maps=2, tq=tq, q_transposed=True)


def _diff_attn(p0, qvt, lam_vecs, subln_col, n_heads, cast_weights):
    b, s, _ = p0.shape
    n_kb = s // ATTN_TILE
    width = n_heads * LANES
    plan = _cast_plan(cast_weights, b)
    cast_specs, cast_shapes = _cast_specs(cast_weights, plan, lambda bb: bb)
    kern = functools.partial(_diff_attn_kernel, n_heads=n_heads, lam_init=0.8 - 0.6 * math.exp(-0.3 * 0),
                             cast_plan=tuple(plan), n_steps=b)
    out = pl.pallas_call(
        kern,
        grid=(b,),
        in_specs=[pl.BlockSpec((1, n_kb, width, ATTN_TILE), lambda bb: (bb, 0, 0, 0)),
                  pl.BlockSpec((1, s, width), lambda bb: (bb, 0, 0)),
                  pl.BlockSpec((1, n_kb, width, ATTN_TILE), lambda bb: (bb, 0, 1, 0)),
                  pl.BlockSpec(lam_vecs.shape, lambda bb: (0, 0)),
                  pl.BlockSpec(subln_col.shape, lambda bb: (0, 0))] + cast_specs,
        out_specs=[pl.BlockSpec((1, s, width), lambda bb: (bb, 0, 0))] + cast_specs,
        out_shape=[jax.ShapeDtypeStruct((b, s, width), BF16)] + cast_shapes,
        scratch_shapes=_attn_flat_scratch(2 * n_heads, LANES),
        compiler_params=_params("arbitrary"),
        name="diff_attn",
    )(qvt, p0, qvt, lam_vecs, subln_col, *cast_weights)
    return out[0], out[1:]


def _mla_attn_kernel(*refs, n_heads, scale, cast_plan, n_steps, n_inner):
    ((cq_ref, kv_ref, qn_ref, kvn_ref, wqt_ref, wk_ref, wvt_ref, cosq_tab, sinq_tab, ck_tab, sak_tab, sbk_tab),
     cast_src, (o_ref,), cast_dst, (m_ref, acc_ref, spre_ref, k_scr, vt_scr)) = (
        _split_cast_refs(refs, 12, 1, len(cast_plan)))
    _cast_blocks(_grid_step(n_inner), cast_plan, n_steps, cast_src, cast_dst)
    seq = kv_ref.shape[1]
    tq = cq_ref.shape[1]
    half = MLA_ROPE // 2

    @pl.when(pl.program_id(1) == 0)
    def _project_keys():
        for r0 in range(0, seq, ROW_TILE):
            rows = slice(r0, r0 + ROW_TILE)
            ckv = _rms(kv_ref[0, rows, :LANES], kvn_ref[...]).astype(BF16)
            kr = _rope_block(kv_ref[0, rows, LANES:], ck_tab[rows, :], sak_tab[rows, :], sbk_tab[rows, :], half)
            k_scr[rows, :] = _dot(jnp.concatenate([ckv, kr.astype(BF16)], axis=-1), wk_ref[...]).astype(BF16)
            vt = _dot_nt(wvt_ref[...], ckv)
            for j in range(ROW_TILE // tq):
                vt_scr[r0 // tq + j] = vt[:, j * tq:(j + 1) * tq].astype(BF16)

    q_t = _dot_nt(wqt_ref[...], _rms(cq_ref[0], qn_ref[...]).astype(BF16))
    cos_t, sin_t = cosq_tab[...], sinq_tab[...]
    q_heads = []
    for h in range(n_heads):
        lo = h * MLA_QK_PAD
        x1 = q_t[lo + MLA_NOPE:lo + MLA_NOPE + half, :]
        x2 = q_t[lo + MLA_NOPE + half:lo + MLA_NOPE + 2 * half, :]
        q_h = jnp.concatenate([q_t[lo:lo + MLA_NOPE, :], x1 * cos_t - x2 * sin_t, x2 * cos_t + x1 * sin_t,
                               q_t[lo + MLA_NOPE + 2 * half:lo + MLA_QK_PAD, :]], axis=0)
        q_heads.append((q_h * scale).astype(BF16))

    _attn_body(q_heads,
               lambda start, h: k_scr[pl.ds(start, tq), h * MLA_QK_PAD:(h + 1) * MLA_QK_PAD],
               lambda kb, h: vt_scr[kb, h * MLA_V:(h + 1) * MLA_V, :],
               m_ref, acc_ref, spre_ref, n_maps=1, q_transposed=True)
    for h in range(n_heads):
        o_ref[0, :, h * MLA_V:(h + 1) * MLA_V] = _attn_out(acc_ref, h, MLA_V).T.astype(o_ref.dtype)


def _mla_attn(p1, q_norm, kv_norm, wq_t, wk, wv_t, q_tabs_t, tabs, n_heads, cast_weights):
    b, s, _ = p1.shape
    n_kb = s // ATTN_TILE
    plan = _cast_plan(cast_weights, b * n_kb)
    cast_specs, cast_shapes = _cast_specs(cast_weights, plan, lambda bb, i: bb * n_kb + i)
    kern = functools.partial(_mla_attn_kernel, n_heads=n_heads, scale=(MLA_NOPE + MLA_ROPE) ** -0.5 * LOG2_E,
                             cast_plan=tuple(plan), n_steps=b * n_kb, n_inner=n_kb)
    const = lambda a: pl.BlockSpec(a.shape, lambda bb, i: (0,) * a.ndim)
    q_tab = pl.BlockSpec((q_tabs_t[0].shape[0], ATTN_TILE), lambda bb, i: (0, i))
    out = pl.pallas_call(
        kern,
        grid=(b, n_kb),
        in_specs=[pl.BlockSpec((1, ATTN_TILE, 2 * LANES), lambda bb, i: (bb, i, 1)),
                  pl.BlockSpec((1, s, 2 * LANES), lambda bb, i: (bb, 0, 2)),
                  const(q_norm), const(kv_norm), const(wq_t), const(wk), const(wv_t),
                  q_tab, q_tab, const(tabs[0]), const(tabs[1]), const(tabs[2])] + cast_specs,
        out_specs=[pl.BlockSpec((1, ATTN_TILE, n_heads * MLA_V), lambda bb, i: (bb, i, 0))] + cast_specs,
        out_shape=[jax.ShapeDtypeStruct((b, s, n_heads * MLA_V), BF16)] + cast_shapes,
        scratch_shapes=_attn_scratch(n_heads, MLA_V) + [
            pltpu.VMEM((s, n_heads * MLA_QK_PAD), BF16),
            pltpu.VMEM((n_kb, n_heads * MLA_V, ATTN_TILE), BF16)],
        compiler_params=_params("arbitrary", "arbitrary"),
        name="mla_attn",
    )(p1, p1, q_norm, kv_norm, wq_t, wk, wv_t, *q_tabs_t, *tabs, *cast_weights)
    return out[0], out[1:]


def _conv_kernel(a_ref, gate_ref, ah_ref, gh_ref, w_ref, b_ref, lg_ref, lb_ref, o_ref, u_ref, ur_ref):
    tt = a_ref.shape[1]
    u_ref[CONV_HALO:, :] = a_ref[0].astype(F32) * _sigmoid(gate_ref[0].astype(F32))
    halo = ah_ref[0].astype(F32) * _sigmoid(gh_ref[0].astype(F32))
    u_ref[:CONV_HALO, :] = jnp.where(pl.program_id(1) > 0, halo, jnp.zeros_like(halo))
    rows = ur_ref.shape[1]
    for r in range(1, SUBLANES):
        ur_ref[r - 1] = u_ref[r:r + rows, :]
    acc = jnp.zeros((tt, a_ref.shape[2]), F32)
    first = CONV_HALO - (CONV_WIDTH - 1)
    for k in range(CONV_WIDTH):
        base, r = divmod(first + k, SUBLANES)
        src = u_ref if r == 0 else ur_ref.at[r - 1]
        acc = acc + src[base * SUBLANES:base * SUBLANES + tt, :] * w_ref[k:k + 1, :]
    y = acc + b_ref[...]
    mu = jnp.mean(y, axis=-1, keepdims=True)
    yc = y - mu
    yn = yc * lax.rsqrt(jnp.mean(yc * yc, axis=-1, keepdims=True) + LN_EPS) * lg_ref[...] + lb_ref[...]
    o_ref[0] = (yn * _sigmoid(yn)).astype(o_ref.dtype)


def _conv_module(p0, dw_w, dw_b, ln_g, ln_b, col0):
    b, s, _ = p0.shape
    c = dw_w.shape[1]
    a_blk, g_blk = col0 // c, col0 // c + 1
    ratio = CONV_TILE // CONV_HALO
    main = lambda blk: pl.BlockSpec((1, CONV_TILE, c), lambda bb, t: (bb, t, blk))
    halo = lambda blk: pl.BlockSpec((1, CONV_HALO, c), lambda bb, t: (bb, jnp.maximum(t * ratio - 1, 0), blk))
    vec = lambda a: pl.BlockSpec(a.shape, lambda bb, t: (0, 0))
    return pl.pallas_call(
        _conv_kernel,
        grid=(b, s // CONV_TILE),
        in_specs=[main(a_blk), main(g_blk), halo(a_blk), halo(g_blk),
                  vec(dw_w), vec(dw_b), vec(ln_g), vec(ln_b)],
        out_specs=pl.BlockSpec((1, CONV_TILE, c), lambda bb, t: (bb, t, 0)),
        out_shape=jax.ShapeDtypeStruct((b, s, c), BF16),
        scratch_shapes=[pltpu.VMEM((CONV_HALO + CONV_TILE, c), F32),
                        pltpu.VMEM((SUBLANES - 1, CONV_HALO + CONV_TILE - SUBLANES, c), F32)],
        compiler_params=_params("parallel", "parallel"),
        name="conv_module",
    )(p0, p0, p0, p0, dw_w, dw_b, ln_g, ln_b)


def _post_kernel(*refs, n_mix, with_next):
    x_ref = refs[0]
    mix_refs = refs[1:1 + n_mix]
    (wo_ref, gpost_ref, gfpre_ref, wg_ref, wu_ref, wd_ref, gfpost_ref) = refs[1 + n_mix:8 + n_mix]
    pos = 8 + n_mix
    if with_next:
        gnext_ref, wnext_ref = refs[pos:pos + 2]
        pos += 2
    h_out_ref = refs[pos]
    pos += 1
    if with_next:
        p_out_ref = refs[pos]
        pos += 1
    hid_ref = refs[pos]

    n_rows = x_ref.shape[0]
    groups = [slice(r0, r0 + n_rows // POST_ROW_GROUPS) for r0 in range(0, n_rows, n_rows // POST_ROW_GROUPS)]

    def out_proj(rows):
        y = None
        row = 0
        for r in mix_refs:
            w = r.shape[-1]
            part = _dot(r[rows, :], wo_ref[row:row + w, :])
            y = part if y is None else y + part
            row += w
        return y

    ys = [out_proj(rows) for rows in groups]
    h1s = [x_ref[rows, :] + _rms(y, gpost_ref[...]) for rows, y in zip(groups, ys)]
    ts = [_rms(h1, gfpre_ref[...]).astype(BF16) for h1 in h1s]
    d_ff = wg_ref.shape[1]
    for j in range(0, d_ff, FFN_CHUNK):
        wdt = min(FFN_CHUNK, d_ff - j)
        for rows, t in zip(groups, ts):
            gate = _dot(t, wg_ref[:, j:j + wdt])
            up = _dot(t, wu_ref[:, j:j + wdt])
            hid_ref[rows, j:j + wdt] = (gate * _sigmoid(gate) * up).astype(BF16)
    fs = [_dot(hid_ref[rows, :], wd_ref[...]) for rows in groups]
    h2s = [h1 + _rms(f, gfpost_ref[...]) for h1, f in zip(h1s, fs)]
    for rows, h2 in zip(groups, h2s):
        h_out_ref[rows, :] = h2
    if with_next:
        t2s = [_rms(h2, gnext_ref[...]).astype(BF16) for h2 in h2s]
        for rows, t2 in zip(groups, t2s):
            p_out_ref[rows, :] = _dot(t2, wnext_ref[...])


def _post(x2, mix_parts, w_out, g_post, g_fpre, w_gate, w_up, w_down, g_fpost, nxt=None):
    n, d = x2.shape
    d_ff = w_gate.shape[1]
    row = lambda a: pl.BlockSpec((ROW_TILE, a.shape[1]), lambda i: (i, 0))
    consts = [w_out, g_post, g_fpre, w_gate, w_up, w_down, g_fpost] + (list(nxt) if nxt else [])
    out_shape = [jax.ShapeDtypeStruct((n, d), F32)]
    out_specs = [pl.BlockSpec((ROW_TILE, d), lambda i: (i, 0))]
    if nxt:
        n_next = nxt[1].shape[1]
        out_shape.append(jax.ShapeDtypeStruct((n, n_next), F32))
        out_specs.append(pl.BlockSpec((ROW_TILE, n_next), lambda i: (i, 0)))
    kern = functools.partial(_post_kernel, n_mix=len(mix_parts), with_next=bool(nxt))
    return pl.pallas_call(
        kern,
        grid=(n // ROW_TILE,),
        in_specs=[row(x2)] + [row(m) for m in mix_parts] + [_const_spec(c.shape) for c in consts],
        out_specs=out_specs,
        out_shape=out_shape,
        scratch_shapes=[pltpu.VMEM((ROW_TILE, d_ff), BF16)],
        compiler_params=_params("parallel"),
        name="post_next" if nxt else "post",
    )(x2, *mix_parts, *consts)


def _gelu_tanh(x):
    return 0.5 * x * (1.0 + jnp.tanh(math.sqrt(2.0 / math.pi) * (x + 0.044715 * (x * x * x))))


def _ssm_kernel(u_ref, lr_ref, li_ref, ldt_ref, bre_ref, bim_ref, cre_ref, cim_ref, d_ref, wg_ref, bg_ref,
                o_ref, bmat_ref, cmat_ref, a_ref, st_ref, us_ref, utm_ref, x_ref, y_ref):
    nb, tt, ch = u_ref.shape
    n_state = lr_ref.shape[1]

    @pl.when(pl.program_id(0) == 0)
    def _init():
        lr, li = lr_ref[...], li_ref[...]
        dt = jnp.exp(ldt_ref[...])
        mag = jnp.exp(lr * dt)
        ab_re = mag * jnp.cos(li * dt)
        ab_im = mag * jnp.sin(li * dt)
        den = lr * lr + li * li
        n_re = ab_re - 1.0
        f_re = (n_re * lr + ab_im * li) / den
        f_im = (ab_im * lr - n_re * li) / den
        br, bi = bre_ref[...], bim_ref[...]
        bmat_ref[:, :n_state] = (f_re * br - f_im * bi).astype(BF16)
        bmat_ref[:, n_state:] = (f_re * bi + f_im * br).astype(BF16)
        cmat_ref[:n_state, :] = cre_ref[...].astype(BF16)
        cmat_ref[n_state:, :] = (-cim_ref[...]).astype(BF16)
        a_ref[0:1, :] = ab_re
        a_ref[1:2, :] = ab_im
        st_ref[...] = jnp.zeros(st_ref.shape, F32)

    n_ublk = ch // LANES
    for b in range(nb):
        for j in range(n_ublk):
            us_ref[j, b * SSM_PITCH:b * SSM_PITCH + tt, :] = u_ref[b, :, j * LANES:(j + 1) * LANES]

    def gather_step(t, carry):
        dst = pl.multiple_of(t * nb, nb)
        for j in range(n_ublk):
            utm_ref[pl.ds(dst, nb), j * LANES:(j + 1) * LANES] = us_ref[j, pl.ds(t, nb, stride=SSM_PITCH), :]
        return carry

    lax.fori_loop(0, tt, gather_step, 0, unroll=8)

    half_rows = tt * nb // 2
    for r0 in (0, half_rows):
        x_ref[r0:r0 + half_rows, :] = _dot(utm_ref[r0:r0 + half_rows, :].astype(BF16), bmat_ref[...])

    a_re = jnp.broadcast_to(a_ref[0:1, :], (nb, n_state))
    a_im = jnp.broadcast_to(a_ref[1:2, :], (nb, n_state))

    def scan_step(t, carry):
        x_re, x_im = carry
        row = pl.multiple_of(t * nb, nb)
        n_re = a_re * x_re - a_im * x_im + x_ref[pl.ds(row, nb), :n_state]
        n_im = a_re * x_im + a_im * x_re + x_ref[pl.ds(row, nb), n_state:]
        x_ref[pl.ds(row, nb), :n_state] = n_re
        x_ref[pl.ds(row, nb), n_state:] = n_im
        return n_re, n_im

    x_re, x_im = lax.fori_loop(0, tt, scan_step, (st_ref[:, :n_state], st_ref[:, n_state:]), unroll=4)
    st_ref[:, :n_state] = x_re
    st_ref[:, n_state:] = x_im

    n_yblk = ch // LANES
    for r0 in (0, half_rows):
        y_tm = _dot(x_ref[r0:r0 + half_rows, :].astype(BF16), cmat_ref[...])
        for j in range(n_yblk):
            y_ref[j, r0:r0 + half_rows, :] = y_tm[:, j * LANES:(j + 1) * LANES]
    for b in range(nb):
        y = jnp.concatenate([y_ref[j, pl.ds(b, tt, stride=nb), :] for j in range(n_yblk)], axis=-1)
        y = y + d_ref[...] * u_ref[b]
        z = _gelu_tanh(y)
        gate = _dot(z.astype(BF16), wg_ref[...]) + bg_ref[...]
        o_ref[b] = (z * _sigmoid(gate)).astype(o_ref.dtype)


def _ssm(p1, rows, b_bd, c_bd, d_row, w_glu, b_glu):
    b, s, _ = p1.shape
    ch = w_glu.shape[0]
    n_state = rows[0].shape[1]
    consts = list(rows) + list(b_bd) + list(c_bd) + [d_row, w_glu, b_glu]
    return pl.pallas_call(
        _ssm_kernel,
        grid=(s // SSM_TILE,),
        in_specs=[pl.BlockSpec((b, SSM_TILE, ch), lambda t: (0, t, 0))] + [_const_spec(c.shape) for c in consts],
        out_specs=pl.BlockSpec((b, SSM_TILE, ch), lambda t: (0, t, 0)),
        out_shape=jax.ShapeDtypeStruct((b, s, ch), BF16),
        scratch_shapes=[pltpu.VMEM((ch, 2 * n_state), BF16),
                        pltpu.VMEM((2 * n_state, ch), BF16),
                        pltpu.VMEM((SUBLANES, n_state), F32),
                        pltpu.VMEM((b, 2 * n_state), F32),
                        pltpu.VMEM((ch // LANES, b * SSM_PITCH, LANES), F32),
                        pltpu.VMEM((SSM_TILE * b, ch), F32),
                        pltpu.VMEM((SSM_TILE * b, 2 * n_state), F32),
                        pltpu.VMEM((ch // LANES, SSM_TILE * b, LANES), F32)],
        compiler_params=_params("arbitrary"),
        name="s5_ssm",
    )(p1, *consts)


def _rope_tables(s, rot_dim, theta):
    inv = theta ** (-jnp.arange(0, rot_dim, 2, dtype=F32) / rot_dim)
    ang = jnp.arange(s, dtype=F32)[:, None] * inv[None, :]
    return jnp.cos(ang), jnp.sin(ang)


def _lane_tables(cos, sin, period):
    s, half = cos.shape
    reps = LANES // period
    one = jnp.ones((s, period - 2 * half), F32)
    zero = jnp.zeros((s, period - 2 * half), F32)
    zh = jnp.zeros((s, half), F32)
    c = jnp.tile(jnp.concatenate([cos, cos, one], axis=1), (1, reps))
    sa = jnp.tile(jnp.concatenate([-sin, zh, zero], axis=1), (1, reps))
    sb = jnp.tile(jnp.concatenate([zh, sin, zero], axis=1), (1, reps))
    return c, sa, sb


def _block_diag(blocks):
    g, r, c = blocks.shape
    eye = jnp.eye(g, dtype=blocks.dtype)
    return (eye[:, None, :, None] * blocks[:, :, None, :]).reshape(g * r, g * c)


def kernel(x, l0_mix_pre, l0_mix_post, l0_w_in, l0_lambda_q1, l0_lambda_k1, l0_lambda_q2, l0_lambda_k2, l0_subln, l0_dw_w, l0_dw_b, l0_conv_ln_g, l0_conv_ln_b, l0_w_out, l0_ffn_pre, l0_ffn_post, l0_w_gate, l0_w_up, l0_w_down, l1_mix_pre, l1_mix_post, l1_w_in, l1_a_re, l1_a_im, l1_log_dt, l1_b_re, l1_b_im, l1_c_re, l1_c_im, l1_d_skip, l1_w_glu, l1_b_glu, l1_q_norm, l1_w_uq, l1_kv_norm, l1_w_ukv, l1_w_out, l1_ffn_pre, l1_ffn_post, l1_w_gate, l1_w_up, l1_w_down):
    b, s, d = x.shape
    n = b * s
    row = lambda v: v.reshape(1, -1).astype(F32)
    bf = lambda w: w.astype(BF16)

    diff_width = 4 * LANES
    n_diff_heads = diff_width // LANES
    conv_ch = l0_dw_w.shape[1]
    ssm_ch = l1_w_glu.shape[0]
    n_groups, n_state_g = l1_a_re.shape
    q_rank = l1_q_norm.shape[0]
    kv_rank = l1_kv_norm.shape[0]
    n_mla_heads = l1_w_uq.shape[1] // (MLA_NOPE + MLA_ROPE)

    cos_a, sin_a = _rope_tables(s, DIFF_ROT, ROPE_THETA)
    tabs_a = _lane_tables(cos_a, sin_a, period=DIFF_HEAD_DIM)
    x2 = x.reshape(n, d)
    w_main = bf(jnp.concatenate([l0_w_in[:, diff_width:2 * diff_width], l0_w_in[:, 3 * diff_width:]], axis=1))
    w_t = bf(jnp.concatenate([l0_w_in[:, :diff_width], l0_w_in[:, 2 * diff_width:3 * diff_width]], axis=1).T)
    p0, qvt_a = _l0_in(x2, row(l0_mix_pre), w_main, w_t, tabs_a, (cos_a.T, sin_a.T), b, s, diff_width)
    p0 = p0.reshape(b, s, -1)
    lam_vecs = jnp.stack([l0_lambda_q1, l0_lambda_k1, l0_lambda_q2, l0_lambda_k2]).astype(F32)
    y_a, (w_out0, w_gate0, w_up0, w_down0) = _diff_attn(
        p0, qvt_a, lam_vecs, l0_subln.reshape(-1, 1).astype(F32), n_diff_heads,
        [l0_w_out, l0_w_gate, l0_w_up, l0_w_down])
    y_b = _conv_module(p0, l0_dw_w.astype(F32), row(l0_dw_b), row(l0_conv_ln_g), row(l0_conv_ln_b),
                       col0=diff_width)

    pad = (-l1_w_in.shape[1]) % LANES
    w_in1 = bf(jnp.pad(l1_w_in, ((0, 0), (0, pad))))
    h2, p1 = _post(x2, [y_a.reshape(n, -1), y_b.reshape(n, -1)], w_out0, row(l0_mix_post),
                   row(l0_ffn_pre), w_gate0, w_up0, w_down0, row(l0_ffn_post),
                   nxt=(row(l1_mix_pre), w_in1))

    state_row = lambda a: a.reshape(1, -1).astype(F32)
    ssm_rows = (state_row(l1_a_re), state_row(l1_a_im),
                state_row(jnp.broadcast_to(l1_log_dt[:, None], (n_groups, n_state_g))))
    b_bd = tuple(_block_diag(jnp.swapaxes(m, 1, 2).astype(F32)) for m in (l1_b_re, l1_b_im))
    c_bd = tuple(_block_diag(jnp.swapaxes(m, 1, 2).astype(F32)) for m in (l1_c_re, l1_c_im))
    y_c = _ssm(p1.reshape(b, s, -1), ssm_rows, b_bd, c_bd, row(l1_d_skip), bf(l1_w_glu), row(l1_b_glu))

    wq = l1_w_uq.reshape(q_rank, n_mla_heads, MLA_NOPE + MLA_ROPE)
    wq = jnp.pad(wq, ((0, 0), (0, 0), (0, MLA_QK_PAD - MLA_NOPE - MLA_ROPE))).reshape(q_rank, -1)
    wkv = l1_w_ukv.reshape(kv_rank, n_mla_heads, MLA_NOPE + MLA_V)
    wk_nope = jnp.pad(wkv[:, :, :MLA_NOPE], ((0, 0), (0, 0), (0, MLA_QK_PAD - MLA_NOPE)))
    route = jnp.pad(jnp.eye(MLA_ROPE, dtype=F32), ((0, LANES - MLA_ROPE), (MLA_NOPE, MLA_QK_PAD - MLA_NOPE - MLA_ROPE)))
    wk_rope = jnp.broadcast_to(route[:, None, :], (LANES, n_mla_heads, MLA_QK_PAD))
    wk = jnp.concatenate([wk_nope, wk_rope], axis=0).reshape(kv_rank + LANES, -1)
    wv_t = wkv[:, :, MLA_NOPE:].reshape(kv_rank, -1).T
    cos_d, sin_d = _rope_tables(s, MLA_ROPE, MLA_ROPE_THETA)
    tabs_d = _lane_tables(cos_d, sin_d, period=LANES)
    y_d, (w_out1, w_gate1, w_up1, w_down1) = _mla_attn(
        p1.reshape(b, s, -1), row(l1_q_norm), row(l1_kv_norm), bf(wq.T), bf(wk), bf(wv_t),
        (cos_d.T, sin_d.T), tabs_d, n_mla_heads, [l1_w_out, l1_w_gate, l1_w_up, l1_w_down])

    (out,) = _post(h2, [y_c.reshape(n, -1), y_d.reshape(n, -1)], w_out1, row(l1_mix_post),
                   row(l1_ffn_pre), w_gate1, w_up1, w_down1, row(l1_ffn_post))
    return out.reshape(b, s, d)
```

```python
import functools
import math

import jax
import jax.numpy as jnp
from jax import lax
from jax.experimental import pallas as pl
from jax.experimental.pallas import tpu as pltpu

F32 = jnp.float32
BF16 = jnp.bfloat16

LANES = 128
SUBLANES = 8
BF16_SUBLANES = 16
VMEM_LIMIT_BYTES = 56 * 1024 * 1024

CHUNK = 64
RMS_EPS = 1e-6
LN_EPS = 1e-5
ROPE_THETA = 500000.0
MLA_ROPE_THETA = 10000.0
DIFF_HEAD_DIM = 64
DIFF_ROT = 16
CONV_WIDTH = 31
CONV_HALO = 32
SSM_GROUP = 16
SSM_STATE = 64
MLA_NOPE = 128
MLA_ROPE = 64
MLA_V = 128
MLA_QK_PAD = 256

ROW_TILE = 512
ATTN_TILE = 256
ATTN_LOOKAHEAD = 3
ATTN_SUM_ROWS = 16
LOG2_E = math.log2(math.e)
CONV_TILE = 256
SSM_TILE = 128
SSM_PITCH = SSM_TILE + SUBLANES
FFN_CHUNK = 512
POST_ROW_GROUPS = 2


def _params(*sem):
    return pltpu.CompilerParams(dimension_semantics=sem, vmem_limit_bytes=VMEM_LIMIT_BYTES)


def _rms(x, g):
    return x * lax.rsqrt(jnp.mean(x * x, axis=-1, keepdims=True) + RMS_EPS) * g


def _sigmoid(x):
    return 1.0 / (1.0 + jnp.exp(-x))


def _dot(a, b):
    return jnp.dot(a, b, preferred_element_type=F32)


def _dot_nt(a, b):
    return lax.dot_general(a, b, (((1,), (1,)), ((), ())), preferred_element_type=F32)


def _rope_block(x, c, sa, sb, shift):
    return x * c + pltpu.roll(x, LANES - shift, 1) * sa + pltpu.roll(x, shift, 1) * sb


def _const_spec(shape):
    nd = len(shape)
    return pl.BlockSpec(shape, lambda *_: (0,) * nd, pipeline_mode=pl.Buffered(1))


def _store_vt(vt_ref, vt):
    for j in range(ROW_TILE // ATTN_TILE):
        vt_ref[0, j] = vt[:, j * ATTN_TILE:(j + 1) * ATTN_TILE].astype(vt_ref.dtype)


def _vt_out(b, seq, width):
    tiles_per_seq = seq // ROW_TILE
    per_tile = ROW_TILE // ATTN_TILE
    spec = pl.BlockSpec((1, per_tile, width, ATTN_TILE),
                        lambda i: (i // tiles_per_seq, i % tiles_per_seq, 0, 0))
    return spec, jax.ShapeDtypeStruct((b, seq // ATTN_TILE, width, ATTN_TILE), BF16)


def _l0_in_kernel(x_ref, g_ref, w_ref, wt_ref, c_ref, sa_ref, sb_ref, cosq_ref, sinq_ref, o_ref, qvt_ref, *,
                  n_k_blocks, q_rows, scale):
    t = _rms(x_ref[...], g_ref[...]).astype(BF16)
    p = _dot(t, w_ref[...])
    c, sa, sb = c_ref[...], sa_ref[...], sb_ref[...]
    half = DIFF_ROT // 2
    for j in range(n_k_blocks):
        blk = _rope_block(p[:, j * LANES:(j + 1) * LANES], c, sa, sb, half)
        o_ref[:, j * LANES:(j + 1) * LANES] = blk.astype(o_ref.dtype)
    rest = n_k_blocks * LANES
    o_ref[:, rest:] = p[:, rest:].astype(o_ref.dtype)
    qv_t = _dot_nt(wt_ref[...], t)
    cos_t, sin_t = cosq_ref[...], sinq_ref[...]
    pieces = []
    for lo in range(0, q_rows, DIFF_HEAD_DIM):
        x1, x2 = qv_t[lo:lo + half, :], qv_t[lo + half:lo + 2 * half, :]
        pieces += [(x1 * cos_t - x2 * sin_t) * scale, (x2 * cos_t + x1 * sin_t) * scale,
                   qv_t[lo + 2 * half:lo + DIFF_HEAD_DIM, :] * scale]
    _store_vt(qvt_ref, jnp.concatenate(pieces + [qv_t[q_rows:, :]], axis=0))


def _l0_in(x2, g_pre, w_main, w_t, tabs, q_tabs_t, b, seq, q_rows):
    n, d = x2.shape
    n_out = w_main.shape[1]
    tiles_per_seq = seq // ROW_TILE
    kern = functools.partial(_l0_in_kernel, n_k_blocks=q_rows // LANES, q_rows=q_rows,
                             scale=DIFF_HEAD_DIM ** -0.5 * LOG2_E)
    tab_spec = pl.BlockSpec((ROW_TILE, LANES), lambda i: (i % tiles_per_seq, 0))
    qtab_spec = pl.BlockSpec((q_tabs_t[0].shape[0], ROW_TILE), lambda i: (0, i % tiles_per_seq))
    vt_spec, vt_shape = _vt_out(b, seq, w_t.shape[0])
    return pl.pallas_call(
        kern,
        grid=(n // ROW_TILE,),
        in_specs=[pl.BlockSpec((ROW_TILE, d), lambda i: (i, 0)),
                  _const_spec((1, d)), _const_spec(w_main.shape), _const_spec(w_t.shape),
                  tab_spec, tab_spec, tab_spec, qtab_spec, qtab_spec],
        out_specs=[pl.BlockSpec((ROW_TILE, n_out), lambda i: (i, 0)), vt_spec],
        out_shape=[jax.ShapeDtypeStruct((n, n_out), BF16), vt_shape],
        compiler_params=_params("parallel"),
        name="l0_in",
    )(x2, g_pre, w_main, w_t, *tabs, *q_tabs_t)


def _attn_flat(q_heads_at, k_at, vt_at, finish_tile, m_ref, acc_ref, *, n_tiles, n_maps, tq, q_transposed):
    dk_axis = 0 if q_transposed else 1
    qs_cache = {}

    def qs_of(i):
        if i not in qs_cache:
            qs = []
            for q in q_heads_at(i):
                if n_maps == 1:
                    qs.append(q)
                else:
                    width = q.shape[dk_axis] // n_maps
                    pos = lax.broadcasted_iota(jnp.int32, q.shape, dk_axis)
                    for c in range(n_maps):
                        qs.append(jnp.where((pos >= c * width) & (pos < (c + 1) * width), q, jnp.zeros_like(q)))
            qs_cache[i] = qs
        return qs_cache[i]

    n_chains = len(qs_of(0))
    ones = jnp.ones((ATTN_SUM_ROWS, tq), BF16)
    kc = lax.broadcasted_iota(jnp.int32, (tq, tq), 0) // CHUNK
    qc = lax.broadcasted_iota(jnp.int32, (tq, tq), 1) // CHUNK
    keep = kc <= qc
    items = [(i, kb, ch) for i in range(n_tiles) for kb in range(i + 1) for ch in range(n_chains)]

    def scores(i, kb, ch):
        k = k_at(kb, ch // n_maps)
        return _dot(k, qs_of(i)[ch]) if q_transposed else _dot_nt(k, qs_of(i)[ch])

    ss = {n: scores(*items[n]) for n in range(min(ATTN_LOOKAHEAD, len(items)))}
    for n, (i, kb, ch) in enumerate(items):
        if n + ATTN_LOOKAHEAD < len(items):
            ss[n + ATTN_LOOKAHEAD] = scores(*items[n + ATTN_LOOKAHEAD])
        s = ss.pop(n)
        if kb == i:
            s = jnp.where(keep, s, -jnp.inf)
        vt = jnp.concatenate([vt_at(kb, ch // n_maps), ones], axis=0)
        slot = i % 2
        if kb == 0:
            m_new = jnp.max(s, axis=0, keepdims=True)
            acc_new = _dot(vt, jnp.exp2(s - m_new).astype(vt.dtype))
        else:
            m_old = m_ref[slot, ch]
            m_new = jnp.maximum(m_old, jnp.max(s, axis=0, keepdims=True))
            acc_new = (jnp.exp2(m_old - m_new) * acc_ref[slot, ch]
                       + _dot(vt, jnp.exp2(s - m_new).astype(vt.dtype)))
        m_ref[slot, ch] = m_new
        acc_ref[slot, ch] = acc_new
        if kb == i and ch == n_chains - 1:
            finish_tile(i, slot)
            del qs_cache[i]


def _attn_flat_scratch(n_chains, dv):
    return [pltpu.VMEM((2, n_chains, 1, ATTN_TILE), F32),
            pltpu.VMEM((2, n_chains, dv + ATTN_SUM_ROWS, ATTN_TILE), F32)]


def _attn_out(acc_ref, ch, dv):
    return acc_ref[ch, :dv, :] / acc_ref[ch, dv:dv + 1, :]


def _cast_specs(weights, n_steps):
    specs = []
    for w in weights:
        rows = w.shape[0] // n_steps
        assert rows * n_steps == w.shape[0] and rows % BF16_SUBLANES == 0, w.shape
        specs.append(pl.BlockSpec((rows, w.shape[1]), lambda bb: (bb, 0)))
    return specs, [jax.ShapeDtypeStruct(w.shape, BF16) for w in weights]


def _cast_blocks(src_refs, dst_refs):
    for src, dst in zip(src_refs, dst_refs):
        dst[...] = src[...].astype(dst.dtype)


def _split_cast_refs(refs, n_in, n_out, n_cast):
    main_in, cast_src = refs[:n_in], refs[n_in:n_in + n_cast]
    rest = refs[n_in + n_cast:]
    return main_in, cast_src, rest[:n_out], rest[n_out:n_out + n_cast], rest[n_out + n_cast:]


def _diff_attn_kernel(*refs, n_heads, lam_init, n_cast):
    (qt_ref, k_ref, vt_ref, lam_ref, sub_ref), cast_src, (o_ref,), cast_dst, (m_ref, acc_ref) = (
        _split_cast_refs(refs, 5, 1, n_cast))
    _cast_blocks(cast_src, cast_dst)
    n_tiles, tq = qt_ref.shape[1], qt_ref.shape[3]
    lv = lam_ref[...]
    lam = (jnp.exp(jnp.sum(lv[0:1] * lv[1:2], axis=-1, keepdims=True))
           - jnp.exp(jnp.sum(lv[2:3] * lv[3:4], axis=-1, keepdims=True)) + lam_init)

    def finish_tile(i, slot):
        for h in range(n_heads):
            o_t = _attn_out(acc_ref.at[slot], 2 * h, LANES) - lam * _attn_out(acc_ref.at[slot], 2 * h + 1, LANES)
            inv = lax.rsqrt(jnp.mean(o_t * o_t, axis=0, keepdims=True) + RMS_EPS)
            y_t = o_t * inv * sub_ref[...] * (1.0 - lam_init)
            o_ref[0, i * tq:(i + 1) * tq, h * LANES:(h + 1) * LANES] = y_t.T.astype(o_ref.dtype)

    _attn_flat(lambda i: [qt_ref[0, i, h * LANES:(h + 1) * LANES, :] for h in range(n_heads)],
               lambda kb, h: k_ref[0, kb * tq:(kb + 1) * tq, h * LANES:(h + 1) * LANES],
               lambda kb, h: vt_ref[0, kb, h * LANES:(h + 1) * LANES, :],
               finish_tile, m_ref, acc_ref, n_tiles=n_tiles, n_maps=2, tq=tq, q_transposed=True)


def _diff_attn(p0, qvt, lam_vecs, subln_col, n_heads, cast_weights):
    b, s, _ = p0.shape
    n_kb = s // ATTN_TILE
    width = n_heads * LANES
    cast_specs, cast_shapes = _cast_specs(cast_weights, b)
    kern = functools.partial(_diff_attn_kernel, n_heads=n_heads, lam_init=0.8 - 0.6 * math.exp(-0.3 * 0),
                             n_cast=len(cast_weights))
    out = pl.pallas_call(
        kern,
        grid=(b,),
        in_specs=[pl.BlockSpec((1, n_kb, width, ATTN_TILE), lambda bb: (bb, 0, 0, 0)),
                  pl.BlockSpec((1, s, width), lambda bb: (bb, 0, 0)),
                  pl.BlockSpec((1, n_kb, width, ATTN_TILE), lambda bb: (bb, 0, 1, 0)),
                  pl.BlockSpec(lam_vecs.shape, lambda bb: (0, 0)),
                  pl.BlockSpec(subln_col.shape, lambda bb: (0, 0))] + cast_specs,
        out_specs=[pl.BlockSpec((1, s, width), lambda bb: (bb, 0, 0))] + cast_specs,
        out_shape=[jax.ShapeDtypeStruct((b, s, width), BF16)] + cast_shapes,
        scratch_shapes=_attn_flat_scratch(2 * n_heads, LANES),
        compiler_params=_params("arbitrary"),
        name="diff_attn",
    )(qvt, p0, qvt, lam_vecs, subln_col, *cast_weights)
    return out[0], out[1:]


def _mla_attn_kernel(*refs, n_heads, scale, n_cast):
    ((cq_ref, kv_ref, qn_ref, kvn_ref, wqt_ref, wk_ref, wvt_ref, cosq_tab, sinq_tab, ck_tab, sak_tab, sbk_tab),
     cast_src, (o_ref,), cast_dst, (m_ref, acc_ref, k_scr, vt_scr)) = (
        _split_cast_refs(refs, 12, 1, n_cast))
    _cast_blocks(cast_src, cast_dst)
    seq = kv_ref.shape[1]
    tq = ATTN_TILE
    half = MLA_ROPE // 2
    projected = set()

    def project_keys(chunk):
        if chunk in projected:
            return
        projected.add(chunk)
        rows = slice(chunk * ROW_TILE, (chunk + 1) * ROW_TILE)
        ckv = _rms(kv_ref[0, rows, :LANES], kvn_ref[...]).astype(BF16)
        kr = _rope_block(kv_ref[0, rows, LANES:], ck_tab[rows, :], sak_tab[rows, :], sbk_tab[rows, :], half)
        k_scr[rows, :] = _dot(jnp.concatenate([ckv, kr.astype(BF16)], axis=-1), wk_ref[...]).astype(BF16)
        vt = _dot_nt(wvt_ref[...], ckv)
        for j in range(ROW_TILE // tq):
            vt_scr[chunk * (ROW_TILE // tq) + j] = vt[:, j * tq:(j + 1) * tq].astype(BF16)

    def q_heads_at(i):
        cols = slice(i * tq, (i + 1) * tq)
        q_t = _dot_nt(wqt_ref[...], _rms(cq_ref[0, cols, :], qn_ref[...]).astype(BF16))
        cos_t, sin_t = cosq_tab[:, cols], sinq_tab[:, cols]
        q_heads = []
        for h in range(n_heads):
            lo = h * MLA_QK_PAD
            x1 = q_t[lo + MLA_NOPE:lo + MLA_NOPE + half, :]
            x2 = q_t[lo + MLA_NOPE + half:lo + MLA_NOPE + 2 * half, :]
            q_h = jnp.concatenate([q_t[lo:lo + MLA_NOPE, :], x1 * cos_t - x2 * sin_t, x2 * cos_t + x1 * sin_t,
                                   q_t[lo + MLA_NOPE + 2 * half:lo + MLA_QK_PAD, :]], axis=0)
            q_heads.append((q_h * scale).astype(BF16))
        return q_heads

    def k_at(kb, h):
        project_keys(kb * tq // ROW_TILE)
        return k_scr[kb * tq:(kb + 1) * tq, h * MLA_QK_PAD:(h + 1) * MLA_QK_PAD]

    def finish_tile(i, slot):
        for h in range(n_heads):
            o_ref[0, i * tq:(i + 1) * tq, h * MLA_V:(h + 1) * MLA_V] = (
                _attn_out(acc_ref.at[slot], h, MLA_V).T.astype(o_ref.dtype))

    _attn_flat(q_heads_at, k_at, lambda kb, h: vt_scr[kb, h * MLA_V:(h + 1) * MLA_V, :],
               finish_tile, m_ref, acc_ref, n_tiles=seq // tq, n_maps=1, tq=tq, q_transposed=True)


def _mla_attn(p1, q_norm, kv_norm, wq_t, wk, wv_t, q_tabs_t, tabs, n_heads, cast_weights):
    b, s, _ = p1.shape
    n_kb = s // ATTN_TILE
    cast_specs, cast_shapes = _cast_specs(cast_weights, b)
    kern = functools.partial(_mla_attn_kernel, n_heads=n_heads, scale=(MLA_NOPE + MLA_ROPE) ** -0.5 * LOG2_E,
                             n_cast=len(cast_weights))
    const = lambda a: pl.BlockSpec(a.shape, lambda bb: (0,) * a.ndim)
    out = pl.pallas_call(
        kern,
        grid=(b,),
        in_specs=[pl.BlockSpec((1, s, 2 * LANES), lambda bb: (bb, 0, 1)),
                  pl.BlockSpec((1, s, 2 * LANES), lambda bb: (bb, 0, 2)),
                  const(q_norm), const(kv_norm), const(wq_t), const(wk), const(wv_t),
                  const(q_tabs_t[0]), const(q_tabs_t[1]),
                  const(tabs[0]), const(tabs[1]), const(tabs[2])] + cast_specs,
        out_specs=[pl.BlockSpec((1, s, n_heads * MLA_V), lambda bb: (bb, 0, 0))] + cast_specs,
        out_shape=[jax.ShapeDtypeStruct((b, s, n_heads * MLA_V), BF16)] + cast_shapes,
        scratch_shapes=_attn_flat_scratch(n_heads, MLA_V) + [
            pltpu.VMEM((s, n_heads * MLA_QK_PAD), BF16),
            pltpu.VMEM((n_kb, n_heads * MLA_V, ATTN_TILE), BF16)],
        compiler_params=_params("arbitrary"),
        name="mla_attn",
    )(p1, p1, q_norm, kv_norm, wq_t, wk, wv_t, *q_tabs_t, *tabs, *cast_weights)
    return out[0], out[1:]


def _conv_kernel(a_ref, gate_ref, ah_ref, gh_ref, w_ref, b_ref, lg_ref, lb_ref, o_ref, u_ref, ur_ref):
    tt = a_ref.shape[1]
    u_ref[CONV_HALO:, :] = a_ref[0].astype(F32) * _sigmoid(gate_ref[0].astype(F32))
    halo = ah_ref[0].astype(F32) * _sigmoid(gh_ref[0].astype(F32))
    u_ref[:CONV_HALO, :] = jnp.where(pl.program_id(1) > 0, halo, jnp.zeros_like(halo))
    rows = ur_ref.shape[1]
    for r in range(1, SUBLANES):
        ur_ref[r - 1] = u_ref[r:r + rows, :]
    acc = jnp.zeros((tt, a_ref.shape[2]), F32)
    first = CONV_HALO - (CONV_WIDTH - 1)
    for k in range(CONV_WIDTH):
        base, r = divmod(first + k, SUBLANES)
        src = u_ref if r == 0 else ur_ref.at[r - 1]
        acc = acc + src[base * SUBLANES:base * SUBLANES + tt, :] * w_ref[k:k + 1, :]
    y = acc + b_ref[...]
    mu = jnp.mean(y, axis=-1, keepdims=True)
    yc = y - mu
    yn = yc * lax.rsqrt(jnp.mean(yc * yc, axis=-1, keepdims=True) + LN_EPS) * lg_ref[...] + lb_ref[...]
    o_ref[0] = (yn * _sigmoid(yn)).astype(o_ref.dtype)


def _conv_module(p0, dw_w, dw_b, ln_g, ln_b, col0):
    b, s, _ = p0.shape
    c = dw_w.shape[1]
    a_blk, g_blk = col0 // c, col0 // c + 1
    ratio = CONV_TILE // CONV_HALO
    main = lambda blk: pl.BlockSpec((1, CONV_TILE, c), lambda bb, t: (bb, t, blk))
    halo = lambda blk: pl.BlockSpec((1, CONV_HALO, c), lambda bb, t: (bb, jnp.maximum(t * ratio - 1, 0), blk))
    vec = lambda a: pl.BlockSpec(a.shape, lambda bb, t: (0, 0))
    return pl.pallas_call(
        _conv_kernel,
        grid=(b, s // CONV_TILE),
        in_specs=[main(a_blk), main(g_blk), halo(a_blk), halo(g_blk),
                  vec(dw_w), vec(dw_b), vec(ln_g), vec(ln_b)],
        out_specs=pl.BlockSpec((1, CONV_TILE, c), lambda bb, t: (bb, t, 0)),
        out_shape=jax.ShapeDtypeStruct((b, s, c), BF16),
        scratch_shapes=[pltpu.VMEM((CONV_HALO + CONV_TILE, c), F32),
                        pltpu.VMEM((SUBLANES - 1, CONV_HALO + CONV_TILE - SUBLANES, c), F32)],
        compiler_params=_params("parallel", "parallel"),
        name="conv_module",
    )(p0, p0, p0, p0, dw_w, dw_b, ln_g, ln_b)


def _post_kernel(*refs, n_mix, with_next):
    x_ref = refs[0]
    mix_refs = refs[1:1 + n_mix]
    (wo_ref, gpost_ref, gfpre_ref, wg_ref, wu_ref, wd_ref, gfpost_ref) = refs[1 + n_mix:8 + n_mix]
    pos = 8 + n_mix
    if with_next:
        gnext_ref, wnext_ref = refs[pos:pos + 2]
        pos += 2
    h_out_ref = refs[pos]
    pos += 1
    if with_next:
        p_out_ref = refs[pos]
        pos += 1
    hid_ref = refs[pos]

    n_rows = x_ref.shape[0]
    groups = [slice(r0, r0 + n_rows // POST_ROW_GROUPS) for r0 in range(0, n_rows, n_rows // POST_ROW_GROUPS)]

    def out_proj(rows):
        y = None
        row = 0
        for r in mix_refs:
            w = r.shape[-1]
            part = _dot(r[rows, :], wo_ref[row:row + w, :])
            y = part if y is None else y + part
            row += w
        return y

    ys = [out_proj(rows) for rows in groups]
    h1s = [x_ref[rows, :] + _rms(y, gpost_ref[...]) for rows, y in zip(groups, ys)]
    ts = [_rms(h1, gfpre_ref[...]).astype(BF16) for h1 in h1s]
    d_ff = wg_ref.shape[1]
    for j in range(0, d_ff, FFN_CHUNK):
        wdt = min(FFN_CHUNK, d_ff - j)
        for rows, t in zip(groups, ts):
            gate = _dot(t, wg_ref[:, j:j + wdt])
            up = _dot(t, wu_ref[:, j:j + wdt])
            hid_ref[rows, j:j + wdt] = (gate * _sigmoid(gate) * up).astype(BF16)
    fs = [_dot(hid_ref[rows, :], wd_ref[...]) for rows in groups]
    h2s = [h1 + _rms(f, gfpost_ref[...]) for h1, f in zip(h1s, fs)]
    for rows, h2 in zip(groups, h2s):
        h_out_ref[rows, :] = h2
    if with_next:
        t2s = [_rms(h2, gnext_ref[...]).astype(BF16) for h2 in h2s]
        for rows, t2 in zip(groups, t2s):
            p_out_ref[rows, :] = _dot(t2, wnext_ref[...])


def _post(x2, mix_parts, w_out, g_post, g_fpre, w_gate, w_up, w_down, g_fpost, nxt=None):
    n, d = x2.shape
    d_ff = w_gate.shape[1]
    row = lambda a: pl.BlockSpec((ROW_TILE, a.shape[1]), lambda i: (i, 0))
    consts = [w_out, g_post, g_fpre, w_gate, w_up, w_down, g_fpost] + (list(nxt) if nxt else [])
    out_shape = [jax.ShapeDtypeStruct((n, d), F32)]
    out_specs = [pl.BlockSpec((ROW_TILE, d), lambda i: (i, 0))]
    if nxt:
        n_next = nxt[1].shape[1]
        out_shape.append(jax.ShapeDtypeStruct((n, n_next), F32))
        out_specs.append(pl.BlockSpec((ROW_TILE, n_next), lambda i: (i, 0)))
    kern = functools.partial(_post_kernel, n_mix=len(mix_parts), with_next=bool(nxt))
    return pl.pallas_call(
        kern,
        grid=(n // ROW_TILE,),
        in_specs=[row(x2)] + [row(m) for m in mix_parts] + [_const_spec(c.shape) for c in consts],
        out_specs=out_specs,
        out_shape=out_shape,
        scratch_shapes=[pltpu.VMEM((ROW_TILE, d_ff), BF16)],
        compiler_params=_params("parallel"),
        name="post_next" if nxt else "post",
    )(x2, *mix_parts, *consts)


def _gelu_tanh(x):
    return 0.5 * x * (1.0 + jnp.tanh(math.sqrt(2.0 / math.pi) * (x + 0.044715 * (x * x * x))))


def _ssm_kernel(u_ref, lr_ref, li_ref, ldt_ref, bre_ref, bim_ref, cre_ref, cim_ref, d_ref, wg_ref, bg_ref,
                o_ref, bmat_ref, cmat_ref, a_ref, st_ref, us_ref, utm_ref, x_ref, y_ref):
    nb, tt, ch = u_ref.shape
    n_state = lr_ref.shape[1]

    @pl.when(pl.program_id(0) == 0)
    def _init():
        lr, li = lr_ref[...], li_ref[...]
        dt = jnp.exp(ldt_ref[...])
        mag = jnp.exp(lr * dt)
        ab_re = mag * jnp.cos(li * dt)
        ab_im = mag * jnp.sin(li * dt)
        den = lr * lr + li * li
        n_re = ab_re - 1.0
        f_re = (n_re * lr + ab_im * li) / den
        f_im = (ab_im * lr - n_re * li) / den
        br, bi = bre_ref[...], bim_ref[...]
        bmat_ref[:, :n_state] = (f_re * br - f_im * bi).astype(BF16)
        bmat_ref[:, n_state:] = (f_re * bi + f_im * br).astype(BF16)
        cmat_ref[:n_state, :] = cre_ref[...].astype(BF16)
        cmat_ref[n_state:, :] = (-cim_ref[...]).astype(BF16)
        a_ref[0:1, :] = ab_re
        a_ref[1:2, :] = ab_im
        st_ref[...] = jnp.zeros(st_ref.shape, F32)

    n_ublk = ch // LANES
    for b in range(nb):
        for j in range(n_ublk):
            us_ref[j, b * SSM_PITCH:b * SSM_PITCH + tt, :] = u_ref[b, :, j * LANES:(j + 1) * LANES]

    def gather_step(t, carry):
        dst = pl.multiple_of(t * nb, nb)
        for j in range(n_ublk):
            utm_ref[pl.ds(dst, nb), j * LANES:(j + 1) * LANES] = us_ref[j, pl.ds(t, nb, stride=SSM_PITCH), :]
        return carry

    lax.fori_loop(0, tt, gather_step, 0, unroll=8)

    half_rows = tt * nb // 2
    for r0 in (0, half_rows):
        x_ref[r0:r0 + half_rows, :] = _dot(utm_ref[r0:r0 + half_rows, :].astype(BF16), bmat_ref[...])

    a_re = jnp.broadcast_to(a_ref[0:1, :], (nb, n_state))
    a_im = jnp.broadcast_to(a_ref[1:2, :], (nb, n_state))

    def scan_step(t, carry):
        x_re, x_im = carry
        row = pl.multiple_of(t * nb, nb)
        n_re = a_re * x_re - a_im * x_im + x_ref[pl.ds(row, nb), :n_state]
        n_im = a_re * x_im + a_im * x_re + x_ref[pl.ds(row, nb), n_state:]
        x_ref[pl.ds(row, nb), :n_state] = n_re
        x_ref[pl.ds(row, nb), n_state:] = n_im
        return n_re, n_im

    x_re, x_im = lax.fori_loop(0, tt, scan_step, (st_ref[:, :n_state], st_ref[:, n_state:]), unroll=4)
    st_ref[:, :n_state] = x_re
    st_ref[:, n_state:] = x_im

    n_yblk = ch // LANES
    for r0 in (0, half_rows):
        y_tm = _dot(x_ref[r0:r0 + half_rows, :].astype(BF16), cmat_ref[...])
        for j in range(n_yblk):
            y_ref[j, r0:r0 + half_rows, :] = y_tm[:, j * LANES:(j + 1) * LANES]
    for b in range(nb):
        y = jnp.concatenate([y_ref[j, pl.ds(b, tt, stride=nb), :] for j in range(n_yblk)], axis=-1)
        y = y + d_ref[...] * u_ref[b]
        z = _gelu_tanh(y)
        gate = _dot(z.astype(BF16), wg_ref[...]) + bg_ref[...]
        o_ref[b] = (z * _sigmoid(gate)).astype(o_ref.dtype)


def _ssm(p1, rows, b_bd, c_bd, d_row, w_glu, b_glu):
    b, s, _ = p1.shape
    ch = w_glu.shape[0]
    n_state = rows[0].shape[1]
    consts = list(rows) + list(b_bd) + list(c_bd) + [d_row, w_glu, b_glu]
    return pl.pallas_call(
        _ssm_kernel,
        grid=(s // SSM_TILE,),
        in_specs=[pl.BlockSpec((b, SSM_TILE, ch), lambda t: (0, t, 0))] + [_const_spec(c.shape) for c in consts],
        out_specs=pl.BlockSpec((b, SSM_TILE, ch), lambda t: (0, t, 0)),
        out_shape=jax.ShapeDtypeStruct((b, s, ch), BF16),
        scratch_shapes=[pltpu.VMEM((ch, 2 * n_state), BF16),
                        pltpu.VMEM((2 * n_state, ch), BF16),
                        pltpu.VMEM((SUBLANES, n_state), F32),
                        pltpu.VMEM((b, 2 * n_state), F32),
                        pltpu.VMEM((ch // LANES, b * SSM_PITCH, LANES), F32),
                        pltpu.VMEM((SSM_TILE * b, ch), F32),
                        pltpu.VMEM((SSM_TILE * b, 2 * n_state), F32),
                        pltpu.VMEM((ch // LANES, SSM_TILE * b, LANES), F32)],
        compiler_params=_params("arbitrary"),
        name="s5_ssm",
    )(p1, *consts)


def _rope_tables(s, rot_dim, theta):
    inv = theta ** (-jnp.arange(0, rot_dim, 2, dtype=F32) / rot_dim)
    ang = jnp.arange(s, dtype=F32)[:, None] * inv[None, :]
    return jnp.cos(ang), jnp.sin(ang)


def _lane_tables(cos, sin, period):
    s, half = cos.shape
    reps = LANES // period
    one = jnp.ones((s, period - 2 * half), F32)
    zero = jnp.zeros((s, period - 2 * half), F32)
    zh = jnp.zeros((s, half), F32)
    c = jnp.tile(jnp.concatenate([cos, cos, one], axis=1), (1, reps))
    sa = jnp.tile(jnp.concatenate([-sin, zh, zero], axis=1), (1, reps))
    sb = jnp.tile(jnp.concatenate([zh, sin, zero], axis=1), (1, reps))
    return c, sa, sb


def _block_diag(blocks):
    g, r, c = blocks.shape
    eye = jnp.eye(g, dtype=blocks.dtype)
    return (eye[:, None, :, None] * blocks[:, :, None, :]).reshape(g * r, g * c)


def kernel(x, l0_mix_pre, l0_mix_post, l0_w_in, l0_lambda_q1, l0_lambda_k1, l0_lambda_q2, l0_lambda_k2, l0_subln, l0_dw_w, l0_dw_b, l0_conv_ln_g, l0_conv_ln_b, l0_w_out, l0_ffn_pre, l0_ffn_post, l0_w_gate, l0_w_up, l0_w_down, l1_mix_pre, l1_mix_post, l1_w_in, l1_a_re, l1_a_im, l1_log_dt, l1_b_re, l1_b_im, l1_c_re, l1_c_im, l1_d_skip, l1_w_glu, l1_b_glu, l1_q_norm, l1_w_uq, l1_kv_norm, l1_w_ukv, l1_w_out, l1_ffn_pre, l1_ffn_post, l1_w_gate, l1_w_up, l1_w_down):
    b, s, d = x.shape
    n = b * s
    row = lambda v: v.reshape(1, -1).astype(F32)
    bf = lambda w: w.astype(BF16)

    diff_width = 4 * LANES
    n_diff_heads = diff_width // LANES
    conv_ch = l0_dw_w.shape[1]
    ssm_ch = l1_w_glu.shape[0]
    n_groups, n_state_g = l1_a_re.shape
    q_rank = l1_q_norm.shape[0]
    kv_rank = l1_kv_norm.shape[0]
    n_mla_heads = l1_w_uq.shape[1] // (MLA_NOPE + MLA_ROPE)

    cos_a, sin_a = _rope_tables(s, DIFF_ROT, ROPE_THETA)
    tabs_a = _lane_tables(cos_a, sin_a, period=DIFF_HEAD_DIM)
    x2 = x.reshape(n, d)
    w_main = bf(jnp.concatenate([l0_w_in[:, diff_width:2 * diff_width], l0_w_in[:, 3 * diff_width:]], axis=1))
    w_t = bf(jnp.concatenate([l0_w_in[:, :diff_width], l0_w_in[:, 2 * diff_width:3 * diff_width]], axis=1).T)
    p0, qvt_a = _l0_in(x2, row(l0_mix_pre), w_main, w_t, tabs_a, (cos_a.T, sin_a.T), b, s, diff_width)
    p0 = p0.reshape(b, s, -1)
    lam_vecs = jnp.stack([l0_lambda_q1, l0_lambda_k1, l0_lambda_q2, l0_lambda_k2]).astype(F32)
    y_a, (w_out0, w_gate0, w_up0, w_down0) = _diff_attn(
        p0, qvt_a, lam_vecs, l0_subln.reshape(-1, 1).astype(F32), n_diff_heads,
        [l0_w_out, l0_w_gate, l0_w_up, l0_w_down])
    y_b = _conv_module(p0, l0_dw_w.astype(F32), row(l0_dw_b), row(l0_conv_ln_g), row(l0_conv_ln_b),
                       col0=diff_width)

    pad = (-l1_w_in.shape[1]) % LANES
    w_in1 = bf(jnp.pad(l1_w_in, ((0, 0), (0, pad))))
    h2, p1 = _post(x2, [y_a.reshape(n, -1), y_b.reshape(n, -1)], w_out0, row(l0_mix_post),
                   row(l0_ffn_pre), w_gate0, w_up0, w_down0, row(l0_ffn_post),
                   nxt=(row(l1_mix_pre), w_in1))

    state_row = lambda a: a.reshape(1, -1).astype(F32)
    ssm_rows = (state_row(l1_a_re), state_row(l1_a_im),
                state_row(jnp.broadcast_to(l1_log_dt[:, None], (n_groups, n_state_g))))
    b_bd = tuple(_block_diag(jnp.swapaxes(m, 1, 2).astype(F32)) for m in (l1_b_re, l1_b_im))
    c_bd = tuple(_block_diag(jnp.swapaxes(m, 1, 2).astype(F32)) for m in (l1_c_re, l1_c_im))
    y_c = _ssm(p1.reshape(b, s, -1), ssm_rows, b_bd, c_bd, row(l1_d_skip), bf(l1_w_glu), row(l1_b_glu))

    wq = l1_w_uq.reshape(q_rank, n_mla_heads, MLA_NOPE + MLA_ROPE)
    wq = jnp.pad(wq, ((0, 0), (0, 0), (0, MLA_QK_PAD - MLA_NOPE - MLA_ROPE))).reshape(q_rank, -1)
    wkv = l1_w_ukv.reshape(kv_rank, n_mla_heads, MLA_NOPE + MLA_V)
    wk_nope = jnp.pad(wkv[:, :, :MLA_NOPE], ((0, 0), (0, 0), (0, MLA_QK_PAD - MLA_NOPE)))
    route = jnp.pad(jnp.eye(MLA_ROPE, dtype=F32), ((0, LANES - MLA_ROPE), (MLA_NOPE, MLA_QK_PAD - MLA_NOPE - MLA_ROPE)))
    wk_rope = jnp.broadcast_to(route[:, None, :], (LANES, n_mla_heads, MLA_QK_PAD))
    wk = jnp.concatenate([wk_nope, wk_rope], axis=0).reshape(kv_rank + LANES, -1)
    wv_t = wkv[:, :, MLA_NOPE:].reshape(kv_rank, -1).T
    cos_d, sin_d = _rope_tables(s, MLA_ROPE, MLA_ROPE_THETA)
    tabs_d = _lane_tables(cos_d, sin_d, period=LANES)
    y_d, (w_out1, w_gate1, w_up1, w_down1) = _mla_attn(
        p1.reshape(b, s, -1), row(l1_q_norm), row(l1_kv_norm), bf(wq.T), bf(wk), bf(wv_t),
        (cos_d.T, sin_d.T), tabs_d, n_mla_heads, [l1_w_out, l1_w_gate, l1_w_up, l1_w_down])

    (out,) = _post(h2, [y_c.reshape(n, -1), y_d.reshape(n, -1)], w_out1, row(l1_mix_post),
                   row(l1_ffn_pre), w_gate1, w_up1, w_down1, row(l1_ffn_post))
    return out.reshape(b, s, d)
```

```python
import functools
import math

import jax
import jax.numpy as jnp
from jax import lax
from jax.experimental import pallas as pl
from jax.experimental.pallas import tpu as pltpu

F32 = jnp.float32
BF16 = jnp.bfloat16

LANES = 128
SUBLANES = 8
BF16_SUBLANES = 16
VMEM_LIMIT_BYTES = 56 * 1024 * 1024

CHUNK = 64
RMS_EPS = 1e-6
LN_EPS = 1e-5
ROPE_THETA = 500000.0
MLA_ROPE_THETA = 10000.0
DIFF_HEAD_DIM = 64
DIFF_ROT = 16
CONV_WIDTH = 31
CONV_HALO = 32
SSM_GROUP = 16
SSM_STATE = 64
MLA_NOPE = 128
MLA_ROPE = 64
MLA_V = 128
MLA_QK_PAD = 256

ROW_TILE = 512
ATTN_TILE = 256
ATTN_LOOKAHEAD = 5
ATTN_SUM_ROWS = 16
LOG2_E = math.log2(math.e)
CONV_TILE = 256
SSM_TILE = 128
SSM_PITCH = SSM_TILE + SUBLANES
FFN_CHUNK = 512
POST_ROW_GROUPS = 2


def _params(*sem):
    return pltpu.CompilerParams(dimension_semantics=sem, vmem_limit_bytes=VMEM_LIMIT_BYTES)


def _rms(x, g):
    return x * lax.rsqrt(jnp.mean(x * x, axis=-1, keepdims=True) + RMS_EPS) * g


def _sigmoid(x):
    return 1.0 / (1.0 + jnp.exp(-x))


def _dot(a, b):
    return jnp.dot(a, b, preferred_element_type=F32)


def _dot_nt(a, b):
    return lax.dot_general(a, b, (((1,), (1,)), ((), ())), preferred_element_type=F32)


def _rope_block(x, c, sa, sb, shift):
    return x * c + pltpu.roll(x, LANES - shift, 1) * sa + pltpu.roll(x, shift, 1) * sb


def _const_spec(shape):
    nd = len(shape)
    return pl.BlockSpec(shape, lambda *_: (0,) * nd, pipeline_mode=pl.Buffered(1))


def _vt_out(b, seq, width):
    tiles_per_seq = seq // ROW_TILE
    per_tile = ROW_TILE // ATTN_TILE
    spec = pl.BlockSpec((1, per_tile, width, ATTN_TILE),
                        lambda i: (i // tiles_per_seq, i % tiles_per_seq, 0, 0))
    return spec, jax.ShapeDtypeStruct((b, seq // ATTN_TILE, width, ATTN_TILE), BF16)


def _l0_in_kernel(x_ref, g_ref, w_ref, wt_ref, c_ref, sa_ref, sb_ref, cosq_ref, sinq_ref, o_ref, qvt_ref, *,
                  n_k_blocks, q_rows, scale):
    half = DIFF_ROT // 2
    groups = [slice(r0, r0 + ATTN_TILE) for r0 in range(0, x_ref.shape[0], ATTN_TILE)]
    ts = [_rms(x_ref[rows, :], g_ref[...]).astype(BF16) for rows in groups]
    ps = [_dot(t, w_ref[...]) for t in ts]
    qv_ts = [_dot_nt(wt_ref[...], t) for t in ts]
    for j, (rows, p, qv_t) in enumerate(zip(groups, ps, qv_ts)):
        c, sa, sb = c_ref[rows, :], sa_ref[rows, :], sb_ref[rows, :]
        for blk_i in range(n_k_blocks):
            blk = _rope_block(p[:, blk_i * LANES:(blk_i + 1) * LANES], c, sa, sb, half)
            o_ref[rows, blk_i * LANES:(blk_i + 1) * LANES] = blk.astype(o_ref.dtype)
        rest = n_k_blocks * LANES
        o_ref[rows, rest:] = p[:, rest:].astype(o_ref.dtype)
        cos_t, sin_t = cosq_ref[:, rows], sinq_ref[:, rows]
        pieces = []
        for lo in range(0, q_rows, DIFF_HEAD_DIM):
            x1, x2 = qv_t[lo:lo + half, :], qv_t[lo + half:lo + 2 * half, :]
            pieces += [(x1 * cos_t - x2 * sin_t) * scale, (x2 * cos_t + x1 * sin_t) * scale,
                       qv_t[lo + 2 * half:lo + DIFF_HEAD_DIM, :] * scale]
        qvt_ref[0, j] = jnp.concatenate(pieces + [qv_t[q_rows:, :]], axis=0).astype(qvt_ref.dtype)


def _l0_in(x2, g_pre, w_main, w_t, tabs, q_tabs_t, b, seq, q_rows):
    n, d = x2.shape
    n_out = w_main.shape[1]
    tiles_per_seq = seq // ROW_TILE
    kern = functools.partial(_l0_in_kernel, n_k_blocks=q_rows // LANES, q_rows=q_rows,
                             scale=DIFF_HEAD_DIM ** -0.5 * LOG2_E)
    tab_spec = pl.BlockSpec((ROW_TILE, LANES), lambda i: (i % tiles_per_seq, 0))
    qtab_spec = pl.BlockSpec((q_tabs_t[0].shape[0], ROW_TILE), lambda i: (0, i % tiles_per_seq))
    vt_spec, vt_shape = _vt_out(b, seq, w_t.shape[0])
    return pl.pallas_call(
        kern,
        grid=(n // ROW_TILE,),
        in_specs=[pl.BlockSpec((ROW_TILE, d), lambda i: (i, 0)),
                  _const_spec((1, d)), _const_spec(w_main.shape), _const_spec(w_t.shape),
                  tab_spec, tab_spec, tab_spec, qtab_spec, qtab_spec],
        out_specs=[pl.BlockSpec((ROW_TILE, n_out), lambda i: (i, 0)), vt_spec],
        out_shape=[jax.ShapeDtypeStruct((n, n_out), BF16), vt_shape],
        compiler_params=_params("parallel"),
        name="l0_in",
    )(x2, g_pre, w_main, w_t, *tabs, *q_tabs_t)


def _attn_flat(q_heads_at, k_at, vt_at, finish_tile, m_ref, acc_ref, *, n_tiles, n_maps, tq, q_transposed):
    dk_axis = 0 if q_transposed else 1
    qs_cache = {}

    def qs_of(i):
        if i not in qs_cache:
            qs = []
            for q in q_heads_at(i):
                if n_maps == 1:
                    qs.append(q)
                else:
                    width = q.shape[dk_axis] // n_maps
                    pos = lax.broadcasted_iota(jnp.int32, q.shape, dk_axis)
                    for c in range(n_maps):
                        qs.append(jnp.where((pos >= c * width) & (pos < (c + 1) * width), q, jnp.zeros_like(q)))
            qs_cache[i] = qs
        return qs_cache[i]

    n_chains = len(qs_of(0))
    ones = jnp.ones((ATTN_SUM_ROWS, tq), BF16)
    kc = lax.broadcasted_iota(jnp.int32, (tq, tq), 0) // CHUNK
    qc = lax.broadcasted_iota(jnp.int32, (tq, tq), 1) // CHUNK
    keep = kc <= qc
    items = [(i, kb, ch) for i in range(n_tiles) for kb in range(i + 1) for ch in range(n_chains)]

    def scores(i, kb, ch):
        k = k_at(kb, ch // n_maps)
        return _dot(k, qs_of(i)[ch]) if q_transposed else _dot_nt(k, qs_of(i)[ch])

    ss = {n: scores(*items[n]) for n in range(min(ATTN_LOOKAHEAD, len(items)))}
    for n, (i, kb, ch) in enumerate(items):
        if n + ATTN_LOOKAHEAD < len(items):
            ss[n + ATTN_LOOKAHEAD] = scores(*items[n + ATTN_LOOKAHEAD])
        s = ss.pop(n)
        if kb == i:
            s = jnp.where(keep, s, -jnp.inf)
        vt = jnp.concatenate([vt_at(kb, ch // n_maps), ones], axis=0)
        slot = i % 2
        if kb == 0:
            m_new = jnp.max(s, axis=0, keepdims=True)
            acc_new = _dot(vt, jnp.exp2(s - m_new).astype(vt.dtype))
        else:
            m_old = m_ref[slot, ch]
            m_new = jnp.maximum(m_old, jnp.max(s, axis=0, keepdims=True))
            acc_new = (jnp.exp2(m_old - m_new) * acc_ref[slot, ch]
                       + _dot(vt, jnp.exp2(s - m_new).astype(vt.dtype)))
        m_ref[slot, ch] = m_new
        acc_ref[slot, ch] = acc_new
        if kb == i and ch == n_chains - 1:
            finish_tile(i, slot)
            del qs_cache[i]


def _attn_flat_scratch(n_chains, dv):
    return [pltpu.VMEM((2, n_chains, 1, ATTN_TILE), F32),
            pltpu.VMEM((2, n_chains, dv + ATTN_SUM_ROWS, ATTN_TILE), F32)]


def _attn_out(acc_ref, ch, dv):
    return acc_ref[ch, :dv, :] / acc_ref[ch, dv:dv + 1, :]


def _cast_specs(weights, n_steps):
    specs = []
    for w in weights:
        rows = w.shape[0] // n_steps
        assert rows * n_steps == w.shape[0] and rows % BF16_SUBLANES == 0, w.shape
        specs.append(pl.BlockSpec((rows, w.shape[1]), lambda bb: (bb, 0)))
    return specs, [jax.ShapeDtypeStruct(w.shape, BF16) for w in weights]


def _cast_blocks(src_refs, dst_refs):
    for src, dst in zip(src_refs, dst_refs):
        dst[...] = src[...].astype(dst.dtype)


def _split_cast_refs(refs, n_in, n_out, n_cast):
    main_in, cast_src = refs[:n_in], refs[n_in:n_in + n_cast]
    rest = refs[n_in + n_cast:]
    return main_in, cast_src, rest[:n_out], rest[n_out:n_out + n_cast], rest[n_out + n_cast:]


def _diff_attn_kernel(*refs, n_heads, lam_init, n_cast):
    (qt_ref, k_ref, vt_ref, lam_ref, sub_ref), cast_src, (o_ref,), cast_dst, (m_ref, acc_ref) = (
        _split_cast_refs(refs, 5, 1, n_cast))
    _cast_blocks(cast_src, cast_dst)
    n_tiles, tq = qt_ref.shape[1], qt_ref.shape[3]
    lv = lam_ref[...]
    lam = (jnp.exp(jnp.sum(lv[0:1] * lv[1:2], axis=-1, keepdims=True))
           - jnp.exp(jnp.sum(lv[2:3] * lv[3:4], axis=-1, keepdims=True)) + lam_init)

    def finish_tile(i, slot):
        for h in range(n_heads):
            o_t = _attn_out(acc_ref.at[slot], 2 * h, LANES) - lam * _attn_out(acc_ref.at[slot], 2 * h + 1, LANES)
            inv = lax.rsqrt(jnp.mean(o_t * o_t, axis=0, keepdims=True) + RMS_EPS)
            y_t = o_t * inv * sub_ref[...] * (1.0 - lam_init)
            o_ref[0, i * tq:(i + 1) * tq, h * LANES:(h + 1) * LANES] = y_t.T.astype(o_ref.dtype)

    _attn_flat(lambda i: [qt_ref[0, i, h * LANES:(h + 1) * LANES, :] for h in range(n_heads)],
               lambda kb, h: k_ref[0, kb * tq:(kb + 1) * tq, h * LANES:(h + 1) * LANES],
               lambda kb, h: vt_ref[0, kb, h * LANES:(h + 1) * LANES, :],
               finish_tile, m_ref, acc_ref, n_tiles=n_tiles, n_maps=2, tq=tq, q_transposed=True)


def _diff_attn(p0, qvt, lam_vecs, subln_col, n_heads, cast_weights):
    b, s, _ = p0.shape
    n_kb = s // ATTN_TILE
    width = n_heads * LANES
    cast_specs, cast_shapes = _cast_specs(cast_weights, b)
    kern = functools.partial(_diff_attn_kernel, n_heads=n_heads, lam_init=0.8 - 0.6 * math.exp(-0.3 * 0),
                             n_cast=len(cast_weights))
    out = pl.pallas_call(
        kern,
        grid=(b,),
        in_specs=[pl.BlockSpec((1, n_kb, width, ATTN_TILE), lambda bb: (bb, 0, 0, 0)),
                  pl.BlockSpec((1, s, width), lambda bb: (bb, 0, 0)),
                  pl.BlockSpec((1, n_kb, width, ATTN_TILE), lambda bb: (bb, 0, 1, 0)),
                  pl.BlockSpec(lam_vecs.shape, lambda bb: (0, 0)),
                  pl.BlockSpec(subln_col.shape, lambda bb: (0, 0))] + cast_specs,
        out_specs=[pl.BlockSpec((1, s, width), lambda bb: (bb, 0, 0))] + cast_specs,
        out_shape=[jax.ShapeDtypeStruct((b, s, width), BF16)] + cast_shapes,
        scratch_shapes=_attn_flat_scratch(2 * n_heads, LANES),
        compiler_params=_params("arbitrary"),
        name="diff_attn",
    )(qvt, p0, qvt, lam_vecs, subln_col, *cast_weights)
    return out[0], out[1:]


def _mla_attn_kernel(*refs, n_heads, scale, n_cast):
    ((cq_ref, kv_ref, qn_ref, kvn_ref, wqt_ref, wk_ref, wvt_ref, cosq_tab, sinq_tab, ck_tab, sak_tab, sbk_tab),
     cast_src, (o_ref,), cast_dst, (m_ref, acc_ref, k_scr, vt_scr)) = (
        _split_cast_refs(refs, 12, 1, n_cast))
    _cast_blocks(cast_src, cast_dst)
    seq = kv_ref.shape[1]
    tq = ATTN_TILE
    half = MLA_ROPE // 2
    projected = set()

    def project_keys(chunk):
        if chunk in projected:
            return
        projected.add(chunk)
        rows = slice(chunk * ROW_TILE, (chunk + 1) * ROW_TILE)
        ckv = _rms(kv_ref[0, rows, :LANES], kvn_ref[...]).astype(BF16)
        kr = _rope_block(kv_ref[0, rows, LANES:], ck_tab[rows, :], sak_tab[rows, :], sbk_tab[rows, :], half)
        k_scr[rows, :] = _dot(jnp.concatenate([ckv, kr.astype(BF16)], axis=-1), wk_ref[...]).astype(BF16)
        vt = _dot_nt(wvt_ref[...], ckv)
        for j in range(ROW_TILE // tq):
            vt_scr[chunk * (ROW_TILE // tq) + j] = vt[:, j * tq:(j + 1) * tq].astype(BF16)

    def q_heads_at(i):
        cols = slice(i * tq, (i + 1) * tq)
        q_t = _dot_nt(wqt_ref[...], _rms(cq_ref[0, cols, :], qn_ref[...]).astype(BF16))
        cos_t, sin_t = cosq_tab[:, cols], sinq_tab[:, cols]
        q_heads = []
        for h in range(n_heads):
            lo = h * MLA_QK_PAD
            x1 = q_t[lo + MLA_NOPE:lo + MLA_NOPE + half, :]
            x2 = q_t[lo + MLA_NOPE + half:lo + MLA_NOPE + 2 * half, :]
            q_h = jnp.concatenate([q_t[lo:lo + MLA_NOPE, :], x1 * cos_t - x2 * sin_t, x2 * cos_t + x1 * sin_t,
                                   q_t[lo + MLA_NOPE + 2 * half:lo + MLA_QK_PAD, :]], axis=0)
            q_heads.append((q_h * scale).astype(BF16))
        return q_heads

    def k_at(kb, h):
        project_keys(kb * tq // ROW_TILE)
        return k_scr[kb * tq:(kb + 1) * tq, h * MLA_QK_PAD:(h + 1) * MLA_QK_PAD]

    def finish_tile(i, slot):
        for h in range(n_heads):
            o_ref[0, i * tq:(i + 1) * tq, h * MLA_V:(h + 1) * MLA_V] = (
                _attn_out(acc_ref.at[slot], h, MLA_V).T.astype(o_ref.dtype))

    _attn_flat(q_heads_at, k_at, lambda kb, h: vt_scr[kb, h * MLA_V:(h + 1) * MLA_V, :],
               finish_tile, m_ref, acc_ref, n_tiles=seq // tq, n_maps=1, tq=tq, q_transposed=True)


def _mla_attn(p1, q_norm, kv_norm, wq_t, wk, wv_t, q_tabs_t, tabs, n_heads, cast_weights):
    b, s, _ = p1.shape
    n_kb = s // ATTN_TILE
    cast_specs, cast_shapes = _cast_specs(cast_weights, b)
    kern = functools.partial(_mla_attn_kernel, n_heads=n_heads, scale=(MLA_NOPE + MLA_ROPE) ** -0.5 * LOG2_E,
                             n_cast=len(cast_weights))
    const = lambda a: pl.BlockSpec(a.shape, lambda bb: (0,) * a.ndim)
    out = pl.pallas_call(
        kern,
        grid=(b,),
        in_specs=[pl.BlockSpec((1, s, 2 * LANES), lambda bb: (bb, 0, 1)),
                  pl.BlockSpec((1, s, 2 * LANES), lambda bb: (bb, 0, 2)),
                  const(q_norm), const(kv_norm), const(wq_t), const(wk), const(wv_t),
                  const(q_tabs_t[0]), const(q_tabs_t[1]),
                  const(tabs[0]), const(tabs[1]), const(tabs[2])] + cast_specs,
        out_specs=[pl.BlockSpec((1, s, n_heads * MLA_V), lambda bb: (bb, 0, 0))] + cast_specs,
        out_shape=[jax.ShapeDtypeStruct((b, s, n_heads * MLA_V), BF16)] + cast_shapes,
        scratch_shapes=_attn_flat_scratch(n_heads, MLA_V) + [
            pltpu.VMEM((s, n_heads * MLA_QK_PAD), BF16),
            pltpu.VMEM((n_kb, n_heads * MLA_V, ATTN_TILE), BF16)],
        compiler_params=_params("arbitrary"),
        name="mla_attn",
    )(p1, p1, q_norm, kv_norm, wq_t, wk, wv_t, *q_tabs_t, *tabs, *cast_weights)
    return out[0], out[1:]


def _conv_kernel(a_ref, gate_ref, ah_ref, gh_ref, w_ref, b_ref, lg_ref, lb_ref, o_ref, u_ref, ur_ref):
    tt = a_ref.shape[1]
    u_ref[CONV_HALO:, :] = a_ref[0].astype(F32) * _sigmoid(gate_ref[0].astype(F32))
    halo = ah_ref[0].astype(F32) * _sigmoid(gh_ref[0].astype(F32))
    u_ref[:CONV_HALO, :] = jnp.where(pl.program_id(1) > 0, halo, jnp.zeros_like(halo))
    rows = ur_ref.shape[1]
    for r in range(1, SUBLANES):
        ur_ref[r - 1] = u_ref[r:r + rows, :]
    acc = jnp.zeros((tt, a_ref.shape[2]), F32)
    first = CONV_HALO - (CONV_WIDTH - 1)
    for k in range(CONV_WIDTH):
        base, r = divmod(first + k, SUBLANES)
        src = u_ref if r == 0 else ur_ref.at[r - 1]
        acc = acc + src[base * SUBLANES:base * SUBLANES + tt, :] * w_ref[k:k + 1, :]
    y = acc + b_ref[...]
    mu = jnp.mean(y, axis=-1, keepdims=True)
    yc = y - mu
    yn = yc * lax.rsqrt(jnp.mean(yc * yc, axis=-1, keepdims=True) + LN_EPS) * lg_ref[...] + lb_ref[...]
    o_ref[0] = (yn * _sigmoid(yn)).astype(o_ref.dtype)


def _conv_module(p0, dw_w, dw_b, ln_g, ln_b, col0):
    b, s, _ = p0.shape
    c = dw_w.shape[1]
    a_blk, g_blk = col0 // c, col0 // c + 1
    ratio = CONV_TILE // CONV_HALO
    main = lambda blk: pl.BlockSpec((1, CONV_TILE, c), lambda bb, t: (bb, t, blk))
    halo = lambda blk: pl.BlockSpec((1, CONV_HALO, c), lambda bb, t: (bb, jnp.maximum(t * ratio - 1, 0), blk))
    vec = lambda a: pl.BlockSpec(a.shape, lambda bb, t: (0, 0))
    return pl.pallas_call(
        _conv_kernel,
        grid=(b, s // CONV_TILE),
        in_specs=[main(a_blk), main(g_blk), halo(a_blk), halo(g_blk),
                  vec(dw_w), vec(dw_b), vec(ln_g), vec(ln_b)],
        out_specs=pl.BlockSpec((1, CONV_TILE, c), lambda bb, t: (bb, t, 0)),
        out_shape=jax.ShapeDtypeStruct((b, s, c), BF16),
        scratch_shapes=[pltpu.VMEM((CONV_HALO + CONV_TILE, c), F32),
                        pltpu.VMEM((SUBLANES - 1, CONV_HALO + CONV_TILE - SUBLANES, c), F32)],
        compiler_params=_params("parallel", "parallel"),
        name="conv_module",
    )(p0, p0, p0, p0, dw_w, dw_b, ln_g, ln_b)


def _post_kernel(*refs, n_mix, with_next):
    x_ref = refs[0]
    mix_refs = refs[1:1 + n_mix]
    (wo_ref, gpost_ref, gfpre_ref, wg_ref, wu_ref, wd_ref, gfpost_ref) = refs[1 + n_mix:8 + n_mix]
    pos = 8 + n_mix
    if with_next:
        gnext_ref, wnext_ref = refs[pos:pos + 2]
        pos += 2
    h_out_ref = refs[pos]
    pos += 1
    if with_next:
        p_out_ref = refs[pos]
        pos += 1
    hid_ref = refs[pos]

    n_rows = x_ref.shape[0]
    groups = [slice(r0, r0 + n_rows // POST_ROW_GROUPS) for r0 in range(0, n_rows, n_rows // POST_ROW_GROUPS)]

    def out_proj(rows):
        y = None
        row = 0
        for r in mix_refs:
            w = r.shape[-1]
            part = _dot(r[rows, :], wo_ref[row:row + w, :])
            y = part if y is None else y + part
            row += w
        return y

    ys = [out_proj(rows) for rows in groups]
    h1s = [x_ref[rows, :] + _rms(y, gpost_ref[...]) for rows, y in zip(groups, ys)]
    ts = [_rms(h1, gfpre_ref[...]).astype(BF16) for h1 in h1s]
    d_ff = wg_ref.shape[1]
    for j in range(0, d_ff, FFN_CHUNK):
        wdt = min(FFN_CHUNK, d_ff - j)
        for rows, t in zip(groups, ts):
            gate = _dot(t, wg_ref[:, j:j + wdt])
            up = _dot(t, wu_ref[:, j:j + wdt])
            hid_ref[rows, j:j + wdt] = (gate * _sigmoid(gate) * up).astype(BF16)
    fs = [_dot(hid_ref[rows, :], wd_ref[...]) for rows in groups]
    h2s = [h1 + _rms(f, gfpost_ref[...]) for h1, f in zip(h1s, fs)]
    for rows, h2 in zip(groups, h2s):
        h_out_ref[rows, :] = h2
    if with_next:
        t2s = [_rms(h2, gnext_ref[...]).astype(BF16) for h2 in h2s]
        for rows, t2 in zip(groups, t2s):
            p_out_ref[rows, :] = _dot(t2, wnext_ref[...])


def _post(x2, mix_parts, w_out, g_post, g_fpre, w_gate, w_up, w_down, g_fpost, nxt=None):
    n, d = x2.shape
    d_ff = w_gate.shape[1]
    row = lambda a: pl.BlockSpec((ROW_TILE, a.shape[1]), lambda i: (i, 0))
    consts = [w_out, g_post, g_fpre, w_gate, w_up, w_down, g_fpost] + (list(nxt) if nxt else [])
    out_shape = [jax.ShapeDtypeStruct((n, d), F32)]
    out_specs = [pl.BlockSpec((ROW_TILE, d), lambda i: (i, 0))]
    if nxt:
        n_next = nxt[1].shape[1]
        out_shape.append(jax.ShapeDtypeStruct((n, n_next), F32))
        out_specs.append(pl.BlockSpec((ROW_TILE, n_next), lambda i: (i, 0)))
    kern = functools.partial(_post_kernel, n_mix=len(mix_parts), with_next=bool(nxt))
    return pl.pallas_call(
        kern,
        grid=(n // ROW_TILE,),
        in_specs=[row(x2)] + [row(m) for m in mix_parts] + [_const_spec(c.shape) for c in consts],
        out_specs=out_specs,
        out_shape=out_shape,
        scratch_shapes=[pltpu.VMEM((ROW_TILE, d_ff), BF16)],
        compiler_params=_params("parallel"),
        name="post_next" if nxt else "post",
    )(x2, *mix_parts, *consts)


def _gelu_tanh(x):
    return 0.5 * x * (1.0 + jnp.tanh(math.sqrt(2.0 / math.pi) * (x + 0.044715 * (x * x * x))))


def _ssm_kernel(u_ref, lr_ref, li_ref, ldt_ref, bre_ref, bim_ref, cre_ref, cim_ref, d_ref, wg_ref, bg_ref,
                o_ref, bmat_ref, cmat_ref, a_ref, st_ref, us_ref, utm_ref, x_ref, y_ref):
    nb, tt, ch = u_ref.shape
    n_state = lr_ref.shape[1]

    @pl.when(pl.program_id(0) == 0)
    def _init():
        lr, li = lr_ref[...], li_ref[...]
        dt = jnp.exp(ldt_ref[...])
        mag = jnp.exp(lr * dt)
        ab_re = mag * jnp.cos(li * dt)
        ab_im = mag * jnp.sin(li * dt)
        den = lr * lr + li * li
        n_re = ab_re - 1.0
        f_re = (n_re * lr + ab_im * li) / den
        f_im = (ab_im * lr - n_re * li) / den
        br, bi = bre_ref[...], bim_ref[...]
        bmat_ref[:, :n_state] = (f_re * br - f_im * bi).astype(BF16)
        bmat_ref[:, n_state:] = (f_re * bi + f_im * br).astype(BF16)
        cmat_ref[:n_state, :] = cre_ref[...].astype(BF16)
        cmat_ref[n_state:, :] = (-cim_ref[...]).astype(BF16)
        a_ref[0:1, :] = ab_re
        a_ref[1:2, :] = ab_im
        st_ref[...] = jnp.zeros(st_ref.shape, F32)

    n_ublk = ch // LANES
    for b in range(nb):
        for j in range(n_ublk):
            us_ref[j, b * SSM_PITCH:b * SSM_PITCH + tt, :] = u_ref[b, :, j * LANES:(j + 1) * LANES]

    def gather_step(t, carry):
        dst = pl.multiple_of(t * nb, nb)
        for j in range(n_ublk):
            utm_ref[pl.ds(dst, nb), j * LANES:(j + 1) * LANES] = us_ref[j, pl.ds(t, nb, stride=SSM_PITCH), :]
        return carry

    lax.fori_loop(0, tt, gather_step, 0, unroll=8)

    half_rows = tt * nb // 2
    for r0 in (0, half_rows):
        x_ref[r0:r0 + half_rows, :] = _dot(utm_ref[r0:r0 + half_rows, :].astype(BF16), bmat_ref[...])

    a_re = jnp.broadcast_to(a_ref[0:1, :], (nb, n_state))
    a_im = jnp.broadcast_to(a_ref[1:2, :], (nb, n_state))

    def scan_step(t, carry):
        x_re, x_im = carry
        row = pl.multiple_of(t * nb, nb)
        n_re = a_re * x_re - a_im * x_im + x_ref[pl.ds(row, nb), :n_state]
        n_im = a_re * x_im + a_im * x_re + x_ref[pl.ds(row, nb), n_state:]
        x_ref[pl.ds(row, nb), :n_state] = n_re
        x_ref[pl.ds(row, nb), n_state:] = n_im
        return n_re, n_im

    x_re, x_im = lax.fori_loop(0, tt, scan_step, (st_ref[:, :n_state], st_ref[:, n_state:]), unroll=4)
    st_ref[:, :n_state] = x_re
    st_ref[:, n_state:] = x_im

    n_yblk = ch // LANES
    for r0 in (0, half_rows):
        y_tm = _dot(x_ref[r0:r0 + half_rows, :].astype(BF16), cmat_ref[...])
        for j in range(n_yblk):
            y_ref[j, r0:r0 + half_rows, :] = y_tm[:, j * LANES:(j + 1) * LANES]
    for b in range(nb):
        y = jnp.concatenate([y_ref[j, pl.ds(b, tt, stride=nb), :] for j in range(n_yblk)], axis=-1)
        y = y + d_ref[...] * u_ref[b]
        z = _gelu_tanh(y)
        gate = _dot(z.astype(BF16), wg_ref[...]) + bg_ref[...]
        o_ref[b] = (z * _sigmoid(gate)).astype(o_ref.dtype)


def _ssm(p1, rows, b_bd, c_bd, d_row, w_glu, b_glu):
    b, s, _ = p1.shape
    ch = w_glu.shape[0]
    n_state = rows[0].shape[1]
    consts = list(rows) + list(b_bd) + list(c_bd) + [d_row, w_glu, b_glu]
    return pl.pallas_call(
        _ssm_kernel,
        grid=(s // SSM_TILE,),
        in_specs=[pl.BlockSpec((b, SSM_TILE, ch), lambda t: (0, t, 0))] + [_const_spec(c.shape) for c in consts],
        out_specs=pl.BlockSpec((b, SSM_TILE, ch), lambda t: (0, t, 0)),
        out_shape=jax.ShapeDtypeStruct((b, s, ch), BF16),
        scratch_shapes=[pltpu.VMEM((ch, 2 * n_state), BF16),
                        pltpu.VMEM((2 * n_state, ch), BF16),
                        pltpu.VMEM((SUBLANES, n_state), F32),
                        pltpu.VMEM((b, 2 * n_state), F32),
                        pltpu.VMEM((ch // LANES, b * SSM_PITCH, LANES), F32),
                        pltpu.VMEM((SSM_TILE * b, ch), F32),
                        pltpu.VMEM((SSM_TILE * b, 2 * n_state), F32),
                        pltpu.VMEM((ch // LANES, SSM_TILE * b, LANES), F32)],
        compiler_params=_params("arbitrary"),
        name="s5_ssm",
    )(p1, *consts)


def _rope_tables(s, rot_dim, theta):
    inv = theta ** (-jnp.arange(0, rot_dim, 2, dtype=F32) / rot_dim)
    ang = jnp.arange(s, dtype=F32)[:, None] * inv[None, :]
    return jnp.cos(ang), jnp.sin(ang)


def _lane_tables(cos, sin, period):
    s, half = cos.shape
    reps = LANES // period
    one = jnp.ones((s, period - 2 * half), F32)
    zero = jnp.zeros((s, period - 2 * half), F32)
    zh = jnp.zeros((s, half), F32)
    c = jnp.tile(jnp.concatenate([cos, cos, one], axis=1), (1, reps))
    sa = jnp.tile(jnp.concatenate([-sin, zh, zero], axis=1), (1, reps))
    sb = jnp.tile(jnp.concatenate([zh, sin, zero], axis=1), (1, reps))
    return c, sa, sb


def _block_diag(blocks):
    g, r, c = blocks.shape
    eye = jnp.eye(g, dtype=blocks.dtype)
    return (eye[:, None, :, None] * blocks[:, :, None, :]).reshape(g * r, g * c)


def kernel(x, l0_mix_pre, l0_mix_post, l0_w_in, l0_lambda_q1, l0_lambda_k1, l0_lambda_q2, l0_lambda_k2, l0_subln, l0_dw_w, l0_dw_b, l0_conv_ln_g, l0_conv_ln_b, l0_w_out, l0_ffn_pre, l0_ffn_post, l0_w_gate, l0_w_up, l0_w_down, l1_mix_pre, l1_mix_post, l1_w_in, l1_a_re, l1_a_im, l1_log_dt, l1_b_re, l1_b_im, l1_c_re, l1_c_im, l1_d_skip, l1_w_glu, l1_b_glu, l1_q_norm, l1_w_uq, l1_kv_norm, l1_w_ukv, l1_w_out, l1_ffn_pre, l1_ffn_post, l1_w_gate, l1_w_up, l1_w_down):
    b, s, d = x.shape
    n = b * s
    row = lambda v: v.reshape(1, -1).astype(F32)
    bf = lambda w: w.astype(BF16)

    diff_width = 4 * LANES
    n_diff_heads = diff_width // LANES
    conv_ch = l0_dw_w.shape[1]
    ssm_ch = l1_w_glu.shape[0]
    n_groups, n_state_g = l1_a_re.shape
    q_rank = l1_q_norm.shape[0]
    kv_rank = l1_kv_norm.shape[0]
    n_mla_heads = l1_w_uq.shape[1] // (MLA_NOPE + MLA_ROPE)

    cos_a, sin_a = _rope_tables(s, DIFF_ROT, ROPE_THETA)
    tabs_a = _lane_tables(cos_a, sin_a, period=DIFF_HEAD_DIM)
    x2 = x.reshape(n, d)
    w_main = bf(jnp.concatenate([l0_w_in[:, diff_width:2 * diff_width], l0_w_in[:, 3 * diff_width:]], axis=1))
    w_t = bf(jnp.concatenate([l0_w_in[:, :diff_width], l0_w_in[:, 2 * diff_width:3 * diff_width]], axis=1).T)
    p0, qvt_a = _l0_in(x2, row(l0_mix_pre), w_main, w_t, tabs_a, (cos_a.T, sin_a.T), b, s, diff_width)
    p0 = p0.reshape(b, s, -1)
    lam_vecs = jnp.stack([l0_lambda_q1, l0_lambda_k1, l0_lambda_q2, l0_lambda_k2]).astype(F32)
    y_a, (w_out0, w_gate0, w_up0, w_down0) = _diff_attn(
        p0, qvt_a, lam_vecs, l0_subln.reshape(-1, 1).astype(F32), n_diff_heads,
        [l0_w_out, l0_w_gate, l0_w_up, l0_w_down])
    y_b = _conv_module(p0, l0_dw_w.astype(F32), row(l0_dw_b), row(l0_conv_ln_g), row(l0_conv_ln_b),
                       col0=diff_width)

    pad = (-l1_w_in.shape[1]) % LANES
    w_in1 = bf(jnp.pad(l1_w_in, ((0, 0), (0, pad))))
    h2, p1 = _post(x2, [y_a.reshape(n, -1), y_b.reshape(n, -1)], w_out0, row(l0_mix_post),
                   row(l0_ffn_pre), w_gate0, w_up0, w_down0, row(l0_ffn_post),
                   nxt=(row(l1_mix_pre), w_in1))

    state_row = lambda a: a.reshape(1, -1).astype(F32)
    ssm_rows = (state_row(l1_a_re), state_row(l1_a_im),
                state_row(jnp.broadcast_to(l1_log_dt[:, None], (n_groups, n_state_g))))
    b_bd = tuple(_block_diag(jnp.swapaxes(m, 1, 2).astype(F32)) for m in (l1_b_re, l1_b_im))
    c_bd = tuple(_block_diag(jnp.swapaxes(m, 1, 2).astype(F32)) for m in (l1_c_re, l1_c_im))
    y_c = _ssm(p1.reshape(b, s, -1), ssm_rows, b_bd, c_bd, row(l1_d_skip), bf(l1_w_glu), row(l1_b_glu))

    wq = l1_w_uq.reshape(q_rank, n_mla_heads, MLA_NOPE + MLA_ROPE)
    wq = jnp.pad(wq, ((0, 0), (0, 0), (0, MLA_QK_PAD - MLA_NOPE - MLA_ROPE))).reshape(q_rank, -1)
    wkv = l1_w_ukv.reshape(kv_rank, n_mla_heads, MLA_NOPE + MLA_V)
    wk_nope = jnp.pad(wkv[:, :, :MLA_NOPE], ((0, 0), (0, 0), (0, MLA_QK_PAD - MLA_NOPE)))
    route = jnp.pad(jnp.eye(MLA_ROPE, dtype=F32), ((0, LANES - MLA_ROPE), (MLA_NOPE, MLA_QK_PAD - MLA_NOPE - MLA_ROPE)))
    wk_rope = jnp.broadcast_to(route[:, None, :], (LANES, n_mla_heads, MLA_QK_PAD))
    wk = jnp.concatenate([wk_nope, wk_rope], axis=0).reshape(kv_rank + LANES, -1)
    wv_t = wkv[:, :, MLA_NOPE:].reshape(kv_rank, -1).T
    cos_d, sin_d = _rope_tables(s, MLA_ROPE, MLA_ROPE_THETA)
    tabs_d = _lane_tables(cos_d, sin_d, period=LANES)
    y_d, (w_out1, w_gate1, w_up1, w_down1) = _mla_attn(
        p1.reshape(b, s, -1), row(l1_q_norm), row(l1_kv_norm), bf(wq.T), bf(wk), bf(wv_t),
        (cos_d.T, sin_d.T), tabs_d, n_mla_heads, [l1_w_out, l1_w_gate, l1_w_up, l1_w_down])

    (out,) = _post(h2, [y_c.reshape(n, -1), y_d.reshape(n, -1)], w_out1, row(l1_mix_post),
                   row(l1_ffn_pre), w_gate1, w_up1, w_down1, row(l1_ffn_post))
    return out.reshape(b, s, d)
```

```python
import functools
import math

import jax
import jax.numpy as jnp
import numpy as np
from jax import lax
from jax.experimental import pallas as pl
from jax.experimental.pallas import tpu as pltpu

F32 = jnp.float32
BF16 = jnp.bfloat16

LANES = 128
SUBLANES = 8
BF16_SUBLANES = 16
VMEM_LIMIT_BYTES = 56 * 1024 * 1024

CHUNK = 64
RMS_EPS = 1e-6
LN_EPS = 1e-5
ROPE_THETA = 500000.0
MLA_ROPE_THETA = 10000.0
DIFF_HEAD_DIM = 64
DIFF_ROT = 16
CONV_WIDTH = 31
CONV_HALO = 32
SSM_GROUP = 16
SSM_STATE = 64
MLA_NOPE = 128
MLA_ROPE = 64
MLA_V = 128
MLA_QK_PAD = 256

ROW_TILE = 512
ATTN_TILE = 256
ATTN_LOOKAHEAD = 5
ATTN_SUM_ROWS = 16
LOG2_E = math.log2(math.e)
CONV_TILE = 256
SSM_TILE = 128
SSM_PITCH = SSM_TILE + SUBLANES
FFN_CHUNK = 512
POST_ROW_GROUPS = 2


def _params(*sem):
    return pltpu.CompilerParams(dimension_semantics=sem, vmem_limit_bytes=VMEM_LIMIT_BYTES)


def _rms(x, g):
    return x * lax.rsqrt(jnp.mean(x * x, axis=-1, keepdims=True) + RMS_EPS) * g


def _sigmoid(x):
    return 1.0 / (1.0 + jnp.exp(-x))


def _dot(a, b):
    return jnp.dot(a, b, preferred_element_type=F32)


def _dot_nt(a, b):
    return lax.dot_general(a, b, (((1,), (1,)), ((), ())), preferred_element_type=F32)


def _rope_block(x, c, sa, sb, shift):
    return x * c + pltpu.roll(x, LANES - shift, 1) * sa + pltpu.roll(x, shift, 1) * sb


def _const_spec(shape):
    nd = len(shape)
    return pl.BlockSpec(shape, lambda *_: (0,) * nd, pipeline_mode=pl.Buffered(1))


def _vt_out(b, seq, width):
    tiles_per_seq = seq // ROW_TILE
    per_tile = ROW_TILE // ATTN_TILE
    spec = pl.BlockSpec((1, per_tile, width, ATTN_TILE),
                        lambda i: (i // tiles_per_seq, i % tiles_per_seq, 0, 0))
    return spec, jax.ShapeDtypeStruct((b, seq // ATTN_TILE, width, ATTN_TILE), BF16)


def _l0_in_kernel(x_ref, g_ref, win_ref, c_ref, sa_ref, sb_ref, cosq_ref, sinq_ref, o_ref, qvt_ref,
                  w_ref, wt_ref, *, n_k_blocks, q_rows, scale):
    @pl.when(pl.program_id(0) == 0)
    def _regroup_weights():
        w_ref[:, :q_rows] = win_ref[:, q_rows:2 * q_rows].astype(BF16)
        w_ref[:, q_rows:] = win_ref[:, 3 * q_rows:].astype(BF16)
        wt_ref[:q_rows, :] = win_ref[:, :q_rows].T.astype(BF16)
        wt_ref[q_rows:, :] = win_ref[:, 2 * q_rows:3 * q_rows].T.astype(BF16)

    half = DIFF_ROT // 2
    groups = [slice(r0, r0 + ATTN_TILE) for r0 in range(0, x_ref.shape[0], ATTN_TILE)]
    ts = [_rms(x_ref[rows, :], g_ref[...]).astype(BF16) for rows in groups]
    ps = [_dot(t, w_ref[...]) for t in ts]
    qv_ts = [_dot_nt(wt_ref[...], t) for t in ts]
    for j, (rows, p, qv_t) in enumerate(zip(groups, ps, qv_ts)):
        c, sa, sb = c_ref[rows, :], sa_ref[rows, :], sb_ref[rows, :]
        for blk_i in range(n_k_blocks):
            blk = _rope_block(p[:, blk_i * LANES:(blk_i + 1) * LANES], c, sa, sb, half)
            o_ref[rows, blk_i * LANES:(blk_i + 1) * LANES] = blk.astype(o_ref.dtype)
        rest = n_k_blocks * LANES
        o_ref[rows, rest:] = p[:, rest:].astype(o_ref.dtype)
        cos_t, sin_t = cosq_ref[:, rows], sinq_ref[:, rows]
        pieces = []
        for lo in range(0, q_rows, DIFF_HEAD_DIM):
            x1, x2 = qv_t[lo:lo + half, :], qv_t[lo + half:lo + 2 * half, :]
            pieces += [(x1 * cos_t - x2 * sin_t) * scale, (x2 * cos_t + x1 * sin_t) * scale,
                       qv_t[lo + 2 * half:lo + DIFF_HEAD_DIM, :] * scale]
        qvt_ref[0, j] = jnp.concatenate(pieces + [qv_t[q_rows:, :]], axis=0).astype(qvt_ref.dtype)


def _l0_in(x2, g_pre, w_in, tabs, q_tabs_t, b, seq, q_rows):
    n, d = x2.shape
    n_out = w_in.shape[1] - 2 * q_rows
    tiles_per_seq = seq // ROW_TILE
    kern = functools.partial(_l0_in_kernel, n_k_blocks=q_rows // LANES, q_rows=q_rows,
                             scale=DIFF_HEAD_DIM ** -0.5 * LOG2_E)
    tab_spec = pl.BlockSpec((ROW_TILE, LANES), lambda i: (i % tiles_per_seq, 0))
    qtab_spec = pl.BlockSpec((q_tabs_t[0].shape[0], ROW_TILE), lambda i: (0, i % tiles_per_seq))
    vt_spec, vt_shape = _vt_out(b, seq, 2 * q_rows)
    return pl.pallas_call(
        kern,
        grid=(n // ROW_TILE,),
        in_specs=[pl.BlockSpec((ROW_TILE, d), lambda i: (i, 0)),
                  _const_spec((1, d)), _const_spec(w_in.shape),
                  tab_spec, tab_spec, tab_spec, qtab_spec, qtab_spec],
        out_specs=[pl.BlockSpec((ROW_TILE, n_out), lambda i: (i, 0)), vt_spec],
        out_shape=[jax.ShapeDtypeStruct((n, n_out), BF16), vt_shape],
        scratch_shapes=[pltpu.VMEM((d, n_out), BF16), pltpu.VMEM((2 * q_rows, d), BF16)],
        compiler_params=_params("arbitrary"),
        name="l0_in",
    )(x2, g_pre, w_in, *tabs, *q_tabs_t)


def _attn_flat(q_heads_at, k_at, vt_at, finish_tile, m_ref, acc_ref, *, n_tiles, n_maps, tq, q_transposed):
    dk_axis = 0 if q_transposed else 1
    qs_cache = {}

    def qs_of(i):
        if i not in qs_cache:
            qs = []
            for q in q_heads_at(i):
                if n_maps == 1:
                    qs.append(q)
                else:
                    width = q.shape[dk_axis] // n_maps
                    pos = lax.broadcasted_iota(jnp.int32, q.shape, dk_axis)
                    for c in range(n_maps):
                        qs.append(jnp.where((pos >= c * width) & (pos < (c + 1) * width), q, jnp.zeros_like(q)))
            qs_cache[i] = qs
        return qs_cache[i]

    n_chains = len(qs_of(0))
    ones = jnp.ones((ATTN_SUM_ROWS, tq), BF16)
    kc = lax.broadcasted_iota(jnp.int32, (tq, tq), 0) // CHUNK
    qc = lax.broadcasted_iota(jnp.int32, (tq, tq), 1) // CHUNK
    keep = kc <= qc
    items = [(i, kb, ch) for i in range(n_tiles) for kb in range(i + 1) for ch in range(n_chains)]

    def scores(i, kb, ch):
        k = k_at(kb, ch // n_maps)
        return _dot(k, qs_of(i)[ch]) if q_transposed else _dot_nt(k, qs_of(i)[ch])

    ss = {n: scores(*items[n]) for n in range(min(ATTN_LOOKAHEAD, len(items)))}
    for n, (i, kb, ch) in enumerate(items):
        if n + ATTN_LOOKAHEAD < len(items):
            ss[n + ATTN_LOOKAHEAD] = scores(*items[n + ATTN_LOOKAHEAD])
        s = ss.pop(n)
        if kb == i:
            s = jnp.where(keep, s, -jnp.inf)
        vt = jnp.concatenate([vt_at(kb, ch // n_maps), ones], axis=0)
        slot = i % 2
        if kb == 0:
            m_new = jnp.max(s, axis=0, keepdims=True)
            acc_new = _dot(vt, jnp.exp2(s - m_new).astype(vt.dtype))
        else:
            m_old = m_ref[slot, ch]
            m_new = jnp.maximum(m_old, jnp.max(s, axis=0, keepdims=True))
            acc_new = (jnp.exp2(m_old - m_new) * acc_ref[slot, ch]
                       + _dot(vt, jnp.exp2(s - m_new).astype(vt.dtype)))
        m_ref[slot, ch] = m_new
        acc_ref[slot, ch] = acc_new
        if kb == i and ch == n_chains - 1:
            finish_tile(i, slot)
            del qs_cache[i]


def _attn_flat_scratch(n_chains, dv):
    return [pltpu.VMEM((2, n_chains, 1, ATTN_TILE), F32),
            pltpu.VMEM((2, n_chains, dv + ATTN_SUM_ROWS, ATTN_TILE), F32)]


def _attn_out(acc_ref, ch, dv):
    return acc_ref[ch, :dv, :] / acc_ref[ch, dv:dv + 1, :]


def _cast_specs(weights, n_steps):
    specs = []
    for w in weights:
        rows = w.shape[0] // n_steps
        assert rows * n_steps == w.shape[0] and rows % BF16_SUBLANES == 0, w.shape
        specs.append(pl.BlockSpec((rows, w.shape[1]), lambda bb: (bb, 0)))
    return specs, [jax.ShapeDtypeStruct(w.shape, BF16) for w in weights]


def _cast_blocks(src_refs, dst_refs):
    for src, dst in zip(src_refs, dst_refs):
        dst[...] = src[...].astype(dst.dtype)


def _split_cast_refs(refs, n_in, n_out, n_cast):
    main_in, cast_src = refs[:n_in], refs[n_in:n_in + n_cast]
    rest = refs[n_in + n_cast:]
    return main_in, cast_src, rest[:n_out], rest[n_out:n_out + n_cast], rest[n_out + n_cast:]


def _diff_attn_kernel(*refs, n_heads, lam_init, n_cast):
    (qt_ref, k_ref, vt_ref, lam_ref, sub_ref), cast_src, (o_ref,), cast_dst, (m_ref, acc_ref) = (
        _split_cast_refs(refs, 5, 1, n_cast))
    _cast_blocks(cast_src, cast_dst)
    n_tiles, tq = qt_ref.shape[1], qt_ref.shape[3]
    lv = lam_ref[...]
    lam = (jnp.exp(jnp.sum(lv[0:1] * lv[1:2], axis=-1, keepdims=True))
           - jnp.exp(jnp.sum(lv[2:3] * lv[3:4], axis=-1, keepdims=True)) + lam_init)

    def finish_tile(i, slot):
        for h in range(n_heads):
            o_t = _attn_out(acc_ref.at[slot], 2 * h, LANES) - lam * _attn_out(acc_ref.at[slot], 2 * h + 1, LANES)
            inv = lax.rsqrt(jnp.mean(o_t * o_t, axis=0, keepdims=True) + RMS_EPS)
            y_t = o_t * inv * sub_ref[...] * (1.0 - lam_init)
            o_ref[0, i * tq:(i + 1) * tq, h * LANES:(h + 1) * LANES] = y_t.T.astype(o_ref.dtype)

    _attn_flat(lambda i: [qt_ref[0, i, h * LANES:(h + 1) * LANES, :] for h in range(n_heads)],
               lambda kb, h: k_ref[0, kb * tq:(kb + 1) * tq, h * LANES:(h + 1) * LANES],
               lambda kb, h: vt_ref[0, kb, h * LANES:(h + 1) * LANES, :],
               finish_tile, m_ref, acc_ref, n_tiles=n_tiles, n_maps=2, tq=tq, q_transposed=True)


def _diff_attn(p0, qvt, lam_vecs, subln_col, n_heads, cast_weights):
    b, s, _ = p0.shape
    n_kb = s // ATTN_TILE
    width = n_heads * LANES
    cast_specs, cast_shapes = _cast_specs(cast_weights, b)
    kern = functools.partial(_diff_attn_kernel, n_heads=n_heads, lam_init=0.8 - 0.6 * math.exp(-0.3 * 0),
                             n_cast=len(cast_weights))
    out = pl.pallas_call(
        kern,
        grid=(b,),
        in_specs=[pl.BlockSpec((1, n_kb, width, ATTN_TILE), lambda bb: (bb, 0, 0, 0)),
                  pl.BlockSpec((1, s, width), lambda bb: (bb, 0, 0)),
                  pl.BlockSpec((1, n_kb, width, ATTN_TILE), lambda bb: (bb, 0, 1, 0)),
                  pl.BlockSpec(lam_vecs.shape, lambda bb: (0, 0)),
                  pl.BlockSpec(subln_col.shape, lambda bb: (0, 0))] + cast_specs,
        out_specs=[pl.BlockSpec((1, s, width), lambda bb: (bb, 0, 0))] + cast_specs,
        out_shape=[jax.ShapeDtypeStruct((b, s, width), BF16)] + cast_shapes,
        scratch_shapes=_attn_flat_scratch(2 * n_heads, LANES),
        compiler_params=_params("arbitrary"),
        name="diff_attn",
    )(qvt, p0, qvt, lam_vecs, subln_col, *cast_weights)
    return out[0], out[1:]


def _mla_attn_kernel(*refs, n_heads, scale, n_cast):
    ((cq_ref, kv_ref, qn_ref, kvn_ref, wqt_ref, wk_ref, wvt_ref, cosq_tab, sinq_tab, ck_tab, sak_tab, sbk_tab),
     cast_src, (o_ref,), cast_dst, (m_ref, acc_ref, k_scr, vt_scr)) = (
        _split_cast_refs(refs, 12, 1, n_cast))
    _cast_blocks(cast_src, cast_dst)
    seq = kv_ref.shape[1]
    tq = ATTN_TILE
    half = MLA_ROPE // 2
    projected = set()

    def project_keys(chunk):
        if chunk in projected:
            return
        projected.add(chunk)
        rows = slice(chunk * ROW_TILE, (chunk + 1) * ROW_TILE)
        ckv = _rms(kv_ref[0, rows, :LANES], kvn_ref[...]).astype(BF16)
        kr = _rope_block(kv_ref[0, rows, LANES:], ck_tab[rows, :], sak_tab[rows, :], sbk_tab[rows, :], half)
        k_scr[rows, :] = _dot(jnp.concatenate([ckv, kr.astype(BF16)], axis=-1), wk_ref[...]).astype(BF16)
        vt = _dot_nt(wvt_ref[...], ckv)
        for j in range(ROW_TILE // tq):
            vt_scr[chunk * (ROW_TILE // tq) + j] = vt[:, j * tq:(j + 1) * tq].astype(BF16)

    def q_heads_at(i):
        cols = slice(i * tq, (i + 1) * tq)
        q_t = _dot_nt(wqt_ref[...], _rms(cq_ref[0, cols, :], qn_ref[...]).astype(BF16))
        cos_t, sin_t = cosq_tab[:, cols], sinq_tab[:, cols]
        q_heads = []
        for h in range(n_heads):
            lo = h * MLA_QK_PAD
            x1 = q_t[lo + MLA_NOPE:lo + MLA_NOPE + half, :]
            x2 = q_t[lo + MLA_NOPE + half:lo + MLA_NOPE + 2 * half, :]
            q_h = jnp.concatenate([q_t[lo:lo + MLA_NOPE, :], x1 * cos_t - x2 * sin_t, x2 * cos_t + x1 * sin_t,
                                   q_t[lo + MLA_NOPE + 2 * half:lo + MLA_QK_PAD, :]], axis=0)
            q_heads.append((q_h * scale).astype(BF16))
        return q_heads

    def k_at(kb, h):
        project_keys(kb * tq // ROW_TILE)
        return k_scr[kb * tq:(kb + 1) * tq, h * MLA_QK_PAD:(h + 1) * MLA_QK_PAD]

    def finish_tile(i, slot):
        for h in range(n_heads):
            o_ref[0, i * tq:(i + 1) * tq, h * MLA_V:(h + 1) * MLA_V] = (
                _attn_out(acc_ref.at[slot], h, MLA_V).T.astype(o_ref.dtype))

    _attn_flat(q_heads_at, k_at, lambda kb, h: vt_scr[kb, h * MLA_V:(h + 1) * MLA_V, :],
               finish_tile, m_ref, acc_ref, n_tiles=seq // tq, n_maps=1, tq=tq, q_transposed=True)


def _mla_attn(p1, q_norm, kv_norm, wq_t, wk, wv_t, q_tabs_t, tabs, n_heads, cast_weights):
    b, s, _ = p1.shape
    n_kb = s // ATTN_TILE
    cast_specs, cast_shapes = _cast_specs(cast_weights, b)
    kern = functools.partial(_mla_attn_kernel, n_heads=n_heads, scale=(MLA_NOPE + MLA_ROPE) ** -0.5 * LOG2_E,
                             n_cast=len(cast_weights))
    const = lambda a: pl.BlockSpec(a.shape, lambda bb: (0,) * a.ndim)
    out = pl.pallas_call(
        kern,
        grid=(b,),
        in_specs=[pl.BlockSpec((1, s, 2 * LANES), lambda bb: (bb, 0, 1)),
                  pl.BlockSpec((1, s, 2 * LANES), lambda bb: (bb, 0, 2)),
                  const(q_norm), const(kv_norm), const(wq_t), const(wk), const(wv_t),
                  const(q_tabs_t[0]), const(q_tabs_t[1]),
                  const(tabs[0]), const(tabs[1]), const(tabs[2])] + cast_specs,
        out_specs=[pl.BlockSpec((1, s, n_heads * MLA_V), lambda bb: (bb, 0, 0))] + cast_specs,
        out_shape=[jax.ShapeDtypeStruct((b, s, n_heads * MLA_V), BF16)] + cast_shapes,
        scratch_shapes=_attn_flat_scratch(n_heads, MLA_V) + [
            pltpu.VMEM((s, n_heads * MLA_QK_PAD), BF16),
            pltpu.VMEM((n_kb, n_heads * MLA_V, ATTN_TILE), BF16)],
        compiler_params=_params("arbitrary"),
        name="mla_attn",
    )(p1, p1, q_norm, kv_norm, wq_t, wk, wv_t, *q_tabs_t, *tabs, *cast_weights)
    return out[0], out[1:]


def _conv_kernel(a_ref, gate_ref, ah_ref, gh_ref, w_ref, b_ref, lg_ref, lb_ref, o_ref, u_ref, ur_ref):
    tt = a_ref.shape[1]
    u_ref[CONV_HALO:, :] = a_ref[0].astype(F32) * _sigmoid(gate_ref[0].astype(F32))
    halo = ah_ref[0].astype(F32) * _sigmoid(gh_ref[0].astype(F32))
    u_ref[:CONV_HALO, :] = jnp.where(pl.program_id(1) > 0, halo, jnp.zeros_like(halo))
    rows = ur_ref.shape[1]
    for r in range(1, SUBLANES):
        ur_ref[r - 1] = u_ref[r:r + rows, :]
    acc = jnp.zeros((tt, a_ref.shape[2]), F32)
    first = CONV_HALO - (CONV_WIDTH - 1)
    for k in range(CONV_WIDTH):
        base, r = divmod(first + k, SUBLANES)
        src = u_ref if r == 0 else ur_ref.at[r - 1]
        acc = acc + src[base * SUBLANES:base * SUBLANES + tt, :] * w_ref[k:k + 1, :]
    y = acc + b_ref[...]
    mu = jnp.mean(y, axis=-1, keepdims=True)
    yc = y - mu
    yn = yc * lax.rsqrt(jnp.mean(yc * yc, axis=-1, keepdims=True) + LN_EPS) * lg_ref[...] + lb_ref[...]
    o_ref[0] = (yn * _sigmoid(yn)).astype(o_ref.dtype)


def _conv_module(p0, dw_w, dw_b, ln_g, ln_b, col0):
    b, s, _ = p0.shape
    c = dw_w.shape[1]
    a_blk, g_blk = col0 // c, col0 // c + 1
    ratio = CONV_TILE // CONV_HALO
    main = lambda blk: pl.BlockSpec((1, CONV_TILE, c), lambda bb, t: (bb, t, blk))
    halo = lambda blk: pl.BlockSpec((1, CONV_HALO, c), lambda bb, t: (bb, jnp.maximum(t * ratio - 1, 0), blk))
    vec = lambda a: pl.BlockSpec(a.shape, lambda bb, t: (0, 0))
    return pl.pallas_call(
        _conv_kernel,
        grid=(b, s // CONV_TILE),
        in_specs=[main(a_blk), main(g_blk), halo(a_blk), halo(g_blk),
                  vec(dw_w), vec(dw_b), vec(ln_g), vec(ln_b)],
        out_specs=pl.BlockSpec((1, CONV_TILE, c), lambda bb, t: (bb, t, 0)),
        out_shape=jax.ShapeDtypeStruct((b, s, c), BF16),
        scratch_shapes=[pltpu.VMEM((CONV_HALO + CONV_TILE, c), F32),
                        pltpu.VMEM((SUBLANES - 1, CONV_HALO + CONV_TILE - SUBLANES, c), F32)],
        compiler_params=_params("parallel", "parallel"),
        name="conv_module",
    )(p0, p0, p0, p0, dw_w, dw_b, ln_g, ln_b)


def _post_kernel(*refs, n_mix, with_next):
    x_ref = refs[0]
    mix_refs = refs[1:1 + n_mix]
    (wo_ref, gpost_ref, gfpre_ref, wg_ref, wu_ref, wd_ref, gfpost_ref) = refs[1 + n_mix:8 + n_mix]
    pos = 8 + n_mix
    if with_next:
        gnext_ref, wnext_ref = refs[pos:pos + 2]
        pos += 2
    h_out_ref = refs[pos]
    pos += 1
    if with_next:
        p_out_ref = refs[pos]
        pos += 1
    hid_ref = refs[pos]

    n_rows = x_ref.shape[0]
    groups = [slice(r0, r0 + n_rows // POST_ROW_GROUPS) for r0 in range(0, n_rows, n_rows // POST_ROW_GROUPS)]

    def out_proj(rows):
        y = None
        row = 0
        for r in mix_refs:
            w = r.shape[-1]
            part = _dot(r[rows, :], wo_ref[row:row + w, :])
            y = part if y is None else y + part
            row += w
        return y

    ys = [out_proj(rows) for rows in groups]
    h1s = [x_ref[rows, :] + _rms(y, gpost_ref[...]) for rows, y in zip(groups, ys)]
    ts = [_rms(h1, gfpre_ref[...]).astype(BF16) for h1 in h1s]
    d_ff = wg_ref.shape[1]
    for j in range(0, d_ff, FFN_CHUNK):
        wdt = min(FFN_CHUNK, d_ff - j)
        for rows, t in zip(groups, ts):
            gate = _dot(t, wg_ref[:, j:j + wdt])
            up = _dot(t, wu_ref[:, j:j + wdt])
            hid_ref[rows, j:j + wdt] = (gate * _sigmoid(gate) * up).astype(BF16)
    fs = [_dot(hid_ref[rows, :], wd_ref[...]) for rows in groups]
    h2s = [h1 + _rms(f, gfpost_ref[...]) for h1, f in zip(h1s, fs)]
    for rows, h2 in zip(groups, h2s):
        h_out_ref[rows, :] = h2
    if with_next:
        t2s = [_rms(h2, gnext_ref[...]).astype(BF16) for h2 in h2s]
        for rows, t2 in zip(groups, t2s):
            p_out_ref[rows, :] = _dot(t2, wnext_ref[...])


def _post(x2, mix_parts, w_out, g_post, g_fpre, w_gate, w_up, w_down, g_fpost, nxt=None):
    n, d = x2.shape
    d_ff = w_gate.shape[1]
    row = lambda a: pl.BlockSpec((ROW_TILE, a.shape[1]), lambda i: (i, 0))
    consts = [w_out, g_post, g_fpre, w_gate, w_up, w_down, g_fpost] + (list(nxt) if nxt else [])
    out_shape = [jax.ShapeDtypeStruct((n, d), F32)]
    out_specs = [pl.BlockSpec((ROW_TILE, d), lambda i: (i, 0))]
    if nxt:
        n_next = nxt[1].shape[1]
        out_shape.append(jax.ShapeDtypeStruct((n, n_next), F32))
        out_specs.append(pl.BlockSpec((ROW_TILE, n_next), lambda i: (i, 0)))
    kern = functools.partial(_post_kernel, n_mix=len(mix_parts), with_next=bool(nxt))
    return pl.pallas_call(
        kern,
        grid=(n // ROW_TILE,),
        in_specs=[row(x2)] + [row(m) for m in mix_parts] + [_const_spec(c.shape) for c in consts],
        out_specs=out_specs,
        out_shape=out_shape,
        scratch_shapes=[pltpu.VMEM((ROW_TILE, d_ff), BF16)],
        compiler_params=_params("parallel"),
        name="post_next" if nxt else "post",
    )(x2, *mix_parts, *consts)


def _gelu_tanh(x):
    return 0.5 * x * (1.0 + jnp.tanh(math.sqrt(2.0 / math.pi) * (x + 0.044715 * (x * x * x))))


def _ssm_kernel(u_ref, lr_ref, li_ref, ldt_ref, bre_ref, bim_ref, cre_ref, cim_ref, d_ref, wg_ref, bg_ref,
                o_ref, bmat_ref, cmat_ref, a_ref, st_ref, us_ref, utm_ref, x_ref, y_ref):
    nb, tt, ch = u_ref.shape
    n_state = lr_ref.shape[1]

    @pl.when(pl.program_id(0) == 0)
    def _init():
        lr, li = lr_ref[...], li_ref[...]
        dt = jnp.exp(ldt_ref[...])
        mag = jnp.exp(lr * dt)
        ab_re = mag * jnp.cos(li * dt)
        ab_im = mag * jnp.sin(li * dt)
        den = lr * lr + li * li
        n_re = ab_re - 1.0
        f_re = (n_re * lr + ab_im * li) / den
        f_im = (ab_im * lr - n_re * li) / den
        br, bi = bre_ref[...], bim_ref[...]
        bmat_ref[:, :n_state] = (f_re * br - f_im * bi).astype(BF16)
        bmat_ref[:, n_state:] = (f_re * bi + f_im * br).astype(BF16)
        cmat_ref[:n_state, :] = cre_ref[...].astype(BF16)
        cmat_ref[n_state:, :] = (-cim_ref[...]).astype(BF16)
        a_ref[0:1, :] = ab_re
        a_ref[1:2, :] = ab_im
        st_ref[...] = jnp.zeros(st_ref.shape, F32)

    n_ublk = ch // LANES
    for b in range(nb):
        for j in range(n_ublk):
            us_ref[j, b * SSM_PITCH:b * SSM_PITCH + tt, :] = u_ref[b, :, j * LANES:(j + 1) * LANES]

    def gather_step(t, carry):
        dst = pl.multiple_of(t * nb, nb)
        for j in range(n_ublk):
            utm_ref[pl.ds(dst, nb), j * LANES:(j + 1) * LANES] = us_ref[j, pl.ds(t, nb, stride=SSM_PITCH), :]
        return carry

    lax.fori_loop(0, tt, gather_step, 0, unroll=8)

    half_rows = tt * nb // 2
    for r0 in (0, half_rows):
        x_ref[r0:r0 + half_rows, :] = _dot(utm_ref[r0:r0 + half_rows, :].astype(BF16), bmat_ref[...])

    a_re = jnp.broadcast_to(a_ref[0:1, :], (nb, n_state))
    a_im = jnp.broadcast_to(a_ref[1:2, :], (nb, n_state))

    def scan_step(t, carry):
        x_re, x_im = carry
        row = pl.multiple_of(t * nb, nb)
        n_re = a_re * x_re - a_im * x_im + x_ref[pl.ds(row, nb), :n_state]
        n_im = a_re * x_im + a_im * x_re + x_ref[pl.ds(row, nb), n_state:]
        x_ref[pl.ds(row, nb), :n_state] = n_re
        x_ref[pl.ds(row, nb), n_state:] = n_im
        return n_re, n_im

    x_re, x_im = lax.fori_loop(0, tt, scan_step, (st_ref[:, :n_state], st_ref[:, n_state:]), unroll=4)
    st_ref[:, :n_state] = x_re
    st_ref[:, n_state:] = x_im

    n_yblk = ch // LANES
    for r0 in (0, half_rows):
        y_tm = _dot(x_ref[r0:r0 + half_rows, :].astype(BF16), cmat_ref[...])
        for j in range(n_yblk):
            y_ref[j, r0:r0 + half_rows, :] = y_tm[:, j * LANES:(j + 1) * LANES]
    for b in range(nb):
        y = jnp.concatenate([y_ref[j, pl.ds(b, tt, stride=nb), :] for j in range(n_yblk)], axis=-1)
        y = y + d_ref[...] * u_ref[b]
        z = _gelu_tanh(y)
        gate = _dot(z.astype(BF16), wg_ref[...]) + bg_ref[...]
        o_ref[b] = (z * _sigmoid(gate)).astype(o_ref.dtype)


def _ssm(p1, rows, b_bd, c_bd, d_row, w_glu, b_glu):
    b, s, _ = p1.shape
    ch = w_glu.shape[0]
    n_state = rows[0].shape[1]
    consts = list(rows) + list(b_bd) + list(c_bd) + [d_row, w_glu, b_glu]
    return pl.pallas_call(
        _ssm_kernel,
        grid=(s // SSM_TILE,),
        in_specs=[pl.BlockSpec((b, SSM_TILE, ch), lambda t: (0, t, 0))] + [_const_spec(c.shape) for c in consts],
        out_specs=pl.BlockSpec((b, SSM_TILE, ch), lambda t: (0, t, 0)),
        out_shape=jax.ShapeDtypeStruct((b, s, ch), BF16),
        scratch_shapes=[pltpu.VMEM((ch, 2 * n_state), BF16),
                        pltpu.VMEM((2 * n_state, ch), BF16),
                        pltpu.VMEM((SUBLANES, n_state), F32),
                        pltpu.VMEM((b, 2 * n_state), F32),
                        pltpu.VMEM((ch // LANES, b * SSM_PITCH, LANES), F32),
                        pltpu.VMEM((SSM_TILE * b, ch), F32),
                        pltpu.VMEM((SSM_TILE * b, 2 * n_state), F32),
                        pltpu.VMEM((ch // LANES, SSM_TILE * b, LANES), F32)],
        compiler_params=_params("arbitrary"),
        name="s5_ssm",
    )(p1, *consts)


def _rope_tables(s, rot_dim, theta):
    inv = float(theta) ** (-np.arange(0, rot_dim, 2, dtype=np.float64) / rot_dim)
    ang = np.arange(s, dtype=np.float64)[:, None] * inv[None, :]
    return np.cos(ang).astype(np.float32), np.sin(ang).astype(np.float32)


def _lane_tables(cos, sin, period):
    s, half = cos.shape
    reps = LANES // period
    one = np.ones((s, period - 2 * half), np.float32)
    zero = np.zeros((s, period - 2 * half), np.float32)
    zh = np.zeros((s, half), np.float32)
    c = np.tile(np.concatenate([cos, cos, one], axis=1), (1, reps))
    sa = np.tile(np.concatenate([-sin, zh, zero], axis=1), (1, reps))
    sb = np.tile(np.concatenate([zh, sin, zero], axis=1), (1, reps))
    return c, sa, sb


def _block_diag(blocks):
    g, r, c = blocks.shape
    eye = jnp.eye(g, dtype=blocks.dtype)
    return (eye[:, None, :, None] * blocks[:, :, None, :]).reshape(g * r, g * c)


def kernel(x, l0_mix_pre, l0_mix_post, l0_w_in, l0_lambda_q1, l0_lambda_k1, l0_lambda_q2, l0_lambda_k2, l0_subln, l0_dw_w, l0_dw_b, l0_conv_ln_g, l0_conv_ln_b, l0_w_out, l0_ffn_pre, l0_ffn_post, l0_w_gate, l0_w_up, l0_w_down, l1_mix_pre, l1_mix_post, l1_w_in, l1_a_re, l1_a_im, l1_log_dt, l1_b_re, l1_b_im, l1_c_re, l1_c_im, l1_d_skip, l1_w_glu, l1_b_glu, l1_q_norm, l1_w_uq, l1_kv_norm, l1_w_ukv, l1_w_out, l1_ffn_pre, l1_ffn_post, l1_w_gate, l1_w_up, l1_w_down):
    b, s, d = x.shape
    n = b * s
    row = lambda v: v.reshape(1, -1).astype(F32)
    bf = lambda w: w.astype(BF16)

    diff_width = 4 * LANES
    n_diff_heads = diff_width // LANES
    conv_ch = l0_dw_w.shape[1]
    ssm_ch = l1_w_glu.shape[0]
    n_groups, n_state_g = l1_a_re.shape
    q_rank = l1_q_norm.shape[0]
    kv_rank = l1_kv_norm.shape[0]
    n_mla_heads = l1_w_uq.shape[1] // (MLA_NOPE + MLA_ROPE)

    cos_a, sin_a = _rope_tables(s, DIFF_ROT, ROPE_THETA)
    tabs_a = _lane_tables(cos_a, sin_a, period=DIFF_HEAD_DIM)
    x2 = x.reshape(n, d)
    p0, qvt_a = _l0_in(x2, row(l0_mix_pre), l0_w_in.astype(F32), tabs_a, (cos_a.T, sin_a.T), b, s, diff_width)
    p0 = p0.reshape(b, s, -1)
    lam_vecs = jnp.stack([l0_lambda_q1, l0_lambda_k1, l0_lambda_q2, l0_lambda_k2]).astype(F32)
    y_a, (w_out0, w_gate0, w_up0, w_down0) = _diff_attn(
        p0, qvt_a, lam_vecs, l0_subln.reshape(-1, 1).astype(F32), n_diff_heads,
        [l0_w_out, l0_w_gate, l0_w_up, l0_w_down])
    y_b = _conv_module(p0, l0_dw_w.astype(F32), row(l0_dw_b), row(l0_conv_ln_g), row(l0_conv_ln_b),
                       col0=diff_width)

    pad = (-l1_w_in.shape[1]) % LANES
    w_in1 = bf(jnp.pad(l1_w_in, ((0, 0), (0, pad))))
    h2, p1 = _post(x2, [y_a.reshape(n, -1), y_b.reshape(n, -1)], w_out0, row(l0_mix_post),
                   row(l0_ffn_pre), w_gate0, w_up0, w_down0, row(l0_ffn_post),
                   nxt=(row(l1_mix_pre), w_in1))

    state_row = lambda a: a.reshape(1, -1).astype(F32)
    ssm_rows = (state_row(l1_a_re), state_row(l1_a_im),
                state_row(jnp.broadcast_to(l1_log_dt[:, None], (n_groups, n_state_g))))
    b_bd = tuple(_block_diag(jnp.swapaxes(m, 1, 2).astype(F32)) for m in (l1_b_re, l1_b_im))
    c_bd = tuple(_block_diag(jnp.swapaxes(m, 1, 2).astype(F32)) for m in (l1_c_re, l1_c_im))
    y_c = _ssm(p1.reshape(b, s, -1), ssm_rows, b_bd, c_bd, row(l1_d_skip), bf(l1_w_glu), row(l1_b_glu))

    wq = l1_w_uq.reshape(q_rank, n_mla_heads, MLA_NOPE + MLA_ROPE)
    wq = jnp.pad(wq, ((0, 0), (0, 0), (0, MLA_QK_PAD - MLA_NOPE - MLA_ROPE))).reshape(q_rank, -1)
    wkv = l1_w_ukv.reshape(kv_rank, n_mla_heads, MLA_NOPE + MLA_V)
    wk_nope = jnp.pad(wkv[:, :, :MLA_NOPE], ((0, 0), (0, 0), (0, MLA_QK_PAD - MLA_NOPE)))
    route = jnp.pad(jnp.eye(MLA_ROPE, dtype=F32), ((0, LANES - MLA_ROPE), (MLA_NOPE, MLA_QK_PAD - MLA_NOPE - MLA_ROPE)))
    wk_rope = jnp.broadcast_to(route[:, None, :], (LANES, n_mla_heads, MLA_QK_PAD))
    wk = jnp.concatenate([wk_nope, wk_rope], axis=0).reshape(kv_rank + LANES, -1)
    wv_t = wkv[:, :, MLA_NOPE:].reshape(kv_rank, -1).T
    cos_d, sin_d = _rope_tables(s, MLA_ROPE, MLA_ROPE_THETA)
    tabs_d = _lane_tables(cos_d, sin_d, period=LANES)
    y_d, (w_out1, w_gate1, w_up1, w_down1) = _mla_attn(
        p1.reshape(b, s, -1), row(l1_q_norm), row(l1_kv_norm), bf(wq.T), bf(wk), bf(wv_t),
        (cos_d.T, sin_d.T), tabs_d, n_mla_heads, [l1_w_out, l1_w_gate, l1_w_up, l1_w_down])

    (out,) = _post(h2, [y_c.reshape(n, -1), y_d.reshape(n, -1)], w_out1, row(l1_mix_post),
                   row(l1_ffn_pre), w_gate1, w_up1, w_down1, row(l1_ffn_post))
    return out.reshape(b, s, d)
```

```python
import functools
import math

import jax
import jax.numpy as jnp
import numpy as np
from jax import lax
from jax.experimental import pallas as pl
from jax.experimental.pallas import tpu as pltpu

F32 = jnp.float32
BF16 = jnp.bfloat16

LANES = 128
SUBLANES = 8
BF16_SUBLANES = 16
VMEM_LIMIT_BYTES = 56 * 1024 * 1024

CHUNK = 64
RMS_EPS = 1e-6
LN_EPS = 1e-5
ROPE_THETA = 500000.0
MLA_ROPE_THETA = 10000.0
DIFF_HEAD_DIM = 64
DIFF_ROT = 16
CONV_WIDTH = 31
CONV_HALO = 32
SSM_GROUP = 16
SSM_STATE = 64
MLA_NOPE = 128
MLA_ROPE = 64
MLA_V = 128
MLA_QK_PAD = 256

ROW_TILE = 512
ATTN_TILE = 256
ATTN_LOOKAHEAD = 5
ATTN_SUM_ROWS = 16
LOG2_E = math.log2(math.e)
CONV_TILE = 256
SSM_TILE = 128
SSM_PITCH = SSM_TILE + SUBLANES
FFN_CHUNK = 512
POST_GROUP_ROWS = 256
POST_LAST_ROW_TILE = 1024


def _params(*sem):
    return pltpu.CompilerParams(dimension_semantics=sem, vmem_limit_bytes=VMEM_LIMIT_BYTES)


def _rms(x, g):
    return x * lax.rsqrt(jnp.mean(x * x, axis=-1, keepdims=True) + RMS_EPS) * g


def _sigmoid(x):
    return 1.0 / (1.0 + jnp.exp(-x))


def _dot(a, b):
    return jnp.dot(a, b, preferred_element_type=F32)


def _dot_nt(a, b):
    return lax.dot_general(a, b, (((1,), (1,)), ((), ())), preferred_element_type=F32)


def _rope_block(x, c, sa, sb, shift):
    return x * c + pltpu.roll(x, LANES - shift, 1) * sa + pltpu.roll(x, shift, 1) * sb


def _const_spec(shape):
    nd = len(shape)
    return pl.BlockSpec(shape, lambda *_: (0,) * nd, pipeline_mode=pl.Buffered(1))


def _vt_out(b, seq, width):
    tiles_per_seq = seq // ROW_TILE
    per_tile = ROW_TILE // ATTN_TILE
    spec = pl.BlockSpec((1, per_tile, width, ATTN_TILE),
                        lambda i: (i // tiles_per_seq, i % tiles_per_seq, 0, 0))
    return spec, jax.ShapeDtypeStruct((b, seq // ATTN_TILE, width, ATTN_TILE), BF16)


def _l0_in_kernel(x_ref, g_ref, win_ref, c_ref, sa_ref, sb_ref, cosq_ref, sinq_ref, o_ref, qvt_ref,
                  w_ref, wt_ref, *, n_k_blocks, q_rows, scale):
    @pl.when(pl.program_id(0) == 0)
    def _regroup_weights():
        w_ref[:, :q_rows] = win_ref[:, q_rows:2 * q_rows].astype(BF16)
        w_ref[:, q_rows:] = win_ref[:, 3 * q_rows:].astype(BF16)
        wt_ref[:q_rows, :] = win_ref[:, :q_rows].T.astype(BF16)
        wt_ref[q_rows:, :] = win_ref[:, 2 * q_rows:3 * q_rows].T.astype(BF16)

    half = DIFF_ROT // 2
    groups = [slice(r0, r0 + ATTN_TILE) for r0 in range(0, x_ref.shape[0], ATTN_TILE)]
    ts = [_rms(x_ref[rows, :], g_ref[...]).astype(BF16) for rows in groups]
    ps = [_dot(t, w_ref[...]) for t in ts]
    qv_ts = [_dot_nt(wt_ref[...], t) for t in ts]
    for j, (rows, p, qv_t) in enumerate(zip(groups, ps, qv_ts)):
        c, sa, sb = c_ref[rows, :], sa_ref[rows, :], sb_ref[rows, :]
        for blk_i in range(n_k_blocks):
            blk = _rope_block(p[:, blk_i * LANES:(blk_i + 1) * LANES], c, sa, sb, half)
            o_ref[rows, blk_i * LANES:(blk_i + 1) * LANES] = blk.astype(o_ref.dtype)
        rest = n_k_blocks * LANES
        o_ref[rows, rest:] = p[:, rest:].astype(o_ref.dtype)
        cos_t, sin_t = cosq_ref[:, rows], sinq_ref[:, rows]
        pieces = []
        for lo in range(0, q_rows, DIFF_HEAD_DIM):
            x1, x2 = qv_t[lo:lo + half, :], qv_t[lo + half:lo + 2 * half, :]
            pieces += [(x1 * cos_t - x2 * sin_t) * scale, (x2 * cos_t + x1 * sin_t) * scale,
                       qv_t[lo + 2 * half:lo + DIFF_HEAD_DIM, :] * scale]
        qvt_ref[0, j] = jnp.concatenate(pieces + [qv_t[q_rows:, :]], axis=0).astype(qvt_ref.dtype)


def _l0_in(x2, g_pre, w_in, tabs, q_tabs_t, b, seq, q_rows):
    n, d = x2.shape
    n_out = w_in.shape[1] - 2 * q_rows
    tiles_per_seq = seq // ROW_TILE
    kern = functools.partial(_l0_in_kernel, n_k_blocks=q_rows // LANES, q_rows=q_rows,
                             scale=DIFF_HEAD_DIM ** -0.5 * LOG2_E)
    tab_spec = pl.BlockSpec((ROW_TILE, LANES), lambda i: (i % tiles_per_seq, 0))
    qtab_spec = pl.BlockSpec((q_tabs_t[0].shape[0], ROW_TILE), lambda i: (0, i % tiles_per_seq))
    vt_spec, vt_shape = _vt_out(b, seq, 2 * q_rows)
    return pl.pallas_call(
        kern,
        grid=(n // ROW_TILE,),
        in_specs=[pl.BlockSpec((ROW_TILE, d), lambda i: (i, 0)),
                  _const_spec((1, d)), _const_spec(w_in.shape),
                  tab_spec, tab_spec, tab_spec, qtab_spec, qtab_spec],
        out_specs=[pl.BlockSpec((ROW_TILE, n_out), lambda i: (i, 0)), vt_spec],
        out_shape=[jax.ShapeDtypeStruct((n, n_out), BF16), vt_shape],
        scratch_shapes=[pltpu.VMEM((d, n_out), BF16), pltpu.VMEM((2 * q_rows, d), BF16)],
        compiler_params=_params("arbitrary"),
        name="l0_in",
    )(x2, g_pre, w_in, *tabs, *q_tabs_t)


def _attn_flat(q_heads_at, k_at, vt_at, finish_tile, m_ref, acc_ref, *, n_tiles, n_maps, tq, q_transposed):
    dk_axis = 0 if q_transposed else 1
    qs_cache = {}

    def qs_of(i):
        if i not in qs_cache:
            qs = []
            for q in q_heads_at(i):
                if n_maps == 1:
                    qs.append(q)
                else:
                    width = q.shape[dk_axis] // n_maps
                    pos = lax.broadcasted_iota(jnp.int32, q.shape, dk_axis)
                    for c in range(n_maps):
                        qs.append(jnp.where((pos >= c * width) & (pos < (c + 1) * width), q, jnp.zeros_like(q)))
            qs_cache[i] = qs
        return qs_cache[i]

    n_chains = len(qs_of(0))
    ones = jnp.ones((ATTN_SUM_ROWS, tq), BF16)
    kc = lax.broadcasted_iota(jnp.int32, (tq, tq), 0) // CHUNK
    qc = lax.broadcasted_iota(jnp.int32, (tq, tq), 1) // CHUNK
    keep = kc <= qc
    items = [(i, kb, ch) for i in range(n_tiles) for kb in range(i + 1) for ch in range(n_chains)]

    def scores(i, kb, ch):
        k = k_at(kb, ch // n_maps)
        return _dot(k, qs_of(i)[ch]) if q_transposed else _dot_nt(k, qs_of(i)[ch])

    ss = {n: scores(*items[n]) for n in range(min(ATTN_LOOKAHEAD, len(items)))}
    for n, (i, kb, ch) in enumerate(items):
        if n + ATTN_LOOKAHEAD < len(items):
            ss[n + ATTN_LOOKAHEAD] = scores(*items[n + ATTN_LOOKAHEAD])
        s = ss.pop(n)
        if kb == i:
            s = jnp.where(keep, s, -jnp.inf)
        vt = jnp.concatenate([vt_at(kb, ch // n_maps), ones], axis=0)
        slot = i % 2
        if kb == 0:
            m_new = jnp.max(s, axis=0, keepdims=True)
            acc_new = _dot(vt, jnp.exp2(s - m_new).astype(vt.dtype))
        else:
            m_old = m_ref[slot, ch]
            m_new = jnp.maximum(m_old, jnp.max(s, axis=0, keepdims=True))
            acc_new = (jnp.exp2(m_old - m_new) * acc_ref[slot, ch]
                       + _dot(vt, jnp.exp2(s - m_new).astype(vt.dtype)))
        m_ref[slot, ch] = m_new
        acc_ref[slot, ch] = acc_new
        if kb == i and ch == n_chains - 1:
            finish_tile(i, slot)
            del qs_cache[i]


def _attn_flat_scratch(n_chains, dv):
    return [pltpu.VMEM((2, n_chains, 1, ATTN_TILE), F32),
            pltpu.VMEM((2, n_chains, dv + ATTN_SUM_ROWS, ATTN_TILE), F32)]


def _attn_out(acc_ref, ch, dv):
    return acc_ref[ch, :dv, :] / acc_ref[ch, dv:dv + 1, :]


def _cast_specs(weights, n_steps):
    specs = []
    for w in weights:
        rows = w.shape[0] // n_steps
        assert rows * n_steps == w.shape[0] and rows % BF16_SUBLANES == 0, w.shape
        specs.append(pl.BlockSpec((rows, w.shape[1]), lambda bb: (bb, 0)))
    return specs, [jax.ShapeDtypeStruct(w.shape, BF16) for w in weights]


def _cast_blocks(src_refs, dst_refs):
    for src, dst in zip(src_refs, dst_refs):
        dst[...] = src[...].astype(dst.dtype)


def _split_cast_refs(refs, n_in, n_out, n_cast):
    main_in, cast_src = refs[:n_in], refs[n_in:n_in + n_cast]
    rest = refs[n_in + n_cast:]
    return main_in, cast_src, rest[:n_out], rest[n_out:n_out + n_cast], rest[n_out + n_cast:]


def _diff_attn_kernel(*refs, n_heads, lam_init, n_cast):
    (qt_ref, k_ref, vt_ref, lam_ref, sub_ref), cast_src, (o_ref,), cast_dst, (m_ref, acc_ref) = (
        _split_cast_refs(refs, 5, 1, n_cast))
    _cast_blocks(cast_src, cast_dst)
    n_tiles, tq = qt_ref.shape[1], qt_ref.shape[3]
    lv = lam_ref[...]
    lam = (jnp.exp(jnp.sum(lv[0:1] * lv[1:2], axis=-1, keepdims=True))
           - jnp.exp(jnp.sum(lv[2:3] * lv[3:4], axis=-1, keepdims=True)) + lam_init)

    def finish_tile(i, slot):
        for h in range(n_heads):
            o_t = _attn_out(acc_ref.at[slot], 2 * h, LANES) - lam * _attn_out(acc_ref.at[slot], 2 * h + 1, LANES)
            inv = lax.rsqrt(jnp.mean(o_t * o_t, axis=0, keepdims=True) + RMS_EPS)
            y_t = o_t * inv * sub_ref[...] * (1.0 - lam_init)
            o_ref[0, i * tq:(i + 1) * tq, h * LANES:(h + 1) * LANES] = y_t.T.astype(o_ref.dtype)

    _attn_flat(lambda i: [qt_ref[0, i, h * LANES:(h + 1) * LANES, :] for h in range(n_heads)],
               lambda kb, h: k_ref[0, kb * tq:(kb + 1) * tq, h * LANES:(h + 1) * LANES],
               lambda kb, h: vt_ref[0, kb, h * LANES:(h + 1) * LANES, :],
               finish_tile, m_ref, acc_ref, n_tiles=n_tiles, n_maps=2, tq=tq, q_transposed=True)


def _diff_attn(p0, qvt, lam_vecs, subln_col, n_heads, cast_weights):
    b, s, _ = p0.shape
    n_kb = s // ATTN_TILE
    width = n_heads * LANES
    cast_specs, cast_shapes = _cast_specs(cast_weights, b)
    kern = functools.partial(_diff_attn_kernel, n_heads=n_heads, lam_init=0.8 - 0.6 * math.exp(-0.3 * 0),
                             n_cast=len(cast_weights))
    out = pl.pallas_call(
        kern,
        grid=(b,),
        in_specs=[pl.BlockSpec((1, n_kb, width, ATTN_TILE), lambda bb: (bb, 0, 0, 0)),
                  pl.BlockSpec((1, s, width), lambda bb: (bb, 0, 0)),
                  pl.BlockSpec((1, n_kb, width, ATTN_TILE), lambda bb: (bb, 0, 1, 0)),
                  pl.BlockSpec(lam_vecs.shape, lambda bb: (0, 0)),
                  pl.BlockSpec(subln_col.shape, lambda bb: (0, 0))] + cast_specs,
        out_specs=[pl.BlockSpec((1, s, width), lambda bb: (bb, 0, 0))] + cast_specs,
        out_shape=[jax.ShapeDtypeStruct((b, s, width), BF16)] + cast_shapes,
        scratch_shapes=_attn_flat_scratch(2 * n_heads, LANES),
        compiler_params=_params("arbitrary"),
        name="diff_attn",
    )(qvt, p0, qvt, lam_vecs, subln_col, *cast_weights)
    return out[0], out[1:]


def _mla_attn_kernel(*refs, n_heads, scale, n_cast):
    ((cq_ref, kv_ref, qn_ref, kvn_ref, wqt_ref, wk_ref, wvt_ref, cosq_tab, sinq_tab, ck_tab, sak_tab, sbk_tab),
     cast_src, (o_ref,), cast_dst, (m_ref, acc_ref, k_scr, vt_scr)) = (
        _split_cast_refs(refs, 12, 1, n_cast))
    _cast_blocks(cast_src, cast_dst)
    seq = kv_ref.shape[1]
    tq = ATTN_TILE
    half = MLA_ROPE // 2
    projected = set()

    def project_keys(chunk):
        if chunk in projected:
            return
        projected.add(chunk)
        rows = slice(chunk * ROW_TILE, (chunk + 1) * ROW_TILE)
        ckv = _rms(kv_ref[0, rows, :LANES], kvn_ref[...]).astype(BF16)
        kr = _rope_block(kv_ref[0, rows, LANES:], ck_tab[rows, :], sak_tab[rows, :], sbk_tab[rows, :], half)
        k_scr[rows, :] = _dot(jnp.concatenate([ckv, kr.astype(BF16)], axis=-1), wk_ref[...]).astype(BF16)
        vt = _dot_nt(wvt_ref[...], ckv)
        for j in range(ROW_TILE // tq):
            vt_scr[chunk * (ROW_TILE // tq) + j] = vt[:, j * tq:(j + 1) * tq].astype(BF16)

    def q_heads_at(i):
        cols = slice(i * tq, (i + 1) * tq)
        q_t = _dot_nt(wqt_ref[...], _rms(cq_ref[0, cols, :], qn_ref[...]).astype(BF16))
        cos_t, sin_t = cosq_tab[:, cols], sinq_tab[:, cols]
        q_heads = []
        for h in range(n_heads):
            lo = h * MLA_QK_PAD
            x1 = q_t[lo + MLA_NOPE:lo + MLA_NOPE + half, :]
            x2 = q_t[lo + MLA_NOPE + half:lo + MLA_NOPE + 2 * half, :]
            q_h = jnp.concatenate([q_t[lo:lo + MLA_NOPE, :], x1 * cos_t - x2 * sin_t, x2 * cos_t + x1 * sin_t,
                                   q_t[lo + MLA_NOPE + 2 * half:lo + MLA_QK_PAD, :]], axis=0)
            q_heads.append((q_h * scale).astype(BF16))
        return q_heads

    def k_at(kb, h):
        project_keys(kb * tq // ROW_TILE)
        return k_scr[kb * tq:(kb + 1) * tq, h * MLA_QK_PAD:(h + 1) * MLA_QK_PAD]

    def finish_tile(i, slot):
        for h in range(n_heads):
            o_ref[0, i * tq:(i + 1) * tq, h * MLA_V:(h + 1) * MLA_V] = (
                _attn_out(acc_ref.at[slot], h, MLA_V).T.astype(o_ref.dtype))

    _attn_flat(q_heads_at, k_at, lambda kb, h: vt_scr[kb, h * MLA_V:(h + 1) * MLA_V, :],
               finish_tile, m_ref, acc_ref, n_tiles=seq // tq, n_maps=1, tq=tq, q_transposed=True)


def _mla_attn(p1, q_norm, kv_norm, wq_t, wk, wv_t, q_tabs_t, tabs, n_heads, cast_weights):
    b, s, _ = p1.shape
    n_kb = s // ATTN_TILE
    cast_specs, cast_shapes = _cast_specs(cast_weights, b)
    kern = functools.partial(_mla_attn_kernel, n_heads=n_heads, scale=(MLA_NOPE + MLA_ROPE) ** -0.5 * LOG2_E,
                             n_cast=len(cast_weights))
    const = lambda a: pl.BlockSpec(a.shape, lambda bb: (0,) * a.ndim)
    out = pl.pallas_call(
        kern,
        grid=(b,),
        in_specs=[pl.BlockSpec((1, s, 2 * LANES), lambda bb: (bb, 0, 1)),
                  pl.BlockSpec((1, s, 2 * LANES), lambda bb: (bb, 0, 2)),
                  const(q_norm), const(kv_norm), const(wq_t), const(wk), const(wv_t),
                  const(q_tabs_t[0]), const(q_tabs_t[1]),
                  const(tabs[0]), const(tabs[1]), const(tabs[2])] + cast_specs,
        out_specs=[pl.BlockSpec((1, s, n_heads * MLA_V), lambda bb: (bb, 0, 0))] + cast_specs,
        out_shape=[jax.ShapeDtypeStruct((b, s, n_heads * MLA_V), BF16)] + cast_shapes,
        scratch_shapes=_attn_flat_scratch(n_heads, MLA_V) + [
            pltpu.VMEM((s, n_heads * MLA_QK_PAD), BF16),
            pltpu.VMEM((n_kb, n_heads * MLA_V, ATTN_TILE), BF16)],
        compiler_params=_params("arbitrary"),
        name="mla_attn",
    )(p1, p1, q_norm, kv_norm, wq_t, wk, wv_t, *q_tabs_t, *tabs, *cast_weights)
    return out[0], out[1:]


def _conv_kernel(a_ref, gate_ref, ah_ref, gh_ref, w_ref, b_ref, lg_ref, lb_ref, o_ref, u_ref, ur_ref):
    tt = a_ref.shape[1]
    u_ref[CONV_HALO:, :] = a_ref[0].astype(F32) * _sigmoid(gate_ref[0].astype(F32))
    halo = ah_ref[0].astype(F32) * _sigmoid(gh_ref[0].astype(F32))
    u_ref[:CONV_HALO, :] = jnp.where(pl.program_id(1) > 0, halo, jnp.zeros_like(halo))
    rows = ur_ref.shape[1]
    for r in range(1, SUBLANES):
        ur_ref[r - 1] = u_ref[r:r + rows, :]
    acc = jnp.zeros((tt, a_ref.shape[2]), F32)
    first = CONV_HALO - (CONV_WIDTH - 1)
    for k in range(CONV_WIDTH):
        base, r = divmod(first + k, SUBLANES)
        src = u_ref if r == 0 else ur_ref.at[r - 1]
        acc = acc + src[base * SUBLANES:base * SUBLANES + tt, :] * w_ref[k:k + 1, :]
    y = acc + b_ref[...]
    mu = jnp.mean(y, axis=-1, keepdims=True)
    yc = y - mu
    yn = yc * lax.rsqrt(jnp.mean(yc * yc, axis=-1, keepdims=True) + LN_EPS) * lg_ref[...] + lb_ref[...]
    o_ref[0] = (yn * _sigmoid(yn)).astype(o_ref.dtype)


def _conv_module(p0, dw_w, dw_b, ln_g, ln_b, col0):
    b, s, _ = p0.shape
    c = dw_w.shape[1]
    a_blk, g_blk = col0 // c, col0 // c + 1
    ratio = CONV_TILE // CONV_HALO
    main = lambda blk: pl.BlockSpec((1, CONV_TILE, c), lambda bb, t: (bb, t, blk))
    halo = lambda blk: pl.BlockSpec((1, CONV_HALO, c), lambda bb, t: (bb, jnp.maximum(t * ratio - 1, 0), blk))
    vec = lambda a: pl.BlockSpec(a.shape, lambda bb, t: (0, 0))
    return pl.pallas_call(
        _conv_kernel,
        grid=(b, s // CONV_TILE),
        in_specs=[main(a_blk), main(g_blk), halo(a_blk), halo(g_blk),
                  vec(dw_w), vec(dw_b), vec(ln_g), vec(ln_b)],
        out_specs=pl.BlockSpec((1, CONV_TILE, c), lambda bb, t: (bb, t, 0)),
        out_shape=jax.ShapeDtypeStruct((b, s, c), BF16),
        scratch_shapes=[pltpu.VMEM((CONV_HALO + CONV_TILE, c), F32),
                        pltpu.VMEM((SUBLANES - 1, CONV_HALO + CONV_TILE - SUBLANES, c), F32)],
        compiler_params=_params("parallel", "parallel"),
        name="conv_module",
    )(p0, p0, p0, p0, dw_w, dw_b, ln_g, ln_b)


def _post_kernel(*refs, n_mix, with_next):
    x_ref = refs[0]
    mix_refs = refs[1:1 + n_mix]
    (wo_ref, gpost_ref, gfpre_ref, wg_ref, wu_ref, wd_ref, gfpost_ref) = refs[1 + n_mix:8 + n_mix]
    pos = 8 + n_mix
    if with_next:
        gnext_ref, wnext_ref = refs[pos:pos + 2]
        pos += 2
    h_out_ref = refs[pos]
    pos += 1
    if with_next:
        p_out_ref = refs[pos]
        pos += 1
    hid_ref = refs[pos]

    n_rows = x_ref.shape[0]
    groups = [slice(r0, r0 + POST_GROUP_ROWS) for r0 in range(0, n_rows, POST_GROUP_ROWS)]

    def out_proj(rows):
        y = None
        row = 0
        for r in mix_refs:
            w = r.shape[-1]
            part = _dot(r[rows, :], wo_ref[row:row + w, :])
            y = part if y is None else y + part
            row += w
        return y

    ys = [out_proj(rows) for rows in groups]
    h1s = [x_ref[rows, :] + _rms(y, gpost_ref[...]) for rows, y in zip(groups, ys)]
    ts = [_rms(h1, gfpre_ref[...]).astype(BF16) for h1 in h1s]
    d_ff = wg_ref.shape[1]
    for j in range(0, d_ff, FFN_CHUNK):
        wdt = min(FFN_CHUNK, d_ff - j)
        for rows, t in zip(groups, ts):
            gate = _dot(t, wg_ref[:, j:j + wdt])
            up = _dot(t, wu_ref[:, j:j + wdt])
            hid_ref[rows, j:j + wdt] = (gate * _sigmoid(gate) * up).astype(BF16)
    fs = [_dot(hid_ref[rows, :], wd_ref[...]) for rows in groups]
    h2s = [h1 + _rms(f, gfpost_ref[...]) for h1, f in zip(h1s, fs)]
    for rows, h2 in zip(groups, h2s):
        h_out_ref[rows, :] = h2
    if with_next:
        t2s = [_rms(h2, gnext_ref[...]).astype(BF16) for h2 in h2s]
        for rows, t2 in zip(groups, t2s):
            p_out_ref[rows, :] = _dot(t2, wnext_ref[...])


def _post(x2, mix_parts, w_out, g_post, g_fpre, w_gate, w_up, w_down, g_fpost, nxt=None):
    n, d = x2.shape
    d_ff = w_gate.shape[1]
    row_tile = ROW_TILE if nxt else POST_LAST_ROW_TILE
    row = lambda a: pl.BlockSpec((row_tile, a.shape[1]), lambda i: (i, 0))
    consts = [w_out, g_post, g_fpre, w_gate, w_up, w_down, g_fpost] + (list(nxt) if nxt else [])
    out_shape = [jax.ShapeDtypeStruct((n, d), F32)]
    out_specs = [pl.BlockSpec((row_tile, d), lambda i: (i, 0))]
    if nxt:
        n_next = nxt[1].shape[1]
        out_shape.append(jax.ShapeDtypeStruct((n, n_next), F32))
        out_specs.append(pl.BlockSpec((row_tile, n_next), lambda i: (i, 0)))
    kern = functools.partial(_post_kernel, n_mix=len(mix_parts), with_next=bool(nxt))
    return pl.pallas_call(
        kern,
        grid=(n // row_tile,),
        in_specs=[row(x2)] + [row(m) for m in mix_parts] + [_const_spec(c.shape) for c in consts],
        out_specs=out_specs,
        out_shape=out_shape,
        scratch_shapes=[pltpu.VMEM((row_tile, d_ff), BF16)],
        compiler_params=_params("parallel"),
        name="post_next" if nxt else "post",
    )(x2, *mix_parts, *consts)


def _gelu_tanh(x):
    return 0.5 * x * (1.0 + jnp.tanh(math.sqrt(2.0 / math.pi) * (x + 0.044715 * (x * x * x))))


def _ssm_kernel(u_ref, lr_ref, li_ref, ldt_ref, bre_ref, bim_ref, cre_ref, cim_ref, d_ref, wg_ref, bg_ref,
                o_ref, bmat_ref, cmat_ref, a_ref, st_ref, us_ref, utm_ref, x_ref, y_ref):
    nb, tt, ch = u_ref.shape
    n_state = lr_ref.shape[1]

    @pl.when(pl.program_id(0) == 0)
    def _init():
        lr, li = lr_ref[...], li_ref[...]
        dt = jnp.exp(ldt_ref[...])
        mag = jnp.exp(lr * dt)
        ab_re = mag * jnp.cos(li * dt)
        ab_im = mag * jnp.sin(li * dt)
        den = lr * lr + li * li
        n_re = ab_re - 1.0
        f_re = (n_re * lr + ab_im * li) / den
        f_im = (ab_im * lr - n_re * li) / den
        br, bi = bre_ref[...], bim_ref[...]
        bmat_ref[:, :n_state] = (f_re * br - f_im * bi).astype(BF16)
        bmat_ref[:, n_state:] = (f_re * bi + f_im * br).astype(BF16)
        cmat_ref[:n_state, :] = cre_ref[...].astype(BF16)
        cmat_ref[n_state:, :] = (-cim_ref[...]).astype(BF16)
        a_ref[0:1, :] = ab_re
        a_ref[1:2, :] = ab_im
        st_ref[...] = jnp.zeros(st_ref.shape, F32)

    n_ublk = ch // LANES
    for b in range(nb):
        for j in range(n_ublk):
            us_ref[j, b * SSM_PITCH:b * SSM_PITCH + tt, :] = u_ref[b, :, j * LANES:(j + 1) * LANES]

    def gather_step(t, carry):
        dst = pl.multiple_of(t * nb, nb)
        for j in range(n_ublk):
            utm_ref[pl.ds(dst, nb), j * LANES:(j + 1) * LANES] = us_ref[j, pl.ds(t, nb, stride=SSM_PITCH), :]
        return carry

    lax.fori_loop(0, tt, gather_step, 0, unroll=8)

    half_rows = tt * nb // 2
    for r0 in (0, half_rows):
        x_ref[r0:r0 + half_rows, :] = _dot(utm_ref[r0:r0 + half_rows, :].astype(BF16), bmat_ref[...])

    a_re = jnp.broadcast_to(a_ref[0:1, :], (nb, n_state))
    a_im = jnp.broadcast_to(a_ref[1:2, :], (nb, n_state))

    def scan_step(t, carry):
        x_re, x_im = carry
        row = pl.multiple_of(t * nb, nb)
        n_re = a_re * x_re - a_im * x_im + x_ref[pl.ds(row, nb), :n_state]
        n_im = a_re * x_im + a_im * x_re + x_ref[pl.ds(row, nb), n_state:]
        x_ref[pl.ds(row, nb), :n_state] = n_re
        x_ref[pl.ds(row, nb), n_state:] = n_im
        return n_re, n_im

    x_re, x_im = lax.fori_loop(0, tt, scan_step, (st_ref[:, :n_state], st_ref[:, n_state:]), unroll=4)
    st_ref[:, :n_state] = x_re
    st_ref[:, n_state:] = x_im

    n_yblk = ch // LANES
    for r0 in (0, half_rows):
        y_tm = _dot(x_ref[r0:r0 + half_rows, :].astype(BF16), cmat_ref[...])
        for j in range(n_yblk):
            y_ref[j, r0:r0 + half_rows, :] = y_tm[:, j * LANES:(j + 1) * LANES]
    for b in range(nb):
        y = jnp.concatenate([y_ref[j, pl.ds(b, tt, stride=nb), :] for j in range(n_yblk)], axis=-1)
        y = y + d_ref[...] * u_ref[b]
        z = _gelu_tanh(y)
        gate = _dot(z.astype(BF16), wg_ref[...]) + bg_ref[...]
        o_ref[b] = (z * _sigmoid(gate)).astype(o_ref.dtype)


def _ssm(p1, rows, b_bd, c_bd, d_row, w_glu, b_glu):
    b, s, _ = p1.shape
    ch = w_glu.shape[0]
    n_state = rows[0].shape[1]
    consts = list(rows) + list(b_bd) + list(c_bd) + [d_row, w_glu, b_glu]
    return pl.pallas_call(
        _ssm_kernel,
        grid=(s // SSM_TILE,),
        in_specs=[pl.BlockSpec((b, SSM_TILE, ch), lambda t: (0, t, 0))] + [_const_spec(c.shape) for c in consts],
        out_specs=pl.BlockSpec((b, SSM_TILE, ch), lambda t: (0, t, 0)),
        out_shape=jax.ShapeDtypeStruct((b, s, ch), BF16),
        scratch_shapes=[pltpu.VMEM((ch, 2 * n_state), BF16),
                        pltpu.VMEM((2 * n_state, ch), BF16),
                        pltpu.VMEM((SUBLANES, n_state), F32),
                        pltpu.VMEM((b, 2 * n_state), F32),
                        pltpu.VMEM((ch // LANES, b * SSM_PITCH, LANES), F32),
                        pltpu.VMEM((SSM_TILE * b, ch), F32),
                        pltpu.VMEM((SSM_TILE * b, 2 * n_state), F32),
                        pltpu.VMEM((ch // LANES, SSM_TILE * b, LANES), F32)],
        compiler_params=_params("arbitrary"),
        name="s5_ssm",
    )(p1, *consts)


def _rope_tables(s, rot_dim, theta):
    inv = float(theta) ** (-np.arange(0, rot_dim, 2, dtype=np.float64) / rot_dim)
    ang = np.arange(s, dtype=np.float64)[:, None] * inv[None, :]
    return np.cos(ang).astype(np.float32), np.sin(ang).astype(np.float32)


def _lane_tables(cos, sin, period):
    s, half = cos.shape
    reps = LANES // period
    one = np.ones((s, period - 2 * half), np.float32)
    zero = np.zeros((s, period - 2 * half), np.float32)
    zh = np.zeros((s, half), np.float32)
    c = np.tile(np.concatenate([cos, cos, one], axis=1), (1, reps))
    sa = np.tile(np.concatenate([-sin, zh, zero], axis=1), (1, reps))
    sb = np.tile(np.concatenate([zh, sin, zero], axis=1), (1, reps))
    return c, sa, sb


def _block_diag(blocks):
    g, r, c = blocks.shape
    eye = jnp.eye(g, dtype=blocks.dtype)
    return (eye[:, None, :, None] * blocks[:, :, None, :]).reshape(g * r, g * c)


def kernel(x, l0_mix_pre, l0_mix_post, l0_w_in, l0_lambda_q1, l0_lambda_k1, l0_lambda_q2, l0_lambda_k2, l0_subln, l0_dw_w, l0_dw_b, l0_conv_ln_g, l0_conv_ln_b, l0_w_out, l0_ffn_pre, l0_ffn_post, l0_w_gate, l0_w_up, l0_w_down, l1_mix_pre, l1_mix_post, l1_w_in, l1_a_re, l1_a_im, l1_log_dt, l1_b_re, l1_b_im, l1_c_re, l1_c_im, l1_d_skip, l1_w_glu, l1_b_glu, l1_q_norm, l1_w_uq, l1_kv_norm, l1_w_ukv, l1_w_out, l1_ffn_pre, l1_ffn_post, l1_w_gate, l1_w_up, l1_w_down):
    b, s, d = x.shape
    n = b * s
    row = lambda v: v.reshape(1, -1).astype(F32)
    bf = lambda w: w.astype(BF16)

    diff_width = 4 * LANES
    n_diff_heads = diff_width // LANES
    conv_ch = l0_dw_w.shape[1]
    ssm_ch = l1_w_glu.shape[0]
    n_groups, n_state_g = l1_a_re.shape
    q_rank = l1_q_norm.shape[0]
    kv_rank = l1_kv_norm.shape[0]
    n_mla_heads = l1_w_uq.shape[1] // (MLA_NOPE + MLA_ROPE)

    cos_a, sin_a = _rope_tables(s, DIFF_ROT, ROPE_THETA)
    tabs_a = _lane_tables(cos_a, sin_a, period=DIFF_HEAD_DIM)
    x2 = x.reshape(n, d)
    p0, qvt_a = _l0_in(x2, row(l0_mix_pre), l0_w_in.astype(F32), tabs_a, (cos_a.T, sin_a.T), b, s, diff_width)
    p0 = p0.reshape(b, s, -1)
    lam_vecs = jnp.stack([l0_lambda_q1, l0_lambda_k1, l0_lambda_q2, l0_lambda_k2]).astype(F32)
    y_a, (w_out0, w_gate0, w_up0, w_down0) = _diff_attn(
        p0, qvt_a, lam_vecs, l0_subln.reshape(-1, 1).astype(F32), n_diff_heads,
        [l0_w_out, l0_w_gate, l0_w_up, l0_w_down])
    y_b = _conv_module(p0, l0_dw_w.astype(F32), row(l0_dw_b), row(l0_conv_ln_g), row(l0_conv_ln_b),
                       col0=diff_width)

    pad = (-l1_w_in.shape[1]) % LANES
    w_in1 = bf(jnp.pad(l1_w_in, ((0, 0), (0, pad))))
    h2, p1 = _post(x2, [y_a.reshape(n, -1), y_b.reshape(n, -1)], w_out0, row(l0_mix_post),
                   row(l0_ffn_pre), w_gate0, w_up0, w_down0, row(l0_ffn_post),
                   nxt=(row(l1_mix_pre), w_in1))

    state_row = lambda a: a.reshape(1, -1).astype(F32)
    ssm_rows = (state_row(l1_a_re), state_row(l1_a_im),
                state_row(jnp.broadcast_to(l1_log_dt[:, None], (n_groups, n_state_g))))
    b_bd = tuple(_block_diag(jnp.swapaxes(m, 1, 2).astype(F32)) for m in (l1_b_re, l1_b_im))
    c_bd = tuple(_block_diag(jnp.swapaxes(m, 1, 2).astype(F32)) for m in (l1_c_re, l1_c_im))
    y_c = _ssm(p1.reshape(b, s, -1), ssm_rows, b_bd, c_bd, row(l1_d_skip), bf(l1_w_glu), row(l1_b_glu))

    wq = l1_w_uq.reshape(q_rank, n_mla_heads, MLA_NOPE + MLA_ROPE)
    wq = jnp.pad(wq, ((0, 0), (0, 0), (0, MLA_QK_PAD - MLA_NOPE - MLA_ROPE))).reshape(q_rank, -1)
    wkv = l1_w_ukv.reshape(kv_rank, n_mla_heads, MLA_NOPE + MLA_V)
    wk_nope = jnp.pad(wkv[:, :, :MLA_NOPE], ((0, 0), (0, 0), (0, MLA_QK_PAD - MLA_NOPE)))
    route = jnp.pad(jnp.eye(MLA_ROPE, dtype=F32), ((0, LANES - MLA_ROPE), (MLA_NOPE, MLA_QK_PAD - MLA_NOPE - MLA_ROPE)))
    wk_rope = jnp.broadcast_to(route[:, None, :], (LANES, n_mla_heads, MLA_QK_PAD))
    wk = jnp.concatenate([wk_nope, wk_rope], axis=0).reshape(kv_rank + LANES, -1)
    wv_t = wkv[:, :, MLA_NOPE:].reshape(kv_rank, -1).T
    cos_d, sin_d = _rope_tables(s, MLA_ROPE, MLA_ROPE_THETA)
    tabs_d = _lane_tables(cos_d, sin_d, period=LANES)
    y_d, (w_out1, w_gate1, w_up1, w_down1) = _mla_attn(
        p1.reshape(b, s, -1), row(l1_q_norm), row(l1_kv_norm), bf(wq.T), bf(wk), bf(wv_t),
        (cos_d.T, sin_d.T), tabs_d, n_mla_heads, [l1_w_out, l1_w_gate, l1_w_up, l1_w_down])

    (out,) = _post(h2, [y_c.reshape(n, -1), y_d.reshape(n, -1)], w_out1, row(l1_mix_post),
                   row(l1_ffn_pre), w_gate1, w_up1, w_down1, row(l1_ffn_post))
    return out.reshape(b, s, d)
```

```python
import functools
import math

import jax
import jax.numpy as jnp
import numpy as np
from jax import lax
from jax.experimental import pallas as pl
from jax.experimental.pallas import tpu as pltpu

F32 = jnp.float32
BF16 = jnp.bfloat16

LANES = 128
SUBLANES = 8
BF16_SUBLANES = 16
VMEM_LIMIT_BYTES = 60 * 1024 * 1024

CHUNK = 64
RMS_EPS = 1e-6
LN_EPS = 1e-5
ROPE_THETA = 500000.0
MLA_ROPE_THETA = 10000.0
DIFF_HEAD_DIM = 64
DIFF_ROT = 16
CONV_WIDTH = 31
CONV_HALO = 32
SSM_GROUP = 16
SSM_STATE = 64
MLA_NOPE = 128
MLA_ROPE = 64
MLA_V = 128
MLA_QK_PAD = 256

ROW_TILE = 512
ATTN_TILE = 256
ATTN_LOOKAHEAD = 5
ATTN_SUM_ROWS = 16
LOG2_E = math.log2(math.e)
CONV_TILE = 256
SSM_TILE = 128
SSM_PITCH = SSM_TILE + SUBLANES
FFN_CHUNK = 512
POST_GROUP_ROWS = 256
POST_ROW_TILE = 1024


def _params(*sem):
    return pltpu.CompilerParams(dimension_semantics=sem, vmem_limit_bytes=VMEM_LIMIT_BYTES)


def _rms(x, g):
    return x * lax.rsqrt(jnp.mean(x * x, axis=-1, keepdims=True) + RMS_EPS) * g


def _sigmoid(x):
    return 1.0 / (1.0 + jnp.exp(-x))


def _dot(a, b):
    return jnp.dot(a, b, preferred_element_type=F32)


def _dot_nt(a, b):
    return lax.dot_general(a, b, (((1,), (1,)), ((), ())), preferred_element_type=F32)


def _rope_block(x, c, sa, sb, shift):
    return x * c + pltpu.roll(x, LANES - shift, 1) * sa + pltpu.roll(x, shift, 1) * sb


def _const_spec(shape):
    nd = len(shape)
    return pl.BlockSpec(shape, lambda *_: (0,) * nd, pipeline_mode=pl.Buffered(1))


def _vt_out(b, seq, width):
    tiles_per_seq = seq // ROW_TILE
    per_tile = ROW_TILE // ATTN_TILE
    spec = pl.BlockSpec((1, per_tile, width, ATTN_TILE),
                        lambda i: (i // tiles_per_seq, i % tiles_per_seq, 0, 0))
    return spec, jax.ShapeDtypeStruct((b, seq // ATTN_TILE, width, ATTN_TILE), BF16)


def _l0_in_kernel(x_ref, g_ref, win_ref, c_ref, sa_ref, sb_ref, cosq_ref, sinq_ref, o_ref, qvt_ref,
                  w_ref, wt_ref, *, n_k_blocks, q_rows, scale):
    @pl.when(pl.program_id(0) == 0)
    def _regroup_weights():
        w_ref[:, :q_rows] = win_ref[:, q_rows:2 * q_rows].astype(BF16)
        w_ref[:, q_rows:] = win_ref[:, 3 * q_rows:].astype(BF16)
        wt_ref[:q_rows, :] = win_ref[:, :q_rows].T.astype(BF16)
        wt_ref[q_rows:, :] = win_ref[:, 2 * q_rows:3 * q_rows].T.astype(BF16)

    half = DIFF_ROT // 2
    groups = [slice(r0, r0 + ATTN_TILE) for r0 in range(0, x_ref.shape[0], ATTN_TILE)]
    ts = [_rms(x_ref[rows, :], g_ref[...]).astype(BF16) for rows in groups]
    ps = [_dot(t, w_ref[...]) for t in ts]
    qv_ts = [_dot_nt(wt_ref[...], t) for t in ts]
    for j, (rows, p, qv_t) in enumerate(zip(groups, ps, qv_ts)):
        c, sa, sb = c_ref[rows, :], sa_ref[rows, :], sb_ref[rows, :]
        for blk_i in range(n_k_blocks):
            blk = _rope_block(p[:, blk_i * LANES:(blk_i + 1) * LANES], c, sa, sb, half)
            o_ref[rows, blk_i * LANES:(blk_i + 1) * LANES] = blk.astype(o_ref.dtype)
        rest = n_k_blocks * LANES
        o_ref[rows, rest:] = p[:, rest:].astype(o_ref.dtype)
        cos_t, sin_t = cosq_ref[:, rows], sinq_ref[:, rows]
        pieces = []
        for lo in range(0, q_rows, DIFF_HEAD_DIM):
            x1, x2 = qv_t[lo:lo + half, :], qv_t[lo + half:lo + 2 * half, :]
            pieces += [(x1 * cos_t - x2 * sin_t) * scale, (x2 * cos_t + x1 * sin_t) * scale,
                       qv_t[lo + 2 * half:lo + DIFF_HEAD_DIM, :] * scale]
        qvt_ref[0, j] = jnp.concatenate(pieces + [qv_t[q_rows:, :]], axis=0).astype(qvt_ref.dtype)


def _l0_in(x2, g_pre, w_in, tabs, q_tabs_t, b, seq, q_rows):
    n, d = x2.shape
    n_out = w_in.shape[1] - 2 * q_rows
    tiles_per_seq = seq // ROW_TILE
    kern = functools.partial(_l0_in_kernel, n_k_blocks=q_rows // LANES, q_rows=q_rows,
                             scale=DIFF_HEAD_DIM ** -0.5 * LOG2_E)
    tab_spec = pl.BlockSpec((ROW_TILE, LANES), lambda i: (i % tiles_per_seq, 0))
    qtab_spec = pl.BlockSpec((q_tabs_t[0].shape[0], ROW_TILE), lambda i: (0, i % tiles_per_seq))
    vt_spec, vt_shape = _vt_out(b, seq, 2 * q_rows)
    return pl.pallas_call(
        kern,
        grid=(n // ROW_TILE,),
        in_specs=[pl.BlockSpec((ROW_TILE, d), lambda i: (i, 0)),
                  _const_spec((1, d)), _const_spec(w_in.shape),
                  tab_spec, tab_spec, tab_spec, qtab_spec, qtab_spec],
        out_specs=[pl.BlockSpec((ROW_TILE, n_out), lambda i: (i, 0)), vt_spec],
        out_shape=[jax.ShapeDtypeStruct((n, n_out), BF16), vt_shape],
        scratch_shapes=[pltpu.VMEM((d, n_out), BF16), pltpu.VMEM((2 * q_rows, d), BF16)],
        compiler_params=_params("arbitrary"),
        name="l0_in",
    )(x2, g_pre, w_in, *tabs, *q_tabs_t)


def _attn_flat(q_heads_at, k_at, vt_at, finish_tile, m_ref, acc_ref, *, n_tiles, n_maps, tq, q_transposed):
    dk_axis = 0 if q_transposed else 1
    qs_cache = {}

    def qs_of(i):
        if i not in qs_cache:
            qs = []
            for q in q_heads_at(i):
                if n_maps == 1:
                    qs.append(q)
                else:
                    width = q.shape[dk_axis] // n_maps
                    pos = lax.broadcasted_iota(jnp.int32, q.shape, dk_axis)
                    for c in range(n_maps):
                        qs.append(jnp.where((pos >= c * width) & (pos < (c + 1) * width), q, jnp.zeros_like(q)))
            qs_cache[i] = qs
        return qs_cache[i]

    n_chains = len(qs_of(0))
    ones = jnp.ones((ATTN_SUM_ROWS, tq), BF16)
    kc = lax.broadcasted_iota(jnp.int32, (tq, tq), 0) // CHUNK
    qc = lax.broadcasted_iota(jnp.int32, (tq, tq), 1) // CHUNK
    keep = kc <= qc
    items = [(i, kb, ch) for i in range(n_tiles) for kb in range(i + 1) for ch in range(n_chains)]

    def scores(i, kb, ch):
        k = k_at(kb, ch // n_maps)
        return _dot(k, qs_of(i)[ch]) if q_transposed else _dot_nt(k, qs_of(i)[ch])

    ss = {n: scores(*items[n]) for n in range(min(ATTN_LOOKAHEAD, len(items)))}
    for n, (i, kb, ch) in enumerate(items):
        if n + ATTN_LOOKAHEAD < len(items):
            ss[n + ATTN_LOOKAHEAD] = scores(*items[n + ATTN_LOOKAHEAD])
        s = ss.pop(n)
        if kb == i:
            s = jnp.where(keep, s, -jnp.inf)
        vt = jnp.concatenate([vt_at(kb, ch // n_maps), ones], axis=0)
        slot = i % 2
        if kb == 0:
            m_new = jnp.max(s, axis=0, keepdims=True)
            acc_new = _dot(vt, jnp.exp2(s - m_new).astype(vt.dtype))
        else:
            m_old = m_ref[slot, ch]
            m_new = jnp.maximum(m_old, jnp.max(s, axis=0, keepdims=True))
            acc_new = (jnp.exp2(m_old - m_new) * acc_ref[slot, ch]
                       + _dot(vt, jnp.exp2(s - m_new).astype(vt.dtype)))
        m_ref[slot, ch] = m_new
        acc_ref[slot, ch] = acc_new
        if kb == i and ch == n_chains - 1:
            finish_tile(i, slot)
            del qs_cache[i]


def _attn_flat_scratch(n_chains, dv):
    return [pltpu.VMEM((2, n_chains, 1, ATTN_TILE), F32),
            pltpu.VMEM((2, n_chains, dv + ATTN_SUM_ROWS, ATTN_TILE), F32)]


def _attn_out(acc_ref, ch, dv):
    return acc_ref[ch, :dv, :] / acc_ref[ch, dv:dv + 1, :]


def _cast_specs(weights, n_steps):
    specs = []
    for w in weights:
        rows = w.shape[0] // n_steps
        assert rows * n_steps == w.shape[0] and rows % BF16_SUBLANES == 0, w.shape
        specs.append(pl.BlockSpec((rows, w.shape[1]), lambda bb: (bb, 0)))
    return specs, [jax.ShapeDtypeStruct(w.shape, BF16) for w in weights]


def _cast_blocks(src_refs, dst_refs):
    for src, dst in zip(src_refs, dst_refs):
        dst[...] = src[...].astype(dst.dtype)


def _split_cast_refs(refs, n_in, n_out, n_cast):
    main_in, cast_src = refs[:n_in], refs[n_in:n_in + n_cast]
    rest = refs[n_in + n_cast:]
    return main_in, cast_src, rest[:n_out], rest[n_out:n_out + n_cast], rest[n_out + n_cast:]


def _diff_attn_kernel(*refs, n_heads, lam_init, n_cast):
    (qt_ref, k_ref, vt_ref, lam_ref, sub_ref), cast_src, (o_ref,), cast_dst, (m_ref, acc_ref) = (
        _split_cast_refs(refs, 5, 1, n_cast))
    _cast_blocks(cast_src, cast_dst)
    n_tiles, tq = qt_ref.shape[1], qt_ref.shape[3]
    lv = lam_ref[...]
    lam = (jnp.exp(jnp.sum(lv[0:1] * lv[1:2], axis=-1, keepdims=True))
           - jnp.exp(jnp.sum(lv[2:3] * lv[3:4], axis=-1, keepdims=True)) + lam_init)

    def finish_tile(i, slot):
        for h in range(n_heads):
            o_t = _attn_out(acc_ref.at[slot], 2 * h, LANES) - lam * _attn_out(acc_ref.at[slot], 2 * h + 1, LANES)
            inv = lax.rsqrt(jnp.mean(o_t * o_t, axis=0, keepdims=True) + RMS_EPS)
            y_t = o_t * inv * sub_ref[...] * (1.0 - lam_init)
            o_ref[0, i * tq:(i + 1) * tq, h * LANES:(h + 1) * LANES] = y_t.T.astype(o_ref.dtype)

    _attn_flat(lambda i: [qt_ref[0, i, h * LANES:(h + 1) * LANES, :] for h in range(n_heads)],
               lambda kb, h: k_ref[0, kb * tq:(kb + 1) * tq, h * LANES:(h + 1) * LANES],
               lambda kb, h: vt_ref[0, kb, h * LANES:(h + 1) * LANES, :],
               finish_tile, m_ref, acc_ref, n_tiles=n_tiles, n_maps=2, tq=tq, q_transposed=True)


def _diff_attn(p0, qvt, lam_vecs, subln_col, n_heads, cast_weights):
    b, s, _ = p0.shape
    n_kb = s // ATTN_TILE
    width = n_heads * LANES
    cast_specs, cast_shapes = _cast_specs(cast_weights, b)
    kern = functools.partial(_diff_attn_kernel, n_heads=n_heads, lam_init=0.8 - 0.6 * math.exp(-0.3 * 0),
                             n_cast=len(cast_weights))
    out = pl.pallas_call(
        kern,
        grid=(b,),
        in_specs=[pl.BlockSpec((1, n_kb, width, ATTN_TILE), lambda bb: (bb, 0, 0, 0)),
                  pl.BlockSpec((1, s, width), lambda bb: (bb, 0, 0)),
                  pl.BlockSpec((1, n_kb, width, ATTN_TILE), lambda bb: (bb, 0, 1, 0)),
                  pl.BlockSpec(lam_vecs.shape, lambda bb: (0, 0)),
                  pl.BlockSpec(subln_col.shape, lambda bb: (0, 0))] + cast_specs,
        out_specs=[pl.BlockSpec((1, s, width), lambda bb: (bb, 0, 0))] + cast_specs,
        out_shape=[jax.ShapeDtypeStruct((b, s, width), BF16)] + cast_shapes,
        scratch_shapes=_attn_flat_scratch(2 * n_heads, LANES),
        compiler_params=_params("arbitrary"),
        name="diff_attn",
    )(qvt, p0, qvt, lam_vecs, subln_col, *cast_weights)
    return out[0], out[1:]


def _mla_attn_kernel(*refs, n_heads, scale, n_cast):
    ((cq_ref, kv_ref, qn_ref, kvn_ref, wqt_ref, wk_ref, wvt_ref, cosq_tab, sinq_tab, ck_tab, sak_tab, sbk_tab),
     cast_src, (o_ref,), cast_dst, (m_ref, acc_ref, k_scr, vt_scr)) = (
        _split_cast_refs(refs, 12, 1, n_cast))
    _cast_blocks(cast_src, cast_dst)
    seq = kv_ref.shape[1]
    tq = ATTN_TILE
    half = MLA_ROPE // 2
    projected = set()

    def project_keys(chunk):
        if chunk in projected:
            return
        projected.add(chunk)
        rows = slice(chunk * ROW_TILE, (chunk + 1) * ROW_TILE)
        ckv = _rms(kv_ref[0, rows, :LANES].astype(F32), kvn_ref[...]).astype(BF16)
        kr = _rope_block(kv_ref[0, rows, LANES:].astype(F32), ck_tab[rows, :], sak_tab[rows, :], sbk_tab[rows, :],
                         half)
        k_scr[rows, :] = _dot(jnp.concatenate([ckv, kr.astype(BF16)], axis=-1), wk_ref[...]).astype(BF16)
        vt = _dot_nt(wvt_ref[...], ckv)
        for j in range(ROW_TILE // tq):
            vt_scr[chunk * (ROW_TILE // tq) + j] = vt[:, j * tq:(j + 1) * tq].astype(BF16)

    def q_heads_at(i):
        cols = slice(i * tq, (i + 1) * tq)
        q_t = _dot_nt(wqt_ref[...], _rms(cq_ref[0, cols, :].astype(F32), qn_ref[...]).astype(BF16))
        cos_t, sin_t = cosq_tab[:, cols], sinq_tab[:, cols]
        q_heads = []
        for h in range(n_heads):
            lo = h * MLA_QK_PAD
            x1 = q_t[lo + MLA_NOPE:lo + MLA_NOPE + half, :]
            x2 = q_t[lo + MLA_NOPE + half:lo + MLA_NOPE + 2 * half, :]
            q_h = jnp.concatenate([q_t[lo:lo + MLA_NOPE, :], x1 * cos_t - x2 * sin_t, x2 * cos_t + x1 * sin_t,
                                   q_t[lo + MLA_NOPE + 2 * half:lo + MLA_QK_PAD, :]], axis=0)
            q_heads.append((q_h * scale).astype(BF16))
        return q_heads

    def k_at(kb, h):
        project_keys(kb * tq // ROW_TILE)
        return k_scr[kb * tq:(kb + 1) * tq, h * MLA_QK_PAD:(h + 1) * MLA_QK_PAD]

    def finish_tile(i, slot):
        for h in range(n_heads):
            o_ref[0, i * tq:(i + 1) * tq, h * MLA_V:(h + 1) * MLA_V] = (
                _attn_out(acc_ref.at[slot], h, MLA_V).T.astype(o_ref.dtype))

    _attn_flat(q_heads_at, k_at, lambda kb, h: vt_scr[kb, h * MLA_V:(h + 1) * MLA_V, :],
               finish_tile, m_ref, acc_ref, n_tiles=seq // tq, n_maps=1, tq=tq, q_transposed=True)


def _mla_attn(p1, q_norm, kv_norm, wq_t, wk, wv_t, q_tabs_t, tabs, n_heads, cast_weights):
    b, s, _ = p1.shape
    n_kb = s // ATTN_TILE
    cast_specs, cast_shapes = _cast_specs(cast_weights, b)
    kern = functools.partial(_mla_attn_kernel, n_heads=n_heads, scale=(MLA_NOPE + MLA_ROPE) ** -0.5 * LOG2_E,
                             n_cast=len(cast_weights))
    const = lambda a: pl.BlockSpec(a.shape, lambda bb: (0,) * a.ndim)
    out = pl.pallas_call(
        kern,
        grid=(b,),
        in_specs=[pl.BlockSpec((1, s, 2 * LANES), lambda bb: (bb, 0, 1)),
                  pl.BlockSpec((1, s, 2 * LANES), lambda bb: (bb, 0, 2)),
                  const(q_norm), const(kv_norm), const(wq_t), const(wk), const(wv_t),
                  const(q_tabs_t[0]), const(q_tabs_t[1]),
                  const(tabs[0]), const(tabs[1]), const(tabs[2])] + cast_specs,
        out_specs=[pl.BlockSpec((1, s, n_heads * MLA_V), lambda bb: (bb, 0, 0))] + cast_specs,
        out_shape=[jax.ShapeDtypeStruct((b, s, n_heads * MLA_V), BF16)] + cast_shapes,
        scratch_shapes=_attn_flat_scratch(n_heads, MLA_V) + [
            pltpu.VMEM((s, n_heads * MLA_QK_PAD), BF16),
            pltpu.VMEM((n_kb, n_heads * MLA_V, ATTN_TILE), BF16)],
        compiler_params=_params("arbitrary"),
        name="mla_attn",
    )(p1, p1, q_norm, kv_norm, wq_t, wk, wv_t, *q_tabs_t, *tabs, *cast_weights)
    return out[0], out[1:]


def _conv_kernel(a_ref, gate_ref, ah_ref, gh_ref, w_ref, b_ref, lg_ref, lb_ref, o_ref, u_ref, ur_ref):
    tt = a_ref.shape[1]
    u_ref[CONV_HALO:, :] = a_ref[0].astype(F32) * _sigmoid(gate_ref[0].astype(F32))
    halo = ah_ref[0].astype(F32) * _sigmoid(gh_ref[0].astype(F32))
    u_ref[:CONV_HALO, :] = jnp.where(pl.program_id(1) > 0, halo, jnp.zeros_like(halo))
    rows = ur_ref.shape[1]
    for r in range(1, SUBLANES):
        ur_ref[r - 1] = u_ref[r:r + rows, :]
    acc = jnp.zeros((tt, a_ref.shape[2]), F32)
    first = CONV_HALO - (CONV_WIDTH - 1)
    for k in range(CONV_WIDTH):
        base, r = divmod(first + k, SUBLANES)
        src = u_ref if r == 0 else ur_ref.at[r - 1]
        acc = acc + src[base * SUBLANES:base * SUBLANES + tt, :] * w_ref[k:k + 1, :]
    y = acc + b_ref[...]
    mu = jnp.mean(y, axis=-1, keepdims=True)
    yc = y - mu
    yn = yc * lax.rsqrt(jnp.mean(yc * yc, axis=-1, keepdims=True) + LN_EPS) * lg_ref[...] + lb_ref[...]
    o_ref[0] = (yn * _sigmoid(yn)).astype(o_ref.dtype)


def _conv_module(p0, dw_w, dw_b, ln_g, ln_b, col0):
    b, s, _ = p0.shape
    c = dw_w.shape[1]
    a_blk, g_blk = col0 // c, col0 // c + 1
    ratio = CONV_TILE // CONV_HALO
    main = lambda blk: pl.BlockSpec((1, CONV_TILE, c), lambda bb, t: (bb, t, blk))
    halo = lambda blk: pl.BlockSpec((1, CONV_HALO, c), lambda bb, t: (bb, jnp.maximum(t * ratio - 1, 0), blk))
    vec = lambda a: pl.BlockSpec(a.shape, lambda bb, t: (0, 0))
    return pl.pallas_call(
        _conv_kernel,
        grid=(b, s // CONV_TILE),
        in_specs=[main(a_blk), main(g_blk), halo(a_blk), halo(g_blk),
                  vec(dw_w), vec(dw_b), vec(ln_g), vec(ln_b)],
        out_specs=pl.BlockSpec((1, CONV_TILE, c), lambda bb, t: (bb, t, 0)),
        out_shape=jax.ShapeDtypeStruct((b, s, c), BF16),
        scratch_shapes=[pltpu.VMEM((CONV_HALO + CONV_TILE, c), F32),
                        pltpu.VMEM((SUBLANES - 1, CONV_HALO + CONV_TILE - SUBLANES, c), F32)],
        compiler_params=_params("parallel", "parallel"),
        name="conv_module",
    )(p0, p0, p0, p0, dw_w, dw_b, ln_g, ln_b)


def _post_kernel(*refs, n_mix, with_next):
    x_ref = refs[0]
    mix_refs = refs[1:1 + n_mix]
    (wo_ref, gpost_ref, gfpre_ref, wg_ref, wu_ref, wd_ref, gfpost_ref) = refs[1 + n_mix:8 + n_mix]
    pos = 8 + n_mix
    if with_next:
        gnext_ref, wnext_ref = refs[pos:pos + 2]
        pos += 2
    h_out_ref = refs[pos]
    pos += 1
    if with_next:
        p_out_ref = refs[pos]
        pos += 1
    hid_ref = refs[pos]

    n_rows = x_ref.shape[0]
    groups = [slice(r0, r0 + POST_GROUP_ROWS) for r0 in range(0, n_rows, POST_GROUP_ROWS)]

    def out_proj(rows):
        y = None
        row = 0
        for r in mix_refs:
            w = r.shape[-1]
            part = _dot(r[rows, :], wo_ref[row:row + w, :])
            y = part if y is None else y + part
            row += w
        return y

    ys = [out_proj(rows) for rows in groups]
    h1s = [x_ref[rows, :] + _rms(y, gpost_ref[...]) for rows, y in zip(groups, ys)]
    ts = [_rms(h1, gfpre_ref[...]).astype(BF16) for h1 in h1s]
    d_ff = wg_ref.shape[1]
    for j in range(0, d_ff, FFN_CHUNK):
        wdt = min(FFN_CHUNK, d_ff - j)
        for rows, t in zip(groups, ts):
            gate = _dot(t, wg_ref[:, j:j + wdt])
            up = _dot(t, wu_ref[:, j:j + wdt])
            hid_ref[rows, j:j + wdt] = (gate * _sigmoid(gate) * up).astype(BF16)
    fs = [_dot(hid_ref[rows, :], wd_ref[...]) for rows in groups]
    h2s = [h1 + _rms(f, gfpost_ref[...]) for h1, f in zip(h1s, fs)]
    for rows, h2 in zip(groups, h2s):
        h_out_ref[rows, :] = h2
    if with_next:
        t2s = [_rms(h2, gnext_ref[...]).astype(BF16) for h2 in h2s]
        for rows, t2 in zip(groups, t2s):
            p_out_ref[rows, :] = _dot(t2, wnext_ref[...]).astype(p_out_ref.dtype)


def _post(x2, mix_parts, w_out, g_post, g_fpre, w_gate, w_up, w_down, g_fpost, nxt=None):
    n, d = x2.shape
    d_ff = w_gate.shape[1]
    row_tile = POST_ROW_TILE
    row = lambda a: pl.BlockSpec((row_tile, a.shape[1]), lambda i: (i, 0))
    consts = [w_out, g_post, g_fpre, w_gate, w_up, w_down, g_fpost] + (list(nxt) if nxt else [])
    out_shape = [jax.ShapeDtypeStruct((n, d), F32)]
    out_specs = [pl.BlockSpec((row_tile, d), lambda i: (i, 0))]
    if nxt:
        n_next = nxt[1].shape[1]
        out_shape.append(jax.ShapeDtypeStruct((n, n_next), BF16))
        out_specs.append(pl.BlockSpec((row_tile, n_next), lambda i: (i, 0)))
    kern = functools.partial(_post_kernel, n_mix=len(mix_parts), with_next=bool(nxt))
    return pl.pallas_call(
        kern,
        grid=(n // row_tile,),
        in_specs=[row(x2)] + [row(m) for m in mix_parts] + [_const_spec(c.shape) for c in consts],
        out_specs=out_specs,
        out_shape=out_shape,
        scratch_shapes=[pltpu.VMEM((row_tile, d_ff), BF16)],
        compiler_params=_params("parallel"),
        name="post_next" if nxt else "post",
    )(x2, *mix_parts, *consts)


def _gelu_tanh(x):
    return 0.5 * x * (1.0 + jnp.tanh(math.sqrt(2.0 / math.pi) * (x + 0.044715 * (x * x * x))))


def _ssm_kernel(u_ref, lr_ref, li_ref, ldt_ref, bre_ref, bim_ref, cre_ref, cim_ref, d_ref, wg_ref, bg_ref,
                o_ref, bmat_ref, cmat_ref, a_ref, st_ref, us_ref, utm_ref, x_ref, y_ref):
    nb, tt, ch = u_ref.shape
    n_state = lr_ref.shape[1]

    @pl.when(pl.program_id(0) == 0)
    def _init():
        lr, li = lr_ref[...], li_ref[...]
        dt = jnp.exp(ldt_ref[...])
        mag = jnp.exp(lr * dt)
        ab_re = mag * jnp.cos(li * dt)
        ab_im = mag * jnp.sin(li * dt)
        den = lr * lr + li * li
        n_re = ab_re - 1.0
        f_re = (n_re * lr + ab_im * li) / den
        f_im = (ab_im * lr - n_re * li) / den
        br, bi = bre_ref[...], bim_ref[...]
        bmat_ref[:, :n_state] = (f_re * br - f_im * bi).astype(BF16)
        bmat_ref[:, n_state:] = (f_re * bi + f_im * br).astype(BF16)
        cmat_ref[:n_state, :] = cre_ref[...].astype(BF16)
        cmat_ref[n_state:, :] = (-cim_ref[...]).astype(BF16)
        a_ref[0:1, :] = ab_re
        a_ref[1:2, :] = ab_im
        st_ref[...] = jnp.zeros(st_ref.shape, F32)

    n_ublk = ch // LANES
    for b in range(nb):
        for j in range(n_ublk):
            us_ref[j, b * SSM_PITCH:b * SSM_PITCH + tt, :] = u_ref[b, :, j * LANES:(j + 1) * LANES].astype(F32)

    def gather_step(t, carry):
        dst = pl.multiple_of(t * nb, nb)
        for j in range(n_ublk):
            utm_ref[pl.ds(dst, nb), j * LANES:(j + 1) * LANES] = us_ref[j, pl.ds(t, nb, stride=SSM_PITCH), :]
        return carry

    lax.fori_loop(0, tt, gather_step, 0, unroll=8)

    half_rows = tt * nb // 2
    for r0 in (0, half_rows):
        x_ref[r0:r0 + half_rows, :] = _dot(utm_ref[r0:r0 + half_rows, :].astype(BF16), bmat_ref[...])

    a_re = jnp.broadcast_to(a_ref[0:1, :], (nb, n_state))
    a_im = jnp.broadcast_to(a_ref[1:2, :], (nb, n_state))

    def scan_step(t, carry):
        x_re, x_im = carry
        row = pl.multiple_of(t * nb, nb)
        n_re = a_re * x_re - a_im * x_im + x_ref[pl.ds(row, nb), :n_state]
        n_im = a_re * x_im + a_im * x_re + x_ref[pl.ds(row, nb), n_state:]
        x_ref[pl.ds(row, nb), :n_state] = n_re
        x_ref[pl.ds(row, nb), n_state:] = n_im
        return n_re, n_im

    x_re, x_im = lax.fori_loop(0, tt, scan_step, (st_ref[:, :n_state], st_ref[:, n_state:]), unroll=4)
    st_ref[:, :n_state] = x_re
    st_ref[:, n_state:] = x_im

    n_yblk = ch // LANES
    for r0 in (0, half_rows):
        y_tm = _dot(x_ref[r0:r0 + half_rows, :].astype(BF16), cmat_ref[...])
        for j in range(n_yblk):
            y_ref[j, r0:r0 + half_rows, :] = y_tm[:, j * LANES:(j + 1) * LANES]
    for b in range(nb):
        y = jnp.concatenate([y_ref[j, pl.ds(b, tt, stride=nb), :] for j in range(n_yblk)], axis=-1)
        y = y + d_ref[...] * u_ref[b].astype(F32)
        z = _gelu_tanh(y)
        gate = _dot(z.astype(BF16), wg_ref[...]) + bg_ref[...]
        o_ref[b] = (z * _sigmoid(gate)).astype(o_ref.dtype)


def _ssm(p1, rows, b_bd, c_bd, d_row, w_glu, b_glu):
    b, s, _ = p1.shape
    ch = w_glu.shape[0]
    n_state = rows[0].shape[1]
    consts = list(rows) + list(b_bd) + list(c_bd) + [d_row, w_glu, b_glu]
    return pl.pallas_call(
        _ssm_kernel,
        grid=(s // SSM_TILE,),
        in_specs=[pl.BlockSpec((b, SSM_TILE, ch), lambda t: (0, t, 0))] + [_const_spec(c.shape) for c in consts],
        out_specs=pl.BlockSpec((b, SSM_TILE, ch), lambda t: (0, t, 0)),
        out_shape=jax.ShapeDtypeStruct((b, s, ch), BF16),
        scratch_shapes=[pltpu.VMEM((ch, 2 * n_state), BF16),
                        pltpu.VMEM((2 * n_state, ch), BF16),
                        pltpu.VMEM((SUBLANES, n_state), F32),
                        pltpu.VMEM((b, 2 * n_state), F32),
                        pltpu.VMEM((ch // LANES, b * SSM_PITCH, LANES), F32),
                        pltpu.VMEM((SSM_TILE * b, ch), F32),
                        pltpu.VMEM((SSM_TILE * b, 2 * n_state), F32),
                        pltpu.VMEM((ch // LANES, SSM_TILE * b, LANES), F32)],
        compiler_params=_params("arbitrary"),
        name="s5_ssm",
    )(p1, *consts)


def _rope_tables(s, rot_dim, theta):
    inv = float(theta) ** (-np.arange(0, rot_dim, 2, dtype=np.float64) / rot_dim)
    ang = np.arange(s, dtype=np.float64)[:, None] * inv[None, :]
    return np.cos(ang).astype(np.float32), np.sin(ang).astype(np.float32)


def _lane_tables(cos, sin, period):
    s, half = cos.shape
    reps = LANES // period
    one = np.ones((s, period - 2 * half), np.float32)
    zero = np.zeros((s, period - 2 * half), np.float32)
    zh = np.zeros((s, half), np.float32)
    c = np.tile(np.concatenate([cos, cos, one], axis=1), (1, reps))
    sa = np.tile(np.concatenate([-sin, zh, zero], axis=1), (1, reps))
    sb = np.tile(np.concatenate([zh, sin, zero], axis=1), (1, reps))
    return c, sa, sb


def _block_diag(blocks):
    g, r, c = blocks.shape
    eye = jnp.eye(g, dtype=blocks.dtype)
    return (eye[:, None, :, None] * blocks[:, :, None, :]).reshape(g * r, g * c)


def kernel(x, l0_mix_pre, l0_mix_post, l0_w_in, l0_lambda_q1, l0_lambda_k1, l0_lambda_q2, l0_lambda_k2, l0_subln, l0_dw_w, l0_dw_b, l0_conv_ln_g, l0_conv_ln_b, l0_w_out, l0_ffn_pre, l0_ffn_post, l0_w_gate, l0_w_up, l0_w_down, l1_mix_pre, l1_mix_post, l1_w_in, l1_a_re, l1_a_im, l1_log_dt, l1_b_re, l1_b_im, l1_c_re, l1_c_im, l1_d_skip, l1_w_glu, l1_b_glu, l1_q_norm, l1_w_uq, l1_kv_norm, l1_w_ukv, l1_w_out, l1_ffn_pre, l1_ffn_post, l1_w_gate, l1_w_up, l1_w_down):
    b, s, d = x.shape
    n = b * s
    row = lambda v: v.reshape(1, -1).astype(F32)
    bf = lambda w: w.astype(BF16)

    diff_width = 4 * LANES
    n_diff_heads = diff_width // LANES
    conv_ch = l0_dw_w.shape[1]
    ssm_ch = l1_w_glu.shape[0]
    n_groups, n_state_g = l1_a_re.shape
    q_rank = l1_q_norm.shape[0]
    kv_rank = l1_kv_norm.shape[0]
    n_mla_heads = l1_w_uq.shape[1] // (MLA_NOPE + MLA_ROPE)

    cos_a, sin_a = _rope_tables(s, DIFF_ROT, ROPE_THETA)
    tabs_a = _lane_tables(cos_a, sin_a, period=DIFF_HEAD_DIM)
    x2 = x.reshape(n, d)
    p0, qvt_a = _l0_in(x2, row(l0_mix_pre), l0_w_in.astype(F32), tabs_a, (cos_a.T, sin_a.T), b, s, diff_width)
    p0 = p0.reshape(b, s, -1)
    lam_vecs = jnp.stack([l0_lambda_q1, l0_lambda_k1, l0_lambda_q2, l0_lambda_k2]).astype(F32)
    y_a, (w_out0, w_gate0, w_up0, w_down0) = _diff_attn(
        p0, qvt_a, lam_vecs, l0_subln.reshape(-1, 1).astype(F32), n_diff_heads,
        [l0_w_out, l0_w_gate, l0_w_up, l0_w_down])
    y_b = _conv_module(p0, l0_dw_w.astype(F32), row(l0_dw_b), row(l0_conv_ln_g), row(l0_conv_ln_b),
                       col0=diff_width)

    pad = (-l1_w_in.shape[1]) % LANES
    w_in1 = bf(jnp.pad(l1_w_in, ((0, 0), (0, pad))))
    h2, p1 = _post(x2, [y_a.reshape(n, -1), y_b.reshape(n, -1)], w_out0, row(l0_mix_post),
                   row(l0_ffn_pre), w_gate0, w_up0, w_down0, row(l0_ffn_post),
                   nxt=(row(l1_mix_pre), w_in1))

    state_row = lambda a: a.reshape(1, -1).astype(F32)
    ssm_rows = (state_row(l1_a_re), state_row(l1_a_im),
                state_row(jnp.broadcast_to(l1_log_dt[:, None], (n_groups, n_state_g))))
    b_bd = tuple(_block_diag(jnp.swapaxes(m, 1, 2).astype(F32)) for m in (l1_b_re, l1_b_im))
    c_bd = tuple(_block_diag(jnp.swapaxes(m, 1, 2).astype(F32)) for m in (l1_c_re, l1_c_im))
    y_c = _ssm(p1.reshape(b, s, -1), ssm_rows, b_bd, c_bd, row(l1_d_skip), bf(l1_w_glu), row(l1_b_glu))

    wq = l1_w_uq.reshape(q_rank, n_mla_heads, MLA_NOPE + MLA_ROPE)
    wq = jnp.pad(wq, ((0, 0), (0, 0), (0, MLA_QK_PAD - MLA_NOPE - MLA_ROPE))).reshape(q_rank, -1)
    wkv = l1_w_ukv.reshape(kv_rank, n_mla_heads, MLA_NOPE + MLA_V)
    wk_nope = jnp.pad(wkv[:, :, :MLA_NOPE], ((0, 0), (0, 0), (0, MLA_QK_PAD - MLA_NOPE)))
    route = jnp.pad(jnp.eye(MLA_ROPE, dtype=F32), ((0, LANES - MLA_ROPE), (MLA_NOPE, MLA_QK_PAD - MLA_NOPE - MLA_ROPE)))
    wk_rope = jnp.broadcast_to(route[:, None, :], (LANES, n_mla_heads, MLA_QK_PAD))
    wk = jnp.concatenate([wk_nope, wk_rope], axis=0).reshape(kv_rank + LANES, -1)
    wv_t = wkv[:, :, MLA_NOPE:].reshape(kv_rank, -1).T
    cos_d, sin_d = _rope_tables(s, MLA_ROPE, MLA_ROPE_THETA)
    tabs_d = _lane_tables(cos_d, sin_d, period=LANES)
    y_d, (w_out1, w_gate1, w_up1, w_down1) = _mla_attn(
        p1.reshape(b, s, -1), row(l1_q_norm), row(l1_kv_norm), bf(wq.T), bf(wk), bf(wv_t),
        (cos_d.T, sin_d.T), tabs_d, n_mla_heads, [l1_w_out, l1_w_gate, l1_w_up, l1_w_down])

    (out,) = _post(h2, [y_c.reshape(n, -1), y_d.reshape(n, -1)], w_out1, row(l1_mix_post),
                   row(l1_ffn_pre), w_gate1, w_up1, w_down1, row(l1_ffn_post))
    return out.reshape(b, s, d)
```

```python
import functools
import math

import jax
import jax.numpy as jnp
import numpy as np
from jax import lax
from jax.experimental import pallas as pl
from jax.experimental.pallas import tpu as pltpu

F32 = jnp.float32
BF16 = jnp.bfloat16

LANES = 128
SUBLANES = 8
BF16_SUBLANES = 16
VMEM_MAX_MIB = 60
L0_IN_VMEM_MIB = 40
DIFF_ATTN_VMEM_MIB = 44
MLA_ATTN_VMEM_MIB = 56
CONV_VMEM_MIB = 16
POST_VMEM_MIB = 60
SSM_VMEM_MIB = 40

CHUNK = 64
RMS_EPS = 1e-6
LN_EPS = 1e-5
ROPE_THETA = 500000.0
MLA_ROPE_THETA = 10000.0
DIFF_HEAD_DIM = 64
DIFF_ROT = 16
CONV_WIDTH = 31
CONV_HALO = 32
SSM_GROUP = 16
SSM_STATE = 64
MLA_NOPE = 128
MLA_ROPE = 64
MLA_V = 128
MLA_QK_PAD = 256

ROW_TILE = 512
ATTN_TILE = 256
ATTN_LOOKAHEAD = 5
ATTN_SUM_ROWS = 16
LOG2_E = math.log2(math.e)
CONV_TILE = 256
SSM_TILE = 128
SSM_PITCH = SSM_TILE + SUBLANES
FFN_CHUNK = 512
POST_GROUP_ROWS = 256
POST_ROW_TILE = 1024


def _params(vmem_mib, *sem):
    assert vmem_mib <= VMEM_MAX_MIB
    return pltpu.CompilerParams(dimension_semantics=sem, vmem_limit_bytes=vmem_mib * 1024 * 1024)


def _rms(x, g):
    return x * lax.rsqrt(jnp.mean(x * x, axis=-1, keepdims=True) + RMS_EPS) * g


def _sigmoid(x):
    return 1.0 / (1.0 + jnp.exp(-x))


def _dot(a, b):
    return jnp.dot(a, b, preferred_element_type=F32)


def _dot_nt(a, b):
    return lax.dot_general(a, b, (((1,), (1,)), ((), ())), preferred_element_type=F32)


def _rope_block(x, c, sa, sb, shift):
    return x * c + pltpu.roll(x, LANES - shift, 1) * sa + pltpu.roll(x, shift, 1) * sb


def _const_spec(shape):
    nd = len(shape)
    return pl.BlockSpec(shape, lambda *_: (0,) * nd, pipeline_mode=pl.Buffered(1))


def _vt_out(b, seq, width):
    tiles_per_seq = seq // ROW_TILE
    per_tile = ROW_TILE // ATTN_TILE
    spec = pl.BlockSpec((1, per_tile, width, ATTN_TILE),
                        lambda i: (i // tiles_per_seq, i % tiles_per_seq, 0, 0))
    return spec, jax.ShapeDtypeStruct((b, seq // ATTN_TILE, width, ATTN_TILE), BF16)


def _l0_in_kernel(x_ref, g_ref, win_ref, c_ref, sa_ref, sb_ref, cosq_ref, sinq_ref, o_ref, qvt_ref,
                  w_ref, wt_ref, *, n_k_blocks, q_rows, scale):
    @pl.when(pl.program_id(0) == 0)
    def _regroup_weights():
        w_ref[:, :q_rows] = win_ref[:, q_rows:2 * q_rows].astype(BF16)
        w_ref[:, q_rows:] = win_ref[:, 3 * q_rows:].astype(BF16)
        wt_ref[:q_rows, :] = win_ref[:, :q_rows].T.astype(BF16)
        wt_ref[q_rows:, :] = win_ref[:, 2 * q_rows:3 * q_rows].T.astype(BF16)

    half = DIFF_ROT // 2
    groups = [slice(r0, r0 + ATTN_TILE) for r0 in range(0, x_ref.shape[0], ATTN_TILE)]
    ts = [_rms(x_ref[rows, :], g_ref[...]).astype(BF16) for rows in groups]
    ps = [_dot(t, w_ref[...]) for t in ts]
    qv_ts = [_dot_nt(wt_ref[...], t) for t in ts]
    for j, (rows, p, qv_t) in enumerate(zip(groups, ps, qv_ts)):
        c, sa, sb = c_ref[rows, :], sa_ref[rows, :], sb_ref[rows, :]
        for blk_i in range(n_k_blocks):
            blk = _rope_block(p[:, blk_i * LANES:(blk_i + 1) * LANES], c, sa, sb, half)
            o_ref[rows, blk_i * LANES:(blk_i + 1) * LANES] = blk.astype(o_ref.dtype)
        rest = n_k_blocks * LANES
        o_ref[rows, rest:] = p[:, rest:].astype(o_ref.dtype)
        cos_t, sin_t = cosq_ref[:, rows], sinq_ref[:, rows]
        pieces = []
        for lo in range(0, q_rows, DIFF_HEAD_DIM):
            x1, x2 = qv_t[lo:lo + half, :], qv_t[lo + half:lo + 2 * half, :]
            pieces += [(x1 * cos_t - x2 * sin_t) * scale, (x2 * cos_t + x1 * sin_t) * scale,
                       qv_t[lo + 2 * half:lo + DIFF_HEAD_DIM, :] * scale]
        qvt_ref[0, j] = jnp.concatenate(pieces + [qv_t[q_rows:, :]], axis=0).astype(qvt_ref.dtype)


def _l0_in(x2, g_pre, w_in, tabs, q_tabs_t, b, seq, q_rows):
    n, d = x2.shape
    n_out = w_in.shape[1] - 2 * q_rows
    tiles_per_seq = seq // ROW_TILE
    kern = functools.partial(_l0_in_kernel, n_k_blocks=q_rows // LANES, q_rows=q_rows,
                             scale=DIFF_HEAD_DIM ** -0.5 * LOG2_E)
    tab_spec = pl.BlockSpec((ROW_TILE, LANES), lambda i: (i % tiles_per_seq, 0))
    qtab_spec = pl.BlockSpec((q_tabs_t[0].shape[0], ROW_TILE), lambda i: (0, i % tiles_per_seq))
    vt_spec, vt_shape = _vt_out(b, seq, 2 * q_rows)
    return pl.pallas_call(
        kern,
        grid=(n // ROW_TILE,),
        in_specs=[pl.BlockSpec((ROW_TILE, d), lambda i: (i, 0)),
                  _const_spec((1, d)), _const_spec(w_in.shape),
                  tab_spec, tab_spec, tab_spec, qtab_spec, qtab_spec],
        out_specs=[pl.BlockSpec((ROW_TILE, n_out), lambda i: (i, 0)), vt_spec],
        out_shape=[jax.ShapeDtypeStruct((n, n_out), BF16), vt_shape],
        scratch_shapes=[pltpu.VMEM((d, n_out), BF16), pltpu.VMEM((2 * q_rows, d), BF16)],
        compiler_params=_params(L0_IN_VMEM_MIB, "arbitrary"),
        name="l0_in",
    )(x2, g_pre, w_in, *tabs, *q_tabs_t)


def _attn_flat(q_heads_at, k_at, vt_at, finish_tile, m_ref, acc_ref, *, n_tiles, n_maps, tq, q_transposed):
    dk_axis = 0 if q_transposed else 1
    qs_cache = {}

    def qs_of(i):
        if i not in qs_cache:
            qs = []
            for q in q_heads_at(i):
                if n_maps == 1:
                    qs.append(q)
                else:
                    width = q.shape[dk_axis] // n_maps
                    pos = lax.broadcasted_iota(jnp.int32, q.shape, dk_axis)
                    for c in range(n_maps):
                        qs.append(jnp.where((pos >= c * width) & (pos < (c + 1) * width), q, jnp.zeros_like(q)))
            qs_cache[i] = qs
        return qs_cache[i]

    n_chains = len(qs_of(0))
    ones = jnp.ones((ATTN_SUM_ROWS, tq), BF16)
    kc = lax.broadcasted_iota(jnp.int32, (tq, tq), 0) // CHUNK
    qc = lax.broadcasted_iota(jnp.int32, (tq, tq), 1) // CHUNK
    keep = kc <= qc
    items = [(i, kb, ch) for i in range(n_tiles) for kb in range(i + 1) for ch in range(n_chains)]

    def scores(i, kb, ch):
        k = k_at(kb, ch // n_maps)
        return _dot(k, qs_of(i)[ch]) if q_transposed else _dot_nt(k, qs_of(i)[ch])

    ss = {n: scores(*items[n]) for n in range(min(ATTN_LOOKAHEAD, len(items)))}
    for n, (i, kb, ch) in enumerate(items):
        if n + ATTN_LOOKAHEAD < len(items):
            ss[n + ATTN_LOOKAHEAD] = scores(*items[n + ATTN_LOOKAHEAD])
        s = ss.pop(n)
        if kb == i:
            s = jnp.where(keep, s, -jnp.inf)
        vt = jnp.concatenate([vt_at(kb, ch // n_maps), ones], axis=0)
        slot = i % 2
        if kb == 0:
            m_new = jnp.max(s, axis=0, keepdims=True)
            acc_new = _dot(vt, jnp.exp2(s - m_new).astype(vt.dtype))
        else:
            m_old = m_ref[slot, ch]
            m_new = jnp.maximum(m_old, jnp.max(s, axis=0, keepdims=True))
            acc_new = (jnp.exp2(m_old - m_new) * acc_ref[slot, ch]
                       + _dot(vt, jnp.exp2(s - m_new).astype(vt.dtype)))
        m_ref[slot, ch] = m_new
        acc_ref[slot, ch] = acc_new
        if kb == i and ch == n_chains - 1:
            finish_tile(i, slot)
            del qs_cache[i]


def _attn_flat_scratch(n_chains, dv):
    return [pltpu.VMEM((2, n_chains, 1, ATTN_TILE), F32),
            pltpu.VMEM((2, n_chains, dv + ATTN_SUM_ROWS, ATTN_TILE), F32)]


def _attn_out(acc_ref, ch, dv):
    return acc_ref[ch, :dv, :] / acc_ref[ch, dv:dv + 1, :]


def _cast_specs(weights, n_steps):
    specs = []
    for w in weights:
        rows = w.shape[0] // n_steps
        assert rows * n_steps == w.shape[0] and rows % BF16_SUBLANES == 0, w.shape
        specs.append(pl.BlockSpec((rows, w.shape[1]), lambda bb: (bb, 0)))
    return specs, [jax.ShapeDtypeStruct(w.shape, BF16) for w in weights]


def _cast_blocks(src_refs, dst_refs):
    for src, dst in zip(src_refs, dst_refs):
        dst[...] = src[...].astype(dst.dtype)


def _split_cast_refs(refs, n_in, n_out, n_cast):
    main_in, cast_src = refs[:n_in], refs[n_in:n_in + n_cast]
    rest = refs[n_in + n_cast:]
    return main_in, cast_src, rest[:n_out], rest[n_out:n_out + n_cast], rest[n_out + n_cast:]


def _diff_attn_kernel(*refs, n_heads, lam_init, n_cast):
    (qt_ref, k_ref, vt_ref, lam_ref, sub_ref), cast_src, (o_ref,), cast_dst, (m_ref, acc_ref) = (
        _split_cast_refs(refs, 5, 1, n_cast))
    _cast_blocks(cast_src, cast_dst)
    n_tiles, tq = qt_ref.shape[1], qt_ref.shape[3]
    lv = lam_ref[...]
    lam = (jnp.exp(jnp.sum(lv[0:1] * lv[1:2], axis=-1, keepdims=True))
           - jnp.exp(jnp.sum(lv[2:3] * lv[3:4], axis=-1, keepdims=True)) + lam_init)

    def finish_tile(i, slot):
        for h in range(n_heads):
            o_t = _attn_out(acc_ref.at[slot], 2 * h, LANES) - lam * _attn_out(acc_ref.at[slot], 2 * h + 1, LANES)
            inv = lax.rsqrt(jnp.mean(o_t * o_t, axis=0, keepdims=True) + RMS_EPS)
            y_t = o_t * inv * sub_ref[...] * (1.0 - lam_init)
            o_ref[0, i * tq:(i + 1) * tq, h * LANES:(h + 1) * LANES] = y_t.T.astype(o_ref.dtype)

    _attn_flat(lambda i: [qt_ref[0, i, h * LANES:(h + 1) * LANES, :] for h in range(n_heads)],
               lambda kb, h: k_ref[0, kb * tq:(kb + 1) * tq, h * LANES:(h + 1) * LANES],
               lambda kb, h: vt_ref[0, kb, h * LANES:(h + 1) * LANES, :],
               finish_tile, m_ref, acc_ref, n_tiles=n_tiles, n_maps=2, tq=tq, q_transposed=True)


def _diff_attn(p0, qvt, lam_vecs, subln_col, n_heads, cast_weights):
    b, s, _ = p0.shape
    n_kb = s // ATTN_TILE
    width = n_heads * LANES
    cast_specs, cast_shapes = _cast_specs(cast_weights, b)
    kern = functools.partial(_diff_attn_kernel, n_heads=n_heads, lam_init=0.8 - 0.6 * math.exp(-0.3 * 0),
                             n_cast=len(cast_weights))
    out = pl.pallas_call(
        kern,
        grid=(b,),
        in_specs=[pl.BlockSpec((1, n_kb, width, ATTN_TILE), lambda bb: (bb, 0, 0, 0)),
                  pl.BlockSpec((1, s, width), lambda bb: (bb, 0, 0)),
                  pl.BlockSpec((1, n_kb, width, ATTN_TILE), lambda bb: (bb, 0, 1, 0)),
                  pl.BlockSpec(lam_vecs.shape, lambda bb: (0, 0)),
                  pl.BlockSpec(subln_col.shape, lambda bb: (0, 0))] + cast_specs,
        out_specs=[pl.BlockSpec((1, s, width), lambda bb: (bb, 0, 0))] + cast_specs,
        out_shape=[jax.ShapeDtypeStruct((b, s, width), BF16)] + cast_shapes,
        scratch_shapes=_attn_flat_scratch(2 * n_heads, LANES),
        compiler_params=_params(DIFF_ATTN_VMEM_MIB, "arbitrary"),
        name="diff_attn",
    )(qvt, p0, qvt, lam_vecs, subln_col, *cast_weights)
    return out[0], out[1:]


def _mla_attn_kernel(*refs, n_heads, scale, n_cast):
    ((cq_ref, kv_ref, qn_ref, kvn_ref, wqt_ref, wk_ref, wvt_ref, cosq_tab, sinq_tab, ck_tab, sak_tab, sbk_tab),
     cast_src, (o_ref,), cast_dst, (m_ref, acc_ref, k_scr, vt_scr)) = (
        _split_cast_refs(refs, 12, 1, n_cast))
    _cast_blocks(cast_src, cast_dst)
    seq = kv_ref.shape[1]
    tq = ATTN_TILE
    half = MLA_ROPE // 2
    projected = set()

    def project_keys(chunk):
        if chunk in projected:
            return
        projected.add(chunk)
        rows = slice(chunk * ROW_TILE, (chunk + 1) * ROW_TILE)
        ckv = _rms(kv_ref[0, rows, :LANES].astype(F32), kvn_ref[...]).astype(BF16)
        kr = _rope_block(kv_ref[0, rows, LANES:].astype(F32), ck_tab[rows, :], sak_tab[rows, :], sbk_tab[rows, :],
                         half)
        k_scr[rows, :] = _dot(jnp.concatenate([ckv, kr.astype(BF16)], axis=-1), wk_ref[...]).astype(BF16)
        vt = _dot_nt(wvt_ref[...], ckv)
        for j in range(ROW_TILE // tq):
            vt_scr[chunk * (ROW_TILE // tq) + j] = vt[:, j * tq:(j + 1) * tq].astype(BF16)

    def q_heads_at(i):
        cols = slice(i * tq, (i + 1) * tq)
        q_t = _dot_nt(wqt_ref[...], _rms(cq_ref[0, cols, :].astype(F32), qn_ref[...]).astype(BF16))
        cos_t, sin_t = cosq_tab[:, cols], sinq_tab[:, cols]
        q_heads = []
        for h in range(n_heads):
            lo = h * MLA_QK_PAD
            x1 = q_t[lo + MLA_NOPE:lo + MLA_NOPE + half, :]
            x2 = q_t[lo + MLA_NOPE + half:lo + MLA_NOPE + 2 * half, :]
            q_h = jnp.concatenate([q_t[lo:lo + MLA_NOPE, :], x1 * cos_t - x2 * sin_t, x2 * cos_t + x1 * sin_t,
                                   q_t[lo + MLA_NOPE + 2 * half:lo + MLA_QK_PAD, :]], axis=0)
            q_heads.append((q_h * scale).astype(BF16))
        return q_heads

    def k_at(kb, h):
        project_keys(kb * tq // ROW_TILE)
        return k_scr[kb * tq:(kb + 1) * tq, h * MLA_QK_PAD:(h + 1) * MLA_QK_PAD]

    def finish_tile(i, slot):
        for h in range(n_heads):
            o_ref[0, i * tq:(i + 1) * tq, h * MLA_V:(h + 1) * MLA_V] = (
                _attn_out(acc_ref.at[slot], h, MLA_V).T.astype(o_ref.dtype))

    _attn_flat(q_heads_at, k_at, lambda kb, h: vt_scr[kb, h * MLA_V:(h + 1) * MLA_V, :],
               finish_tile, m_ref, acc_ref, n_tiles=seq // tq, n_maps=1, tq=tq, q_transposed=True)


def _mla_attn(p1, q_norm, kv_norm, wq_t, wk, wv_t, q_tabs_t, tabs, n_heads, cast_weights):
    b, s, _ = p1.shape
    n_kb = s // ATTN_TILE
    cast_specs, cast_shapes = _cast_specs(cast_weights, b)
    kern = functools.partial(_mla_attn_kernel, n_heads=n_heads, scale=(MLA_NOPE + MLA_ROPE) ** -0.5 * LOG2_E,
                             n_cast=len(cast_weights))
    const = lambda a: pl.BlockSpec(a.shape, lambda bb: (0,) * a.ndim)
    out = pl.pallas_call(
        kern,
        grid=(b,),
        in_specs=[pl.BlockSpec((1, s, 2 * LANES), lambda bb: (bb, 0, 1)),
                  pl.BlockSpec((1, s, 2 * LANES), lambda bb: (bb, 0, 2)),
                  const(q_norm), const(kv_norm), const(wq_t), const(wk), const(wv_t),
                  const(q_tabs_t[0]), const(q_tabs_t[1]),
                  const(tabs[0]), const(tabs[1]), const(tabs[2])] + cast_specs,
        out_specs=[pl.BlockSpec((1, s, n_heads * MLA_V), lambda bb: (bb, 0, 0))] + cast_specs,
        out_shape=[jax.ShapeDtypeStruct((b, s, n_heads * MLA_V), BF16)] + cast_shapes,
        scratch_shapes=_attn_flat_scratch(n_heads, MLA_V) + [
            pltpu.VMEM((s, n_heads * MLA_QK_PAD), BF16),
            pltpu.VMEM((n_kb, n_heads * MLA_V, ATTN_TILE), BF16)],
        compiler_params=_params(MLA_ATTN_VMEM_MIB, "arbitrary"),
        name="mla_attn",
    )(p1, p1, q_norm, kv_norm, wq_t, wk, wv_t, *q_tabs_t, *tabs, *cast_weights)
    return out[0], out[1:]


def _conv_kernel(a_ref, gate_ref, ah_ref, gh_ref, w_ref, b_ref, lg_ref, lb_ref, o_ref, u_ref, ur_ref):
    tt = a_ref.shape[1]
    u_ref[CONV_HALO:, :] = a_ref[0].astype(F32) * _sigmoid(gate_ref[0].astype(F32))
    halo = ah_ref[0].astype(F32) * _sigmoid(gh_ref[0].astype(F32))
    u_ref[:CONV_HALO, :] = jnp.where(pl.program_id(1) > 0, halo, jnp.zeros_like(halo))
    rows = ur_ref.shape[1]
    for r in range(1, SUBLANES):
        ur_ref[r - 1] = u_ref[r:r + rows, :]
    acc = jnp.zeros((tt, a_ref.shape[2]), F32)
    first = CONV_HALO - (CONV_WIDTH - 1)
    for k in range(CONV_WIDTH):
        base, r = divmod(first + k, SUBLANES)
        src = u_ref if r == 0 else ur_ref.at[r - 1]
        acc = acc + src[base * SUBLANES:base * SUBLANES + tt, :] * w_ref[k:k + 1, :]
    y = acc + b_ref[...]
    mu = jnp.mean(y, axis=-1, keepdims=True)
    yc = y - mu
    yn = yc * lax.rsqrt(jnp.mean(yc * yc, axis=-1, keepdims=True) + LN_EPS) * lg_ref[...] + lb_ref[...]
    o_ref[0] = (yn * _sigmoid(yn)).astype(o_ref.dtype)


def _conv_module(p0, dw_w, dw_b, ln_g, ln_b, col0):
    b, s, _ = p0.shape
    c = dw_w.shape[1]
    a_blk, g_blk = col0 // c, col0 // c + 1
    ratio = CONV_TILE // CONV_HALO
    main = lambda blk: pl.BlockSpec((1, CONV_TILE, c), lambda bb, t: (bb, t, blk))
    halo = lambda blk: pl.BlockSpec((1, CONV_HALO, c), lambda bb, t: (bb, jnp.maximum(t * ratio - 1, 0), blk))
    vec = lambda a: pl.BlockSpec(a.shape, lambda bb, t: (0, 0))
    return pl.pallas_call(
        _conv_kernel,
        grid=(b, s // CONV_TILE),
        in_specs=[main(a_blk), main(g_blk), halo(a_blk), halo(g_blk),
                  vec(dw_w), vec(dw_b), vec(ln_g), vec(ln_b)],
        out_specs=pl.BlockSpec((1, CONV_TILE, c), lambda bb, t: (bb, t, 0)),
        out_shape=jax.ShapeDtypeStruct((b, s, c), BF16),
        scratch_shapes=[pltpu.VMEM((CONV_HALO + CONV_TILE, c), F32),
                        pltpu.VMEM((SUBLANES - 1, CONV_HALO + CONV_TILE - SUBLANES, c), F32)],
        compiler_params=_params(CONV_VMEM_MIB, "parallel", "parallel"),
        name="conv_module",
    )(p0, p0, p0, p0, dw_w, dw_b, ln_g, ln_b)


def _post_kernel(*refs, n_mix, with_next):
    x_ref = refs[0]
    mix_refs = refs[1:1 + n_mix]
    (wo_ref, gpost_ref, gfpre_ref, wg_ref, wu_ref, wd_ref, gfpost_ref) = refs[1 + n_mix:8 + n_mix]
    pos = 8 + n_mix
    if with_next:
        gnext_ref, wnext_ref = refs[pos:pos + 2]
        pos += 2
    h_out_ref = refs[pos]
    pos += 1
    if with_next:
        p_out_ref = refs[pos]
        pos += 1
    hid_ref = refs[pos]

    n_rows = x_ref.shape[0]
    groups = [slice(r0, r0 + POST_GROUP_ROWS) for r0 in range(0, n_rows, POST_GROUP_ROWS)]

    def out_proj(rows):
        y = None
        row = 0
        for r in mix_refs:
            w = r.shape[-1]
            part = _dot(r[rows, :], wo_ref[row:row + w, :])
            y = part if y is None else y + part
            row += w
        return y

    ys = [out_proj(rows) for rows in groups]
    h1s = [x_ref[rows, :] + _rms(y, gpost_ref[...]) for rows, y in zip(groups, ys)]
    ts = [_rms(h1, gfpre_ref[...]).astype(BF16) for h1 in h1s]
    d_ff = wg_ref.shape[1]
    for j in range(0, d_ff, FFN_CHUNK):
        wdt = min(FFN_CHUNK, d_ff - j)
        for rows, t in zip(groups, ts):
            gate = _dot(t, wg_ref[:, j:j + wdt])
            up = _dot(t, wu_ref[:, j:j + wdt])
            hid_ref[rows, j:j + wdt] = (gate * _sigmoid(gate) * up).astype(BF16)
    fs = [_dot(hid_ref[rows, :], wd_ref[...]) for rows in groups]
    h2s = [h1 + _rms(f, gfpost_ref[...]) for h1, f in zip(h1s, fs)]
    for rows, h2 in zip(groups, h2s):
        h_out_ref[rows, :] = h2
    if with_next:
        t2s = [_rms(h2, gnext_ref[...]).astype(BF16) for h2 in h2s]
        for rows, t2 in zip(groups, t2s):
            p_out_ref[rows, :] = _dot(t2, wnext_ref[...]).astype(p_out_ref.dtype)


def _post(x2, mix_parts, w_out, g_post, g_fpre, w_gate, w_up, w_down, g_fpost, nxt=None):
    n, d = x2.shape
    d_ff = w_gate.shape[1]
    row_tile = POST_ROW_TILE
    row = lambda a: pl.BlockSpec((row_tile, a.shape[1]), lambda i: (i, 0))
    consts = [w_out, g_post, g_fpre, w_gate, w_up, w_down, g_fpost] + (list(nxt) if nxt else [])
    out_shape = [jax.ShapeDtypeStruct((n, d), F32)]
    out_specs = [pl.BlockSpec((row_tile, d), lambda i: (i, 0))]
    if nxt:
        n_next = nxt[1].shape[1]
        out_shape.append(jax.ShapeDtypeStruct((n, n_next), BF16))
        out_specs.append(pl.BlockSpec((row_tile, n_next), lambda i: (i, 0)))
    kern = functools.partial(_post_kernel, n_mix=len(mix_parts), with_next=bool(nxt))
    return pl.pallas_call(
        kern,
        grid=(n // row_tile,),
        in_specs=[row(x2)] + [row(m) for m in mix_parts] + [_const_spec(c.shape) for c in consts],
        out_specs=out_specs,
        out_shape=out_shape,
        scratch_shapes=[pltpu.VMEM((row_tile, d_ff), BF16)],
        compiler_params=_params(POST_VMEM_MIB, "parallel"),
        name="post_next" if nxt else "post",
    )(x2, *mix_parts, *consts)


def _gelu_tanh(x):
    return 0.5 * x * (1.0 + jnp.tanh(math.sqrt(2.0 / math.pi) * (x + 0.044715 * (x * x * x))))


def _ssm_kernel(u_ref, lr_ref, li_ref, ldt_ref, bre_ref, bim_ref, cre_ref, cim_ref, d_ref, wg_ref, bg_ref,
                o_ref, bmat_ref, cmat_ref, a_ref, st_ref, us_ref, utm_ref, x_ref, y_ref):
    nb, tt, ch = u_ref.shape
    n_state = lr_ref.shape[1]

    @pl.when(pl.program_id(0) == 0)
    def _init():
        lr, li = lr_ref[...], li_ref[...]
        dt = jnp.exp(ldt_ref[...])
        mag = jnp.exp(lr * dt)
        ab_re = mag * jnp.cos(li * dt)
        ab_im = mag * jnp.sin(li * dt)
        den = lr * lr + li * li
        n_re = ab_re - 1.0
        f_re = (n_re * lr + ab_im * li) / den
        f_im = (ab_im * lr - n_re * li) / den
        br, bi = bre_ref[...], bim_ref[...]
        bmat_ref[:, :n_state] = (f_re * br - f_im * bi).astype(BF16)
        bmat_ref[:, n_state:] = (f_re * bi + f_im * br).astype(BF16)
        cmat_ref[:n_state, :] = cre_ref[...].astype(BF16)
        cmat_ref[n_state:, :] = (-cim_ref[...]).astype(BF16)
        a_ref[0:1, :] = ab_re
        a_ref[1:2, :] = ab_im
        st_ref[...] = jnp.zeros(st_ref.shape, F32)

    n_ublk = ch // LANES
    for b in range(nb):
        for j in range(n_ublk):
            us_ref[j, b * SSM_PITCH:b * SSM_PITCH + tt, :] = u_ref[b, :, j * LANES:(j + 1) * LANES].astype(F32)

    def gather_step(t, carry):
        dst = pl.multiple_of(t * nb, nb)
        for j in range(n_ublk):
            utm_ref[pl.ds(dst, nb), j * LANES:(j + 1) * LANES] = us_ref[j, pl.ds(t, nb, stride=SSM_PITCH), :]
        return carry

    lax.fori_loop(0, tt, gather_step, 0, unroll=8)

    half_rows = tt * nb // 2
    for r0 in (0, half_rows):
        x_ref[r0:r0 + half_rows, :] = _dot(utm_ref[r0:r0 + half_rows, :].astype(BF16), bmat_ref[...])

    a_re = jnp.broadcast_to(a_ref[0:1, :], (nb, n_state))
    a_im = jnp.broadcast_to(a_ref[1:2, :], (nb, n_state))

    def scan_step(t, carry):
        x_re, x_im = carry
        row = pl.multiple_of(t * nb, nb)
        n_re = a_re * x_re - a_im * x_im + x_ref[pl.ds(row, nb), :n_state]
        n_im = a_re * x_im + a_im * x_re + x_ref[pl.ds(row, nb), n_state:]
        x_ref[pl.ds(row, nb), :n_state] = n_re
        x_ref[pl.ds(row, nb), n_state:] = n_im
        return n_re, n_im

    x_re, x_im = lax.fori_loop(0, tt, scan_step, (st_ref[:, :n_state], st_ref[:, n_state:]), unroll=4)
    st_ref[:, :n_state] = x_re
    st_ref[:, n_state:] = x_im

    n_yblk = ch // LANES
    for r0 in (0, half_rows):
        y_tm = _dot(x_ref[r0:r0 + half_rows, :].astype(BF16), cmat_ref[...])
        for j in range(n_yblk):
            y_ref[j, r0:r0 + half_rows, :] = y_tm[:, j * LANES:(j + 1) * LANES]
    for b in range(nb):
        y = jnp.concatenate([y_ref[j, pl.ds(b, tt, stride=nb), :] for j in range(n_yblk)], axis=-1)
        y = y + d_ref[...] * u_ref[b].astype(F32)
        z = _gelu_tanh(y)
        gate = _dot(z.astype(BF16), wg_ref[...]) + bg_ref[...]
        o_ref[b] = (z * _sigmoid(gate)).astype(o_ref.dtype)


def _ssm(p1, rows, b_bd, c_bd, d_row, w_glu, b_glu):
    b, s, _ = p1.shape
    ch = w_glu.shape[0]
    n_state = rows[0].shape[1]
    consts = list(rows) + list(b_bd) + list(c_bd) + [d_row, w_glu, b_glu]
    return pl.pallas_call(
        _ssm_kernel,
        grid=(s // SSM_TILE,),
        in_specs=[pl.BlockSpec((b, SSM_TILE, ch), lambda t: (0, t, 0))] + [_const_spec(c.shape) for c in consts],
        out_specs=pl.BlockSpec((b, SSM_TILE, ch), lambda t: (0, t, 0)),
        out_shape=jax.ShapeDtypeStruct((b, s, ch), BF16),
        scratch_shapes=[pltpu.VMEM((ch, 2 * n_state), BF16),
                        pltpu.VMEM((2 * n_state, ch), BF16),
                        pltpu.VMEM((SUBLANES, n_state), F32),
                        pltpu.VMEM((b, 2 * n_state), F32),
                        pltpu.VMEM((ch // LANES, b * SSM_PITCH, LANES), F32),
                        pltpu.VMEM((SSM_TILE * b, ch), F32),
                        pltpu.VMEM((SSM_TILE * b, 2 * n_state), F32),
                        pltpu.VMEM((ch // LANES, SSM_TILE * b, LANES), F32)],
        compiler_params=_params(SSM_VMEM_MIB, "arbitrary"),
        name="s5_ssm",
    )(p1, *consts)


def _rope_tables(s, rot_dim, theta):
    inv = float(theta) ** (-np.arange(0, rot_dim, 2, dtype=np.float64) / rot_dim)
    ang = np.arange(s, dtype=np.float64)[:, None] * inv[None, :]
    return np.cos(ang).astype(np.float32), np.sin(ang).astype(np.float32)


def _lane_tables(cos, sin, period):
    s, half = cos.shape
    reps = LANES // period
    one = np.ones((s, period - 2 * half), np.float32)
    zero = np.zeros((s, period - 2 * half), np.float32)
    zh = np.zeros((s, half), np.float32)
    c = np.tile(np.concatenate([cos, cos, one], axis=1), (1, reps))
    sa = np.tile(np.concatenate([-sin, zh, zero], axis=1), (1, reps))
    sb = np.tile(np.concatenate([zh, sin, zero], axis=1), (1, reps))
    return c, sa, sb


def _block_diag(blocks):
    g, r, c = blocks.shape
    eye = jnp.eye(g, dtype=blocks.dtype)
    return (eye[:, None, :, None] * blocks[:, :, None, :]).reshape(g * r, g * c)


def kernel(x, l0_mix_pre, l0_mix_post, l0_w_in, l0_lambda_q1, l0_lambda_k1, l0_lambda_q2, l0_lambda_k2, l0_subln, l0_dw_w, l0_dw_b, l0_conv_ln_g, l0_conv_ln_b, l0_w_out, l0_ffn_pre, l0_ffn_post, l0_w_gate, l0_w_up, l0_w_down, l1_mix_pre, l1_mix_post, l1_w_in, l1_a_re, l1_a_im, l1_log_dt, l1_b_re, l1_b_im, l1_c_re, l1_c_im, l1_d_skip, l1_w_glu, l1_b_glu, l1_q_norm, l1_w_uq, l1_kv_norm, l1_w_ukv, l1_w_out, l1_ffn_pre, l1_ffn_post, l1_w_gate, l1_w_up, l1_w_down):
    b, s, d = x.shape
    n = b * s
    row = lambda v: v.reshape(1, -1).astype(F32)
    bf = lambda w: w.astype(BF16)

    diff_width = 4 * LANES
    n_diff_heads = diff_width // LANES
    conv_ch = l0_dw_w.shape[1]
    ssm_ch = l1_w_glu.shape[0]
    n_groups, n_state_g = l1_a_re.shape
    q_rank = l1_q_norm.shape[0]
    kv_rank = l1_kv_norm.shape[0]
    n_mla_heads = l1_w_uq.shape[1] // (MLA_NOPE + MLA_ROPE)

    cos_a, sin_a = _rope_tables(s, DIFF_ROT, ROPE_THETA)
    tabs_a = _lane_tables(cos_a, sin_a, period=DIFF_HEAD_DIM)
    x2 = x.reshape(n, d)
    p0, qvt_a = _l0_in(x2, row(l0_mix_pre), l0_w_in.astype(F32), tabs_a, (cos_a.T, sin_a.T), b, s, diff_width)
    p0 = p0.reshape(b, s, -1)
    lam_vecs = jnp.stack([l0_lambda_q1, l0_lambda_k1, l0_lambda_q2, l0_lambda_k2]).astype(F32)
    y_a, (w_out0, w_gate0, w_up0, w_down0) = _diff_attn(
        p0, qvt_a, lam_vecs, l0_subln.reshape(-1, 1).astype(F32), n_diff_heads,
        [l0_w_out, l0_w_gate, l0_w_up, l0_w_down])
    y_b = _conv_module(p0, l0_dw_w.astype(F32), row(l0_dw_b), row(l0_conv_ln_g), row(l0_conv_ln_b),
                       col0=diff_width)

    pad = (-l1_w_in.shape[1]) % LANES
    w_in1 = bf(jnp.pad(l1_w_in, ((0, 0), (0, pad))))
    h2, p1 = _post(x2, [y_a.reshape(n, -1), y_b.reshape(n, -1)], w_out0, row(l0_mix_post),
                   row(l0_ffn_pre), w_gate0, w_up0, w_down0, row(l0_ffn_post),
                   nxt=(row(l1_mix_pre), w_in1))

    state_row = lambda a: a.reshape(1, -1).astype(F32)
    ssm_rows = (state_row(l1_a_re), state_row(l1_a_im),
                state_row(jnp.broadcast_to(l1_log_dt[:, None], (n_groups, n_state_g))))
    b_bd = tuple(_block_diag(jnp.swapaxes(m, 1, 2).astype(F32)) for m in (l1_b_re, l1_b_im))
    c_bd = tuple(_block_diag(jnp.swapaxes(m, 1, 2).astype(F32)) for m in (l1_c_re, l1_c_im))
    y_c = _ssm(p1.reshape(b, s, -1), ssm_rows, b_bd, c_bd, row(l1_d_skip), bf(l1_w_glu), row(l1_b_glu))

    wq = l1_w_uq.reshape(q_rank, n_mla_heads, MLA_NOPE + MLA_ROPE)
    wq = jnp.pad(wq, ((0, 0), (0, 0), (0, MLA_QK_PAD - MLA_NOPE - MLA_ROPE))).reshape(q_rank, -1)
    wkv = l1_w_ukv.reshape(kv_rank, n_mla_heads, MLA_NOPE + MLA_V)
    wk_nope = jnp.pad(wkv[:, :, :MLA_NOPE], ((0, 0), (0, 0), (0, MLA_QK_PAD - MLA_NOPE)))
    route = jnp.pad(jnp.eye(MLA_ROPE, dtype=F32), ((0, LANES - MLA_ROPE), (MLA_NOPE, MLA_QK_PAD - MLA_NOPE - MLA_ROPE)))
    wk_rope = jnp.broadcast_to(route[:, None, :], (LANES, n_mla_heads, MLA_QK_PAD))
    wk = jnp.concatenate([wk_nope, wk_rope], axis=0).reshape(kv_rank + LANES, -1)
    wv_t = wkv[:, :, MLA_NOPE:].reshape(kv_rank, -1).T
    cos_d, sin_d = _rope_tables(s, MLA_ROPE, MLA_ROPE_THETA)
    tabs_d = _lane_tables(cos_d, sin_d, period=LANES)
    y_d, (w_out1, w_gate1, w_up1, w_down1) = _mla_attn(
        p1.reshape(b, s, -1), row(l1_q_norm), row(l1_kv_norm), bf(wq.T), bf(wk), bf(wv_t),
        (cos_d.T, sin_d.T), tabs_d, n_mla_heads, [l1_w_out, l1_w_gate, l1_w_up, l1_w_down])

    (out,) = _post(h2, [y_c.reshape(n, -1), y_d.reshape(n, -1)], w_out1, row(l1_mix_post),
                   row(l1_ffn_pre), w_gate1, w_up1, w_down1, row(l1_ffn_post))
    return out.reshape(b, s, d)
```

```python
import functools
import math

import jax
import jax.numpy as jnp
import numpy as np
from jax import lax
from jax.experimental import pallas as pl
from jax.experimental.pallas import tpu as pltpu

F32 = jnp.float32
BF16 = jnp.bfloat16

LANES = 128
SUBLANES = 8
BF16_SUBLANES = 16
VMEM_MAX_MIB = 60
L0_IN_VMEM_MIB = 40
DIFF_ATTN_VMEM_MIB = 56
MLA_ATTN_VMEM_MIB = 56
CONV_VMEM_MIB = 16
POST_VMEM_MIB = 56
SSM_VMEM_MIB = 40

CHUNK = 64
RMS_EPS = 1e-6
LN_EPS = 1e-5
ROPE_THETA = 500000.0
MLA_ROPE_THETA = 10000.0
DIFF_HEAD_DIM = 64
DIFF_ROT = 16
CONV_WIDTH = 31
CONV_HALO = 32
SSM_GROUP = 16
SSM_STATE = 64
MLA_NOPE = 128
MLA_ROPE = 64
MLA_V = 128
MLA_QK_PAD = 256

ROW_TILE = 512
ATTN_TILE = 256
ATTN_LOOKAHEAD = 5
ATTN_SUM_ROWS = 16
LOG2_E = math.log2(math.e)
CONV_TILE = 256
SSM_TILE = 128
SSM_PITCH = SSM_TILE + SUBLANES
FFN_CHUNK = 512
POST_GROUP_ROWS = 256
POST_ROW_TILE = 512


def _params(vmem_mib, *sem):
    assert vmem_mib <= VMEM_MAX_MIB
    return pltpu.CompilerParams(dimension_semantics=sem, vmem_limit_bytes=vmem_mib * 1024 * 1024)


def _rms(x, g):
    return x * lax.rsqrt(jnp.mean(x * x, axis=-1, keepdims=True) + RMS_EPS) * g


def _sigmoid(x):
    return 1.0 / (1.0 + jnp.exp(-x))


def _dot(a, b):
    return jnp.dot(a, b, preferred_element_type=F32)


def _dot_nt(a, b):
    return lax.dot_general(a, b, (((1,), (1,)), ((), ())), preferred_element_type=F32)


def _rope_block(x, c, sa, sb, shift):
    return x * c + pltpu.roll(x, LANES - shift, 1) * sa + pltpu.roll(x, shift, 1) * sb


def _const_spec(shape):
    nd = len(shape)
    return pl.BlockSpec(shape, lambda *_: (0,) * nd, pipeline_mode=pl.Buffered(1))


def _vt_out(b, seq, width):
    tiles_per_seq = seq // ROW_TILE
    per_tile = ROW_TILE // ATTN_TILE
    spec = pl.BlockSpec((1, per_tile, width, ATTN_TILE),
                        lambda i: (i // tiles_per_seq, i % tiles_per_seq, 0, 0))
    return spec, jax.ShapeDtypeStruct((b, seq // ATTN_TILE, width, ATTN_TILE), BF16)


def _l0_in_kernel(x_ref, g_ref, win_ref, c_ref, sa_ref, sb_ref, cosq_ref, sinq_ref, o_ref, qvt_ref,
                  w_ref, wt_ref, *, n_k_blocks, q_rows, scale):
    @pl.when(pl.program_id(0) == 0)
    def _regroup_weights():
        w_ref[:, :q_rows] = win_ref[:, q_rows:2 * q_rows].astype(BF16)
        w_ref[:, q_rows:] = win_ref[:, 3 * q_rows:].astype(BF16)
        wt_ref[:q_rows, :] = win_ref[:, :q_rows].T.astype(BF16)
        wt_ref[q_rows:, :] = win_ref[:, 2 * q_rows:3 * q_rows].T.astype(BF16)

    half = DIFF_ROT // 2
    groups = [slice(r0, r0 + ATTN_TILE) for r0 in range(0, x_ref.shape[0], ATTN_TILE)]
    ts = [_rms(x_ref[rows, :], g_ref[...]).astype(BF16) for rows in groups]
    ps = [_dot(t, w_ref[...]) for t in ts]
    qv_ts = [_dot_nt(wt_ref[...], t) for t in ts]
    for j, (rows, p, qv_t) in enumerate(zip(groups, ps, qv_ts)):
        c, sa, sb = c_ref[rows, :], sa_ref[rows, :], sb_ref[rows, :]
        for blk_i in range(n_k_blocks):
            blk = _rope_block(p[:, blk_i * LANES:(blk_i + 1) * LANES], c, sa, sb, half)
            o_ref[rows, blk_i * LANES:(blk_i + 1) * LANES] = blk.astype(o_ref.dtype)
        rest = n_k_blocks * LANES
        o_ref[rows, rest:] = p[:, rest:].astype(o_ref.dtype)
        cos_t, sin_t = cosq_ref[:, rows], sinq_ref[:, rows]
        pieces = []
        for lo in range(0, q_rows, DIFF_HEAD_DIM):
            x1, x2 = qv_t[lo:lo + half, :], qv_t[lo + half:lo + 2 * half, :]
            pieces += [(x1 * cos_t - x2 * sin_t) * scale, (x2 * cos_t + x1 * sin_t) * scale,
                       qv_t[lo + 2 * half:lo + DIFF_HEAD_DIM, :] * scale]
        qvt_ref[0, j] = jnp.concatenate(pieces + [qv_t[q_rows:, :]], axis=0).astype(qvt_ref.dtype)


def _l0_in(x2, g_pre, w_in, tabs, q_tabs_t, b, seq, q_rows):
    n, d = x2.shape
    n_out = w_in.shape[1] - 2 * q_rows
    tiles_per_seq = seq // ROW_TILE
    kern = functools.partial(_l0_in_kernel, n_k_blocks=q_rows // LANES, q_rows=q_rows,
                             scale=DIFF_HEAD_DIM ** -0.5 * LOG2_E)
    tab_spec = pl.BlockSpec((ROW_TILE, LANES), lambda i: (i % tiles_per_seq, 0))
    qtab_spec = pl.BlockSpec((q_tabs_t[0].shape[0], ROW_TILE), lambda i: (0, i % tiles_per_seq))
    vt_spec, vt_shape = _vt_out(b, seq, 2 * q_rows)
    return pl.pallas_call(
        kern,
        grid=(n // ROW_TILE,),
        in_specs=[pl.BlockSpec((ROW_TILE, d), lambda i: (i, 0)),
                  _const_spec((1, d)), _const_spec(w_in.shape),
                  tab_spec, tab_spec, tab_spec, qtab_spec, qtab_spec],
        out_specs=[pl.BlockSpec((ROW_TILE, n_out), lambda i: (i, 0)), vt_spec],
        out_shape=[jax.ShapeDtypeStruct((n, n_out), BF16), vt_shape],
        scratch_shapes=[pltpu.VMEM((d, n_out), BF16), pltpu.VMEM((2 * q_rows, d), BF16)],
        compiler_params=_params(L0_IN_VMEM_MIB, "arbitrary"),
        name="l0_in",
    )(x2, g_pre, w_in, *tabs, *q_tabs_t)


def _attn_flat(q_heads_at, k_at, vt_at, finish_tile, m_ref, acc_ref, *, n_tiles, n_maps, tq, q_transposed):
    dk_axis = 0 if q_transposed else 1
    qs_cache = {}

    def qs_of(i):
        if i not in qs_cache:
            qs = []
            for q in q_heads_at(i):
                if n_maps == 1:
                    qs.append(q)
                else:
                    width = q.shape[dk_axis] // n_maps
                    pos = lax.broadcasted_iota(jnp.int32, q.shape, dk_axis)
                    for c in range(n_maps):
                        qs.append(jnp.where((pos >= c * width) & (pos < (c + 1) * width), q, jnp.zeros_like(q)))
            qs_cache[i] = qs
        return qs_cache[i]

    n_chains = len(qs_of(0))
    ones = jnp.ones((ATTN_SUM_ROWS, tq), BF16)
    kc = lax.broadcasted_iota(jnp.int32, (tq, tq), 0) // CHUNK
    qc = lax.broadcasted_iota(jnp.int32, (tq, tq), 1) // CHUNK
    keep = kc <= qc
    items = [(i, kb, ch) for i in range(n_tiles) for kb in range(i + 1) for ch in range(n_chains)]

    def scores(i, kb, ch):
        k = k_at(kb, ch // n_maps)
        return _dot(k, qs_of(i)[ch]) if q_transposed else _dot_nt(k, qs_of(i)[ch])

    ss = {n: scores(*items[n]) for n in range(min(ATTN_LOOKAHEAD, len(items)))}
    for n, (i, kb, ch) in enumerate(items):
        if n + ATTN_LOOKAHEAD < len(items):
            ss[n + ATTN_LOOKAHEAD] = scores(*items[n + ATTN_LOOKAHEAD])
        s = ss.pop(n)
        if kb == i:
            s = jnp.where(keep, s, -jnp.inf)
        vt = jnp.concatenate([vt_at(kb, ch // n_maps), ones], axis=0)
        slot = i % 2
        if kb == 0:
            m_new = jnp.max(s, axis=0, keepdims=True)
            acc_new = _dot(vt, jnp.exp2(s - m_new).astype(vt.dtype))
        else:
            m_old = m_ref[slot, ch]
            m_new = jnp.maximum(m_old, jnp.max(s, axis=0, keepdims=True))
            acc_new = (jnp.exp2(m_old - m_new) * acc_ref[slot, ch]
                       + _dot(vt, jnp.exp2(s - m_new).astype(vt.dtype)))
        m_ref[slot, ch] = m_new
        acc_ref[slot, ch] = acc_new
        if kb == i and ch == n_chains - 1:
            finish_tile(i, slot)
            del qs_cache[i]


def _attn_flat_scratch(n_chains, dv):
    return [pltpu.VMEM((2, n_chains, 1, ATTN_TILE), F32),
            pltpu.VMEM((2, n_chains, dv + ATTN_SUM_ROWS, ATTN_TILE), F32)]


def _attn_out(acc_ref, ch, dv):
    return acc_ref[ch, :dv, :] / acc_ref[ch, dv:dv + 1, :]


def _cast_specs(weights, n_steps):
    specs = []
    for w in weights:
        rows = w.shape[0] // n_steps
        assert rows * n_steps == w.shape[0] and rows % BF16_SUBLANES == 0, w.shape
        specs.append(pl.BlockSpec((rows, w.shape[1]), lambda bb: (bb, 0)))
    return specs, [jax.ShapeDtypeStruct(w.shape, BF16) for w in weights]


def _cast_blocks(src_refs, dst_refs):
    for src, dst in zip(src_refs, dst_refs):
        dst[...] = src[...].astype(dst.dtype)


def _split_cast_refs(refs, n_in, n_out, n_cast):
    main_in, cast_src = refs[:n_in], refs[n_in:n_in + n_cast]
    rest = refs[n_in + n_cast:]
    return main_in, cast_src, rest[:n_out], rest[n_out:n_out + n_cast], rest[n_out + n_cast:]


def _diff_attn_kernel(*refs, n_heads, lam_init, n_cast):
    (qt_ref, k_ref, vt_ref, lam_ref, sub_ref), cast_src, (o_ref,), cast_dst, (m_ref, acc_ref) = (
        _split_cast_refs(refs, 5, 1, n_cast))
    _cast_blocks(cast_src, cast_dst)
    n_tiles, tq = qt_ref.shape[1], qt_ref.shape[3]
    lv = lam_ref[...]
    lam = (jnp.exp(jnp.sum(lv[0:1] * lv[1:2], axis=-1, keepdims=True))
           - jnp.exp(jnp.sum(lv[2:3] * lv[3:4], axis=-1, keepdims=True)) + lam_init)

    def finish_tile(i, slot):
        for h in range(n_heads):
            o_t = _attn_out(acc_ref.at[slot], 2 * h, LANES) - lam * _attn_out(acc_ref.at[slot], 2 * h + 1, LANES)
            inv = lax.rsqrt(jnp.mean(o_t * o_t, axis=0, keepdims=True) + RMS_EPS)
            y_t = o_t * inv * sub_ref[...] * (1.0 - lam_init)
            o_ref[0, i * tq:(i + 1) * tq, h * LANES:(h + 1) * LANES] = y_t.T.astype(o_ref.dtype)

    _attn_flat(lambda i: [qt_ref[0, i, h * LANES:(h + 1) * LANES, :] for h in range(n_heads)],
               lambda kb, h: k_ref[0, kb * tq:(kb + 1) * tq, h * LANES:(h + 1) * LANES],
               lambda kb, h: vt_ref[0, kb, h * LANES:(h + 1) * LANES, :],
               finish_tile, m_ref, acc_ref, n_tiles=n_tiles, n_maps=2, tq=tq, q_transposed=True)


def _diff_attn(p0, qvt, lam_vecs, subln_col, n_heads, cast_weights):
    b, s, _ = p0.shape
    n_kb = s // ATTN_TILE
    width = n_heads * LANES
    cast_specs, cast_shapes = _cast_specs(cast_weights, b)
    kern = functools.partial(_diff_attn_kernel, n_heads=n_heads, lam_init=0.8 - 0.6 * math.exp(-0.3 * 0),
                             n_cast=len(cast_weights))
    out = pl.pallas_call(
        kern,
        grid=(b,),
        in_specs=[pl.BlockSpec((1, n_kb, width, ATTN_TILE), lambda bb: (bb, 0, 0, 0)),
                  pl.BlockSpec((1, s, width), lambda bb: (bb, 0, 0)),
                  pl.BlockSpec((1, n_kb, width, ATTN_TILE), lambda bb: (bb, 0, 1, 0)),
                  pl.BlockSpec(lam_vecs.shape, lambda bb: (0, 0)),
                  pl.BlockSpec(subln_col.shape, lambda bb: (0, 0))] + cast_specs,
        out_specs=[pl.BlockSpec((1, s, width), lambda bb: (bb, 0, 0))] + cast_specs,
        out_shape=[jax.ShapeDtypeStruct((b, s, width), BF16)] + cast_shapes,
        scratch_shapes=_attn_flat_scratch(2 * n_heads, LANES),
        compiler_params=_params(DIFF_ATTN_VMEM_MIB, "arbitrary"),
        name="diff_attn",
    )(qvt, p0, qvt, lam_vecs, subln_col, *cast_weights)
    return out[0], out[1:]


def _mla_attn_kernel(*refs, n_heads, scale, n_cast):
    ((cq_ref, kv_ref, qn_ref, kvn_ref, wqt_ref, wk_ref, wvt_ref, cosq_tab, sinq_tab, ck_tab, sak_tab, sbk_tab),
     cast_src, (o_ref,), cast_dst, (m_ref, acc_ref, k_scr, vt_scr)) = (
        _split_cast_refs(refs, 12, 1, n_cast))
    _cast_blocks(cast_src, cast_dst)
    seq = kv_ref.shape[1]
    tq = ATTN_TILE
    half = MLA_ROPE // 2
    projected = set()

    def project_keys(chunk):
        if chunk in projected:
            return
        projected.add(chunk)
        rows = slice(chunk * ROW_TILE, (chunk + 1) * ROW_TILE)
        ckv = _rms(kv_ref[0, rows, :LANES], kvn_ref[...]).astype(BF16)
        kr = _rope_block(kv_ref[0, rows, LANES:], ck_tab[rows, :], sak_tab[rows, :], sbk_tab[rows, :], half)
        k_scr[rows, :] = _dot(jnp.concatenate([ckv, kr.astype(BF16)], axis=-1), wk_ref[...]).astype(BF16)
        vt = _dot_nt(wvt_ref[...], ckv)
        for j in range(ROW_TILE // tq):
            vt_scr[chunk * (ROW_TILE // tq) + j] = vt[:, j * tq:(j + 1) * tq].astype(BF16)

    def q_heads_at(i):
        cols = slice(i * tq, (i + 1) * tq)
        q_t = _dot_nt(wqt_ref[...], _rms(cq_ref[0, cols, :], qn_ref[...]).astype(BF16))
        cos_t, sin_t = cosq_tab[:, cols], sinq_tab[:, cols]
        q_heads = []
        for h in range(n_heads):
            lo = h * MLA_QK_PAD
            x1 = q_t[lo + MLA_NOPE:lo + MLA_NOPE + half, :]
            x2 = q_t[lo + MLA_NOPE + half:lo + MLA_NOPE + 2 * half, :]
            q_h = jnp.concatenate([q_t[lo:lo + MLA_NOPE, :], x1 * cos_t - x2 * sin_t, x2 * cos_t + x1 * sin_t,
                                   q_t[lo + MLA_NOPE + 2 * half:lo + MLA_QK_PAD, :]], axis=0)
            q_heads.append((q_h * scale).astype(BF16))
        return q_heads

    def k_at(kb, h):
        project_keys(kb * tq // ROW_TILE)
        return k_scr[kb * tq:(kb + 1) * tq, h * MLA_QK_PAD:(h + 1) * MLA_QK_PAD]

    def finish_tile(i, slot):
        for h in range(n_heads):
            o_ref[0, i * tq:(i + 1) * tq, h * MLA_V:(h + 1) * MLA_V] = (
                _attn_out(acc_ref.at[slot], h, MLA_V).T.astype(o_ref.dtype))

    _attn_flat(q_heads_at, k_at, lambda kb, h: vt_scr[kb, h * MLA_V:(h + 1) * MLA_V, :],
               finish_tile, m_ref, acc_ref, n_tiles=seq // tq, n_maps=1, tq=tq, q_transposed=True)


def _mla_attn(p1, q_norm, kv_norm, wq_t, wk, wv_t, q_tabs_t, tabs, n_heads, cast_weights):
    b, s, _ = p1.shape
    n_kb = s // ATTN_TILE
    cast_specs, cast_shapes = _cast_specs(cast_weights, b)
    kern = functools.partial(_mla_attn_kernel, n_heads=n_heads, scale=(MLA_NOPE + MLA_ROPE) ** -0.5 * LOG2_E,
                             n_cast=len(cast_weights))
    const = lambda a: pl.BlockSpec(a.shape, lambda bb: (0,) * a.ndim)
    out = pl.pallas_call(
        kern,
        grid=(b,),
        in_specs=[pl.BlockSpec((1, s, 2 * LANES), lambda bb: (bb, 0, 1)),
                  pl.BlockSpec((1, s, 2 * LANES), lambda bb: (bb, 0, 2)),
                  const(q_norm), const(kv_norm), const(wq_t), const(wk), const(wv_t),
                  const(q_tabs_t[0]), const(q_tabs_t[1]),
                  const(tabs[0]), const(tabs[1]), const(tabs[2])] + cast_specs,
        out_specs=[pl.BlockSpec((1, s, n_heads * MLA_V), lambda bb: (bb, 0, 0))] + cast_specs,
        out_shape=[jax.ShapeDtypeStruct((b, s, n_heads * MLA_V), BF16)] + cast_shapes,
        scratch_shapes=_attn_flat_scratch(n_heads, MLA_V) + [
            pltpu.VMEM((s, n_heads * MLA_QK_PAD), BF16),
            pltpu.VMEM((n_kb, n_heads * MLA_V, ATTN_TILE), BF16)],
        compiler_params=_params(MLA_ATTN_VMEM_MIB, "arbitrary"),
        name="mla_attn",
    )(p1, p1, q_norm, kv_norm, wq_t, wk, wv_t, *q_tabs_t, *tabs, *cast_weights)
    return out[0], out[1:]


def _conv_kernel(a_ref, gate_ref, ah_ref, gh_ref, w_ref, b_ref, lg_ref, lb_ref, o_ref, u_ref, ur_ref):
    tt = a_ref.shape[1]
    u_ref[CONV_HALO:, :] = a_ref[0].astype(F32) * _sigmoid(gate_ref[0].astype(F32))
    halo = ah_ref[0].astype(F32) * _sigmoid(gh_ref[0].astype(F32))
    u_ref[:CONV_HALO, :] = jnp.where(pl.program_id(1) > 0, halo, jnp.zeros_like(halo))
    rows = ur_ref.shape[1]
    for r in range(1, SUBLANES):
        ur_ref[r - 1] = u_ref[r:r + rows, :]
    acc = jnp.zeros((tt, a_ref.shape[2]), F32)
    first = CONV_HALO - (CONV_WIDTH - 1)
    for k in range(CONV_WIDTH):
        base, r = divmod(first + k, SUBLANES)
        src = u_ref if r == 0 else ur_ref.at[r - 1]
        acc = acc + src[base * SUBLANES:base * SUBLANES + tt, :] * w_ref[k:k + 1, :]
    y = acc + b_ref[...]
    mu = jnp.mean(y, axis=-1, keepdims=True)
    yc = y - mu
    yn = yc * lax.rsqrt(jnp.mean(yc * yc, axis=-1, keepdims=True) + LN_EPS) * lg_ref[...] + lb_ref[...]
    o_ref[0] = (yn * _sigmoid(yn)).astype(o_ref.dtype)


def _conv_module(p0, dw_w, dw_b, ln_g, ln_b, col0):
    b, s, _ = p0.shape
    c = dw_w.shape[1]
    a_blk, g_blk = col0 // c, col0 // c + 1
    ratio = CONV_TILE // CONV_HALO
    main = lambda blk: pl.BlockSpec((1, CONV_TILE, c), lambda bb, t: (bb, t, blk))
    halo = lambda blk: pl.BlockSpec((1, CONV_HALO, c), lambda bb, t: (bb, jnp.maximum(t * ratio - 1, 0), blk))
    vec = lambda a: pl.BlockSpec(a.shape, lambda bb, t: (0, 0))
    return pl.pallas_call(
        _conv_kernel,
        grid=(b, s // CONV_TILE),
        in_specs=[main(a_blk), main(g_blk), halo(a_blk), halo(g_blk),
                  vec(dw_w), vec(dw_b), vec(ln_g), vec(ln_b)],
        out_specs=pl.BlockSpec((1, CONV_TILE, c), lambda bb, t: (bb, t, 0)),
        out_shape=jax.ShapeDtypeStruct((b, s, c), BF16),
        scratch_shapes=[pltpu.VMEM((CONV_HALO + CONV_TILE, c), F32),
                        pltpu.VMEM((SUBLANES - 1, CONV_HALO + CONV_TILE - SUBLANES, c), F32)],
        compiler_params=_params(CONV_VMEM_MIB, "parallel", "parallel"),
        name="conv_module",
    )(p0, p0, p0, p0, dw_w, dw_b, ln_g, ln_b)


def _post_kernel(*refs, n_mix, with_next):
    x_ref = refs[0]
    mix_refs = refs[1:1 + n_mix]
    (wo_ref, gpost_ref, gfpre_ref, wg_ref, wu_ref, wd_ref, gfpost_ref) = refs[1 + n_mix:8 + n_mix]
    pos = 8 + n_mix
    if with_next:
        gnext_ref, wnext_ref = refs[pos:pos + 2]
        pos += 2
    h_out_ref = refs[pos]
    pos += 1
    if with_next:
        p_out_ref = refs[pos]
        pos += 1
    hid_ref = refs[pos]

    n_rows = x_ref.shape[0]
    groups = [slice(r0, r0 + POST_GROUP_ROWS) for r0 in range(0, n_rows, POST_GROUP_ROWS)]

    def out_proj(rows):
        y = None
        row = 0
        for r in mix_refs:
            w = r.shape[-1]
            part = _dot(r[rows, :], wo_ref[row:row + w, :])
            y = part if y is None else y + part
            row += w
        return y

    ys = [out_proj(rows) for rows in groups]
    h1s = [x_ref[rows, :] + _rms(y, gpost_ref[...]) for rows, y in zip(groups, ys)]
    ts = [_rms(h1, gfpre_ref[...]).astype(BF16) for h1 in h1s]
    d_ff = wg_ref.shape[1]
    for j in range(0, d_ff, FFN_CHUNK):
        wdt = min(FFN_CHUNK, d_ff - j)
        for rows, t in zip(groups, ts):
            gate = _dot(t, wg_ref[:, j:j + wdt])
            up = _dot(t, wu_ref[:, j:j + wdt])
            hid_ref[rows, j:j + wdt] = (gate * _sigmoid(gate) * up).astype(BF16)
    fs = [_dot(hid_ref[rows, :], wd_ref[...]) for rows in groups]
    h2s = [h1 + _rms(f, gfpost_ref[...]) for h1, f in zip(h1s, fs)]
    for rows, h2 in zip(groups, h2s):
        h_out_ref[rows, :] = h2
    if with_next:
        t2s = [_rms(h2, gnext_ref[...]).astype(BF16) for h2 in h2s]
        for rows, t2 in zip(groups, t2s):
            p_out_ref[rows, :] = _dot(t2, wnext_ref[...])


def _post(x2, mix_parts, w_out, g_post, g_fpre, w_gate, w_up, w_down, g_fpost, nxt=None):
    n, d = x2.shape
    d_ff = w_gate.shape[1]
    row_tile = POST_ROW_TILE
    row = lambda a: pl.BlockSpec((row_tile, a.shape[1]), lambda i: (i, 0))
    consts = [w_out, g_post, g_fpre, w_gate, w_up, w_down, g_fpost] + (list(nxt) if nxt else [])
    out_shape = [jax.ShapeDtypeStruct((n, d), F32)]
    out_specs = [pl.BlockSpec((row_tile, d), lambda i: (i, 0))]
    if nxt:
        n_next = nxt[1].shape[1]
        out_shape.append(jax.ShapeDtypeStruct((n, n_next), F32))
        out_specs.append(pl.BlockSpec((row_tile, n_next), lambda i: (i, 0)))
    kern = functools.partial(_post_kernel, n_mix=len(mix_parts), with_next=bool(nxt))
    return pl.pallas_call(
        kern,
        grid=(n // row_tile,),
        in_specs=[row(x2)] + [row(m) for m in mix_parts] + [_const_spec(c.shape) for c in consts],
        out_specs=out_specs,
        out_shape=out_shape,
        scratch_shapes=[pltpu.VMEM((row_tile, d_ff), BF16)],
        compiler_params=_params(POST_VMEM_MIB, "parallel"),
        name="post_next" if nxt else "post",
    )(x2, *mix_parts, *consts)


def _gelu_tanh(x):
    return 0.5 * x * (1.0 + jnp.tanh(math.sqrt(2.0 / math.pi) * (x + 0.044715 * (x * x * x))))


def _ssm_kernel(u_ref, lr_ref, li_ref, ldt_ref, bre_ref, bim_ref, cre_ref, cim_ref, d_ref, wg_ref, bg_ref,
                o_ref, bmat_ref, cmat_ref, a_ref, st_ref, us_ref, utm_ref, x_ref, y_ref):
    nb, tt, ch = u_ref.shape
    n_state = lr_ref.shape[1]

    @pl.when(pl.program_id(0) == 0)
    def _init():
        lr, li = lr_ref[...], li_ref[...]
        dt = jnp.exp(ldt_ref[...])
        mag = jnp.exp(lr * dt)
        ab_re = mag * jnp.cos(li * dt)
        ab_im = mag * jnp.sin(li * dt)
        den = lr * lr + li * li
        n_re = ab_re - 1.0
        f_re = (n_re * lr + ab_im * li) / den
        f_im = (ab_im * lr - n_re * li) / den
        br, bi = bre_ref[...], bim_ref[...]
        bmat_ref[:, :n_state] = (f_re * br - f_im * bi).astype(BF16)
        bmat_ref[:, n_state:] = (f_re * bi + f_im * br).astype(BF16)
        cmat_ref[:n_state, :] = cre_ref[...].astype(BF16)
        cmat_ref[n_state:, :] = (-cim_ref[...]).astype(BF16)
        a_ref[0:1, :] = ab_re
        a_ref[1:2, :] = ab_im
        st_ref[...] = jnp.zeros(st_ref.shape, F32)

    n_ublk = ch // LANES
    for b in range(nb):
        for j in range(n_ublk):
            us_ref[j, b * SSM_PITCH:b * SSM_PITCH + tt, :] = u_ref[b, :, j * LANES:(j + 1) * LANES]

    def gather_step(t, carry):
        dst = pl.multiple_of(t * nb, nb)
        for j in range(n_ublk):
            utm_ref[pl.ds(dst, nb), j * LANES:(j + 1) * LANES] = us_ref[j, pl.ds(t, nb, stride=SSM_PITCH), :]
        return carry

    lax.fori_loop(0, tt, gather_step, 0, unroll=8)

    half_rows = tt * nb // 2
    for r0 in (0, half_rows):
        x_ref[r0:r0 + half_rows, :] = _dot(utm_ref[r0:r0 + half_rows, :].astype(BF16), bmat_ref[...])

    a_re = jnp.broadcast_to(a_ref[0:1, :], (nb, n_state))
    a_im = jnp.broadcast_to(a_ref[1:2, :], (nb, n_state))

    def scan_step(t, carry):
        x_re, x_im = carry
        row = pl.multiple_of(t * nb, nb)
        n_re = a_re * x_re - a_im * x_im + x_ref[pl.ds(row, nb), :n_state]
        n_im = a_re * x_im + a_im * x_re + x_ref[pl.ds(row, nb), n_state:]
        x_ref[pl.ds(row, nb), :n_state] = n_re
        x_ref[pl.ds(row, nb), n_state:] = n_im
        return n_re, n_im

    x_re, x_im = lax.fori_loop(0, tt, scan_step, (st_ref[:, :n_state], st_ref[:, n_state:]), unroll=4)
    st_ref[:, :n_state] = x_re
    st_ref[:, n_state:] = x_im

    n_yblk = ch // LANES
    for r0 in (0, half_rows):
        y_tm = _dot(x_ref[r0:r0 + half_rows, :].astype(BF16), cmat_ref[...])
        for j in range(n_yblk):
            y_ref[j, r0:r0 + half_rows, :] = y_tm[:, j * LANES:(j + 1) * LANES]
    for b in range(nb):
        y = jnp.concatenate([y_ref[j, pl.ds(b, tt, stride=nb), :] for j in range(n_yblk)], axis=-1)
        y = y + d_ref[...] * u_ref[b]
        z = _gelu_tanh(y)
        gate = _dot(z.astype(BF16), wg_ref[...]) + bg_ref[...]
        o_ref[b] = (z * _sigmoid(gate)).astype(o_ref.dtype)


def _ssm(p1, rows, b_bd, c_bd, d_row, w_glu, b_glu):
    b, s, _ = p1.shape
    ch = w_glu.shape[0]
    n_state = rows[0].shape[1]
    consts = list(rows) + list(b_bd) + list(c_bd) + [d_row, w_glu, b_glu]
    return pl.pallas_call(
        _ssm_kernel,
        grid=(s // SSM_TILE,),
        in_specs=[pl.BlockSpec((b, SSM_TILE, ch), lambda t: (0, t, 0))] + [_const_spec(c.shape) for c in consts],
        out_specs=pl.BlockSpec((b, SSM_TILE, ch), lambda t: (0, t, 0)),
        out_shape=jax.ShapeDtypeStruct((b, s, ch), BF16),
        scratch_shapes=[pltpu.VMEM((ch, 2 * n_state), BF16),
                        pltpu.VMEM((2 * n_state, ch), BF16),
                        pltpu.VMEM((SUBLANES, n_state), F32),
                        pltpu.VMEM((b, 2 * n_state), F32),
                        pltpu.VMEM((ch // LANES, b * SSM_PITCH, LANES), F32),
                        pltpu.VMEM((SSM_TILE * b, ch), F32),
                        pltpu.VMEM((SSM_TILE * b, 2 * n_state), F32),
                        pltpu.VMEM((ch // LANES, SSM_TILE * b, LANES), F32)],
        compiler_params=_params(SSM_VMEM_MIB, "arbitrary"),
        name="s5_ssm",
    )(p1, *consts)


def _rope_tables(s, rot_dim, theta):
    inv = float(theta) ** (-np.arange(0, rot_dim, 2, dtype=np.float64) / rot_dim)
    ang = np.arange(s, dtype=np.float64)[:, None] * inv[None, :]
    return np.cos(ang).astype(np.float32), np.sin(ang).astype(np.float32)


def _lane_tables(cos, sin, period):
    s, half = cos.shape
    reps = LANES // period
    one = np.ones((s, period - 2 * half), np.float32)
    zero = np.zeros((s, period - 2 * half), np.float32)
    zh = np.zeros((s, half), np.float32)
    c = np.tile(np.concatenate([cos, cos, one], axis=1), (1, reps))
    sa = np.tile(np.concatenate([-sin, zh, zero], axis=1), (1, reps))
    sb = np.tile(np.concatenate([zh, sin, zero], axis=1), (1, reps))
    return c, sa, sb


def _block_diag(blocks):
    g, r, c = blocks.shape
    eye = jnp.eye(g, dtype=blocks.dtype)
    return (eye[:, None, :, None] * blocks[:, :, None, :]).reshape(g * r, g * c)


def kernel(x, l0_mix_pre, l0_mix_post, l0_w_in, l0_lambda_q1, l0_lambda_k1, l0_lambda_q2, l0_lambda_k2, l0_subln, l0_dw_w, l0_dw_b, l0_conv_ln_g, l0_conv_ln_b, l0_w_out, l0_ffn_pre, l0_ffn_post, l0_w_gate, l0_w_up, l0_w_down, l1_mix_pre, l1_mix_post, l1_w_in, l1_a_re, l1_a_im, l1_log_dt, l1_b_re, l1_b_im, l1_c_re, l1_c_im, l1_d_skip, l1_w_glu, l1_b_glu, l1_q_norm, l1_w_uq, l1_kv_norm, l1_w_ukv, l1_w_out, l1_ffn_pre, l1_ffn_post, l1_w_gate, l1_w_up, l1_w_down):
    b, s, d = x.shape
    n = b * s
    row = lambda v: v.reshape(1, -1).astype(F32)
    bf = lambda w: w.astype(BF16)

    diff_width = 4 * LANES
    n_diff_heads = diff_width // LANES
    conv_ch = l0_dw_w.shape[1]
    ssm_ch = l1_w_glu.shape[0]
    n_groups, n_state_g = l1_a_re.shape
    q_rank = l1_q_norm.shape[0]
    kv_rank = l1_kv_norm.shape[0]
    n_mla_heads = l1_w_uq.shape[1] // (MLA_NOPE + MLA_ROPE)

    cos_a, sin_a = _rope_tables(s, DIFF_ROT, ROPE_THETA)
    tabs_a = _lane_tables(cos_a, sin_a, period=DIFF_HEAD_DIM)
    x2 = x.reshape(n, d)
    p0, qvt_a = _l0_in(x2, row(l0_mix_pre), l0_w_in.astype(F32), tabs_a, (cos_a.T, sin_a.T), b, s, diff_width)
    p0 = p0.reshape(b, s, -1)
    lam_vecs = jnp.stack([l0_lambda_q1, l0_lambda_k1, l0_lambda_q2, l0_lambda_k2]).astype(F32)
    y_a, (w_out0, w_gate0, w_up0, w_down0) = _diff_attn(
        p0, qvt_a, lam_vecs, l0_subln.reshape(-1, 1).astype(F32), n_diff_heads,
        [l0_w_out, l0_w_gate, l0_w_up, l0_w_down])
    y_b = _conv_module(p0, l0_dw_w.astype(F32), row(l0_dw_b), row(l0_conv_ln_g), row(l0_conv_ln_b),
                       col0=diff_width)

    pad = (-l1_w_in.shape[1]) % LANES
    w_in1 = bf(jnp.pad(l1_w_in, ((0, 0), (0, pad))))
    h2, p1 = _post(x2, [y_a.reshape(n, -1), y_b.reshape(n, -1)], w_out0, row(l0_mix_post),
                   row(l0_ffn_pre), w_gate0, w_up0, w_down0, row(l0_ffn_post),
                   nxt=(row(l1_mix_pre), w_in1))

    state_row = lambda a: a.reshape(1, -1).astype(F32)
    ssm_rows = (state_row(l1_a_re), state_row(l1_a_im),
                state_row(jnp.broadcast_to(l1_log_dt[:, None], (n_groups, n_state_g))))
    b_bd = tuple(_block_diag(jnp.swapaxes(m, 1, 2).astype(F32)) for m in (l1_b_re, l1_b_im))
    c_bd = tuple(_block_diag(jnp.swapaxes(m, 1, 2).astype(F32)) for m in (l1_c_re, l1_c_im))
    y_c = _ssm(p1.reshape(b, s, -1), ssm_rows, b_bd, c_bd, row(l1_d_skip), bf(l1_w_glu), row(l1_b_glu))

    wq = l1_w_uq.reshape(q_rank, n_mla_heads, MLA_NOPE + MLA_ROPE)
    wq = jnp.pad(wq, ((0, 0), (0, 0), (0, MLA_QK_PAD - MLA_NOPE - MLA_ROPE))).reshape(q_rank, -1)
    wkv = l1_w_ukv.reshape(kv_rank, n_mla_heads, MLA_NOPE + MLA_V)
    wk_nope = jnp.pad(wkv[:, :, :MLA_NOPE], ((0, 0), (0, 0), (0, MLA_QK_PAD - MLA_NOPE)))
    route = jnp.pad(jnp.eye(MLA_ROPE, dtype=F32), ((0, LANES - MLA_ROPE), (MLA_NOPE, MLA_QK_PAD - MLA_NOPE - MLA_ROPE)))
    wk_rope = jnp.broadcast_to(route[:, None, :], (LANES, n_mla_heads, MLA_QK_PAD))
    wk = jnp.concatenate([wk_nope, wk_rope], axis=0).reshape(kv_rank + LANES, -1)
    wv_t = wkv[:, :, MLA_NOPE:].reshape(kv_rank, -1).T
    cos_d, sin_d = _rope_tables(s, MLA_ROPE, MLA_ROPE_THETA)
    tabs_d = _lane_tables(cos_d, sin_d, period=LANES)
    y_d, (w_out1, w_gate1, w_up1, w_down1) = _mla_attn(
        p1.reshape(b, s, -1), row(l1_q_norm), row(l1_kv_norm), bf(wq.T), bf(wk), bf(wv_t),
        (cos_d.T, sin_d.T), tabs_d, n_mla_heads, [l1_w_out, l1_w_gate, l1_w_up, l1_w_down])

    (out,) = _post(h2, [y_c.reshape(n, -1), y_d.reshape(n, -1)], w_out1, row(l1_mix_post),
                   row(l1_ffn_pre), w_gate1, w_up1, w_down1, row(l1_ffn_post))
    return out.reshape(b, s, d)
```

```python
import functools
import math

import jax
import jax.numpy as jnp
import numpy as np
from jax import lax
from jax.experimental import pallas as pl
from jax.experimental.pallas import tpu as pltpu

F32 = jnp.float32
BF16 = jnp.bfloat16

LANES = 128
SUBLANES = 8
BF16_SUBLANES = 16
VMEM_MAX_MIB = 60
L0_IN_VMEM_MIB = 40
DIFF_ATTN_VMEM_MIB = 56
MLA_ATTN_VMEM_MIB = 56
CONV_VMEM_MIB = 56
POST_VMEM_MIB = 56
SSM_VMEM_MIB = 56

CHUNK = 64
RMS_EPS = 1e-6
LN_EPS = 1e-5
ROPE_THETA = 500000.0
MLA_ROPE_THETA = 10000.0
DIFF_HEAD_DIM = 64
DIFF_ROT = 16
CONV_WIDTH = 31
CONV_HALO = 32
MLA_NOPE = 128
MLA_ROPE = 64
MLA_V = 128
MLA_QK_PAD = 256

ROW_TILE = 512
ATTN_TILE = 256
DIFF_ATTN_LOOKAHEAD = 5
MLA_ATTN_LOOKAHEAD = 6
ATTN_SUM_ROWS = 16
LOG2_E = math.log2(math.e)
CONV_TILE = 512
SSM_TILE = 256
SSM_PITCH = SSM_TILE + SUBLANES
FFN_CHUNK = 512
POST_GROUP_ROWS = 256
POST_ROW_TILE = 512


def _params(vmem_mib, *sem):
    assert vmem_mib <= VMEM_MAX_MIB
    return pltpu.CompilerParams(dimension_semantics=sem, vmem_limit_bytes=vmem_mib * 1024 * 1024)


def _rms(x, g):
    return x * lax.rsqrt(jnp.mean(x * x, axis=-1, keepdims=True) + RMS_EPS) * g


def _sigmoid(x):
    return 1.0 / (1.0 + jnp.exp(-x))


def _dot(a, b):
    return jnp.dot(a, b, preferred_element_type=F32)


def _dot_nt(a, b):
    return lax.dot_general(a, b, (((1,), (1,)), ((), ())), preferred_element_type=F32)


def _rope_block(x, c, sa, sb, shift):
    return x * c + pltpu.roll(x, LANES - shift, 1) * sa + pltpu.roll(x, shift, 1) * sb


def _const_spec(shape):
    nd = len(shape)
    return pl.BlockSpec(shape, lambda *_: (0,) * nd, pipeline_mode=pl.Buffered(1))


def _vt_out(b, seq, width):
    tiles_per_seq = seq // ROW_TILE
    per_tile = ROW_TILE // ATTN_TILE
    spec = pl.BlockSpec((1, per_tile, width, ATTN_TILE),
                        lambda i: (i // tiles_per_seq, i % tiles_per_seq, 0, 0))
    return spec, jax.ShapeDtypeStruct((b, seq // ATTN_TILE, width, ATTN_TILE), BF16)


def _l0_in_kernel(x_ref, g_ref, win_ref, c_ref, sa_ref, sb_ref, cosq_ref, sinq_ref, o_ref, qvt_ref,
                  w_ref, wt_ref, *, n_k_blocks, q_rows, scale):
    @pl.when(pl.program_id(0) == 0)
    def _regroup_weights():
        w_ref[:, :q_rows] = win_ref[:, q_rows:2 * q_rows].astype(BF16)
        w_ref[:, q_rows:] = win_ref[:, 3 * q_rows:].astype(BF16)
        wt_ref[:q_rows, :] = win_ref[:, :q_rows].T.astype(BF16)
        wt_ref[q_rows:, :] = win_ref[:, 2 * q_rows:3 * q_rows].T.astype(BF16)

    half = DIFF_ROT // 2
    groups = [slice(r0, r0 + ATTN_TILE) for r0 in range(0, x_ref.shape[0], ATTN_TILE)]
    ts = [_rms(x_ref[rows, :], g_ref[...]).astype(BF16) for rows in groups]
    ps = [_dot(t, w_ref[...]) for t in ts]
    qv_ts = [_dot_nt(wt_ref[...], t) for t in ts]
    for j, (rows, p, qv_t) in enumerate(zip(groups, ps, qv_ts)):
        c, sa, sb = c_ref[rows, :], sa_ref[rows, :], sb_ref[rows, :]
        for blk_i in range(n_k_blocks):
            blk = _rope_block(p[:, blk_i * LANES:(blk_i + 1) * LANES], c, sa, sb, half)
            o_ref[rows, blk_i * LANES:(blk_i + 1) * LANES] = blk.astype(o_ref.dtype)
        rest = n_k_blocks * LANES
        o_ref[rows, rest:] = p[:, rest:].astype(o_ref.dtype)
        cos_t, sin_t = cosq_ref[:, rows], sinq_ref[:, rows]
        pieces = []
        for lo in range(0, q_rows, DIFF_HEAD_DIM):
            x1, x2 = qv_t[lo:lo + half, :], qv_t[lo + half:lo + 2 * half, :]
            pieces += [(x1 * cos_t - x2 * sin_t) * scale, (x2 * cos_t + x1 * sin_t) * scale,
                       qv_t[lo + 2 * half:lo + DIFF_HEAD_DIM, :] * scale]
        qvt_ref[0, j] = jnp.concatenate(pieces + [qv_t[q_rows:, :]], axis=0).astype(qvt_ref.dtype)


def _l0_in(x2, g_pre, w_in, tabs, q_tabs_t, b, seq, q_rows):
    n, d = x2.shape
    n_out = w_in.shape[1] - 2 * q_rows
    tiles_per_seq = seq // ROW_TILE
    kern = functools.partial(_l0_in_kernel, n_k_blocks=q_rows // LANES, q_rows=q_rows,
                             scale=DIFF_HEAD_DIM ** -0.5 * LOG2_E)
    tab_spec = pl.BlockSpec((ROW_TILE, LANES), lambda i: (i % tiles_per_seq, 0))
    qtab_spec = pl.BlockSpec((q_tabs_t[0].shape[0], ROW_TILE), lambda i: (0, i % tiles_per_seq))
    vt_spec, vt_shape = _vt_out(b, seq, 2 * q_rows)
    return pl.pallas_call(
        kern,
        grid=(n // ROW_TILE,),
        in_specs=[pl.BlockSpec((ROW_TILE, d), lambda i: (i, 0)),
                  _const_spec((1, d)), _const_spec(w_in.shape),
                  tab_spec, tab_spec, tab_spec, qtab_spec, qtab_spec],
        out_specs=[pl.BlockSpec((ROW_TILE, n_out), lambda i: (i, 0)), vt_spec],
        out_shape=[jax.ShapeDtypeStruct((n, n_out), BF16), vt_shape],
        scratch_shapes=[pltpu.VMEM((d, n_out), BF16), pltpu.VMEM((2 * q_rows, d), BF16)],
        compiler_params=_params(L0_IN_VMEM_MIB, "arbitrary"),
        name="l0_in",
    )(x2, g_pre, w_in, *tabs, *q_tabs_t)


def _attn_flat(q_heads_at, k_at, vt_at, finish_tile, m_ref, acc_ref, *, n_tiles, n_maps, tq, q_transposed,
               lookahead):
    dk_axis = 0 if q_transposed else 1
    qs_cache = {}

    def qs_of(i):
        if i not in qs_cache:
            qs = []
            for q in q_heads_at(i):
                if n_maps == 1:
                    qs.append(q)
                else:
                    width = q.shape[dk_axis] // n_maps
                    pos = lax.broadcasted_iota(jnp.int32, q.shape, dk_axis)
                    for c in range(n_maps):
                        qs.append(jnp.where((pos >= c * width) & (pos < (c + 1) * width), q, jnp.zeros_like(q)))
            qs_cache[i] = qs
        return qs_cache[i]

    n_chains = len(qs_of(0))
    ones = jnp.ones((ATTN_SUM_ROWS, tq), BF16)
    kc = lax.broadcasted_iota(jnp.int32, (tq, tq), 0) // CHUNK
    qc = lax.broadcasted_iota(jnp.int32, (tq, tq), 1) // CHUNK
    keep = kc <= qc
    items = [(i, kb, ch) for i in range(n_tiles) for kb in range(i + 1) for ch in range(n_chains)]

    def scores(i, kb, ch):
        k = k_at(kb, ch // n_maps)
        return _dot(k, qs_of(i)[ch]) if q_transposed else _dot_nt(k, qs_of(i)[ch])

    ss = {n: scores(*items[n]) for n in range(min(lookahead, len(items)))}
    for n, (i, kb, ch) in enumerate(items):
        if n + lookahead < len(items):
            ss[n + lookahead] = scores(*items[n + lookahead])
        s = ss.pop(n)
        if kb == i:
            s = jnp.where(keep, s, -jnp.inf)
        vt = jnp.concatenate([vt_at(kb, ch // n_maps), ones], axis=0)
        slot = i % 2
        if kb == 0:
            m_new = jnp.max(s, axis=0, keepdims=True)
            acc_new = _dot(vt, jnp.exp2(s - m_new).astype(vt.dtype))
        else:
            m_old = m_ref[slot, ch]
            m_new = jnp.maximum(m_old, jnp.max(s, axis=0, keepdims=True))
            acc_new = (jnp.exp2(m_old - m_new) * acc_ref[slot, ch]
                       + _dot(vt, jnp.exp2(s - m_new).astype(vt.dtype)))
        m_ref[slot, ch] = m_new
        acc_ref[slot, ch] = acc_new
        if kb == i and ch == n_chains - 1:
            finish_tile(i, slot)
            del qs_cache[i]


def _attn_flat_scratch(n_chains, dv):
    return [pltpu.VMEM((2, n_chains, 1, ATTN_TILE), F32),
            pltpu.VMEM((2, n_chains, dv + ATTN_SUM_ROWS, ATTN_TILE), F32)]


def _attn_out(acc_ref, ch, dv):
    return acc_ref[ch, :dv, :] / acc_ref[ch, dv:dv + 1, :]


def _cast_specs(weights, n_steps):
    specs = []
    for w in weights:
        rows = w.shape[0] // n_steps
        assert rows * n_steps == w.shape[0] and rows % BF16_SUBLANES == 0, w.shape
        specs.append(pl.BlockSpec((rows, w.shape[1]), lambda bb: (bb, 0)))
    return specs, [jax.ShapeDtypeStruct(w.shape, BF16) for w in weights]


def _cast_blocks(src_refs, dst_refs):
    for src, dst in zip(src_refs, dst_refs):
        dst[...] = src[...].astype(dst.dtype)


def _split_cast_refs(refs, n_in, n_out, n_cast):
    main_in, cast_src = refs[:n_in], refs[n_in:n_in + n_cast]
    rest = refs[n_in + n_cast:]
    return main_in, cast_src, rest[:n_out], rest[n_out:n_out + n_cast], rest[n_out + n_cast:]


def _diff_attn_kernel(*refs, n_heads, lam_init, n_cast):
    (qt_ref, k_ref, vt_ref, lam_ref, sub_ref), cast_src, (o_ref,), cast_dst, (m_ref, acc_ref) = (
        _split_cast_refs(refs, 5, 1, n_cast))
    _cast_blocks(cast_src, cast_dst)
    n_tiles, tq = qt_ref.shape[1], qt_ref.shape[3]
    lv = lam_ref[...]
    lam = (jnp.exp(jnp.sum(lv[0:1] * lv[1:2], axis=-1, keepdims=True))
           - jnp.exp(jnp.sum(lv[2:3] * lv[3:4], axis=-1, keepdims=True)) + lam_init)

    def finish_tile(i, slot):
        for h in range(n_heads):
            o_t = _attn_out(acc_ref.at[slot], 2 * h, LANES) - lam * _attn_out(acc_ref.at[slot], 2 * h + 1, LANES)
            inv = lax.rsqrt(jnp.mean(o_t * o_t, axis=0, keepdims=True) + RMS_EPS)
            y_t = o_t * inv * sub_ref[...] * (1.0 - lam_init)
            o_ref[0, i * tq:(i + 1) * tq, h * LANES:(h + 1) * LANES] = y_t.T.astype(o_ref.dtype)

    _attn_flat(lambda i: [qt_ref[0, i, h * LANES:(h + 1) * LANES, :] for h in range(n_heads)],
               lambda kb, h: k_ref[0, kb * tq:(kb + 1) * tq, h * LANES:(h + 1) * LANES],
               lambda kb, h: vt_ref[0, kb, h * LANES:(h + 1) * LANES, :],
               finish_tile, m_ref, acc_ref, n_tiles=n_tiles, n_maps=2, tq=tq, q_transposed=True,
               lookahead=DIFF_ATTN_LOOKAHEAD)


def _diff_attn(p0, qvt, lam_vecs, subln_col, n_heads, cast_weights):
    b, s, _ = p0.shape
    n_kb = s // ATTN_TILE
    width = n_heads * LANES
    cast_specs, cast_shapes = _cast_specs(cast_weights, b)
    kern = functools.partial(_diff_attn_kernel, n_heads=n_heads, lam_init=0.8 - 0.6 * math.exp(-0.3 * 0),
                             n_cast=len(cast_weights))
    out = pl.pallas_call(
        kern,
        grid=(b,),
        in_specs=[pl.BlockSpec((1, n_kb, width, ATTN_TILE), lambda bb: (bb, 0, 0, 0)),
                  pl.BlockSpec((1, s, width), lambda bb: (bb, 0, 0)),
                  pl.BlockSpec((1, n_kb, width, ATTN_TILE), lambda bb: (bb, 0, 1, 0)),
                  pl.BlockSpec(lam_vecs.shape, lambda bb: (0, 0)),
                  pl.BlockSpec(subln_col.shape, lambda bb: (0, 0))] + cast_specs,
        out_specs=[pl.BlockSpec((1, s, width), lambda bb: (bb, 0, 0))] + cast_specs,
        out_shape=[jax.ShapeDtypeStruct((b, s, width), BF16)] + cast_shapes,
        scratch_shapes=_attn_flat_scratch(2 * n_heads, LANES),
        compiler_params=_params(DIFF_ATTN_VMEM_MIB, "arbitrary"),
        name="diff_attn",
    )(qvt, p0, qvt, lam_vecs, subln_col, *cast_weights)
    return out[0], out[1:]


def _mla_attn_kernel(*refs, n_heads, scale, n_cast):
    ((cq_ref, kv_ref, qn_ref, kvn_ref, wqt_ref, wk_ref, wvt_ref, cosq_tab, sinq_tab, ck_tab, sak_tab, sbk_tab),
     cast_src, (o_ref,), cast_dst, (m_ref, acc_ref, k_scr, vt_scr)) = (
        _split_cast_refs(refs, 12, 1, n_cast))
    _cast_blocks(cast_src, cast_dst)
    seq = kv_ref.shape[1]
    tq = ATTN_TILE
    half = MLA_ROPE // 2
    projected = set()

    def project_keys(chunk):
        if chunk in projected:
            return
        projected.add(chunk)
        rows = slice(chunk * ROW_TILE, (chunk + 1) * ROW_TILE)
        ckv = _rms(kv_ref[0, rows, :LANES], kvn_ref[...]).astype(BF16)
        kr = _rope_block(kv_ref[0, rows, LANES:], ck_tab[rows, :], sak_tab[rows, :], sbk_tab[rows, :], half)
        k_scr[rows, :] = _dot(jnp.concatenate([ckv, kr.astype(BF16)], axis=-1), wk_ref[...]).astype(BF16)
        vt = _dot_nt(wvt_ref[...], ckv)
        for j in range(ROW_TILE // tq):
            vt_scr[chunk * (ROW_TILE // tq) + j] = vt[:, j * tq:(j + 1) * tq].astype(BF16)

    def q_heads_at(i):
        cols = slice(i * tq, (i + 1) * tq)
        q_t = _dot_nt(wqt_ref[...], _rms(cq_ref[0, cols, :], qn_ref[...]).astype(BF16))
        cos_t, sin_t = cosq_tab[:, cols], sinq_tab[:, cols]
        q_heads = []
        for h in range(n_heads):
            lo = h * MLA_QK_PAD
            x1 = q_t[lo + MLA_NOPE:lo + MLA_NOPE + half, :]
            x2 = q_t[lo + MLA_NOPE + half:lo + MLA_NOPE + 2 * half, :]
            q_h = jnp.concatenate([q_t[lo:lo + MLA_NOPE, :], x1 * cos_t - x2 * sin_t, x2 * cos_t + x1 * sin_t,
                                   q_t[lo + MLA_NOPE + 2 * half:lo + MLA_QK_PAD, :]], axis=0)
            q_heads.append((q_h * scale).astype(BF16))
        return q_heads

    def k_at(kb, h):
        project_keys(kb * tq // ROW_TILE)
        return k_scr[kb * tq:(kb + 1) * tq, h * MLA_QK_PAD:(h + 1) * MLA_QK_PAD]

    def finish_tile(i, slot):
        for h in range(n_heads):
            o_ref[0, i * tq:(i + 1) * tq, h * MLA_V:(h + 1) * MLA_V] = (
                _attn_out(acc_ref.at[slot], h, MLA_V).T.astype(o_ref.dtype))

    _attn_flat(q_heads_at, k_at, lambda kb, h: vt_scr[kb, h * MLA_V:(h + 1) * MLA_V, :],
               finish_tile, m_ref, acc_ref, n_tiles=seq // tq, n_maps=1, tq=tq, q_transposed=True,
               lookahead=MLA_ATTN_LOOKAHEAD)


def _mla_attn(p1, q_norm, kv_norm, wq_t, wk, wv_t, q_tabs_t, tabs, n_heads, cast_weights):
    b, s, _ = p1.shape
    n_kb = s // ATTN_TILE
    cast_specs, cast_shapes = _cast_specs(cast_weights, b)
    kern = functools.partial(_mla_attn_kernel, n_heads=n_heads, scale=(MLA_NOPE + MLA_ROPE) ** -0.5 * LOG2_E,
                             n_cast=len(cast_weights))
    const = lambda a: pl.BlockSpec(a.shape, lambda bb: (0,) * a.ndim)
    out = pl.pallas_call(
        kern,
        grid=(b,),
        in_specs=[pl.BlockSpec((1, s, 2 * LANES), lambda bb: (bb, 0, 1)),
                  pl.BlockSpec((1, s, 2 * LANES), lambda bb: (bb, 0, 2)),
                  const(q_norm), const(kv_norm), const(wq_t), const(wk), const(wv_t),
                  const(q_tabs_t[0]), const(q_tabs_t[1]),
                  const(tabs[0]), const(tabs[1]), const(tabs[2])] + cast_specs,
        out_specs=[pl.BlockSpec((1, s, n_heads * MLA_V), lambda bb: (bb, 0, 0))] + cast_specs,
        out_shape=[jax.ShapeDtypeStruct((b, s, n_heads * MLA_V), BF16)] + cast_shapes,
        scratch_shapes=_attn_flat_scratch(n_heads, MLA_V) + [
            pltpu.VMEM((s, n_heads * MLA_QK_PAD), BF16),
            pltpu.VMEM((n_kb, n_heads * MLA_V, ATTN_TILE), BF16)],
        compiler_params=_params(MLA_ATTN_VMEM_MIB, "arbitrary"),
        name="mla_attn",
    )(p1, p1, q_norm, kv_norm, wq_t, wk, wv_t, *q_tabs_t, *tabs, *cast_weights)
    return out[0], out[1:]


def _conv_kernel(a_ref, gate_ref, ah_ref, gh_ref, w_ref, b_ref, lg_ref, lb_ref, o_ref, u_ref, ur_ref):
    tt = a_ref.shape[1]
    u_ref[CONV_HALO:, :] = a_ref[0].astype(F32) * _sigmoid(gate_ref[0].astype(F32))
    halo = ah_ref[0].astype(F32) * _sigmoid(gh_ref[0].astype(F32))
    u_ref[:CONV_HALO, :] = jnp.where(pl.program_id(1) > 0, halo, jnp.zeros_like(halo))
    rows = ur_ref.shape[1]
    for r in range(1, SUBLANES):
        ur_ref[r - 1] = u_ref[r:r + rows, :]
    acc = jnp.zeros((tt, a_ref.shape[2]), F32)
    first = CONV_HALO - (CONV_WIDTH - 1)
    for k in range(CONV_WIDTH):
        base, r = divmod(first + k, SUBLANES)
        src = u_ref if r == 0 else ur_ref.at[r - 1]
        acc = acc + src[base * SUBLANES:base * SUBLANES + tt, :] * w_ref[k:k + 1, :]
    y = acc + b_ref[...]
    mu = jnp.mean(y, axis=-1, keepdims=True)
    yc = y - mu
    yn = yc * lax.rsqrt(jnp.mean(yc * yc, axis=-1, keepdims=True) + LN_EPS) * lg_ref[...] + lb_ref[...]
    o_ref[0] = (yn * _sigmoid(yn)).astype(o_ref.dtype)


def _conv_module(p0, dw_w, dw_b, ln_g, ln_b, col0):
    b, s, _ = p0.shape
    c = dw_w.shape[1]
    a_blk, g_blk = col0 // c, col0 // c + 1
    ratio = CONV_TILE // CONV_HALO
    main = lambda blk: pl.BlockSpec((1, CONV_TILE, c), lambda bb, t: (bb, t, blk))
    halo = lambda blk: pl.BlockSpec((1, CONV_HALO, c), lambda bb, t: (bb, jnp.maximum(t * ratio - 1, 0), blk))
    vec = lambda a: pl.BlockSpec(a.shape, lambda bb, t: (0, 0))
    return pl.pallas_call(
        _conv_kernel,
        grid=(b, s // CONV_TILE),
        in_specs=[main(a_blk), main(g_blk), halo(a_blk), halo(g_blk),
                  vec(dw_w), vec(dw_b), vec(ln_g), vec(ln_b)],
        out_specs=pl.BlockSpec((1, CONV_TILE, c), lambda bb, t: (bb, t, 0)),
        out_shape=jax.ShapeDtypeStruct((b, s, c), BF16),
        scratch_shapes=[pltpu.VMEM((CONV_HALO + CONV_TILE, c), F32),
                        pltpu.VMEM((SUBLANES - 1, CONV_HALO + CONV_TILE - SUBLANES, c), F32)],
        compiler_params=_params(CONV_VMEM_MIB, "parallel", "parallel"),
        name="conv_module",
    )(p0, p0, p0, p0, dw_w, dw_b, ln_g, ln_b)


def _post_kernel(*refs, n_mix, with_next):
    x_ref = refs[0]
    mix_refs = refs[1:1 + n_mix]
    (wo_ref, gpost_ref, gfpre_ref, wg_ref, wu_ref, wd_ref, gfpost_ref) = refs[1 + n_mix:8 + n_mix]
    pos = 8 + n_mix
    if with_next:
        gnext_ref, wnext_ref = refs[pos:pos + 2]
        pos += 2
    h_out_ref = refs[pos]
    pos += 1
    if with_next:
        p_out_ref = refs[pos]
        pos += 1
    hid_ref = refs[pos]

    n_rows = x_ref.shape[0]
    groups = [slice(r0, r0 + POST_GROUP_ROWS) for r0 in range(0, n_rows, POST_GROUP_ROWS)]

    def out_proj(rows):
        y = None
        row = 0
        for r in mix_refs:
            w = r.shape[-1]
            part = _dot(r[rows, :], wo_ref[row:row + w, :])
            y = part if y is None else y + part
            row += w
        return y

    ys = [out_proj(rows) for rows in groups]
    h1s = [x_ref[rows, :] + _rms(y, gpost_ref[...]) for rows, y in zip(groups, ys)]
    ts = [_rms(h1, gfpre_ref[...]).astype(BF16) for h1 in h1s]
    d_ff = wg_ref.shape[1]
    for j in range(0, d_ff, FFN_CHUNK):
        wdt = min(FFN_CHUNK, d_ff - j)
        for rows, t in zip(groups, ts):
            gate = _dot(t, wg_ref[:, j:j + wdt])
            up = _dot(t, wu_ref[:, j:j + wdt])
            hid_ref[rows, j:j + wdt] = (gate * _sigmoid(gate) * up).astype(BF16)
    fs = [_dot(hid_ref[rows, :], wd_ref[...]) for rows in groups]
    h2s = [h1 + _rms(f, gfpost_ref[...]) for h1, f in zip(h1s, fs)]
    for rows, h2 in zip(groups, h2s):
        h_out_ref[rows, :] = h2
    if with_next:
        t2s = [_rms(h2, gnext_ref[...]).astype(BF16) for h2 in h2s]
        for rows, t2 in zip(groups, t2s):
            p_out_ref[rows, :] = _dot(t2, wnext_ref[...])


def _post(x2, mix_parts, w_out, g_post, g_fpre, w_gate, w_up, w_down, g_fpost, nxt=None):
    n, d = x2.shape
    d_ff = w_gate.shape[1]
    row_tile = POST_ROW_TILE
    row = lambda a: pl.BlockSpec((row_tile, a.shape[1]), lambda i: (i, 0))
    consts = [w_out, g_post, g_fpre, w_gate, w_up, w_down, g_fpost] + (list(nxt) if nxt else [])
    out_shape = [jax.ShapeDtypeStruct((n, d), F32)]
    out_specs = [pl.BlockSpec((row_tile, d), lambda i: (i, 0))]
    if nxt:
        n_next = nxt[1].shape[1]
        out_shape.append(jax.ShapeDtypeStruct((n, n_next), F32))
        out_specs.append(pl.BlockSpec((row_tile, n_next), lambda i: (i, 0)))
    kern = functools.partial(_post_kernel, n_mix=len(mix_parts), with_next=bool(nxt))
    return pl.pallas_call(
        kern,
        grid=(n // row_tile,),
        in_specs=[row(x2)] + [row(m) for m in mix_parts] + [_const_spec(c.shape) for c in consts],
        out_specs=out_specs,
        out_shape=out_shape,
        scratch_shapes=[pltpu.VMEM((row_tile, d_ff), BF16)],
        compiler_params=_params(POST_VMEM_MIB, "parallel"),
        name="post_next" if nxt else "post",
    )(x2, *mix_parts, *consts)


def _gelu_tanh(x):
    return 0.5 * x * (1.0 + jnp.tanh(math.sqrt(2.0 / math.pi) * (x + 0.044715 * (x * x * x))))


def _ssm_kernel(u_ref, lr_ref, li_ref, ldt_ref, bre_ref, bim_ref, cre_ref, cim_ref, d_ref, wg_ref, bg_ref,
                o_ref, bmat_ref, cmat_ref, a_ref, st_ref, us_ref, utm_ref, x_ref, y_ref):
    nb, tt, ch = u_ref.shape
    n_state = lr_ref.shape[1]

    @pl.when(pl.program_id(0) == 0)
    def _init():
        lr, li = lr_ref[...], li_ref[...]
        dt = jnp.exp(ldt_ref[...])
        mag = jnp.exp(lr * dt)
        ab_re = mag * jnp.cos(li * dt)
        ab_im = mag * jnp.sin(li * dt)
        den = lr * lr + li * li
        n_re = ab_re - 1.0
        f_re = (n_re * lr + ab_im * li) / den
        f_im = (ab_im * lr - n_re * li) / den
        br, bi = bre_ref[...], bim_ref[...]
        bmat_ref[:, :n_state] = (f_re * br - f_im * bi).astype(BF16)
        bmat_ref[:, n_state:] = (f_re * bi + f_im * br).astype(BF16)
        cmat_ref[:n_state, :] = cre_ref[...].astype(BF16)
        cmat_ref[n_state:, :] = (-cim_ref[...]).astype(BF16)
        a_ref[0:1, :] = ab_re
        a_ref[1:2, :] = ab_im
        st_ref[...] = jnp.zeros(st_ref.shape, F32)

    n_ublk = ch // LANES
    for b in range(nb):
        for j in range(n_ublk):
            us_ref[j, b * SSM_PITCH:b * SSM_PITCH + tt, :] = u_ref[b, :, j * LANES:(j + 1) * LANES]

    def gather_step(t, carry):
        dst = pl.multiple_of(t * nb, nb)
        for j in range(n_ublk):
            utm_ref[pl.ds(dst, nb), j * LANES:(j + 1) * LANES] = us_ref[j, pl.ds(t, nb, stride=SSM_PITCH), :]
        return carry

    lax.fori_loop(0, tt, gather_step, 0, unroll=8)

    half_rows = tt * nb // 2
    for r0 in (0, half_rows):
        x_ref[r0:r0 + half_rows, :] = _dot(utm_ref[r0:r0 + half_rows, :].astype(BF16), bmat_ref[...])

    a_re = jnp.broadcast_to(a_ref[0:1, :], (nb, n_state))
    a_im = jnp.broadcast_to(a_ref[1:2, :], (nb, n_state))

    def scan_step(t, carry):
        x_re, x_im = carry
        row = pl.multiple_of(t * nb, nb)
        n_re = a_re * x_re - a_im * x_im + x_ref[pl.ds(row, nb), :n_state]
        n_im = a_re * x_im + a_im * x_re + x_ref[pl.ds(row, nb), n_state:]
        x_ref[pl.ds(row, nb), :n_state] = n_re
        x_ref[pl.ds(row, nb), n_state:] = n_im
        return n_re, n_im

    x_re, x_im = lax.fori_loop(0, tt, scan_step, (st_ref[:, :n_state], st_ref[:, n_state:]), unroll=4)
    st_ref[:, :n_state] = x_re
    st_ref[:, n_state:] = x_im

    n_yblk = ch // LANES
    for r0 in (0, half_rows):
        y_tm = _dot(x_ref[r0:r0 + half_rows, :].astype(BF16), cmat_ref[...])
        for j in range(n_yblk):
            y_ref[j, r0:r0 + half_rows, :] = y_tm[:, j * LANES:(j + 1) * LANES]
    for b in range(nb):
        y = jnp.concatenate([y_ref[j, pl.ds(b, tt, stride=nb), :] for j in range(n_yblk)], axis=-1)
        y = y + d_ref[...] * u_ref[b]
        z = _gelu_tanh(y)
        gate = _dot(z.astype(BF16), wg_ref[...]) + bg_ref[...]
        o_ref[b] = (z * _sigmoid(gate)).astype(o_ref.dtype)


def _ssm(p1, rows, b_bd, c_bd, d_row, w_glu, b_glu):
    b, s, _ = p1.shape
    ch = w_glu.shape[0]
    n_state = rows[0].shape[1]
    consts = list(rows) + list(b_bd) + list(c_bd) + [d_row, w_glu, b_glu]
    return pl.pallas_call(
        _ssm_kernel,
        grid=(s // SSM_TILE,),
        in_specs=[pl.BlockSpec((b, SSM_TILE, ch), lambda t: (0, t, 0))] + [_const_spec(c.shape) for c in consts],
        out_specs=pl.BlockSpec((b, SSM_TILE, ch), lambda t: (0, t, 0)),
        out_shape=jax.ShapeDtypeStruct((b, s, ch), BF16),
        scratch_shapes=[pltpu.VMEM((ch, 2 * n_state), BF16),
                        pltpu.VMEM((2 * n_state, ch), BF16),
                        pltpu.VMEM((SUBLANES, n_state), F32),
                        pltpu.VMEM((b, 2 * n_state), F32),
                        pltpu.VMEM((ch // LANES, b * SSM_PITCH, LANES), F32),
                        pltpu.VMEM((SSM_TILE * b, ch), F32),
                        pltpu.VMEM((SSM_TILE * b, 2 * n_state), F32),
                        pltpu.VMEM((ch // LANES, SSM_TILE * b, LANES), F32)],
        compiler_params=_params(SSM_VMEM_MIB, "arbitrary"),
        name="s5_ssm",
    )(p1, *consts)


def _rope_tables(s, rot_dim, theta):
    inv = float(theta) ** (-np.arange(0, rot_dim, 2, dtype=np.float64) / rot_dim)
    ang = np.arange(s, dtype=np.float64)[:, None] * inv[None, :]
    return np.cos(ang).astype(np.float32), np.sin(ang).astype(np.float32)


def _lane_tables(cos, sin, period):
    s, half = cos.shape
    reps = LANES // period
    one = np.ones((s, period - 2 * half), np.float32)
    zero = np.zeros((s, period - 2 * half), np.float32)
    zh = np.zeros((s, half), np.float32)
    c = np.tile(np.concatenate([cos, cos, one], axis=1), (1, reps))
    sa = np.tile(np.concatenate([-sin, zh, zero], axis=1), (1, reps))
    sb = np.tile(np.concatenate([zh, sin, zero], axis=1), (1, reps))
    return c, sa, sb


def _block_diag(blocks):
    g, r, c = blocks.shape
    eye = jnp.eye(g, dtype=blocks.dtype)
    return (eye[:, None, :, None] * blocks[:, :, None, :]).reshape(g * r, g * c)


def kernel(x, l0_mix_pre, l0_mix_post, l0_w_in, l0_lambda_q1, l0_lambda_k1, l0_lambda_q2, l0_lambda_k2, l0_subln, l0_dw_w, l0_dw_b, l0_conv_ln_g, l0_conv_ln_b, l0_w_out, l0_ffn_pre, l0_ffn_post, l0_w_gate, l0_w_up, l0_w_down, l1_mix_pre, l1_mix_post, l1_w_in, l1_a_re, l1_a_im, l1_log_dt, l1_b_re, l1_b_im, l1_c_re, l1_c_im, l1_d_skip, l1_w_glu, l1_b_glu, l1_q_norm, l1_w_uq, l1_kv_norm, l1_w_ukv, l1_w_out, l1_ffn_pre, l1_ffn_post, l1_w_gate, l1_w_up, l1_w_down):
    b, s, d = x.shape
    n = b * s
    row = lambda v: v.reshape(1, -1).astype(F32)
    bf = lambda w: w.astype(BF16)

    diff_width = 4 * LANES
    n_diff_heads = diff_width // LANES
    conv_ch = l0_dw_w.shape[1]
    ssm_ch = l1_w_glu.shape[0]
    n_groups, n_state_g = l1_a_re.shape
    q_rank = l1_q_norm.shape[0]
    kv_rank = l1_kv_norm.shape[0]
    n_mla_heads = l1_w_uq.shape[1] // (MLA_NOPE + MLA_ROPE)

    cos_a, sin_a = _rope_tables(s, DIFF_ROT, ROPE_THETA)
    tabs_a = _lane_tables(cos_a, sin_a, period=DIFF_HEAD_DIM)
    x2 = x.reshape(n, d)
    p0, qvt_a = _l0_in(x2, row(l0_mix_pre), l0_w_in.astype(F32), tabs_a, (cos_a.T, sin_a.T), b, s, diff_width)
    p0 = p0.reshape(b, s, -1)
    lam_vecs = jnp.stack([l0_lambda_q1, l0_lambda_k1, l0_lambda_q2, l0_lambda_k2]).astype(F32)
    y_a, (w_out0, w_gate0, w_up0, w_down0) = _diff_attn(
        p0, qvt_a, lam_vecs, l0_subln.reshape(-1, 1).astype(F32), n_diff_heads,
        [l0_w_out, l0_w_gate, l0_w_up, l0_w_down])
    y_b = _conv_module(p0, l0_dw_w.astype(F32), row(l0_dw_b), row(l0_conv_ln_g), row(l0_conv_ln_b),
                       col0=diff_width)

    pad = (-l1_w_in.shape[1]) % LANES
    w_in1 = bf(jnp.pad(l1_w_in, ((0, 0), (0, pad))))
    h2, p1 = _post(x2, [y_a.reshape(n, -1), y_b.reshape(n, -1)], w_out0, row(l0_mix_post),
                   row(l0_ffn_pre), w_gate0, w_up0, w_down0, row(l0_ffn_post),
                   nxt=(row(l1_mix_pre), w_in1))

    state_row = lambda a: a.reshape(1, -1).astype(F32)
    ssm_rows = (state_row(l1_a_re), state_row(l1_a_im),
                state_row(jnp.broadcast_to(l1_log_dt[:, None], (n_groups, n_state_g))))
    b_bd = tuple(_block_diag(jnp.swapaxes(m, 1, 2).astype(F32)) for m in (l1_b_re, l1_b_im))
    c_bd = tuple(_block_diag(jnp.swapaxes(m, 1, 2).astype(F32)) for m in (l1_c_re, l1_c_im))
    y_c = _ssm(p1.reshape(b, s, -1), ssm_rows, b_bd, c_bd, row(l1_d_skip), bf(l1_w_glu), row(l1_b_glu))

    wq = l1_w_uq.reshape(q_rank, n_mla_heads, MLA_NOPE + MLA_ROPE)
    wq = jnp.pad(wq, ((0, 0), (0, 0), (0, MLA_QK_PAD - MLA_NOPE - MLA_ROPE))).reshape(q_rank, -1)
    wkv = l1_w_ukv.reshape(kv_rank, n_mla_heads, MLA_NOPE + MLA_V)
    wk_nope = jnp.pad(wkv[:, :, :MLA_NOPE], ((0, 0), (0, 0), (0, MLA_QK_PAD - MLA_NOPE)))
    route = jnp.pad(jnp.eye(MLA_ROPE, dtype=F32), ((0, LANES - MLA_ROPE), (MLA_NOPE, MLA_QK_PAD - MLA_NOPE - MLA_ROPE)))
    wk_rope = jnp.broadcast_to(route[:, None, :], (LANES, n_mla_heads, MLA_QK_PAD))
    wk = jnp.concatenate([wk_nope, wk_rope], axis=0).reshape(kv_rank + LANES, -1)
    wv_t = wkv[:, :, MLA_NOPE:].reshape(kv_rank, -1).T
    cos_d, sin_d = _rope_tables(s, MLA_ROPE, MLA_ROPE_THETA)
    tabs_d = _lane_tables(cos_d, sin_d, period=LANES)
    y_d, (w_out1, w_gate1, w_up1, w_down1) = _mla_attn(
        p1.reshape(b, s, -1), row(l1_q_norm), row(l1_kv_norm), bf(wq.T), bf(wk), bf(wv_t),
        (cos_d.T, sin_d.T), tabs_d, n_mla_heads, [l1_w_out, l1_w_gate, l1_w_up, l1_w_down])

    (out,) = _post(h2, [y_c.reshape(n, -1), y_d.reshape(n, -1)], w_out1, row(l1_mix_post),
                   row(l1_ffn_pre), w_gate1, w_up1, w_down1, row(l1_ffn_post))
    return out.reshape(b, s, d)
```

```python
import functools
import math

import jax
import jax.numpy as jnp
import numpy as np
from jax import lax
from jax.experimental import pallas as pl
from jax.experimental.pallas import tpu as pltpu

F32 = jnp.float32
BF16 = jnp.bfloat16

LANES = 128
SUBLANES = 8
BF16_SUBLANES = 16
VMEM_MAX_MIB = 60
L0_IN_VMEM_MIB = 40
DIFF_ATTN_VMEM_MIB = 56
MLA_ATTN_VMEM_MIB = 56
CONV_VMEM_MIB = 56
POST_VMEM_MIB = 56
SSM_VMEM_MIB = 56

CHUNK = 64
RMS_EPS = 1e-6
LN_EPS = 1e-5
ROPE_THETA = 500000.0
MLA_ROPE_THETA = 10000.0
DIFF_HEAD_DIM = 64
DIFF_ROT = 16
CONV_WIDTH = 31
CONV_HALO = 32
MLA_NOPE = 128
MLA_ROPE = 64
MLA_V = 128
MLA_QK_PAD = 256

ROW_TILE = 512
ATTN_TILE = 256
DIFF_ATTN_LOOKAHEAD = 5
MLA_ATTN_LOOKAHEAD = 6
ATTN_SUM_ROWS = 16
LOG2_E = math.log2(math.e)
CONV_TILE = 512
SSM_TILE = 256
SSM_PITCH = SSM_TILE + SUBLANES
FFN_CHUNK = 512
POST_GROUP_ROWS = 256
POST_ROW_TILE = 512


def _params(vmem_mib, *sem):
    assert vmem_mib <= VMEM_MAX_MIB
    return pltpu.CompilerParams(dimension_semantics=sem, vmem_limit_bytes=vmem_mib * 1024 * 1024)


def _rms(x, g):
    return x * lax.rsqrt(jnp.mean(x * x, axis=-1, keepdims=True) + RMS_EPS) * g


def _sigmoid(x):
    return 0.5 + 0.5 * jnp.tanh(0.5 * x)


def _dot(a, b):
    return jnp.dot(a, b, preferred_element_type=F32)


def _dot_nt(a, b):
    return lax.dot_general(a, b, (((1,), (1,)), ((), ())), preferred_element_type=F32)


def _rope_block(x, c, sa, sb, shift):
    return x * c + pltpu.roll(x, LANES - shift, 1) * sa + pltpu.roll(x, shift, 1) * sb


def _const_spec(shape):
    nd = len(shape)
    return pl.BlockSpec(shape, lambda *_: (0,) * nd, pipeline_mode=pl.Buffered(1))


def _vt_out(b, seq, width):
    tiles_per_seq = seq // ROW_TILE
    per_tile = ROW_TILE // ATTN_TILE
    spec = pl.BlockSpec((1, per_tile, width, ATTN_TILE),
                        lambda i: (i // tiles_per_seq, i % tiles_per_seq, 0, 0))
    return spec, jax.ShapeDtypeStruct((b, seq // ATTN_TILE, width, ATTN_TILE), BF16)


def _l0_in_kernel(x_ref, g_ref, win_ref, c_ref, sa_ref, sb_ref, cosq_ref, sinq_ref, o_ref, qvt_ref,
                  w_ref, wt_ref, *, n_k_blocks, q_rows, scale):
    @pl.when(pl.program_id(0) == 0)
    def _regroup_weights():
        w_ref[:, :q_rows] = win_ref[:, q_rows:2 * q_rows].astype(BF16)
        w_ref[:, q_rows:] = win_ref[:, 3 * q_rows:].astype(BF16)
        wt_ref[:q_rows, :] = win_ref[:, :q_rows].T.astype(BF16)
        wt_ref[q_rows:, :] = win_ref[:, 2 * q_rows:3 * q_rows].T.astype(BF16)

    half = DIFF_ROT // 2
    groups = [slice(r0, r0 + ATTN_TILE) for r0 in range(0, x_ref.shape[0], ATTN_TILE)]
    ts = [_rms(x_ref[rows, :], g_ref[...]).astype(BF16) for rows in groups]
    ps = [_dot(t, w_ref[...]) for t in ts]
    qv_ts = [_dot_nt(wt_ref[...], t) for t in ts]
    for j, (rows, p, qv_t) in enumerate(zip(groups, ps, qv_ts)):
        c, sa, sb = c_ref[rows, :], sa_ref[rows, :], sb_ref[rows, :]
        for blk_i in range(n_k_blocks):
            blk = _rope_block(p[:, blk_i * LANES:(blk_i + 1) * LANES], c, sa, sb, half)
            o_ref[rows, blk_i * LANES:(blk_i + 1) * LANES] = blk.astype(o_ref.dtype)
        rest = n_k_blocks * LANES
        o_ref[rows, rest:] = p[:, rest:].astype(o_ref.dtype)
        cos_t, sin_t = cosq_ref[:, rows], sinq_ref[:, rows]
        pieces = []
        for lo in range(0, q_rows, DIFF_HEAD_DIM):
            x1, x2 = qv_t[lo:lo + half, :], qv_t[lo + half:lo + 2 * half, :]
            pieces += [(x1 * cos_t - x2 * sin_t) * scale, (x2 * cos_t + x1 * sin_t) * scale,
                       qv_t[lo + 2 * half:lo + DIFF_HEAD_DIM, :] * scale]
        qvt_ref[0, j] = jnp.concatenate(pieces + [qv_t[q_rows:, :]], axis=0).astype(qvt_ref.dtype)


def _l0_in(x2, g_pre, w_in, tabs, q_tabs_t, b, seq, q_rows):
    n, d = x2.shape
    n_out = w_in.shape[1] - 2 * q_rows
    tiles_per_seq = seq // ROW_TILE
    kern = functools.partial(_l0_in_kernel, n_k_blocks=q_rows // LANES, q_rows=q_rows,
                             scale=DIFF_HEAD_DIM ** -0.5 * LOG2_E)
    tab_spec = pl.BlockSpec((ROW_TILE, LANES), lambda i: (i % tiles_per_seq, 0))
    qtab_spec = pl.BlockSpec((q_tabs_t[0].shape[0], ROW_TILE), lambda i: (0, i % tiles_per_seq))
    vt_spec, vt_shape = _vt_out(b, seq, 2 * q_rows)
    return pl.pallas_call(
        kern,
        grid=(n // ROW_TILE,),
        in_specs=[pl.BlockSpec((ROW_TILE, d), lambda i: (i, 0)),
                  _const_spec((1, d)), _const_spec(w_in.shape),
                  tab_spec, tab_spec, tab_spec, qtab_spec, qtab_spec],
        out_specs=[pl.BlockSpec((ROW_TILE, n_out), lambda i: (i, 0)), vt_spec],
        out_shape=[jax.ShapeDtypeStruct((n, n_out), BF16), vt_shape],
        scratch_shapes=[pltpu.VMEM((d, n_out), BF16), pltpu.VMEM((2 * q_rows, d), BF16)],
        compiler_params=_params(L0_IN_VMEM_MIB, "arbitrary"),
        name="l0_in",
    )(x2, g_pre, w_in, *tabs, *q_tabs_t)


def _attn_flat(q_heads_at, k_at, vt_at, finish_tile, m_ref, acc_ref, *, n_tiles, n_maps, tq, q_transposed,
               lookahead):
    dk_axis = 0 if q_transposed else 1
    qs_cache = {}

    def qs_of(i):
        if i not in qs_cache:
            qs = []
            for q in q_heads_at(i):
                if n_maps == 1:
                    qs.append(q)
                else:
                    width = q.shape[dk_axis] // n_maps
                    pos = lax.broadcasted_iota(jnp.int32, q.shape, dk_axis)
                    for c in range(n_maps):
                        qs.append(jnp.where((pos >= c * width) & (pos < (c + 1) * width), q, jnp.zeros_like(q)))
            qs_cache[i] = qs
        return qs_cache[i]

    n_chains = len(qs_of(0))
    ones = jnp.ones((ATTN_SUM_ROWS, tq), BF16)
    kc = lax.broadcasted_iota(jnp.int32, (tq, tq), 0) // CHUNK
    qc = lax.broadcasted_iota(jnp.int32, (tq, tq), 1) // CHUNK
    keep = kc <= qc
    items = [(i, kb, ch) for i in range(n_tiles) for kb in range(i + 1) for ch in range(n_chains)]

    def scores(i, kb, ch):
        k = k_at(kb, ch // n_maps)
        return _dot(k, qs_of(i)[ch]) if q_transposed else _dot_nt(k, qs_of(i)[ch])

    ss = {n: scores(*items[n]) for n in range(min(lookahead, len(items)))}
    for n, (i, kb, ch) in enumerate(items):
        if n + lookahead < len(items):
            ss[n + lookahead] = scores(*items[n + lookahead])
        s = ss.pop(n)
        if kb == i:
            s = jnp.where(keep, s, -jnp.inf)
        vt = jnp.concatenate([vt_at(kb, ch // n_maps), ones], axis=0)
        slot = i % 2
        if kb == 0:
            m_new = jnp.max(s, axis=0, keepdims=True)
            acc_new = _dot(vt, jnp.exp2(s - m_new).astype(vt.dtype))
        else:
            m_old = m_ref[slot, ch]
            m_new = jnp.maximum(m_old, jnp.max(s, axis=0, keepdims=True))
            acc_new = (jnp.exp2(m_old - m_new) * acc_ref[slot, ch]
                       + _dot(vt, jnp.exp2(s - m_new).astype(vt.dtype)))
        m_ref[slot, ch] = m_new
        acc_ref[slot, ch] = acc_new
        if kb == i and ch == n_chains - 1:
            finish_tile(i, slot)
            del qs_cache[i]


def _attn_flat_scratch(n_chains, dv):
    return [pltpu.VMEM((2, n_chains, 1, ATTN_TILE), F32),
            pltpu.VMEM((2, n_chains, dv + ATTN_SUM_ROWS, ATTN_TILE), F32)]


def _attn_out(acc_ref, ch, dv):
    return acc_ref[ch, :dv, :] / acc_ref[ch, dv:dv + 1, :]


def _cast_specs(weights, n_steps):
    specs = []
    for w in weights:
        rows = w.shape[0] // n_steps
        assert rows * n_steps == w.shape[0] and rows % BF16_SUBLANES == 0, w.shape
        specs.append(pl.BlockSpec((rows, w.shape[1]), lambda bb: (bb, 0)))
    return specs, [jax.ShapeDtypeStruct(w.shape, BF16) for w in weights]


def _cast_blocks(src_refs, dst_refs):
    for src, dst in zip(src_refs, dst_refs):
        dst[...] = src[...].astype(dst.dtype)


def _split_cast_refs(refs, n_in, n_out, n_cast):
    main_in, cast_src = refs[:n_in], refs[n_in:n_in + n_cast]
    rest = refs[n_in + n_cast:]
    return main_in, cast_src, rest[:n_out], rest[n_out:n_out + n_cast], rest[n_out + n_cast:]


def _diff_attn_kernel(*refs, n_heads, lam_init, n_cast):
    (qt_ref, k_ref, vt_ref, lam_ref, sub_ref), cast_src, (o_ref,), cast_dst, (m_ref, acc_ref) = (
        _split_cast_refs(refs, 5, 1, n_cast))
    _cast_blocks(cast_src, cast_dst)
    n_tiles, tq = qt_ref.shape[1], qt_ref.shape[3]
    lv = lam_ref[...]
    lam = (jnp.exp(jnp.sum(lv[0:1] * lv[1:2], axis=-1, keepdims=True))
           - jnp.exp(jnp.sum(lv[2:3] * lv[3:4], axis=-1, keepdims=True)) + lam_init)

    def finish_tile(i, slot):
        for h in range(n_heads):
            o_t = _attn_out(acc_ref.at[slot], 2 * h, LANES) - lam * _attn_out(acc_ref.at[slot], 2 * h + 1, LANES)
            inv = lax.rsqrt(jnp.mean(o_t * o_t, axis=0, keepdims=True) + RMS_EPS)
            y_t = o_t * inv * sub_ref[...] * (1.0 - lam_init)
            o_ref[0, i * tq:(i + 1) * tq, h * LANES:(h + 1) * LANES] = y_t.T.astype(o_ref.dtype)

    _attn_flat(lambda i: [qt_ref[0, i, h * LANES:(h + 1) * LANES, :] for h in range(n_heads)],
               lambda kb, h: k_ref[0, kb * tq:(kb + 1) * tq, h * LANES:(h + 1) * LANES],
               lambda kb, h: vt_ref[0, kb, h * LANES:(h + 1) * LANES, :],
               finish_tile, m_ref, acc_ref, n_tiles=n_tiles, n_maps=2, tq=tq, q_transposed=True,
               lookahead=DIFF_ATTN_LOOKAHEAD)


def _diff_attn(p0, qvt, lam_vecs, subln_col, n_heads, cast_weights):
    b, s, _ = p0.shape
    n_kb = s // ATTN_TILE
    width = n_heads * LANES
    cast_specs, cast_shapes = _cast_specs(cast_weights, b)
    kern = functools.partial(_diff_attn_kernel, n_heads=n_heads, lam_init=0.8 - 0.6 * math.exp(-0.3 * 0),
                             n_cast=len(cast_weights))
    out = pl.pallas_call(
        kern,
        grid=(b,),
        in_specs=[pl.BlockSpec((1, n_kb, width, ATTN_TILE), lambda bb: (bb, 0, 0, 0)),
                  pl.BlockSpec((1, s, width), lambda bb: (bb, 0, 0)),
                  pl.BlockSpec((1, n_kb, width, ATTN_TILE), lambda bb: (bb, 0, 1, 0)),
                  pl.BlockSpec(lam_vecs.shape, lambda bb: (0, 0)),
                  pl.BlockSpec(subln_col.shape, lambda bb: (0, 0))] + cast_specs,
        out_specs=[pl.BlockSpec((1, s, width), lambda bb: (bb, 0, 0))] + cast_specs,
        out_shape=[jax.ShapeDtypeStruct((b, s, width), BF16)] + cast_shapes,
        scratch_shapes=_attn_flat_scratch(2 * n_heads, LANES),
        compiler_params=_params(DIFF_ATTN_VMEM_MIB, "arbitrary"),
        name="diff_attn",
    )(qvt, p0, qvt, lam_vecs, subln_col, *cast_weights)
    return out[0], out[1:]


def _mla_attn_kernel(*refs, n_heads, scale, n_cast):
    ((cq_ref, kv_ref, qn_ref, kvn_ref, wqt_ref, wk_ref, wvt_ref, cosq_tab, sinq_tab, ck_tab, sak_tab, sbk_tab),
     cast_src, (o_ref,), cast_dst, (m_ref, acc_ref, k_scr, vt_scr)) = (
        _split_cast_refs(refs, 12, 1, n_cast))
    _cast_blocks(cast_src, cast_dst)
    seq = kv_ref.shape[1]
    tq = ATTN_TILE
    half = MLA_ROPE // 2
    projected = set()

    def project_keys(chunk):
        if chunk in projected:
            return
        projected.add(chunk)
        rows = slice(chunk * ROW_TILE, (chunk + 1) * ROW_TILE)
        ckv = _rms(kv_ref[0, rows, :LANES], kvn_ref[...]).astype(BF16)
        kr = _rope_block(kv_ref[0, rows, LANES:], ck_tab[rows, :], sak_tab[rows, :], sbk_tab[rows, :], half)
        k_scr[rows, :] = _dot(jnp.concatenate([ckv, kr.astype(BF16)], axis=-1), wk_ref[...]).astype(BF16)
        vt = _dot_nt(wvt_ref[...], ckv)
        for j in range(ROW_TILE // tq):
            vt_scr[chunk * (ROW_TILE // tq) + j] = vt[:, j * tq:(j + 1) * tq].astype(BF16)

    def q_heads_at(i):
        cols = slice(i * tq, (i + 1) * tq)
        q_t = _dot_nt(wqt_ref[...], _rms(cq_ref[0, cols, :], qn_ref[...]).astype(BF16))
        cos_t, sin_t = cosq_tab[:, cols], sinq_tab[:, cols]
        q_heads = []
        for h in range(n_heads):
            lo = h * MLA_QK_PAD
            x1 = q_t[lo + MLA_NOPE:lo + MLA_NOPE + half, :]
            x2 = q_t[lo + MLA_NOPE + half:lo + MLA_NOPE + 2 * half, :]
            q_h = jnp.concatenate([q_t[lo:lo + MLA_NOPE, :], x1 * cos_t - x2 * sin_t, x2 * cos_t + x1 * sin_t,
                                   q_t[lo + MLA_NOPE + 2 * half:lo + MLA_QK_PAD, :]], axis=0)
            q_heads.append((q_h * scale).astype(BF16))
        return q_heads

    def k_at(kb, h):
        project_keys(kb * tq // ROW_TILE)
        return k_scr[kb * tq:(kb + 1) * tq, h * MLA_QK_PAD:(h + 1) * MLA_QK_PAD]

    def finish_tile(i, slot):
        for h in range(n_heads):
            o_ref[0, i * tq:(i + 1) * tq, h * MLA_V:(h + 1) * MLA_V] = (
                _attn_out(acc_ref.at[slot], h, MLA_V).T.astype(o_ref.dtype))

    _attn_flat(q_heads_at, k_at, lambda kb, h: vt_scr[kb, h * MLA_V:(h + 1) * MLA_V, :],
               finish_tile, m_ref, acc_ref, n_tiles=seq // tq, n_maps=1, tq=tq, q_transposed=True,
               lookahead=MLA_ATTN_LOOKAHEAD)


def _mla_attn(p1, q_norm, kv_norm, wq_t, wk, wv_t, q_tabs_t, tabs, n_heads, cast_weights):
    b, s, _ = p1.shape
    n_kb = s // ATTN_TILE
    cast_specs, cast_shapes = _cast_specs(cast_weights, b)
    kern = functools.partial(_mla_attn_kernel, n_heads=n_heads, scale=(MLA_NOPE + MLA_ROPE) ** -0.5 * LOG2_E,
                             n_cast=len(cast_weights))
    const = lambda a: pl.BlockSpec(a.shape, lambda bb: (0,) * a.ndim)
    out = pl.pallas_call(
        kern,
        grid=(b,),
        in_specs=[pl.BlockSpec((1, s, 2 * LANES), lambda bb: (bb, 0, 1)),
                  pl.BlockSpec((1, s, 2 * LANES), lambda bb: (bb, 0, 2)),
                  const(q_norm), const(kv_norm), const(wq_t), const(wk), const(wv_t),
                  const(q_tabs_t[0]), const(q_tabs_t[1]),
                  const(tabs[0]), const(tabs[1]), const(tabs[2])] + cast_specs,
        out_specs=[pl.BlockSpec((1, s, n_heads * MLA_V), lambda bb: (bb, 0, 0))] + cast_specs,
        out_shape=[jax.ShapeDtypeStruct((b, s, n_heads * MLA_V), BF16)] + cast_shapes,
        scratch_shapes=_attn_flat_scratch(n_heads, MLA_V) + [
            pltpu.VMEM((s, n_heads * MLA_QK_PAD), BF16),
            pltpu.VMEM((n_kb, n_heads * MLA_V, ATTN_TILE), BF16)],
        compiler_params=_params(MLA_ATTN_VMEM_MIB, "arbitrary"),
        name="mla_attn",
    )(p1, p1, q_norm, kv_norm, wq_t, wk, wv_t, *q_tabs_t, *tabs, *cast_weights)
    return out[0], out[1:]


def _conv_kernel(a_ref, gate_ref, ah_ref, gh_ref, w_ref, b_ref, lg_ref, lb_ref, o_ref, u_ref, ur_ref):
    tt = a_ref.shape[1]
    u_ref[CONV_HALO:, :] = a_ref[0].astype(F32) * _sigmoid(gate_ref[0].astype(F32))
    halo = ah_ref[0].astype(F32) * _sigmoid(gh_ref[0].astype(F32))
    u_ref[:CONV_HALO, :] = jnp.where(pl.program_id(1) > 0, halo, jnp.zeros_like(halo))
    rows = ur_ref.shape[1]
    for r in range(1, SUBLANES):
        ur_ref[r - 1] = u_ref[r:r + rows, :]
    acc = jnp.zeros((tt, a_ref.shape[2]), F32)
    first = CONV_HALO - (CONV_WIDTH - 1)
    for k in range(CONV_WIDTH):
        base, r = divmod(first + k, SUBLANES)
        src = u_ref if r == 0 else ur_ref.at[r - 1]
        acc = acc + src[base * SUBLANES:base * SUBLANES + tt, :] * w_ref[k:k + 1, :]
    y = acc + b_ref[...]
    mu = jnp.mean(y, axis=-1, keepdims=True)
    yc = y - mu
    yn = yc * lax.rsqrt(jnp.mean(yc * yc, axis=-1, keepdims=True) + LN_EPS) * lg_ref[...] + lb_ref[...]
    o_ref[0] = (yn * _sigmoid(yn)).astype(o_ref.dtype)


def _conv_module(p0, dw_w, dw_b, ln_g, ln_b, col0):
    b, s, _ = p0.shape
    c = dw_w.shape[1]
    a_blk, g_blk = col0 // c, col0 // c + 1
    ratio = CONV_TILE // CONV_HALO
    main = lambda blk: pl.BlockSpec((1, CONV_TILE, c), lambda bb, t: (bb, t, blk))
    halo = lambda blk: pl.BlockSpec((1, CONV_HALO, c), lambda bb, t: (bb, jnp.maximum(t * ratio - 1, 0), blk))
    vec = lambda a: pl.BlockSpec(a.shape, lambda bb, t: (0, 0))
    return pl.pallas_call(
        _conv_kernel,
        grid=(b, s // CONV_TILE),
        in_specs=[main(a_blk), main(g_blk), halo(a_blk), halo(g_blk),
                  vec(dw_w), vec(dw_b), vec(ln_g), vec(ln_b)],
        out_specs=pl.BlockSpec((1, CONV_TILE, c), lambda bb, t: (bb, t, 0)),
        out_shape=jax.ShapeDtypeStruct((b, s, c), BF16),
        scratch_shapes=[pltpu.VMEM((CONV_HALO + CONV_TILE, c), F32),
                        pltpu.VMEM((SUBLANES - 1, CONV_HALO + CONV_TILE - SUBLANES, c), F32)],
        compiler_params=_params(CONV_VMEM_MIB, "parallel", "parallel"),
        name="conv_module",
    )(p0, p0, p0, p0, dw_w, dw_b, ln_g, ln_b)


def _post_kernel(*refs, n_mix, with_next):
    x_ref = refs[0]
    mix_refs = refs[1:1 + n_mix]
    (wo_ref, gpost_ref, gfpre_ref, wg_ref, wu_ref, wd_ref, gfpost_ref) = refs[1 + n_mix:8 + n_mix]
    pos = 8 + n_mix
    if with_next:
        gnext_ref, wnext_ref = refs[pos:pos + 2]
        pos += 2
    h_out_ref = refs[pos]
    pos += 1
    if with_next:
        p_out_ref = refs[pos]
        pos += 1
    hid_ref = refs[pos]

    n_rows = x_ref.shape[0]
    groups = [slice(r0, r0 + POST_GROUP_ROWS) for r0 in range(0, n_rows, POST_GROUP_ROWS)]

    def out_proj(rows):
        y = None
        row = 0
        for r in mix_refs:
            w = r.shape[-1]
            part = _dot(r[rows, :], wo_ref[row:row + w, :])
            y = part if y is None else y + part
            row += w
        return y

    ys = [out_proj(rows) for rows in groups]
    h1s = [x_ref[rows, :] + _rms(y, gpost_ref[...]) for rows, y in zip(groups, ys)]
    ts = [_rms(h1, gfpre_ref[...]).astype(BF16) for h1 in h1s]
    d_ff = wg_ref.shape[1]
    for j in range(0, d_ff, FFN_CHUNK):
        wdt = min(FFN_CHUNK, d_ff - j)
        for rows, t in zip(groups, ts):
            gate = _dot(t, wg_ref[:, j:j + wdt])
            up = _dot(t, wu_ref[:, j:j + wdt])
            hid_ref[rows, j:j + wdt] = (gate * _sigmoid(gate) * up).astype(BF16)
    fs = [_dot(hid_ref[rows, :], wd_ref[...]) for rows in groups]
    h2s = [h1 + _rms(f, gfpost_ref[...]) for h1, f in zip(h1s, fs)]
    for rows, h2 in zip(groups, h2s):
        h_out_ref[rows, :] = h2
    if with_next:
        t2s = [_rms(h2, gnext_ref[...]).astype(BF16) for h2 in h2s]
        for rows, t2 in zip(groups, t2s):
            p_out_ref[rows, :] = _dot(t2, wnext_ref[...])


def _post(x2, mix_parts, w_out, g_post, g_fpre, w_gate, w_up, w_down, g_fpost, nxt=None):
    n, d = x2.shape
    d_ff = w_gate.shape[1]
    row_tile = POST_ROW_TILE
    row = lambda a: pl.BlockSpec((row_tile, a.shape[1]), lambda i: (i, 0))
    consts = [w_out, g_post, g_fpre, w_gate, w_up, w_down, g_fpost] + (list(nxt) if nxt else [])
    out_shape = [jax.ShapeDtypeStruct((n, d), F32)]
    out_specs = [pl.BlockSpec((row_tile, d), lambda i: (i, 0))]
    if nxt:
        n_next = nxt[1].shape[1]
        out_shape.append(jax.ShapeDtypeStruct((n, n_next), F32))
        out_specs.append(pl.BlockSpec((row_tile, n_next), lambda i: (i, 0)))
    kern = functools.partial(_post_kernel, n_mix=len(mix_parts), with_next=bool(nxt))
    return pl.pallas_call(
        kern,
        grid=(n // row_tile,),
        in_specs=[row(x2)] + [row(m) for m in mix_parts] + [_const_spec(c.shape) for c in consts],
        out_specs=out_specs,
        out_shape=out_shape,
        scratch_shapes=[pltpu.VMEM((row_tile, d_ff), BF16)],
        compiler_params=_params(POST_VMEM_MIB, "parallel"),
        name="post_next" if nxt else "post",
    )(x2, *mix_parts, *consts)


def _gelu_tanh(x):
    return 0.5 * x * (1.0 + jnp.tanh(math.sqrt(2.0 / math.pi) * (x + 0.044715 * (x * x * x))))


def _ssm_kernel(u_ref, lr_ref, li_ref, ldt_ref, bre_ref, bim_ref, cre_ref, cim_ref, d_ref, wg_ref, bg_ref,
                o_ref, bmat_ref, cmat_ref, a_ref, st_ref, us_ref, utm_ref, x_ref, y_ref):
    nb, tt, ch = u_ref.shape
    n_state = lr_ref.shape[1]

    @pl.when(pl.program_id(0) == 0)
    def _init():
        lr, li = lr_ref[...], li_ref[...]
        dt = jnp.exp(ldt_ref[...])
        mag = jnp.exp(lr * dt)
        ab_re = mag * jnp.cos(li * dt)
        ab_im = mag * jnp.sin(li * dt)
        den = lr * lr + li * li
        n_re = ab_re - 1.0
        f_re = (n_re * lr + ab_im * li) / den
        f_im = (ab_im * lr - n_re * li) / den
        br, bi = bre_ref[...], bim_ref[...]
        bmat_ref[:, :n_state] = (f_re * br - f_im * bi).astype(BF16)
        bmat_ref[:, n_state:] = (f_re * bi + f_im * br).astype(BF16)
        cmat_ref[:n_state, :] = cre_ref[...].astype(BF16)
        cmat_ref[n_state:, :] = (-cim_ref[...]).astype(BF16)
        a_ref[0:1, :] = ab_re
        a_ref[1:2, :] = ab_im
        st_ref[...] = jnp.zeros(st_ref.shape, F32)

    n_ublk = ch // LANES
    for b in range(nb):
        for j in range(n_ublk):
            us_ref[j, b * SSM_PITCH:b * SSM_PITCH + tt, :] = u_ref[b, :, j * LANES:(j + 1) * LANES]

    def gather_step(t, carry):
        dst = pl.multiple_of(t * nb, nb)
        for j in range(n_ublk):
            utm_ref[pl.ds(dst, nb), j * LANES:(j + 1) * LANES] = us_ref[j, pl.ds(t, nb, stride=SSM_PITCH), :]
        return carry

    lax.fori_loop(0, tt, gather_step, 0, unroll=8)

    half_rows = tt * nb // 2
    for r0 in (0, half_rows):
        x_ref[r0:r0 + half_rows, :] = _dot(utm_ref[r0:r0 + half_rows, :].astype(BF16), bmat_ref[...])

    a_re = jnp.broadcast_to(a_ref[0:1, :], (nb, n_state))
    a_im = jnp.broadcast_to(a_ref[1:2, :], (nb, n_state))

    def scan_step(t, carry):
        x_re, x_im = carry
        row = pl.multiple_of(t * nb, nb)
        n_re = a_re * x_re - a_im * x_im + x_ref[pl.ds(row, nb), :n_state]
        n_im = a_re * x_im + a_im * x_re + x_ref[pl.ds(row, nb), n_state:]
        x_ref[pl.ds(row, nb), :n_state] = n_re
        x_ref[pl.ds(row, nb), n_state:] = n_im
        return n_re, n_im

    x_re, x_im = lax.fori_loop(0, tt, scan_step, (st_ref[:, :n_state], st_ref[:, n_state:]), unroll=4)
    st_ref[:, :n_state] = x_re
    st_ref[:, n_state:] = x_im

    n_yblk = ch // LANES
    for r0 in (0, half_rows):
        y_tm = _dot(x_ref[r0:r0 + half_rows, :].astype(BF16), cmat_ref[...])
        for j in range(n_yblk):
            y_ref[j, r0:r0 + half_rows, :] = y_tm[:, j * LANES:(j + 1) * LANES]
    for b in range(nb):
        y = jnp.concatenate([y_ref[j, pl.ds(b, tt, stride=nb), :] for j in range(n_yblk)], axis=-1)
        y = y + d_ref[...] * u_ref[b]
        z = _gelu_tanh(y)
        gate = _dot(z.astype(BF16), wg_ref[...]) + bg_ref[...]
        o_ref[b] = (z * _sigmoid(gate)).astype(o_ref.dtype)


def _ssm(p1, rows, b_bd, c_bd, d_row, w_glu, b_glu):
    b, s, _ = p1.shape
    ch = w_glu.shape[0]
    n_state = rows[0].shape[1]
    consts = list(rows) + list(b_bd) + list(c_bd) + [d_row, w_glu, b_glu]
    return pl.pallas_call(
        _ssm_kernel,
        grid=(s // SSM_TILE,),
        in_specs=[pl.BlockSpec((b, SSM_TILE, ch), lambda t: (0, t, 0))] + [_const_spec(c.shape) for c in consts],
        out_specs=pl.BlockSpec((b, SSM_TILE, ch), lambda t: (0, t, 0)),
        out_shape=jax.ShapeDtypeStruct((b, s, ch), BF16),
        scratch_shapes=[pltpu.VMEM((ch, 2 * n_state), BF16),
                        pltpu.VMEM((2 * n_state, ch), BF16),
                        pltpu.VMEM((SUBLANES, n_state), F32),
                        pltpu.VMEM((b, 2 * n_state), F32),
                        pltpu.VMEM((ch // LANES, b * SSM_PITCH, LANES), F32),
                        pltpu.VMEM((SSM_TILE * b, ch), F32),
                        pltpu.VMEM((SSM_TILE * b, 2 * n_state), F32),
                        pltpu.VMEM((ch // LANES, SSM_TILE * b, LANES), F32)],
        compiler_params=_params(SSM_VMEM_MIB, "arbitrary"),
        name="s5_ssm",
    )(p1, *consts)


def _rope_tables(s, rot_dim, theta):
    inv = float(theta) ** (-np.arange(0, rot_dim, 2, dtype=np.float64) / rot_dim)
    ang = np.arange(s, dtype=np.float64)[:, None] * inv[None, :]
    return np.cos(ang).astype(np.float32), np.sin(ang).astype(np.float32)


def _lane_tables(cos, sin, period):
    s, half = cos.shape
    reps = LANES // period
    one = np.ones((s, period - 2 * half), np.float32)
    zero = np.zeros((s, period - 2 * half), np.float32)
    zh = np.zeros((s, half), np.float32)
    c = np.tile(np.concatenate([cos, cos, one], axis=1), (1, reps))
    sa = np.tile(np.concatenate([-sin, zh, zero], axis=1), (1, reps))
    sb = np.tile(np.concatenate([zh, sin, zero], axis=1), (1, reps))
    return c, sa, sb


def _block_diag(blocks):
    g, r, c = blocks.shape
    eye = jnp.eye(g, dtype=blocks.dtype)
    return (eye[:, None, :, None] * blocks[:, :, None, :]).reshape(g * r, g * c)


def kernel(x, l0_mix_pre, l0_mix_post, l0_w_in, l0_lambda_q1, l0_lambda_k1, l0_lambda_q2, l0_lambda_k2, l0_subln, l0_dw_w, l0_dw_b, l0_conv_ln_g, l0_conv_ln_b, l0_w_out, l0_ffn_pre, l0_ffn_post, l0_w_gate, l0_w_up, l0_w_down, l1_mix_pre, l1_mix_post, l1_w_in, l1_a_re, l1_a_im, l1_log_dt, l1_b_re, l1_b_im, l1_c_re, l1_c_im, l1_d_skip, l1_w_glu, l1_b_glu, l1_q_norm, l1_w_uq, l1_kv_norm, l1_w_ukv, l1_w_out, l1_ffn_pre, l1_ffn_post, l1_w_gate, l1_w_up, l1_w_down):
    b, s, d = x.shape
    n = b * s
    row = lambda v: v.reshape(1, -1).astype(F32)
    bf = lambda w: w.astype(BF16)

    diff_width = 4 * LANES
    n_diff_heads = diff_width // LANES
    conv_ch = l0_dw_w.shape[1]
    ssm_ch = l1_w_glu.shape[0]
    n_groups, n_state_g = l1_a_re.shape
    q_rank = l1_q_norm.shape[0]
    kv_rank = l1_kv_norm.shape[0]
    n_mla_heads = l1_w_uq.shape[1] // (MLA_NOPE + MLA_ROPE)

    cos_a, sin_a = _rope_tables(s, DIFF_ROT, ROPE_THETA)
    tabs_a = _lane_tables(cos_a, sin_a, period=DIFF_HEAD_DIM)
    x2 = x.reshape(n, d)
    p0, qvt_a = _l0_in(x2, row(l0_mix_pre), l0_w_in.astype(F32), tabs_a, (cos_a.T, sin_a.T), b, s, diff_width)
    p0 = p0.reshape(b, s, -1)
    lam_vecs = jnp.stack([l0_lambda_q1, l0_lambda_k1, l0_lambda_q2, l0_lambda_k2]).astype(F32)
    y_a, (w_out0, w_gate0, w_up0, w_down0) = _diff_attn(
        p0, qvt_a, lam_vecs, l0_subln.reshape(-1, 1).astype(F32), n_diff_heads,
        [l0_w_out, l0_w_gate, l0_w_up, l0_w_down])
    y_b = _conv_module(p0, l0_dw_w.astype(F32), row(l0_dw_b), row(l0_conv_ln_g), row(l0_conv_ln_b),
                       col0=diff_width)

    pad = (-l1_w_in.shape[1]) % LANES
    w_in1 = bf(jnp.pad(l1_w_in, ((0, 0), (0, pad))))
    h2, p1 = _post(x2, [y_a.reshape(n, -1), y_b.reshape(n, -1)], w_out0, row(l0_mix_post),
                   row(l0_ffn_pre), w_gate0, w_up0, w_down0, row(l0_ffn_post),
                   nxt=(row(l1_mix_pre), w_in1))

    state_row = lambda a: a.reshape(1, -1).astype(F32)
    ssm_rows = (state_row(l1_a_re), state_row(l1_a_im),
                state_row(jnp.broadcast_to(l1_log_dt[:, None], (n_groups, n_state_g))))
    b_bd = tuple(_block_diag(jnp.swapaxes(m, 1, 2).astype(F32)) for m in (l1_b_re, l1_b_im))
    c_bd = tuple(_block_diag(jnp.swapaxes(m, 1, 2).astype(F32)) for m in (l1_c_re, l1_c_im))
    y_c = _ssm(p1.reshape(b, s, -1), ssm_rows, b_bd, c_bd, row(l1_d_skip), bf(l1_w_glu), row(l1_b_glu))

    wq = l1_w_uq.reshape(q_rank, n_mla_heads, MLA_NOPE + MLA_ROPE)
    wq = jnp.pad(wq, ((0, 0), (0, 0), (0, MLA_QK_PAD - MLA_NOPE - MLA_ROPE))).reshape(q_rank, -1)
    wkv = l1_w_ukv.reshape(kv_rank, n_mla_heads, MLA_NOPE + MLA_V)
    wk_nope = jnp.pad(wkv[:, :, :MLA_NOPE], ((0, 0), (0, 0), (0, MLA_QK_PAD - MLA_NOPE)))
    route = jnp.pad(jnp.eye(MLA_ROPE, dtype=F32), ((0, LANES - MLA_ROPE), (MLA_NOPE, MLA_QK_PAD - MLA_NOPE - MLA_ROPE)))
    wk_rope = jnp.broadcast_to(route[:, None, :], (LANES, n_mla_heads, MLA_QK_PAD))
    wk = jnp.concatenate([wk_nope, wk_rope], axis=0).reshape(kv_rank + LANES, -1)
    wv_t = wkv[:, :, MLA_NOPE:].reshape(kv_rank, -1).T
    cos_d, sin_d = _rope_tables(s, MLA_ROPE, MLA_ROPE_THETA)
    tabs_d = _lane_tables(cos_d, sin_d, period=LANES)
    y_d, (w_out1, w_gate1, w_up1, w_down1) = _mla_attn(
        p1.reshape(b, s, -1), row(l1_q_norm), row(l1_kv_norm), bf(wq.T), bf(wk), bf(wv_t),
        (cos_d.T, sin_d.T), tabs_d, n_mla_heads, [l1_w_out, l1_w_gate, l1_w_up, l1_w_down])

    (out,) = _post(h2, [y_c.reshape(n, -1), y_d.reshape(n, -1)], w_out1, row(l1_mix_post),
                   row(l1_ffn_pre), w_gate1, w_up1, w_down1, row(l1_ffn_post))
    return out.reshape(b, s, d)
```
